```python
import math
import jax, jax.numpy as jnp
from jax import lax
import numpy as np

D_MODEL = 2048
BATCH = 1
SEQ = 16384
DEPTH = 4

HEAD_DIM = 64
N_MIXERS = 4
MIX_WIDTH = D_MODEL
GROUP_WIDTH = MIX_WIDTH // N_MIXERS
GROUP_HEADS = GROUP_WIDTH // HEAD_DIM

A_KV_HEADS = 2
A_RADIUS = 128
C_KV_HEADS = 2
C_BLOCK = 128
ROPE_THETA = 10000.0
NA_ROWS = 8
NA_COLS = 16
D_CONFIGS = ((128, 1), (512, 4), (2048, 16))
GRID_W = 64
T5_BUCKETS = 32
T5_MAX_DIST = 1024
T5_HEADS = 2 * GROUP_HEADS
EPS = 1e-6
NEG = -1e30

KV_A = A_KV_HEADS * HEAD_DIM
KV_C = C_KV_HEADS * HEAD_DIM
SPLITS = (GROUP_WIDTH, KV_A, KV_A, GROUP_WIDTH,
          GROUP_WIDTH, GROUP_WIDTH, GROUP_WIDTH, GROUP_WIDTH,
          GROUP_WIDTH, KV_C, KV_C, GROUP_WIDTH,
          GROUP_WIDTH, GROUP_WIDTH, GROUP_WIDTH, GROUP_WIDTH)
IN_WIDTH = sum(SPLITS)

kernel_name = "hybrid_parallel_heads_bidir_encoder"


def rms_norm(x, w):
    xf = x.astype(jnp.float32)
    y = xf * lax.rsqrt(jnp.mean(xf * xf, axis=-1, keepdims=True) + EPS)
    return (y * w.astype(jnp.float32)).astype(x.dtype)


def t5_bucket(rel):
    half = T5_BUCKETS // 2
    exact = half // 2
    n = jnp.abs(rel)
    big = exact + (jnp.log(jnp.maximum(n, exact).astype(jnp.float32) / exact)
                   / math.log(T5_MAX_DIST / exact) * (half - exact)).astype(jnp.int32)
    big = jnp.minimum(big, half - 1)
    return jnp.where(rel > 0, half, 0) + jnp.where(n < exact, n, big)


def t5_window_bias(table, head_lo, blk, step):
    i = jnp.arange(blk)[:, None]
    c = jnp.arange(3 * blk)[None, :]
    rel = (c - blk - i) * step
    b = table.astype(jnp.float32)[t5_bucket(rel)][..., head_lo:head_lo + GROUP_HEADS]
    return jnp.transpose(b, (2, 0, 1))


def neighbour_blocks(x, blk):
    pad = [(0, 0), (blk, blk)] + [(0, 0)] * (x.ndim - 2)
    xp = jnp.pad(x, pad)
    nb = x.shape[1] // blk
    xb = xp.reshape(x.shape[0], nb + 2, blk, *x.shape[2:])
    return jnp.concatenate([xb[:, :-2], xb[:, 1:-1], xb[:, 2:]], axis=2)


def window_attention_stats(q, k, v, radius, bias):
    B, L, Hkv, G, dh = q.shape
    blk = radius
    Lp = -(-L // blk) * blk
    nb = Lp // blk
    qb = jnp.pad(q, [(0, 0), (0, Lp - L)] + [(0, 0)] * 3).reshape(B, nb, blk, Hkv, G, dh)
    padk = [(0, 0), (0, Lp - L)] + [(0, 0)] * 2
    kb = neighbour_blocks(jnp.pad(k, padk), blk)
    vb = neighbour_blocks(jnp.pad(v, padk), blk)
    s = jnp.einsum('bnqhgd,bnchd->bnhgqc', qb, kb).astype(jnp.float32) * dh ** -0.5 + bias
    qpos = jnp.arange(nb)[:, None, None] * blk + jnp.arange(blk)[None, :, None]
    kpos = jnp.arange(nb)[:, None, None] * blk + jnp.arange(3 * blk)[None, None, :] - blk
    valid = (jnp.abs(kpos - qpos) <= radius) & (kpos >= 0) & (kpos < L)
    s = jnp.where(valid[None, :, None, None], s, NEG)
    m = jnp.max(s, axis=-1)
    p = jnp.exp(s - m[..., None])
    l = jnp.sum(p, axis=-1)
    o = jnp.einsum('bnhgqc,bnchd->bnqhgd', p.astype(v.dtype), vb).astype(jnp.float32)
    o = o.reshape(B, Lp, Hkv, G, dh)[:, :L]
    m = jnp.transpose(m, (0, 1, 4, 2, 3)).reshape(B, Lp, Hkv, G)[:, :L]
    l = jnp.transpose(l, (0, 1, 4, 2, 3)).reshape(B, Lp, Hkv, G)[:, :L]
    return m, l, o


def mixer_window_sink(q, k, v, sink, bias):
    B, L = q.shape[:2]
    G = GROUP_HEADS // A_KV_HEADS
    qh = q.reshape(B, L, A_KV_HEADS, G, HEAD_DIM)
    kh = k.reshape(B, L, A_KV_HEADS, HEAD_DIM)
    vh = v.reshape(B, L, A_KV_HEADS, HEAD_DIM)
    m, l, o = window_attention_stats(qh, kh, vh, A_RADIUS, bias)
    sk = sink.astype(jnp.float32).reshape(A_KV_HEADS, G)
    m2 = jnp.maximum(m, sk)
    a = jnp.exp(m - m2)
    den = l * a + jnp.exp(sk - m2)
    out = o * (a / den)[..., None]
    return out.reshape(B, L, GROUP_WIDTH).astype(q.dtype)


def mixer_neighbourhood(q, k, v, rpb):
    B, L = q.shape[:2]
    H = GROUP_HEADS
    rows = L // GRID_W
    kr = min(NA_ROWS, rows)
    qg = q.reshape(B, rows, GRID_W, H, HEAD_DIM)
    kg = k.reshape(B, rows, GRID_W, H, HEAD_DIM)
    vg = v.reshape(B, rows, GRID_W, H, HEAD_DIM)
    cols = jnp.arange(GRID_W)
    col_start = jnp.clip(cols - NA_COLS // 2, 0, GRID_W - NA_COLS)
    col_idx = col_start[:, None] + jnp.arange(NA_COLS)[None, :]
    dc = col_idx - cols[:, None] + (NA_COLS - 1)
    rpb = rpb.astype(jnp.float32)

    def one_row(args):
        q_row, r = args
        start = jnp.clip(r - kr // 2, 0, rows - kr)
        kband = lax.dynamic_slice_in_dim(kg, start, kr, axis=1)[:, :, col_idx]
        vband = lax.dynamic_slice_in_dim(vg, start, kr, axis=1)[:, :, col_idx]
        dr = start + jnp.arange(kr) - r + (NA_ROWS - 1)
        bias = rpb[:, dr[None, :, None], dc[:, None, :]]
        s = jnp.einsum('bwhd,brwjhd->bhwrj', q_row, kband).astype(jnp.float32) * HEAD_DIM ** -0.5
        p = jax.nn.softmax(s + bias[None], axis=(-2, -1))
        return jnp.einsum('bhwrj,brwjhd->bwhd', p.astype(v.dtype), vband)

    out = lax.map(one_row, (jnp.moveaxis(qg, 1, 0), jnp.arange(rows)))
    return jnp.moveaxis(out, 0, 1).reshape(B, L, GROUP_WIDTH)


def axial_rope_tables(L):
    t = jnp.arange(L)
    row = (t // GRID_W).astype(jnp.float32)
    col = (t % GRID_W).astype(jnp.float32)
    axis_dim = HEAD_DIM // 2
    inv = ROPE_THETA ** (-jnp.arange(0, axis_dim, 2, dtype=jnp.float32) / axis_dim)
    ang = jnp.concatenate([row[:, None] * inv[None], col[:, None] * inv[None]], axis=-1)
    return jnp.cos(ang), jnp.sin(ang)


def apply_rope(x, cos, sin):
    xf = x.astype(jnp.float32).reshape(*x.shape[:-1], HEAD_DIM // 2, 2)
    x0, x1 = xf[..., 0], xf[..., 1]
    c = cos[None, :, None]
    s = sin[None, :, None]
    out = jnp.stack([x0 * c - x1 * s, x0 * s + x1 * c], axis=-1)
    return out.reshape(x.shape).astype(x.dtype)


def mixer_axial_dense(q, k, v, q_scale, k_scale, cos, sin):
    B, L = q.shape[:2]
    G = GROUP_HEADS // C_KV_HEADS
    qh = rms_norm(q.reshape(B, L, GROUP_HEADS, HEAD_DIM), q_scale)
    kh = rms_norm(k.reshape(B, L, C_KV_HEADS, HEAD_DIM), k_scale)
    qh = apply_rope(qh, cos, sin).reshape(B, L, C_KV_HEADS, G, HEAD_DIM)
    kh = apply_rope(kh, cos, sin)
    vh = v.reshape(B, L, C_KV_HEADS, HEAD_DIM)
    nb = L // C_BLOCK
    qb = jnp.moveaxis(qh.reshape(B, nb, C_BLOCK, C_KV_HEADS, G, HEAD_DIM), 1, 0)

    def one_block(qblk):
        s = jnp.einsum('bqhgd,bkhd->bhgqk', qblk, kh).astype(jnp.float32) * HEAD_DIM ** -0.5
        p = jax.nn.softmax(s, axis=-1)
        return jnp.einsum('bhgqk,bkhd->bqhgd', p.astype(vh.dtype), vh)

    out = lax.map(one_block, qb)
    return jnp.moveaxis(out, 0, 1).reshape(B, L, GROUP_WIDTH)


def mixer_dilated(q, k, v, biases):
    B, L = q.shape[:2]
    H = GROUP_HEADS
    ms, ls, os_ = [], [], []
    for (window, dil), bias in zip(D_CONFIGS, biases):
        radius = window // 2 // dil

        def to_sub(t):
            return (t.reshape(B, L // dil, dil, H, HEAD_DIM).transpose(0, 2, 1, 3, 4)
                    .reshape(B * dil, L // dil, H, HEAD_DIM))

        def from_sub(t):
            return t.reshape(B, dil, L // dil, *t.shape[2:]).swapaxes(1, 2).reshape(B, L, *t.shape[2:])

        m, l, o = window_attention_stats(to_sub(q)[:, :, :, None], to_sub(k), to_sub(v), radius, bias)
        ms.append(from_sub(m))
        ls.append(from_sub(l))
        os_.append(from_sub(o))
    mmax = jnp.maximum(jnp.maximum(ms[0], ms[1]), ms[2])
    scales = [jnp.exp(m - mmax) for m in ms]
    num = os_[0] * scales[0][..., None] + os_[1] * scales[1][..., None] + os_[2] * scales[2][..., None]
    den = ls[0] * scales[0] + ls[1] * scales[1] + ls[2] * scales[2]
    return (num / den[..., None]).reshape(B, L, GROUP_WIDTH).astype(q.dtype)


def setup_inputs(seed: int = 0) -> dict:
    key = jax.random.key(seed)
    ks = jax.random.split(key, 13)
    D = D_MODEL
    nrm = jax.random.normal
    f32 = jnp.float32
    return {
        "x": nrm(ks[0], (BATCH, SEQ, D), f32),
        "c": nrm(ks[1], (BATCH, D), f32),
        "w_ada": nrm(ks[2], (DEPTH, D, 3 * D), f32) * (0.5 * D ** -0.5),
        "b_ada": 0.02 * nrm(ks[3], (DEPTH, 3 * D), f32),
        "norm_w": 1.0 + 0.02 * nrm(ks[4], (DEPTH, D), f32),
        "w_in": nrm(ks[5], (DEPTH, D, IN_WIDTH), f32) * D ** -0.5,
        "w_out": nrm(ks[6], (DEPTH, MIX_WIDTH, D), f32) * MIX_WIDTH ** -0.5,
        "attn_sink": nrm(ks[7], (DEPTH, GROUP_HEADS), f32),
        "na_rpb": 0.1 * nrm(ks[8], (DEPTH, GROUP_HEADS, 2 * NA_ROWS - 1, 2 * NA_COLS - 1), f32),
        "q_norm_w": 1.0 + 0.02 * nrm(ks[9], (DEPTH, HEAD_DIM), f32),
        "k_norm_w": 1.0 + 0.02 * nrm(ks[10], (DEPTH, HEAD_DIM), f32),
        "t5_table": 0.1 * nrm(ks[11], (T5_BUCKETS, T5_HEADS), f32),
        "final_norm_w": 1.0 + 0.02 * nrm(ks[12], (D,), f32),
    }


def reference(x, c, w_ada, b_ada, norm_w, w_in, w_out, attn_sink, na_rpb, q_norm_w, k_norm_w,
              t5_table, final_norm_w):
    B, L, _ = x.shape
    g_a = GROUP_HEADS // A_KV_HEADS
    bias_a = t5_window_bias(t5_table, 0, A_RADIUS, 1).reshape(A_KV_HEADS, g_a, A_RADIUS, 3 * A_RADIUS)
    bias_d = tuple(t5_window_bias(t5_table, GROUP_HEADS, w // 2 // d, d)[:, None] for (w, d) in D_CONFIGS)
    cos, sin = axial_rope_tables(L)
    offsets = [sum(SPLITS[:j + 1]) for j in range(len(SPLITS) - 1)]
    cond = jax.nn.silu(c)
    for i in range(DEPTH):
        mod = cond @ w_ada[i] + b_ada[i]
        shift, scale, gate = jnp.split(mod, 3, axis=-1)
        h = rms_norm(x, norm_w[i]) * (1 + scale[:, None]) + shift[:, None]
        proj = h @ w_in[i]
        (q_a, k_a, v_a, z_a, q_b, k_b, v_b, z_b,
         q_c, k_c, v_c, z_c, q_d, k_d, v_d, z_d) = jnp.split(proj, offsets, axis=-1)
        y_a = mixer_window_sink(q_a, k_a, v_a, attn_sink[i], bias_a)
        y_b = mixer_neighbourhood(q_b, k_b, v_b, na_rpb[i])
        y_c = mixer_axial_dense(q_c, k_c, v_c, q_norm_w[i], k_norm_w[i], cos, sin)
        y_d = mixer_dilated(q_d, k_d, v_d, bias_d)
        y = jnp.concatenate([y_a * jax.nn.silu(z_a), y_b * jax.nn.silu(z_b),
                             y_c * jax.nn.silu(z_c), y_d * jax.nn.silu(z_d)], axis=-1)
        x = x + gate[:, None] * (y @ w_out[i])
    return rms_norm(x, final_norm_w)
```

```python
import functools
import math

import numpy as np
import jax
import jax.numpy as jnp
from jax import lax
from jax.experimental import pallas as pl
from jax.experimental.pallas import tpu as pltpu

HEAD_DIM = 64
GROUP_WIDTH = 512
GROUP_HEADS = 8
A_KV_HEADS = 2
A_RADIUS = 128
C_KV_HEADS = 2
ROPE_THETA = 10000.0
NA_ROWS = 8
NA_COLS = 16
D_CONFIGS = ((128, 1), (512, 4), (2048, 16))
GRID_W = 64
T5_BUCKETS = 32
T5_MAX_DIST = 1024
EPS = 1e-6
NEG = -1e30

LANES = 128
VMEM_LIMIT = 56 * 1024 * 1024

F32 = jnp.float32
BF16 = jnp.bfloat16

_SRC = {}
_off = 0
for _name, _w in (("q_a", 512), ("k_a", 128), ("v_a", 128), ("z_a", 512),
                  ("q_b", 512), ("k_b", 512), ("v_b", 512), ("z_b", 512),
                  ("q_c", 512), ("k_c", 128), ("v_c", 128), ("z_c", 512),
                  ("q_d", 512), ("k_d", 512), ("v_d", 512), ("z_d", 512)):
    _SRC[_name] = (_off, _w)
    _off += _w
IN_WIDTH = _off
_ORDER = ("q_a", "q_b", "k_b", "v_b", "q_c", "q_d", "k_d", "v_d",
          "z_a", "z_b", "z_c", "z_d", "k_a", "v_a", "k_c", "v_c")
_DST = {}
_off = 0
for _name in _ORDER:
    _DST[_name] = _off
    _off += _SRC[_name][1]
_PERM = np.concatenate([np.arange(_SRC[n][0], _SRC[n][0] + _SRC[n][1]) for n in _ORDER])


def _col(name, width):
    assert _DST[name] % width == 0
    return _DST[name] // width


def _ada_kernel(c_ref, w_ref, b_ref, o_ref):
    c = c_ref[...]
    cond = c * jax.nn.sigmoid(c)
    o_ref[0] = jnp.dot(cond, w_ref[0], preferred_element_type=F32,
                       precision=lax.Precision.HIGHEST) + b_ref[0]


def _ada_mod(c8, w_ada, b_ada):
    depth, d, n3 = w_ada.shape
    tn = 1024
    return pl.pallas_call(
        _ada_kernel,
        grid=(depth, n3 // tn),
        in_specs=[pl.BlockSpec((8, d), lambda i, j: (0, 0)),
                  pl.BlockSpec((1, d, tn), lambda i, j: (i, 0, j)),
                  pl.BlockSpec((1, 1, tn), lambda i, j: (i, 0, j))],
        out_specs=pl.BlockSpec((1, 8, tn), lambda i, j: (i, 0, j)),
        out_shape=jax.ShapeDtypeStruct((depth, 8, n3), F32),
        compiler_params=pltpu.CompilerParams(
            dimension_semantics=("arbitrary", "arbitrary"), vmem_limit_bytes=VMEM_LIMIT),
        name="ada_mod",
    )(c8, w_ada, b_ada.reshape(depth, 1, n3))


def _inproj_kernel(x_ref, nw_ref, sc_ref, sh_ref, w_ref, o_ref, h_scr):
    @pl.when(pl.program_id(1) == 0)
    def _():
        x = x_ref[...]
        y = x * lax.rsqrt(jnp.mean(x * x, axis=-1, keepdims=True) + EPS)
        h = (y * nw_ref[...]) * (1.0 + sc_ref[...]) + sh_ref[...]
        h_scr[...] = h.astype(BF16)

    o_ref[...] = jnp.dot(h_scr[...], w_ref[...], preferred_element_type=F32).astype(o_ref.dtype)


def _inproj(x, nw, scale, shift, w_bf16):
    L, d = x.shape
    n = w_bf16.shape[1]
    tm = min(1024, L)
    tn = 512
    vec = pl.BlockSpec((1, d), lambda i, j: (0, 0))
    return pl.pallas_call(
        _inproj_kernel,
        grid=(L // tm, n // tn),
        in_specs=[pl.BlockSpec((tm, d), lambda i, j: (i, 0)), vec, vec, vec,
                  pl.BlockSpec((d, tn), lambda i, j: (0, j))],
        out_specs=pl.BlockSpec((tm, tn), lambda i, j: (i, j)),
        out_shape=jax.ShapeDtypeStruct((L, n), BF16),
        scratch_shapes=[pltpu.VMEM((tm, d), BF16)],
        compiler_params=pltpu.CompilerParams(
            dimension_semantics=("arbitrary", "arbitrary"), vmem_limit_bytes=VMEM_LIMIT),
        name="inproj",
    )(x, nw, scale, shift, w_bf16)


def _band_kernel(*refs, tq, w_keys, r_lo, heads, group, use_sink, interval):
    if use_sink:
        sink_ref, q_ref, k_ref, v_ref, tab_ref, o_ref = refs
    else:
        q_ref, k_ref, v_ref, tab_ref, o_ref = refs
    n = pl.program_id(1)
    start = pl.multiple_of(n * tq, tq)
    kwin = k_ref[pl.ds(start, w_keys), :]
    vwin = v_ref[pl.ds(start, w_keys), :]
    kpos = n * tq - r_lo + lax.broadcasted_iota(jnp.int32, (1, w_keys), 1)
    lo, hi = interval(n)
    kbias = jnp.where((kpos >= lo) & (kpos < hi), 0.0, NEG).astype(F32)
    for h in range(heads):
        kv = h // group
        q = q_ref[:, h * HEAD_DIM:(h + 1) * HEAD_DIM] * HEAD_DIM ** -0.5
        k = kwin[:, kv * HEAD_DIM:(kv + 1) * HEAD_DIM]
        v = vwin[:, kv * HEAD_DIM:(kv + 1) * HEAD_DIM]
        s = lax.dot_general(q, k, (((1,), (1,)), ((), ())), preferred_element_type=F32)
        s = s + tab_ref[h] + kbias
        m = jnp.max(s, axis=-1, keepdims=True)
        p = jnp.exp(s - m)
        l = jnp.sum(p, axis=-1, keepdims=True)
        o = jnp.dot(p.astype(BF16), v, preferred_element_type=F32)
        if use_sink:
            sk = sink_ref[h]
            m2 = jnp.maximum(m, sk)
            a = jnp.exp(m - m2)
            o = o * (a / (l * a + jnp.exp(sk - m2)))
        else:
            o = o / l
        o_ref[:, h * HEAD_DIM:(h + 1) * HEAD_DIM] = o.astype(o_ref.dtype)


def _band_attention(q_src, q_col, k_pad, v_pad, kv_cols, table, *, tq, r_lo, heads, group,
                    interval, sink=None):
    L = q_src.shape[0]
    lp = k_pad.shape[0]
    w_keys = table.shape[-1]
    n_groups = table.shape[0] // heads
    qw = heads * HEAD_DIM
    kern = functools.partial(_band_kernel, tq=tq, w_keys=w_keys, r_lo=r_lo, heads=heads,
                             group=group, use_sink=sink is not None, interval=interval)
    in_specs = [pl.BlockSpec((tq, qw), lambda g, n: (n, q_col + g)),
                pl.BlockSpec((lp, LANES), lambda g, n: (0, kv_cols[0] + g)),
                pl.BlockSpec((lp, LANES), lambda g, n: (0, kv_cols[1] + g)),
                pl.BlockSpec((heads, tq, w_keys), lambda g, n: (g, 0, 0))]
    args = [q_src, k_pad, v_pad, table]
    if sink is not None:
        in_specs = [pl.BlockSpec(memory_space=pltpu.SMEM)] + in_specs
        args = [sink] + args
    return pl.pallas_call(
        kern,
        grid=(n_groups, L // tq),
        in_specs=in_specs,
        out_specs=pl.BlockSpec((tq, qw), lambda g, n: (n, g)),
        out_shape=jax.ShapeDtypeStruct((L, n_groups * qw), BF16),
        compiler_params=pltpu.CompilerParams(
            dimension_semantics=("arbitrary", "arbitrary"), vmem_limit_bytes=VMEM_LIMIT),
        name="band_attention",
    )(*args)


def _t5_bucket(rel):
    half = T5_BUCKETS // 2
    exact = half // 2
    n = jnp.abs(rel)
    big = exact + (jnp.log(jnp.maximum(n, exact).astype(F32) / exact)
                   / math.log(T5_MAX_DIST / exact) * (half - exact)).astype(jnp.int32)
    big = jnp.minimum(big, half - 1)
    return jnp.where(rel > 0, half, 0) + jnp.where(n < exact, n, big)


def _t5_table(t5_table, head_lo, tq, r_lo, w_keys, mult):
    rel = np.arange(-r_lo - (tq - 1), w_keys - r_lo)
    m = mult(rel)
    b = t5_table.astype(F32)[:, head_lo:head_lo + GROUP_HEADS][_t5_bucket(jnp.asarray(rel, jnp.int32))]
    logm = np.log(np.maximum(m, 1)).astype(np.float32)
    vec = jnp.where(jnp.asarray(m > 0)[:, None], b + logm[:, None], NEG).T
    return jnp.stack([vec[:, tq - 1 - i:tq - 1 - i + w_keys] for i in range(tq)], axis=1)


def _window_mult(rel):
    return (np.abs(rel) <= A_RADIUS).astype(np.int32)


def _dilated_mult(rel):
    m = np.zeros(rel.shape, np.int32)
    for window, dil in D_CONFIGS:
        m += ((rel % dil == 0) & (np.abs(rel) <= window // 2)).astype(np.int32)
    return m


_NA_R_LO = (NA_ROWS - 1) * GRID_W
_NA_W = 2 * NA_ROWS * GRID_W


def _na_table(rpb):
    n_dr, n_dc = 2 * NA_ROWS - 1, 2 * NA_COLS - 1
    wq = np.arange(GRID_W)[:, None]
    c = np.arange(_NA_W)[None, :]
    j, kc = c // GRID_W, c % GRID_W
    col_start = np.clip(wq - NA_COLS // 2, 0, GRID_W - NA_COLS)
    valid = (kc >= col_start) & (kc < col_start + NA_COLS) & (j < n_dr)
    front = GRID_W - NA_COLS
    p = jnp.pad(rpb.astype(F32), ((0, 0), (0, 2 * NA_ROWS - n_dr), (front, 2 * GRID_W - 1 - n_dc - front)))
    b = jnp.stack([p[:, :, GRID_W - 1 - w:2 * GRID_W - 1 - w] for w in range(GRID_W)], axis=1)
    b = b.reshape(rpb.shape[0], GRID_W, _NA_W)
    return jnp.where(jnp.asarray(valid)[None], b, NEG)


def _swap_pairs(x):
    n = x.shape[-1]
    lane = lax.broadcasted_iota(jnp.int32, x.shape, x.ndim - 1)
    return jnp.where(lane % 2 == 0, pltpu.roll(x, n - 1, x.ndim - 1), pltpu.roll(x, 1, x.ndim - 1))


def _head_rms(x, ones_bd, w):
    ms = jnp.dot(x * x, ones_bd, preferred_element_type=F32,
                 precision=lax.Precision.HIGHEST) * (1.0 / HEAD_DIM)
    return x * lax.rsqrt(ms + EPS) * w


def _cprep_kernel(q_ref, k_ref, v_ref, cos_ref, sin_ref, qw_ref, kw_ref, bd_ref,
                  qt_ref, k2_ref, vt_ref):
    cos = cos_ref[...]
    sin = sin_ref[...]
    q = _head_rms(q_ref[...].astype(F32), bd_ref[...], qw_ref[...])
    reps = q.shape[1] // LANES
    q = q * jnp.concatenate([cos] * reps, axis=1) + _swap_pairs(q) * jnp.concatenate([sin] * reps, axis=1)
    qt_ref[...] = (q * HEAD_DIM ** -0.5).T.astype(qt_ref.dtype)
    k = _head_rms(k_ref[...].astype(F32), bd_ref[:LANES, :LANES], kw_ref[...])
    k = (k * cos + _swap_pairs(k) * sin).astype(k2_ref.dtype)
    for g in range(C_KV_HEADS):
        k2_ref[g] = k[:, g * HEAD_DIM:(g + 1) * HEAD_DIM]
    vt_ref[0] = v_ref[...].astype(F32).T.astype(vt_ref.dtype)


def _cprep(proj, cos2, sin2, qw, kw, ones_bd, t):
    L = proj.shape[0]
    vec = lambda width: pl.BlockSpec((1, width), lambda n: (0, 0))
    return pl.pallas_call(
        _cprep_kernel,
        grid=(L // t,),
        in_specs=[pl.BlockSpec((t, GROUP_WIDTH), lambda n: (n, _col("q_c", GROUP_WIDTH))),
                  pl.BlockSpec((t, LANES), lambda n: (n, _col("k_c", LANES))),
                  pl.BlockSpec((t, LANES), lambda n: (n, _col("v_c", LANES))),
                  pl.BlockSpec((t, LANES), lambda n: (n, 0)),
                  pl.BlockSpec((t, LANES), lambda n: (n, 0)),
                  vec(GROUP_WIDTH), vec(LANES),
                  pl.BlockSpec((GROUP_WIDTH, GROUP_WIDTH), lambda n: (0, 0))],
        out_specs=[pl.BlockSpec((GROUP_WIDTH, t), lambda n: (0, n)),
                   pl.BlockSpec((C_KV_HEADS, t, HEAD_DIM), lambda n: (0, n, 0)),
                   pl.BlockSpec((1, LANES, t), lambda n: (n, 0, 0))],
        out_shape=[jax.ShapeDtypeStruct((GROUP_WIDTH, L), BF16),
                   jax.ShapeDtypeStruct((C_KV_HEADS, L, HEAD_DIM), BF16),
                   jax.ShapeDtypeStruct((L // t, LANES, t), BF16)],
        compiler_params=pltpu.CompilerParams(
            dimension_semantics=("arbitrary",), vmem_limit_bytes=VMEM_LIMIT),
        name="dense_prep",
    )(proj, proj, proj, cos2, sin2, qw, kw, ones_bd)


def _flash_kernel(qt_ref, k_ref, vt_ref, o_ref, *, tk, heads):
    n_kv = k_ref.shape[1] // tk
    tq = qt_ref.shape[1]
    outs = []
    for h in range(heads):
        qt = qt_ref[h * HEAD_DIM:(h + 1) * HEAD_DIM, :]

        def body(j, carry, qt=qt):
            m, l, acc = carry
            k = k_ref[0, pl.ds(pl.multiple_of(j * tk, tk), tk), :]
            st = jnp.dot(k, qt, preferred_element_type=F32)
            m_new = jnp.maximum(m, jnp.max(st, axis=0, keepdims=True))
            alpha = jnp.exp(m - m_new)
            pt = jnp.exp(st - m_new)
            l = alpha * l + jnp.sum(pt, axis=0, keepdims=True)
            acc = alpha * acc + jnp.dot(vt_ref[j], pt.astype(BF16), preferred_element_type=F32)
            return m_new, l, acc

        init = (jnp.full((1, tq), NEG, F32), jnp.zeros((1, tq), F32), jnp.zeros((HEAD_DIM, tq), F32))
        _, l, acc = lax.fori_loop(0, n_kv, body, init)
        outs.append(acc / l)
    o_ref[...] = jnp.concatenate(outs, axis=0).T.astype(o_ref.dtype)


def _flash(qt, k2, vt, tq):
    L = qt.shape[1]
    n_kv, _, tk = vt.shape
    heads = GROUP_HEADS // C_KV_HEADS
    qrows = heads * HEAD_DIM
    return pl.pallas_call(
        functools.partial(_flash_kernel, tk=tk, heads=heads),
        grid=(C_KV_HEADS, L // tq),
        in_specs=[pl.BlockSpec((qrows, tq), lambda g, n: (g, n)),
                  pl.BlockSpec((1, L, HEAD_DIM), lambda g, n: (g, 0, 0)),
                  pl.BlockSpec((n_kv, HEAD_DIM, tk), lambda g, n: (0, g, 0))],
        out_specs=pl.BlockSpec((tq, qrows), lambda g, n: (n, g)),
        out_shape=jax.ShapeDtypeStruct((L, GROUP_WIDTH), BF16),
        compiler_params=pltpu.CompilerParams(
            dimension_semantics=("arbitrary", "arbitrary"), vmem_limit_bytes=VMEM_LIMIT),
        name="dense_flash",
    )(qt, k2, vt)


def _outproj_kernel(*refs, final):
    if final:
        x_ref, ya, yb, yc, yd, z_ref, w_ref, g_ref, fw_ref, o_ref = refs
    else:
        x_ref, ya, yb, yc, yd, z_ref, w_ref, g_ref, o_ref = refs
    acc = jnp.zeros(x_ref.shape, F32)
    for gi, y_ref in enumerate((ya, yb, yc, yd)):
        cols = slice(gi * GROUP_WIDTH, (gi + 1) * GROUP_WIDTH)
        z = z_ref[:, cols].astype(F32)
        u = (y_ref[...].astype(F32) * (z * jax.nn.sigmoid(z))).astype(BF16)
        acc = acc + jnp.dot(u, w_ref[cols, :], preferred_element_type=F32)
    xn = x_ref[...] + g_ref[...] * acc
    if final:
        xn = xn * lax.rsqrt(jnp.mean(xn * xn, axis=-1, keepdims=True) + EPS) * fw_ref[...]
    o_ref[...] = xn


def _outproj(x, ys, proj, w_bf16, gate, final_w):
    L, d = x.shape
    tm = min(512, L)
    final = final_w is not None
    row = lambda width: pl.BlockSpec((tm, width), lambda i: (i, 0))
    vec = pl.BlockSpec((1, d), lambda i: (0, 0))
    in_specs = [row(d), row(GROUP_WIDTH), row(GROUP_WIDTH), row(GROUP_WIDTH), row(GROUP_WIDTH),
                pl.BlockSpec((tm, d), lambda i: (i, _col("z_a", d))),
                pl.BlockSpec(w_bf16.shape, lambda i: (0, 0)), vec]
    args = [x, *ys, proj, w_bf16, gate]
    if final:
        in_specs.append(vec)
        args.append(final_w)
    return pl.pallas_call(
        functools.partial(_outproj_kernel, final=final),
        grid=(L // tm,),
        in_specs=in_specs,
        out_specs=row(d),
        out_shape=jax.ShapeDtypeStruct((L, d), F32),
        compiler_params=pltpu.CompilerParams(
            dimension_semantics=("arbitrary",), vmem_limit_bytes=VMEM_LIMIT),
        name="outproj",
    )(*args)


def _rope_tables(L):
    t = np.arange(L)
    axis_dim = HEAD_DIM // 2
    inv = jnp.asarray(ROPE_THETA, F32) ** (-jnp.arange(0, axis_dim, 2, dtype=F32) / axis_dim)
    row = jnp.asarray(t // GRID_W, F32)
    col = jnp.asarray(t % GRID_W, F32)
    ang = jnp.concatenate([row[:, None] * inv[None], col[:, None] * inv[None]], axis=-1)
    cos = jnp.repeat(jnp.cos(ang), 2, axis=-1)
    sin = jnp.repeat(jnp.sin(ang), 2, axis=-1) * jnp.asarray(np.tile([-1.0, 1.0], HEAD_DIM // 2), F32)
    return jnp.tile(cos, (1, LANES // HEAD_DIM)), jnp.tile(sin, (1, LANES // HEAD_DIM))


def _pad_rows(a, front, back):
    return jnp.pad(a, ((front, back), (0, 0)))


A_TQ = 256
D_TQ = 128
D_REACH = D_CONFIGS[-1][0] // 2
C_TQ = 512
C_TK = 512


def _table_a(t5_table):
    return _t5_table(t5_table, 0, A_TQ, A_RADIUS, A_TQ + 2 * A_RADIUS, _window_mult)


def _table_d(t5_table):
    return _t5_table(t5_table, GROUP_HEADS, D_TQ, D_REACH, D_TQ + 2 * D_REACH, _dilated_mult)


def _mixer_a(proj, tab_a, sink):
    L = proj.shape[0]
    kv = _pad_rows(proj[:, _DST["k_a"]:_DST["k_a"] + 2 * LANES], A_RADIUS, A_RADIUS)
    return _band_attention(proj, _col("q_a", GROUP_WIDTH), kv, kv, (0, 1), tab_a,
                           tq=A_TQ, r_lo=A_RADIUS, heads=GROUP_HEADS,
                           group=GROUP_HEADS // A_KV_HEADS, interval=lambda n: (0, L),
                           sink=sink.astype(F32))


def _mixer_b(proj, tab_b):
    rows = proj.shape[0] // GRID_W

    def interval(n):
        lo = jnp.clip(n - NA_ROWS // 2, 0, rows - NA_ROWS) * GRID_W
        return lo, lo + NA_ROWS * GRID_W

    kv = _pad_rows(proj[:, _DST["k_b"]:_DST["k_b"] + 2 * GROUP_WIDTH],
                   _NA_R_LO, _NA_W - GRID_W - _NA_R_LO)
    return _band_attention(proj, _col("q_b", LANES), kv, kv, (0, GROUP_WIDTH // LANES), tab_b,
                           tq=GRID_W, r_lo=_NA_R_LO, heads=2, group=1, interval=interval)


def _mixer_c(proj, q_norm_w, k_norm_w):
    L = proj.shape[0]
    cos2, sin2 = _rope_tables(L)
    ones_bd = jnp.asarray(np.kron(np.eye(GROUP_HEADS), np.ones((HEAD_DIM, HEAD_DIM))), F32)
    qt, k2, vt = _cprep(proj, cos2, sin2, jnp.tile(q_norm_w.astype(F32), GROUP_HEADS)[None],
                        jnp.tile(k_norm_w.astype(F32), LANES // HEAD_DIM)[None], ones_bd, C_TK)
    return _flash(qt, k2, vt, C_TQ)


def _mixer_d(proj, tab_d):
    L = proj.shape[0]
    kv = _pad_rows(proj[:, _DST["k_d"]:_DST["k_d"] + 2 * GROUP_WIDTH], D_REACH, D_REACH)
    return _band_attention(proj, _col("q_d", LANES), kv, kv, (0, GROUP_WIDTH // LANES), tab_d,
                           tq=D_TQ, r_lo=D_REACH, heads=2, group=1, interval=lambda n: (0, L))


def kernel(x, c, w_ada, b_ada, norm_w, w_in, w_out, attn_sink, na_rpb, q_norm_w, k_norm_w,
           t5_table, final_norm_w):
    B, L, D = x.shape
    assert B == 1 and L % 1024 == 0 and L // GRID_W >= NA_ROWS
    depth = w_ada.shape[0]
    x = x[0]

    mod = _ada_mod(jnp.broadcast_to(c, (8, D)), w_ada, b_ada)[:, 0:1, :]
    tab_a = _table_a(t5_table)
    tab_d = _table_d(t5_table)
    for i in range(depth):
        shift, scale, gate = jnp.split(mod[i], 3, axis=-1)
        w_i = w_in[i][:, _PERM].astype(BF16)
        proj = _inproj(x, norm_w[i][None], scale, shift, w_i)
        ys = (_mixer_a(proj, tab_a, attn_sink[i]),
              _mixer_b(proj, _na_table(na_rpb[i])),
              _mixer_c(proj, q_norm_w[i], k_norm_w[i]),
              _mixer_d(proj, tab_d))
        x = _outproj(x, ys, proj, w_out[i].astype(BF16), gate,
                     final_norm_w[None] if i == depth - 1 else None)
    return x[None]
```

```python
import functools
import math

import numpy as np
import jax
import jax.numpy as jnp
from jax import lax
from jax.experimental import pallas as pl
from jax.experimental.pallas import tpu as pltpu

HEAD_DIM = 64
GROUP_WIDTH = 512
GROUP_HEADS = 8
A_KV_HEADS = 2
A_RADIUS = 128
C_KV_HEADS = 2
ROPE_THETA = 10000.0
NA_ROWS = 8
NA_COLS = 16
D_CONFIGS = ((128, 1), (512, 4), (2048, 16))
GRID_W = 64
T5_BUCKETS = 32
T5_MAX_DIST = 1024
EPS = 1e-6
NEG = -1e30

V_ROWS = HEAD_DIM + 16
LANES = 128
VMEM_LIMIT = 56 * 1024 * 1024

F32 = jnp.float32
BF16 = jnp.bfloat16

_SRC = {}
_off = 0
for _name, _w in (("q_a", 512), ("k_a", 128), ("v_a", 128), ("z_a", 512),
                  ("q_b", 512), ("k_b", 512), ("v_b", 512), ("z_b", 512),
                  ("q_c", 512), ("k_c", 128), ("v_c", 128), ("z_c", 512),
                  ("q_d", 512), ("k_d", 512), ("v_d", 512), ("z_d", 512)):
    _SRC[_name] = (_off, _w)
    _off += _w
IN_WIDTH = _off
_ORDER = ("q_a", "q_b", "k_b", "v_b", "q_c", "q_d", "k_d", "v_d",
          "z_a", "z_b", "z_c", "z_d", "k_a", "v_a", "k_c", "v_c")
_DST = {}
_off = 0
for _name in _ORDER:
    _DST[_name] = _off
    _off += _SRC[_name][1]
_PERM = np.concatenate([np.arange(_SRC[n][0], _SRC[n][0] + _SRC[n][1]) for n in _ORDER])


def _col(name, width):
    assert _DST[name] % width == 0
    return _DST[name] // width


def _ada_kernel(c_ref, w_ref, b_ref, o_ref):
    c = c_ref[...]
    cond = c * jax.nn.sigmoid(c)
    o_ref[0] = jnp.dot(cond, w_ref[0], preferred_element_type=F32,
                       precision=lax.Precision.HIGHEST) + b_ref[0]


def _ada_mod(c8, w_ada, b_ada):
    depth, d, n3 = w_ada.shape
    tn = 1024
    return pl.pallas_call(
        _ada_kernel,
        grid=(depth, n3 // tn),
        in_specs=[pl.BlockSpec((8, d), lambda i, j: (0, 0)),
                  pl.BlockSpec((1, d, tn), lambda i, j: (i, 0, j)),
                  pl.BlockSpec((1, 1, tn), lambda i, j: (i, 0, j))],
        out_specs=pl.BlockSpec((1, 8, tn), lambda i, j: (i, 0, j)),
        out_shape=jax.ShapeDtypeStruct((depth, 8, n3), F32),
        compiler_params=pltpu.CompilerParams(
            dimension_semantics=("arbitrary", "arbitrary"), vmem_limit_bytes=VMEM_LIMIT),
        name="ada_mod",
    )(c8, w_ada, b_ada.reshape(depth, 1, n3))


def _inproj_kernel(x_ref, nw_ref, sc_ref, sh_ref, w_ref, o_ref, h_scr):
    @pl.when(pl.program_id(1) == 0)
    def _():
        x = x_ref[...]
        y = x * lax.rsqrt(jnp.mean(x * x, axis=-1, keepdims=True) + EPS)
        h = (y * nw_ref[...]) * (1.0 + sc_ref[...]) + sh_ref[...]
        h_scr[...] = h.astype(BF16)

    o_ref[...] = jnp.dot(h_scr[...], w_ref[...], preferred_element_type=F32).astype(o_ref.dtype)


def _inproj(x, nw, scale, shift, w_bf16):
    L, d = x.shape
    n = w_bf16.shape[1]
    tm = min(1024, L)
    tn = 512
    vec = pl.BlockSpec((1, d), lambda i, j: (0, 0))
    return pl.pallas_call(
        _inproj_kernel,
        grid=(L // tm, n // tn),
        in_specs=[pl.BlockSpec((tm, d), lambda i, j: (i, 0)), vec, vec, vec,
                  pl.BlockSpec((d, tn), lambda i, j: (0, j))],
        out_specs=pl.BlockSpec((tm, tn), lambda i, j: (i, j)),
        out_shape=jax.ShapeDtypeStruct((L, n), BF16),
        scratch_shapes=[pltpu.VMEM((tm, d), BF16)],
        compiler_params=pltpu.CompilerParams(
            dimension_semantics=("arbitrary", "arbitrary"), vmem_limit_bytes=VMEM_LIMIT),
        name="inproj",
    )(x, nw, scale, shift, w_bf16)


def _band_kernel(*refs, tq, w_keys, r_lo, heads, group, use_sink, interval):
    if use_sink:
        sink_ref, q_ref, k_ref, v_ref, tab_ref, o_ref = refs
    else:
        q_ref, k_ref, v_ref, tab_ref, o_ref = refs
    n = pl.program_id(1)
    start = pl.multiple_of(n * tq, tq)
    kwin = k_ref[pl.ds(start, w_keys), :]
    vwin = v_ref[pl.ds(start, w_keys), :]
    kpos = n * tq - r_lo + lax.broadcasted_iota(jnp.int32, (1, w_keys), 1)
    lo, hi = interval(n)
    kbias = jnp.where((kpos >= lo) & (kpos < hi), 0.0, NEG).astype(F32)
    for h in range(heads):
        kv = h // group
        q = q_ref[:, h * HEAD_DIM:(h + 1) * HEAD_DIM] * HEAD_DIM ** -0.5
        k = kwin[:, kv * HEAD_DIM:(kv + 1) * HEAD_DIM]
        v = vwin[:, kv * HEAD_DIM:(kv + 1) * HEAD_DIM]
        s = lax.dot_general(q, k, (((1,), (1,)), ((), ())), preferred_element_type=F32)
        s = s + tab_ref[h] + kbias
        m = jnp.max(s, axis=-1, keepdims=True)
        p = jnp.exp(s - m)
        l = jnp.sum(p, axis=-1, keepdims=True)
        o = jnp.dot(p.astype(BF16), v, preferred_element_type=F32)
        if use_sink:
            sk = sink_ref[h]
            m2 = jnp.maximum(m, sk)
            a = jnp.exp(m - m2)
            o = o * (a / (l * a + jnp.exp(sk - m2)))
        else:
            o = o / l
        o_ref[:, h * HEAD_DIM:(h + 1) * HEAD_DIM] = o.astype(o_ref.dtype)


def _band_attention(q_src, q_col, k_pad, v_pad, kv_cols, table, *, tq, r_lo, heads, group,
                    interval, sink=None):
    L = q_src.shape[0]
    lp = k_pad.shape[0]
    w_keys = table.shape[-1]
    n_groups = table.shape[0] // heads
    qw = heads * HEAD_DIM
    kern = functools.partial(_band_kernel, tq=tq, w_keys=w_keys, r_lo=r_lo, heads=heads,
                             group=group, use_sink=sink is not None, interval=interval)
    in_specs = [pl.BlockSpec((tq, qw), lambda g, n: (n, q_col + g)),
                pl.BlockSpec((lp, LANES), lambda g, n: (0, kv_cols[0] + g)),
                pl.BlockSpec((lp, LANES), lambda g, n: (0, kv_cols[1] + g)),
                pl.BlockSpec((heads, tq, w_keys), lambda g, n: (g, 0, 0))]
    args = [q_src, k_pad, v_pad, table]
    if sink is not None:
        in_specs = [pl.BlockSpec(memory_space=pltpu.SMEM)] + in_specs
        args = [sink] + args
    return pl.pallas_call(
        kern,
        grid=(n_groups, L // tq),
        in_specs=in_specs,
        out_specs=pl.BlockSpec((tq, qw), lambda g, n: (n, g)),
        out_shape=jax.ShapeDtypeStruct((L, n_groups * qw), BF16),
        compiler_params=pltpu.CompilerParams(
            dimension_semantics=("arbitrary", "arbitrary"), vmem_limit_bytes=VMEM_LIMIT),
        name="band_attention",
    )(*args)


def _t5_bucket(rel):
    half = T5_BUCKETS // 2
    exact = half // 2
    n = jnp.abs(rel)
    big = exact + (jnp.log(jnp.maximum(n, exact).astype(F32) / exact)
                   / math.log(T5_MAX_DIST / exact) * (half - exact)).astype(jnp.int32)
    big = jnp.minimum(big, half - 1)
    return jnp.where(rel > 0, half, 0) + jnp.where(n < exact, n, big)


def _t5_table(t5_table, head_lo, tq, r_lo, w_keys, mult):
    rel = np.arange(-r_lo - (tq - 1), w_keys - r_lo)
    m = mult(rel)
    b = t5_table.astype(F32)[:, head_lo:head_lo + GROUP_HEADS][_t5_bucket(jnp.asarray(rel, jnp.int32))]
    logm = np.log(np.maximum(m, 1)).astype(np.float32)
    vec = jnp.where(jnp.asarray(m > 0)[:, None], b + logm[:, None], NEG).T
    return jnp.stack([vec[:, tq - 1 - i:tq - 1 - i + w_keys] for i in range(tq)], axis=1)


def _window_mult(rel):
    return (np.abs(rel) <= A_RADIUS).astype(np.int32)


def _dilated_mult(rel):
    m = np.zeros(rel.shape, np.int32)
    for window, dil in D_CONFIGS:
        m += ((rel % dil == 0) & (np.abs(rel) <= window // 2)).astype(np.int32)
    return m


_NA_R_LO = (NA_ROWS - 1) * GRID_W
_NA_W = 2 * NA_ROWS * GRID_W


def _na_table(rpb):
    n_dr, n_dc = 2 * NA_ROWS - 1, 2 * NA_COLS - 1
    wq = np.arange(GRID_W)[:, None]
    c = np.arange(_NA_W)[None, :]
    j, kc = c // GRID_W, c % GRID_W
    col_start = np.clip(wq - NA_COLS // 2, 0, GRID_W - NA_COLS)
    valid = (kc >= col_start) & (kc < col_start + NA_COLS) & (j < n_dr)
    front = GRID_W - NA_COLS
    p = jnp.pad(rpb.astype(F32), ((0, 0), (0, 2 * NA_ROWS - n_dr), (front, 2 * GRID_W - 1 - n_dc - front)))
    b = jnp.stack([p[:, :, GRID_W - 1 - w:2 * GRID_W - 1 - w] for w in range(GRID_W)], axis=1)
    b = b.reshape(rpb.shape[0], GRID_W, _NA_W)
    return jnp.where(jnp.asarray(valid)[None], b, NEG)


def _swap_pairs(x):
    n = x.shape[-1]
    lane = lax.broadcasted_iota(jnp.int32, x.shape, x.ndim - 1)
    return jnp.where(lane % 2 == 0, pltpu.roll(x, n - 1, x.ndim - 1), pltpu.roll(x, 1, x.ndim - 1))


def _head_rms(x, ones_bd, w):
    ms = jnp.dot(x * x, ones_bd, preferred_element_type=F32,
                 precision=lax.Precision.HIGHEST) * (1.0 / HEAD_DIM)
    return x * lax.rsqrt(ms + EPS) * w


def _cprep_kernel(q_ref, k_ref, v_ref, cos_ref, sin_ref, qw_ref, kw_ref, bd_ref,
                  qt_ref, k2_ref, vt_ref):
    cos = cos_ref[...]
    sin = sin_ref[...]
    q = _head_rms(q_ref[...].astype(F32), bd_ref[...], qw_ref[...])
    reps = q.shape[1] // LANES
    q = q * jnp.concatenate([cos] * reps, axis=1) + _swap_pairs(q) * jnp.concatenate([sin] * reps, axis=1)
    qt_ref[...] = (q * (HEAD_DIM ** -0.5 * math.log2(math.e))).T.astype(qt_ref.dtype)
    k = _head_rms(k_ref[...].astype(F32), bd_ref[:LANES, :LANES], kw_ref[...])
    k = (k * cos + _swap_pairs(k) * sin).astype(k2_ref.dtype)
    vt = v_ref[...].astype(F32).T.astype(vt_ref.dtype)
    ones = jnp.ones((V_ROWS - HEAD_DIM, vt.shape[1]), vt_ref.dtype)
    for g in range(C_KV_HEADS):
        k2_ref[g] = k[:, g * HEAD_DIM:(g + 1) * HEAD_DIM]
        vt_ref[g, 0] = jnp.concatenate([vt[g * HEAD_DIM:(g + 1) * HEAD_DIM], ones], axis=0)


def _cprep(proj, cos2, sin2, qw, kw, ones_bd, t):
    L = proj.shape[0]
    vec = lambda width: pl.BlockSpec((1, width), lambda n: (0, 0))
    return pl.pallas_call(
        _cprep_kernel,
        grid=(L // t,),
        in_specs=[pl.BlockSpec((t, GROUP_WIDTH), lambda n: (n, _col("q_c", GROUP_WIDTH))),
                  pl.BlockSpec((t, LANES), lambda n: (n, _col("k_c", LANES))),
                  pl.BlockSpec((t, LANES), lambda n: (n, _col("v_c", LANES))),
                  pl.BlockSpec((t, LANES), lambda n: (n, 0)),
                  pl.BlockSpec((t, LANES), lambda n: (n, 0)),
                  vec(GROUP_WIDTH), vec(LANES),
                  pl.BlockSpec((GROUP_WIDTH, GROUP_WIDTH), lambda n: (0, 0))],
        out_specs=[pl.BlockSpec((GROUP_WIDTH, t), lambda n: (0, n)),
                   pl.BlockSpec((C_KV_HEADS, t, HEAD_DIM), lambda n: (0, n, 0)),
                   pl.BlockSpec((C_KV_HEADS, 1, V_ROWS, t), lambda n: (0, n, 0, 0))],
        out_shape=[jax.ShapeDtypeStruct((GROUP_WIDTH, L), BF16),
                   jax.ShapeDtypeStruct((C_KV_HEADS, L, HEAD_DIM), BF16),
                   jax.ShapeDtypeStruct((C_KV_HEADS, L // t, V_ROWS, t), BF16)],
        compiler_params=pltpu.CompilerParams(
            dimension_semantics=("arbitrary",), vmem_limit_bytes=VMEM_LIMIT),
        name="dense_prep",
    )(proj, proj, proj, cos2, sin2, qw, kw, ones_bd)


def _flash_kernel(qt_ref, k_ref, vt_ref, o_ref, m_scr, acc_scr, st_scr, *, tk, heads):
    n_kv = k_ref.shape[1] // tk
    m_scr[...] = jnp.full(m_scr.shape, NEG, F32)
    acc_scr[...] = jnp.zeros(acc_scr.shape, F32)

    def scores(j, h):
        k = k_ref[0, pl.ds(pl.multiple_of(j * tk, tk), tk), :]
        qt = qt_ref[h * HEAD_DIM:(h + 1) * HEAD_DIM, :]
        return jnp.dot(k, qt, preferred_element_type=F32)

    st_scr[...] = scores(0, 0)

    def body(j, carry):
        vt = vt_ref[0, j]
        st = st_scr[...]
        for h in range(heads):
            if h + 1 < heads:
                st_next = scores(j, h + 1)
            else:
                st_next = scores(jnp.minimum(j + 1, n_kv - 1), 0)
            m = m_scr[h]
            m_new = jnp.maximum(m, jnp.max(st, axis=0, keepdims=True))
            pt = jnp.exp2(st - m_new).astype(BF16)
            acc_scr[h] = jnp.exp2(m - m_new) * acc_scr[h] + jnp.dot(vt, pt, preferred_element_type=F32)
            m_scr[h] = m_new
            st = st_next
        st_scr[...] = st
        return carry

    lax.fori_loop(0, n_kv, body, 0)
    outs = [acc_scr[h, :HEAD_DIM] / acc_scr[h, HEAD_DIM:HEAD_DIM + 1] for h in range(heads)]
    o_ref[...] = jnp.concatenate(outs, axis=0).T.astype(o_ref.dtype)


def _flash(qt, k2, vt, tq):
    L = qt.shape[1]
    _, n_kv, v_rows, tk = vt.shape
    heads = GROUP_HEADS // C_KV_HEADS
    qrows = heads * HEAD_DIM
    return pl.pallas_call(
        functools.partial(_flash_kernel, tk=tk, heads=heads),
        grid=(C_KV_HEADS, L // tq),
        in_specs=[pl.BlockSpec((qrows, tq), lambda g, n: (g, n)),
                  pl.BlockSpec((1, L, HEAD_DIM), lambda g, n: (g, 0, 0)),
                  pl.BlockSpec((1, n_kv, v_rows, tk), lambda g, n: (g, 0, 0, 0))],
        out_specs=pl.BlockSpec((tq, qrows), lambda g, n: (n, g)),
        out_shape=jax.ShapeDtypeStruct((L, GROUP_WIDTH), BF16),
        scratch_shapes=[pltpu.VMEM((heads, 1, tq), F32), pltpu.VMEM((heads, v_rows, tq), F32),
                        pltpu.VMEM((tk, tq), F32)],
        compiler_params=pltpu.CompilerParams(
            dimension_semantics=("arbitrary", "arbitrary"), vmem_limit_bytes=VMEM_LIMIT),
        name="dense_flash",
    )(qt, k2, vt)


def _outproj_kernel(*refs, final):
    if final:
        x_ref, ya, yb, yc, yd, z_ref, w_ref, g_ref, fw_ref, o_ref = refs
    else:
        x_ref, ya, yb, yc, yd, z_ref, w_ref, g_ref, o_ref = refs
    acc = jnp.zeros(x_ref.shape, F32)
    for gi, y_ref in enumerate((ya, yb, yc, yd)):
        cols = slice(gi * GROUP_WIDTH, (gi + 1) * GROUP_WIDTH)
        z = z_ref[:, cols].astype(F32)
        u = (y_ref[...].astype(F32) * (z * jax.nn.sigmoid(z))).astype(BF16)
        acc = acc + jnp.dot(u, w_ref[cols, :], preferred_element_type=F32)
    xn = x_ref[...] + g_ref[...] * acc
    if final:
        xn = xn * lax.rsqrt(jnp.mean(xn * xn, axis=-1, keepdims=True) + EPS) * fw_ref[...]
    o_ref[...] = xn


def _outproj(x, ys, proj, w_bf16, gate, final_w):
    L, d = x.shape
    tm = min(512, L)
    final = final_w is not None
    row = lambda width: pl.BlockSpec((tm, width), lambda i: (i, 0))
    vec = pl.BlockSpec((1, d), lambda i: (0, 0))
    in_specs = [row(d), row(GROUP_WIDTH), row(GROUP_WIDTH), row(GROUP_WIDTH), row(GROUP_WIDTH),
                pl.BlockSpec((tm, d), lambda i: (i, _col("z_a", d))),
                pl.BlockSpec(w_bf16.shape, lambda i: (0, 0)), vec]
    args = [x, *ys, proj, w_bf16, gate]
    if final:
        in_specs.append(vec)
        args.append(final_w)
    return pl.pallas_call(
        functools.partial(_outproj_kernel, final=final),
        grid=(L // tm,),
        in_specs=in_specs,
        out_specs=row(d),
        out_shape=jax.ShapeDtypeStruct((L, d), F32),
        compiler_params=pltpu.CompilerParams(
            dimension_semantics=("arbitrary",), vmem_limit_bytes=VMEM_LIMIT),
        name="outproj",
    )(*args)


def _rope_tables(L):
    t = np.arange(L)
    axis_dim = HEAD_DIM // 2
    inv = jnp.asarray(ROPE_THETA, F32) ** (-jnp.arange(0, axis_dim, 2, dtype=F32) / axis_dim)
    row = jnp.asarray(t // GRID_W, F32)
    col = jnp.asarray(t % GRID_W, F32)
    ang = jnp.concatenate([row[:, None] * inv[None], col[:, None] * inv[None]], axis=-1)
    cos = jnp.repeat(jnp.cos(ang), 2, axis=-1)
    sin = jnp.repeat(jnp.sin(ang), 2, axis=-1) * jnp.asarray(np.tile([-1.0, 1.0], HEAD_DIM // 2), F32)
    return jnp.tile(cos, (1, LANES // HEAD_DIM)), jnp.tile(sin, (1, LANES // HEAD_DIM))


def _pad_rows(a, front, back):
    return jnp.pad(a, ((front, back), (0, 0)))


A_TQ = 256
D_TQ = 128
D_REACH = D_CONFIGS[-1][0] // 2
C_TQ = 512
C_TK = 512


def _table_a(t5_table):
    return _t5_table(t5_table, 0, A_TQ, A_RADIUS, A_TQ + 2 * A_RADIUS, _window_mult)


def _table_d(t5_table):
    return _t5_table(t5_table, GROUP_HEADS, D_TQ, D_REACH, D_TQ + 2 * D_REACH, _dilated_mult)


def _mixer_a(proj, tab_a, sink):
    L = proj.shape[0]
    kv = _pad_rows(proj[:, _DST["k_a"]:_DST["k_a"] + 2 * LANES], A_RADIUS, A_RADIUS)
    return _band_attention(proj, _col("q_a", GROUP_WIDTH), kv, kv, (0, 1), tab_a,
                           tq=A_TQ, r_lo=A_RADIUS, heads=GROUP_HEADS,
                           group=GROUP_HEADS // A_KV_HEADS, interval=lambda n: (0, L),
                           sink=sink.astype(F32))


def _mixer_b(proj, tab_b):
    rows = proj.shape[0] // GRID_W

    def interval(n):
        lo = jnp.clip(n - NA_ROWS // 2, 0, rows - NA_ROWS) * GRID_W
        return lo, lo + NA_ROWS * GRID_W

    kv = _pad_rows(proj[:, _DST["k_b"]:_DST["k_b"] + 2 * GROUP_WIDTH],
                   _NA_R_LO, _NA_W - GRID_W - _NA_R_LO)
    return _band_attention(proj, _col("q_b", LANES), kv, kv, (0, GROUP_WIDTH // LANES), tab_b,
                           tq=GRID_W, r_lo=_NA_R_LO, heads=2, group=1, interval=interval)


def _mixer_c(proj, q_norm_w, k_norm_w):
    L = proj.shape[0]
    cos2, sin2 = _rope_tables(L)
    ones_bd = jnp.asarray(np.kron(np.eye(GROUP_HEADS), np.ones((HEAD_DIM, HEAD_DIM))), F32)
    qt, k2, vt = _cprep(proj, cos2, sin2, jnp.tile(q_norm_w.astype(F32), GROUP_HEADS)[None],
                        jnp.tile(k_norm_w.astype(F32), LANES // HEAD_DIM)[None], ones_bd, C_TK)
    return _flash(qt, k2, vt, C_TQ)


def _mixer_d(proj, tab_d):
    L = proj.shape[0]
    kv = _pad_rows(proj[:, _DST["k_d"]:_DST["k_d"] + 2 * GROUP_WIDTH], D_REACH, D_REACH)
    return _band_attention(proj, _col("q_d", LANES), kv, kv, (0, GROUP_WIDTH // LANES), tab_d,
                           tq=D_TQ, r_lo=D_REACH, heads=2, group=1, interval=lambda n: (0, L))


def kernel(x, c, w_ada, b_ada, norm_w, w_in, w_out, attn_sink, na_rpb, q_norm_w, k_norm_w,
           t5_table, final_norm_w):
    B, L, D = x.shape
    assert B == 1 and L % 1024 == 0 and L // GRID_W >= NA_ROWS
    depth = w_ada.shape[0]
    x = x[0]

    mod = _ada_mod(jnp.broadcast_to(c, (8, D)), w_ada, b_ada)[:, 0:1, :]
    tab_a = _table_a(t5_table)
    tab_d = _table_d(t5_table)
    for i in range(depth):
        shift, scale, gate = jnp.split(mod[i], 3, axis=-1)
        w_i = w_in[i][:, _PERM].astype(BF16)
        proj = _inproj(x, norm_w[i][None], scale, shift, w_i)
        ys = (_mixer_a(proj, tab_a, attn_sink[i]),
              _mixer_b(proj, _na_table(na_rpb[i])),
              _mixer_c(proj, q_norm_w[i], k_norm_w[i]),
              _mixer_d(proj, tab_d))
        x = _outproj(x, ys, proj, w_out[i].astype(BF16), gate,
                     final_norm_w[None] if i == depth - 1 else None)
    return x[None]
```

```python
import functools
import math

import numpy as np
import jax
import jax.numpy as jnp
from jax import lax
from jax.experimental import pallas as pl
from jax.experimental.pallas import tpu as pltpu

HEAD_DIM = 64
GROUP_WIDTH = 512
GROUP_HEADS = 8
A_KV_HEADS = 2
A_RADIUS = 128
C_KV_HEADS = 2
ROPE_THETA = 10000.0
NA_ROWS = 8
NA_COLS = 16
D_CONFIGS = ((128, 1), (512, 4), (2048, 16))
GRID_W = 64
T5_BUCKETS = 32
T5_MAX_DIST = 1024
EPS = 1e-6
NEG = -1e30

V_ROWS = HEAD_DIM + 16
LANES = 128
VMEM_LIMIT = 56 * 1024 * 1024

F32 = jnp.float32
BF16 = jnp.bfloat16

_SRC = {}
_off = 0
for _name, _w in (("q_a", 512), ("k_a", 128), ("v_a", 128), ("z_a", 512),
                  ("q_b", 512), ("k_b", 512), ("v_b", 512), ("z_b", 512),
                  ("q_c", 512), ("k_c", 128), ("v_c", 128), ("z_c", 512),
                  ("q_d", 512), ("k_d", 512), ("v_d", 512), ("z_d", 512)):
    _SRC[_name] = (_off, _w)
    _off += _w
IN_WIDTH = _off
_ORDER = ("q_a", "q_b", "k_b", "v_b", "q_c", "q_d", "k_d", "v_d",
          "z_a", "z_b", "z_c", "z_d", "k_a", "v_a", "k_c", "v_c")
_DST = {}
_off = 0
for _name in _ORDER:
    _DST[_name] = _off
    _off += _SRC[_name][1]
_PERM = np.concatenate([np.arange(_SRC[n][0], _SRC[n][0] + _SRC[n][1]) for n in _ORDER])


def _col(name, width):
    assert _DST[name] % width == 0
    return _DST[name] // width


def _ada_kernel(c_ref, w_ref, b_ref, o_ref):
    c = c_ref[...]
    cond = c * jax.nn.sigmoid(c)
    o_ref[0] = jnp.dot(cond, w_ref[0], preferred_element_type=F32,
                       precision=lax.Precision.HIGHEST) + b_ref[0]


def _ada_mod(c8, w_ada, b_ada):
    depth, d, n3 = w_ada.shape
    tn = 1024
    return pl.pallas_call(
        _ada_kernel,
        grid=(depth, n3 // tn),
        in_specs=[pl.BlockSpec((8, d), lambda i, j: (0, 0)),
                  pl.BlockSpec((1, d, tn), lambda i, j: (i, 0, j)),
                  pl.BlockSpec((1, 1, tn), lambda i, j: (i, 0, j))],
        out_specs=pl.BlockSpec((1, 8, tn), lambda i, j: (i, 0, j)),
        out_shape=jax.ShapeDtypeStruct((depth, 8, n3), F32),
        compiler_params=pltpu.CompilerParams(
            dimension_semantics=("arbitrary", "arbitrary"), vmem_limit_bytes=VMEM_LIMIT),
        name="ada_mod",
    )(c8, w_ada, b_ada.reshape(depth, 1, n3))


def _inproj_kernel(x_ref, nw_ref, sc_ref, sh_ref, w_ref, o_ref, h_scr):
    @pl.when(pl.program_id(1) == 0)
    def _():
        x = x_ref[...]
        y = x * lax.rsqrt(jnp.mean(x * x, axis=-1, keepdims=True) + EPS)
        h = (y * nw_ref[...]) * (1.0 + sc_ref[...]) + sh_ref[...]
        h_scr[...] = h.astype(BF16)

    o_ref[...] = jnp.dot(h_scr[...], w_ref[...], preferred_element_type=F32).astype(o_ref.dtype)


def _inproj(x, nw, scale, shift, w_bf16):
    L, d = x.shape
    n = w_bf16.shape[1]
    tm = min(1024, L)
    tn = 512
    vec = pl.BlockSpec((1, d), lambda i, j: (0, 0))
    return pl.pallas_call(
        _inproj_kernel,
        grid=(L // tm, n // tn),
        in_specs=[pl.BlockSpec((tm, d), lambda i, j: (i, 0)), vec, vec, vec,
                  pl.BlockSpec((d, tn), lambda i, j: (0, j))],
        out_specs=pl.BlockSpec((tm, tn), lambda i, j: (i, j)),
        out_shape=jax.ShapeDtypeStruct((L, n), BF16),
        scratch_shapes=[pltpu.VMEM((tm, d), BF16)],
        compiler_params=pltpu.CompilerParams(
            dimension_semantics=("arbitrary", "arbitrary"), vmem_limit_bytes=VMEM_LIMIT),
        name="inproj",
    )(x, nw, scale, shift, w_bf16)


LOG2E = math.log2(math.e)
Q_SCALE = HEAD_DIM ** -0.5 * LOG2E


def _qt_prep_kernel(*refs):
    n = len(refs) // 2
    for q_ref, qt_ref in zip(refs[:n], refs[n:]):
        qt_ref[...] = (q_ref[...].astype(F32) * Q_SCALE).T.astype(qt_ref.dtype)


def _qt_prep(proj, names, t=512):
    L = proj.shape[0]
    return pl.pallas_call(
        _qt_prep_kernel,
        grid=(L // t,),
        in_specs=[pl.BlockSpec((t, GROUP_WIDTH), functools.partial(lambda c, n: (n, c), _col(name, GROUP_WIDTH)))
                  for name in names],
        out_specs=[pl.BlockSpec((GROUP_WIDTH, t), lambda n: (0, n)) for _ in names],
        out_shape=[jax.ShapeDtypeStruct((GROUP_WIDTH, L), BF16) for _ in names],
        compiler_params=pltpu.CompilerParams(
            dimension_semantics=("arbitrary",), vmem_limit_bytes=VMEM_LIMIT),
        name="qt_prep",
    )(*([proj] * len(names)))


def _kv_prep_kernel(k_ref, v_ref, kp_ref, vt_ref, *, pad_blocks, tok_blocks):
    b = pl.program_id(0)
    is_token = (b >= pad_blocks) & (b < pad_blocks + tok_blocks)

    @pl.when(is_token)
    def _():
        kp_ref[...] = k_ref[...]
        vt = v_ref[...].astype(F32).T.astype(vt_ref.dtype)
        ones = jnp.ones((V_ROWS - HEAD_DIM, vt.shape[1]), vt_ref.dtype)
        for g in range(vt.shape[0] // HEAD_DIM):
            vt_ref[g * V_ROWS:(g + 1) * V_ROWS, :] = jnp.concatenate(
                [vt[g * HEAD_DIM:(g + 1) * HEAD_DIM], ones], axis=0)

    @pl.when(jnp.logical_not(is_token))
    def _():
        kp_ref[...] = jnp.zeros(kp_ref.shape, kp_ref.dtype)
        vt_ref[...] = jnp.zeros(vt_ref.shape, vt_ref.dtype)


def _kv_prep(proj, k_name, v_name, width, pad, t):
    L = proj.shape[0]
    assert pad % t == 0 and L % t == 0 and width % LANES == 0
    pad_blocks, tok_blocks = pad // t, L // t
    lp = L + 2 * pad
    kc, vc = _col(k_name, width), _col(v_name, width)
    tok = lambda b: jnp.clip(b - pad_blocks, 0, tok_blocks - 1)
    return pl.pallas_call(
        functools.partial(_kv_prep_kernel, pad_blocks=pad_blocks, tok_blocks=tok_blocks),
        grid=(lp // t,),
        in_specs=[pl.BlockSpec((t, width), lambda b: (tok(b), kc)),
                  pl.BlockSpec((t, width), lambda b: (tok(b), vc))],
        out_specs=[pl.BlockSpec((t, width), lambda b: (b, 0)),
                   pl.BlockSpec((width // HEAD_DIM * V_ROWS, t), lambda b: (0, b))],
        out_shape=[jax.ShapeDtypeStruct((lp, width), BF16),
                   jax.ShapeDtypeStruct((width // HEAD_DIM * V_ROWS, lp), BF16)],
        compiler_params=pltpu.CompilerParams(
            dimension_semantics=("arbitrary",), vmem_limit_bytes=VMEM_LIMIT),
        name="kv_prep",
    )(proj, proj)


def _band_kernel(*refs, tq, w_keys, heads, group, use_sink, window, variant):
    if use_sink:
        sink_ref, qt_ref, k_ref, vt_ref, tab_ref, o_ref = refs
    else:
        qt_ref, k_ref, vt_ref, tab_ref, o_ref = refs
    n = pl.program_id(1)
    start, pos = window(n)
    start = pl.multiple_of(start, LANES)
    kwin = k_ref[pl.ds(start, w_keys), :]
    vwin = vt_ref[:, pl.ds(start, w_keys)]
    var = variant(n)

    def attend(kbias):
        def scores(h):
            kv = h // group
            st = jnp.dot(kwin[:, kv * HEAD_DIM:(kv + 1) * HEAD_DIM],
                         qt_ref[h * HEAD_DIM:(h + 1) * HEAD_DIM, :],
                         preferred_element_type=F32) + tab_ref[var, h]
            return st if kbias is None else st + kbias

        outs = []
        st = scores(0)
        for h in range(heads):
            st_next = scores(h + 1) if h + 1 < heads else None
            kv = h // group
            m = jnp.max(st, axis=0, keepdims=True)
            pt = jnp.exp2(st - m).astype(BF16)
            acc = jnp.dot(vwin[kv * V_ROWS:(kv + 1) * V_ROWS], pt, preferred_element_type=F32)
            o, l = acc[:HEAD_DIM], acc[HEAD_DIM:HEAD_DIM + 1]
            if use_sink:
                sk = sink_ref[h] * LOG2E
                m2 = jnp.maximum(m, sk)
                a = jnp.exp2(m - m2)
                o = o * (a / (l * a + jnp.exp2(sk - m2)))
            else:
                o = o / l
            outs.append(o)
            st = st_next
        o_ref[...] = jnp.concatenate(outs, axis=0).T.astype(o_ref.dtype)

    if pos is None:
        attend(None)
    else:
        first, lo, hi = pos
        inside = (first >= lo) & (first + w_keys <= hi)

        @pl.when(inside)
        def _():
            attend(None)

        @pl.when(jnp.logical_not(inside))
        def _():
            kpos = first + lax.broadcasted_iota(jnp.int32, (w_keys, 1), 0)
            attend(jnp.where((kpos >= lo) & (kpos < hi), 0.0, NEG).astype(F32))


def _band_attention(qt, k_pad, vt_pad, table, *, tq, heads, group, window, variant=lambda n: 0,
                    sink=None):
    L = qt.shape[1]
    lp = k_pad.shape[0]
    n_var, _, w_keys, _ = table.shape
    qw = heads * HEAD_DIM
    n_groups = qt.shape[0] // qw
    kv_rows = vt_pad.shape[0] // n_groups
    kern = functools.partial(_band_kernel, tq=tq, w_keys=w_keys, heads=heads, group=group,
                             use_sink=sink is not None, window=window, variant=variant)
    in_specs = [pl.BlockSpec((qw, tq), lambda g, n: (g, n)),
                pl.BlockSpec((lp, LANES), lambda g, n: (0, g)),
                pl.BlockSpec((kv_rows, lp), lambda g, n: (g, 0)),
                pl.BlockSpec((n_var, heads, w_keys, tq), lambda g, n: (0, g, 0, 0))]
    args = [qt, k_pad, vt_pad, table]
    if sink is not None:
        in_specs = [pl.BlockSpec(memory_space=pltpu.SMEM)] + in_specs
        args = [sink] + args
    return pl.pallas_call(
        kern,
        grid=(n_groups, L // tq),
        in_specs=in_specs,
        out_specs=pl.BlockSpec((tq, qw), lambda g, n: (n, g)),
        out_shape=jax.ShapeDtypeStruct((L, n_groups * qw), BF16),
        compiler_params=pltpu.CompilerParams(
            dimension_semantics=("arbitrary", "arbitrary"), vmem_limit_bytes=VMEM_LIMIT),
        name="band_attention",
    )(*args)


def _t5_bucket(rel):
    half = T5_BUCKETS // 2
    exact = half // 2
    n = jnp.abs(rel)
    big = exact + (jnp.log(jnp.maximum(n, exact).astype(F32) / exact)
                   / math.log(T5_MAX_DIST / exact) * (half - exact)).astype(jnp.int32)
    big = jnp.minimum(big, half - 1)
    return jnp.where(rel > 0, half, 0) + jnp.where(n < exact, n, big)


def _t5_table(t5_table, head_lo, tq, r_lo, w_keys, mult):
    rel = np.arange(-r_lo - (tq - 1), w_keys - r_lo)
    m = mult(rel)
    b = t5_table.astype(F32)[:, head_lo:head_lo + GROUP_HEADS][_t5_bucket(jnp.asarray(rel, jnp.int32))]
    logm = np.log(np.maximum(m, 1)).astype(np.float32)
    vec = jnp.where(jnp.asarray(m > 0)[:, None], (b + logm[:, None]) * LOG2E, NEG).T
    return jnp.stack([vec[:, tq - 1 - i:tq - 1 - i + w_keys] for i in range(tq)], axis=2)[None]


def _window_mult(rel):
    return (np.abs(rel) <= A_RADIUS).astype(np.int32)


def _dilated_mult(rel):
    m = np.zeros(rel.shape, np.int32)
    for window, dil in D_CONFIGS:
        m += ((rel % dil == 0) & (np.abs(rel) <= window // 2)).astype(np.int32)
    return m


NA_TILE_ROWS = 4
NA_WIN_ROWS = NA_TILE_ROWS + NA_ROWS
NA_TQ = NA_TILE_ROWS * GRID_W
NA_W = NA_WIN_ROWS * GRID_W


def _na_table(rpb):
    heads, n_dr, n_dc = rpb.shape
    half = NA_ROWS // 2
    front = GRID_W - NA_COLS
    p = jnp.pad(rpb.astype(F32) * LOG2E, ((0, 0), (0, 0), (front, 2 * GRID_W - 1 - n_dc - front)))
    blocks = jnp.stack([p[:, :, GRID_W - 1 - w:2 * GRID_W - 1 - w] for w in range(GRID_W)], axis=-1)
    kc = np.arange(GRID_W)[:, None]
    wq = np.arange(GRID_W)[None, :]
    col_start = np.clip(wq - NA_COLS // 2, 0, GRID_W - NA_COLS)
    col_ok = (kc >= col_start) & (kc < col_start + NA_COLS)
    blocks = jnp.where(jnp.asarray(col_ok)[None, None], blocks, NEG)
    blocks = jnp.concatenate([blocks, jnp.full((heads, 1, GRID_W, GRID_W), NEG, F32)], axis=1)
    j = np.arange(NA_WIN_ROWS)[:, None]
    i = np.arange(NA_TILE_ROWS)[None, :]
    tabs = []
    for dr, row_ok in ((j - i, (j < NA_ROWS) & (i >= 0)),
                       (j - half - i, (j >= i) & (j < i + NA_ROWS)),
                       (j - NA_ROWS - i, (j >= NA_TILE_ROWS) & (i >= 0))):
        idx = np.where(row_ok, dr + NA_ROWS - 1, n_dr)
        assert idx.min() >= 0 and idx.max() <= n_dr
        t = blocks[:, idx]
        tabs.append(jnp.transpose(t, (0, 1, 3, 2, 4)).reshape(heads, NA_W, NA_TQ))
    return jnp.stack(tabs, axis=0)


def _swap_pairs(x):
    n = x.shape[-1]
    lane = lax.broadcasted_iota(jnp.int32, x.shape, x.ndim - 1)
    return jnp.where(lane % 2 == 0, pltpu.roll(x, n - 1, x.ndim - 1), pltpu.roll(x, 1, x.ndim - 1))


def _head_rms(x, ones_bd, w):
    ms = jnp.dot(x * x, ones_bd, preferred_element_type=F32,
                 precision=lax.Precision.HIGHEST) * (1.0 / HEAD_DIM)
    return x * lax.rsqrt(ms + EPS) * w


def _cprep_kernel(q_ref, k_ref, v_ref, cos_ref, sin_ref, qw_ref, kw_ref, bd_ref,
                  qt_ref, k2_ref, vt_ref):
    cos = cos_ref[...]
    sin = sin_ref[...]
    q = _head_rms(q_ref[...].astype(F32), bd_ref[...], qw_ref[...])
    reps = q.shape[1] // LANES
    q = q * jnp.concatenate([cos] * reps, axis=1) + _swap_pairs(q) * jnp.concatenate([sin] * reps, axis=1)
    qt_ref[...] = (q * Q_SCALE).T.astype(qt_ref.dtype)
    k = _head_rms(k_ref[...].astype(F32), bd_ref[:LANES, :LANES], kw_ref[...])
    k = (k * cos + _swap_pairs(k) * sin).astype(k2_ref.dtype)
    vt = v_ref[...].astype(F32).T.astype(vt_ref.dtype)
    ones = jnp.ones((V_ROWS - HEAD_DIM, vt.shape[1]), vt_ref.dtype)
    for g in range(C_KV_HEADS):
        k2_ref[g] = k[:, g * HEAD_DIM:(g + 1) * HEAD_DIM]
        vt_ref[g, 0] = jnp.concatenate([vt[g * HEAD_DIM:(g + 1) * HEAD_DIM], ones], axis=0)


def _cprep(proj, cos2, sin2, qw, kw, ones_bd, t):
    L = proj.shape[0]
    vec = lambda width: pl.BlockSpec((1, width), lambda n: (0, 0))
    return pl.pallas_call(
        _cprep_kernel,
        grid=(L // t,),
        in_specs=[pl.BlockSpec((t, GROUP_WIDTH), lambda n: (n, _col("q_c", GROUP_WIDTH))),
                  pl.BlockSpec((t, LANES), lambda n: (n, _col("k_c", LANES))),
                  pl.BlockSpec((t, LANES), lambda n: (n, _col("v_c", LANES))),
                  pl.BlockSpec((t, LANES), lambda n: (n, 0)),
                  pl.BlockSpec((t, LANES), lambda n: (n, 0)),
                  vec(GROUP_WIDTH), vec(LANES),
                  pl.BlockSpec((GROUP_WIDTH, GROUP_WIDTH), lambda n: (0, 0))],
        out_specs=[pl.BlockSpec((GROUP_WIDTH, t), lambda n: (0, n)),
                   pl.BlockSpec((C_KV_HEADS, t, HEAD_DIM), lambda n: (0, n, 0)),
                   pl.BlockSpec((C_KV_HEADS, 1, V_ROWS, t), lambda n: (0, n, 0, 0))],
        out_shape=[jax.ShapeDtypeStruct((GROUP_WIDTH, L), BF16),
                   jax.ShapeDtypeStruct((C_KV_HEADS, L, HEAD_DIM), BF16),
                   jax.ShapeDtypeStruct((C_KV_HEADS, L // t, V_ROWS, t), BF16)],
        compiler_params=pltpu.CompilerParams(
            dimension_semantics=("arbitrary",), vmem_limit_bytes=VMEM_LIMIT),
        name="dense_prep",
    )(proj, proj, proj, cos2, sin2, qw, kw, ones_bd)


def _flash_kernel(qt_ref, k_ref, vt_ref, o_ref, m_scr, acc_scr, st_scr, *, tk, heads):
    n_kv = k_ref.shape[1] // tk
    m_scr[...] = jnp.full(m_scr.shape, NEG, F32)
    acc_scr[...] = jnp.zeros(acc_scr.shape, F32)

    def scores(j, h):
        k = k_ref[0, pl.ds(pl.multiple_of(j * tk, tk), tk), :]
        qt = qt_ref[h * HEAD_DIM:(h + 1) * HEAD_DIM, :]
        return jnp.dot(k, qt, preferred_element_type=F32)

    st_scr[...] = scores(0, 0)

    def body(j, carry):
        vt = vt_ref[0, j]
        st = st_scr[...]
        for h in range(heads):
            if h + 1 < heads:
                st_next = scores(j, h + 1)
            else:
                st_next = scores(jnp.minimum(j + 1, n_kv - 1), 0)
            m = m_scr[h]
            m_new = jnp.maximum(m, jnp.max(st, axis=0, keepdims=True))
            pt = jnp.exp2(st - m_new).astype(BF16)
            acc_scr[h] = jnp.exp2(m - m_new) * acc_scr[h] + jnp.dot(vt, pt, preferred_element_type=F32)
            m_scr[h] = m_new
            st = st_next
        st_scr[...] = st
        return carry

    lax.fori_loop(0, n_kv, body, 0)
    outs = [acc_scr[h, :HEAD_DIM] / acc_scr[h, HEAD_DIM:HEAD_DIM + 1] for h in range(heads)]
    o_ref[...] = jnp.concatenate(outs, axis=0).T.astype(o_ref.dtype)


def _flash(qt, k2, vt, tq):
    L = qt.shape[1]
    _, n_kv, v_rows, tk = vt.shape
    heads = GROUP_HEADS // C_KV_HEADS
    qrows = heads * HEAD_DIM
    return pl.pallas_call(
        functools.partial(_flash_kernel, tk=tk, heads=heads),
        grid=(C_KV_HEADS, L // tq),
        in_specs=[pl.BlockSpec((qrows, tq), lambda g, n: (g, n)),
                  pl.BlockSpec((1, L, HEAD_DIM), lambda g, n: (g, 0, 0)),
                  pl.BlockSpec((1, n_kv, v_rows, tk), lambda g, n: (g, 0, 0, 0))],
        out_specs=pl.BlockSpec((tq, qrows), lambda g, n: (n, g)),
        out_shape=jax.ShapeDtypeStruct((L, GROUP_WIDTH), BF16),
        scratch_shapes=[pltpu.VMEM((heads, 1, tq), F32), pltpu.VMEM((heads, v_rows, tq), F32),
                        pltpu.VMEM((tk, tq), F32)],
        compiler_params=pltpu.CompilerParams(
            dimension_semantics=("arbitrary", "arbitrary"), vmem_limit_bytes=VMEM_LIMIT),
        name="dense_flash",
    )(qt, k2, vt)


def _outproj_kernel(*refs, final):
    if final:
        x_ref, ya, yb, yc, yd, z_ref, w_ref, g_ref, fw_ref, o_ref = refs
    else:
        x_ref, ya, yb, yc, yd, z_ref, w_ref, g_ref, o_ref = refs
    acc = jnp.zeros(x_ref.shape, F32)
    for gi, y_ref in enumerate((ya, yb, yc, yd)):
        cols = slice(gi * GROUP_WIDTH, (gi + 1) * GROUP_WIDTH)
        z = z_ref[:, cols].astype(F32)
        u = (y_ref[...].astype(F32) * (z * jax.nn.sigmoid(z))).astype(BF16)
        acc = acc + jnp.dot(u, w_ref[cols, :], preferred_element_type=F32)
    xn = x_ref[...] + g_ref[...] * acc
    if final:
        xn = xn * lax.rsqrt(jnp.mean(xn * xn, axis=-1, keepdims=True) + EPS) * fw_ref[...]
    o_ref[...] = xn


def _outproj(x, ys, proj, w_bf16, gate, final_w):
    L, d = x.shape
    tm = min(512, L)
    final = final_w is not None
    row = lambda width: pl.BlockSpec((tm, width), lambda i: (i, 0))
    vec = pl.BlockSpec((1, d), lambda i: (0, 0))
    in_specs = [row(d), row(GROUP_WIDTH), row(GROUP_WIDTH), row(GROUP_WIDTH), row(GROUP_WIDTH),
                pl.BlockSpec((tm, d), lambda i: (i, _col("z_a", d))),
                pl.BlockSpec(w_bf16.shape, lambda i: (0, 0)), vec]
    args = [x, *ys, proj, w_bf16, gate]
    if final:
        in_specs.append(vec)
        args.append(final_w)
    return pl.pallas_call(
        functools.partial(_outproj_kernel, final=final),
        grid=(L // tm,),
        in_specs=in_specs,
        out_specs=row(d),
        out_shape=jax.ShapeDtypeStruct((L, d), F32),
        compiler_params=pltpu.CompilerParams(
            dimension_semantics=("arbitrary",), vmem_limit_bytes=VMEM_LIMIT),
        name="outproj",
    )(*args)


def _rope_tables(L):
    t = np.arange(L)
    axis_dim = HEAD_DIM // 2
    inv = jnp.asarray(ROPE_THETA, F32) ** (-jnp.arange(0, axis_dim, 2, dtype=F32) / axis_dim)
    row = jnp.asarray(t // GRID_W, F32)
    col = jnp.asarray(t % GRID_W, F32)
    ang = jnp.concatenate([row[:, None] * inv[None], col[:, None] * inv[None]], axis=-1)
    cos = jnp.repeat(jnp.cos(ang), 2, axis=-1)
    sin = jnp.repeat(jnp.sin(ang), 2, axis=-1) * jnp.asarray(np.tile([-1.0, 1.0], HEAD_DIM // 2), F32)
    return jnp.tile(cos, (1, LANES // HEAD_DIM)), jnp.tile(sin, (1, LANES // HEAD_DIM))


A_TQ = 256
D_TQ = 256
D_REACH = D_CONFIGS[-1][0] // 2
C_TQ = 512
C_TK = 512


def _table_a(t5_table):
    return _t5_table(t5_table, 0, A_TQ, A_RADIUS, A_TQ + 2 * A_RADIUS, _window_mult)


def _table_d(t5_table):
    return _t5_table(t5_table, GROUP_HEADS, D_TQ, D_REACH, D_TQ + 2 * D_REACH, _dilated_mult)


def _seq_window(L, tq, pad):
    return lambda n: (n * tq, (n * tq - pad, 0, L))


def _mixer_a(proj, qt, tab_a, sink):
    L = proj.shape[0]
    k_pad, vt_pad = _kv_prep(proj, "k_a", "v_a", LANES, A_RADIUS, A_RADIUS)
    return _band_attention(qt, k_pad, vt_pad, tab_a, tq=A_TQ, heads=GROUP_HEADS,
                           group=GROUP_HEADS // A_KV_HEADS, window=_seq_window(L, A_TQ, A_RADIUS),
                           sink=sink.astype(F32))


def _mixer_b(proj, qt, tab_b):
    L = proj.shape[0]
    rows = L // GRID_W
    n_tiles = rows // NA_TILE_ROWS
    k, vt = _kv_prep(proj, "k_b", "v_b", GROUP_WIDTH, 0, 512)
    window = lambda n: (jnp.clip(n * NA_TILE_ROWS - NA_ROWS // 2, 0, rows - NA_WIN_ROWS) * GRID_W, None)
    variant = lambda n: jnp.where(n == 0, 0, jnp.where(n == n_tiles - 1, 2, 1))
    return _band_attention(qt, k, vt, tab_b, tq=NA_TQ, heads=2, group=1, window=window, variant=variant)


def _mixer_c(proj, q_norm_w, k_norm_w):
    L = proj.shape[0]
    cos2, sin2 = _rope_tables(L)
    ones_bd = jnp.asarray(np.kron(np.eye(GROUP_HEADS), np.ones((HEAD_DIM, HEAD_DIM))), F32)
    qt, k2, vt = _cprep(proj, cos2, sin2, jnp.tile(q_norm_w.astype(F32), GROUP_HEADS)[None],
                        jnp.tile(k_norm_w.astype(F32), LANES // HEAD_DIM)[None], ones_bd, C_TK)
    return _flash(qt, k2, vt, C_TQ)


def _mixer_d(proj, qt, tab_d):
    L = proj.shape[0]
    k_pad, vt_pad = _kv_prep(proj, "k_d", "v_d", GROUP_WIDTH, D_REACH, 512)
    return _band_attention(qt, k_pad, vt_pad, tab_d, tq=D_TQ, heads=2, group=1,
                           window=_seq_window(L, D_TQ, D_REACH))


def kernel(x, c, w_ada, b_ada, norm_w, w_in, w_out, attn_sink, na_rpb, q_norm_w, k_norm_w,
           t5_table, final_norm_w):
    B, L, D = x.shape
    assert B == 1 and L % 1024 == 0 and L // GRID_W >= NA_WIN_ROWS
    depth = w_ada.shape[0]
    x = x[0]

    mod = _ada_mod(jnp.broadcast_to(c, (8, D)), w_ada, b_ada)[:, 0:1, :]
    tab_a = _table_a(t5_table)
    tab_d = _table_d(t5_table)
    for i in range(depth):
        shift, scale, gate = jnp.split(mod[i], 3, axis=-1)
        w_i = w_in[i][:, _PERM].astype(BF16)
        proj = _inproj(x, norm_w[i][None], scale, shift, w_i)
        qt_a, qt_b, qt_d = _qt_prep(proj, ("q_a", "q_b", "q_d"))
        ys = (_mixer_a(proj, qt_a, tab_a, attn_sink[i]),
              _mixer_b(proj, qt_b, _na_table(na_rpb[i])),
              _mixer_c(proj, q_norm_w[i], k_norm_w[i]),
              _mixer_d(proj, qt_d, tab_d))
        x = _outproj(x, ys, proj, w_out[i].astype(BF16), gate,
                     final_norm_w[None] if i == depth - 1 else None)
    return x[None]
```

```python
import functools
import math

import numpy as np
import jax
import jax.numpy as jnp
from jax import lax
from jax.experimental import pallas as pl
from jax.experimental.pallas import tpu as pltpu

HEAD_DIM = 64
GROUP_WIDTH = 512
GROUP_HEADS = 8
A_KV_HEADS = 2
A_RADIUS = 128
C_KV_HEADS = 2
ROPE_THETA = 10000.0
NA_ROWS = 8
NA_COLS = 16
D_CONFIGS = ((128, 1), (512, 4), (2048, 16))
GRID_W = 64
T5_BUCKETS = 32
T5_MAX_DIST = 1024
EPS = 1e-6
NEG = -1e30

V_ROWS = HEAD_DIM + 16
LANES = 128
VMEM_LIMIT = 56 * 1024 * 1024

F32 = jnp.float32
BF16 = jnp.bfloat16

_SRC = {}
_off = 0
for _name, _w in (("q_a", 512), ("k_a", 128), ("v_a", 128), ("z_a", 512),
                  ("q_b", 512), ("k_b", 512), ("v_b", 512), ("z_b", 512),
                  ("q_c", 512), ("k_c", 128), ("v_c", 128), ("z_c", 512),
                  ("q_d", 512), ("k_d", 512), ("v_d", 512), ("z_d", 512)):
    _SRC[_name] = (_off, _w)
    _off += _w
IN_WIDTH = _off
_ORDER = ("q_a", "q_b", "k_b", "v_b", "q_c", "q_d", "k_d", "v_d",
          "z_a", "z_b", "z_c", "z_d", "k_a", "v_a", "k_c", "v_c")
_DST = {}
_off = 0
for _name in _ORDER:
    _DST[_name] = _off
    _off += _SRC[_name][1]
_PERM = np.concatenate([np.arange(_SRC[n][0], _SRC[n][0] + _SRC[n][1]) for n in _ORDER])


def _col(name, width):
    assert _DST[name] % width == 0
    return _DST[name] // width


def _ada_kernel(c_ref, w_ref, b_ref, o_ref):
    c = c_ref[...]
    cond = c * jax.nn.sigmoid(c)
    o_ref[0] = jnp.dot(cond, w_ref[0], preferred_element_type=F32,
                       precision=lax.Precision.HIGHEST) + b_ref[0]


def _ada_mod(c8, w_ada, b_ada):
    depth, d, n3 = w_ada.shape
    tn = 1024
    return pl.pallas_call(
        _ada_kernel,
        grid=(depth, n3 // tn),
        in_specs=[pl.BlockSpec((8, d), lambda i, j: (0, 0)),
                  pl.BlockSpec((1, d, tn), lambda i, j: (i, 0, j)),
                  pl.BlockSpec((1, 1, tn), lambda i, j: (i, 0, j))],
        out_specs=pl.BlockSpec((1, 8, tn), lambda i, j: (i, 0, j)),
        out_shape=jax.ShapeDtypeStruct((depth, 8, n3), F32),
        compiler_params=pltpu.CompilerParams(
            dimension_semantics=("arbitrary", "arbitrary"), vmem_limit_bytes=VMEM_LIMIT),
        name="ada_mod",
    )(c8, w_ada, b_ada.reshape(depth, 1, n3))


def _inproj_kernel(x_ref, nw_ref, sc_ref, sh_ref, w_ref, o_ref, h_scr):
    @pl.when(pl.program_id(1) == 0)
    def _():
        x = x_ref[...]
        y = x * lax.rsqrt(jnp.mean(x * x, axis=-1, keepdims=True) + EPS)
        h = (y * nw_ref[...]) * (1.0 + sc_ref[...]) + sh_ref[...]
        h_scr[...] = h.astype(BF16)

    o_ref[...] = jnp.dot(h_scr[...], w_ref[...], preferred_element_type=F32).astype(o_ref.dtype)


def _inproj(x, nw, scale, shift, w_bf16):
    L, d = x.shape
    n = w_bf16.shape[1]
    tm = min(1024, L)
    tn = 512
    vec = pl.BlockSpec((1, d), lambda i, j: (0, 0))
    return pl.pallas_call(
        _inproj_kernel,
        grid=(L // tm, n // tn),
        in_specs=[pl.BlockSpec((tm, d), lambda i, j: (i, 0)), vec, vec, vec,
                  pl.BlockSpec((d, tn), lambda i, j: (0, j))],
        out_specs=pl.BlockSpec((tm, tn), lambda i, j: (i, j)),
        out_shape=jax.ShapeDtypeStruct((L, n), BF16),
        scratch_shapes=[pltpu.VMEM((tm, d), BF16)],
        compiler_params=pltpu.CompilerParams(
            dimension_semantics=("arbitrary", "arbitrary"), vmem_limit_bytes=VMEM_LIMIT),
        name="inproj",
    )(x, nw, scale, shift, w_bf16)


LOG2E = math.log2(math.e)
Q_SCALE = HEAD_DIM ** -0.5 * LOG2E


def _qt_prep_kernel(*refs):
    n = len(refs) // 2
    for q_ref, qt_ref in zip(refs[:n], refs[n:]):
        qt_ref[...] = (q_ref[...].astype(F32) * Q_SCALE).T.astype(qt_ref.dtype)


def _qt_prep(proj, names, t=512):
    L = proj.shape[0]
    return pl.pallas_call(
        _qt_prep_kernel,
        grid=(L // t,),
        in_specs=[pl.BlockSpec((t, GROUP_WIDTH), functools.partial(lambda c, n: (n, c), _col(name, GROUP_WIDTH)))
                  for name in names],
        out_specs=[pl.BlockSpec((GROUP_WIDTH, t), lambda n: (0, n)) for _ in names],
        out_shape=[jax.ShapeDtypeStruct((GROUP_WIDTH, L), BF16) for _ in names],
        compiler_params=pltpu.CompilerParams(
            dimension_semantics=("arbitrary",), vmem_limit_bytes=VMEM_LIMIT),
        name="qt_prep",
    )(*([proj] * len(names)))


def _kv_prep_kernel(k_ref, v_ref, kp_ref, vt_ref, *, pad_blocks, tok_blocks):
    b = pl.program_id(0)
    is_token = (b >= pad_blocks) & (b < pad_blocks + tok_blocks)

    @pl.when(is_token)
    def _():
        kp_ref[...] = k_ref[...]
        vt = v_ref[...].astype(F32).T.astype(vt_ref.dtype)
        ones = jnp.ones((V_ROWS - HEAD_DIM, vt.shape[1]), vt_ref.dtype)
        for g in range(vt.shape[0] // HEAD_DIM):
            vt_ref[g * V_ROWS:(g + 1) * V_ROWS, :] = jnp.concatenate(
                [vt[g * HEAD_DIM:(g + 1) * HEAD_DIM], ones], axis=0)

    @pl.when(jnp.logical_not(is_token))
    def _():
        kp_ref[...] = jnp.zeros(kp_ref.shape, kp_ref.dtype)
        vt_ref[...] = jnp.zeros(vt_ref.shape, vt_ref.dtype)


def _kv_prep(proj, k_name, v_name, width, pad, t):
    L = proj.shape[0]
    assert pad % t == 0 and L % t == 0 and width % LANES == 0
    pad_blocks, tok_blocks = pad // t, L // t
    lp = L + 2 * pad
    kc, vc = _col(k_name, width), _col(v_name, width)
    tok = lambda b: jnp.clip(b - pad_blocks, 0, tok_blocks - 1)
    return pl.pallas_call(
        functools.partial(_kv_prep_kernel, pad_blocks=pad_blocks, tok_blocks=tok_blocks),
        grid=(lp // t,),
        in_specs=[pl.BlockSpec((t, width), lambda b: (tok(b), kc)),
                  pl.BlockSpec((t, width), lambda b: (tok(b), vc))],
        out_specs=[pl.BlockSpec((t, width), lambda b: (b, 0)),
                   pl.BlockSpec((width // HEAD_DIM * V_ROWS, t), lambda b: (0, b))],
        out_shape=[jax.ShapeDtypeStruct((lp, width), BF16),
                   jax.ShapeDtypeStruct((width // HEAD_DIM * V_ROWS, lp), BF16)],
        compiler_params=pltpu.CompilerParams(
            dimension_semantics=("arbitrary",), vmem_limit_bytes=VMEM_LIMIT),
        name="kv_prep",
    )(proj, proj)


def _band_kernel(*refs, tq, w_keys, heads, group, use_sink, window, variant):
    if use_sink:
        sink_ref, qt_ref, k_ref, vt_ref, tab_ref, o_ref = refs
    else:
        qt_ref, k_ref, vt_ref, tab_ref, o_ref = refs
    n = pl.program_id(1)
    start, pos = window(n)
    start = pl.multiple_of(start, LANES)
    kwin = k_ref[pl.ds(start, w_keys), :]
    vwin = vt_ref[:, pl.ds(start, w_keys)]
    var = variant(n)

    def attend(kbias):
        def scores(h):
            kv = h // group
            st = jnp.dot(kwin[:, kv * HEAD_DIM:(kv + 1) * HEAD_DIM],
                         qt_ref[h * HEAD_DIM:(h + 1) * HEAD_DIM, :],
                         preferred_element_type=F32) + tab_ref[var, h]
            return st if kbias is None else st + kbias

        outs = []
        st = scores(0)
        for h in range(heads):
            st_next = scores(h + 1) if h + 1 < heads else None
            kv = h // group
            m = jnp.max(st, axis=0, keepdims=True)
            pt = jnp.exp2(st - m).astype(BF16)
            acc = jnp.dot(vwin[kv * V_ROWS:(kv + 1) * V_ROWS], pt, preferred_element_type=F32)
            o, l = acc[:HEAD_DIM], acc[HEAD_DIM:HEAD_DIM + 1]
            if use_sink:
                sk = sink_ref[h] * LOG2E
                m2 = jnp.maximum(m, sk)
                a = jnp.exp2(m - m2)
                o = o * (a / (l * a + jnp.exp2(sk - m2)))
            else:
                o = o / l
            outs.append(o)
            st = st_next
        o_ref[...] = jnp.concatenate(outs, axis=0).T.astype(o_ref.dtype)

    if pos is None:
        attend(None)
    else:
        first, lo, hi = pos
        inside = (first >= lo) & (first + w_keys <= hi)

        @pl.when(inside)
        def _():
            attend(None)

        @pl.when(jnp.logical_not(inside))
        def _():
            kpos = first + lax.broadcasted_iota(jnp.int32, (w_keys, 1), 0)
            attend(jnp.where((kpos >= lo) & (kpos < hi), 0.0, NEG).astype(F32))


def _band_attention(qt, k_pad, vt_pad, table, *, tq, heads, group, window, variant=lambda n: 0,
                    sink=None):
    L = qt.shape[1]
    lp = k_pad.shape[0]
    n_var, _, w_keys, _ = table.shape
    qw = heads * HEAD_DIM
    n_groups = qt.shape[0] // qw
    kv_rows = vt_pad.shape[0] // n_groups
    kern = functools.partial(_band_kernel, tq=tq, w_keys=w_keys, heads=heads, group=group,
                             use_sink=sink is not None, window=window, variant=variant)
    in_specs = [pl.BlockSpec((qw, tq), lambda g, n: (g, n)),
                pl.BlockSpec((lp, LANES), lambda g, n: (0, g)),
                pl.BlockSpec((kv_rows, lp), lambda g, n: (g, 0)),
                pl.BlockSpec((n_var, heads, w_keys, tq), lambda g, n: (0, g, 0, 0))]
    args = [qt, k_pad, vt_pad, table]
    if sink is not None:
        in_specs = [pl.BlockSpec(memory_space=pltpu.SMEM)] + in_specs
        args = [sink] + args
    return pl.pallas_call(
        kern,
        grid=(n_groups, L // tq),
        in_specs=in_specs,
        out_specs=pl.BlockSpec((tq, qw), lambda g, n: (n, g)),
        out_shape=jax.ShapeDtypeStruct((L, n_groups * qw), BF16),
        compiler_params=pltpu.CompilerParams(
            dimension_semantics=("arbitrary", "arbitrary"), vmem_limit_bytes=VMEM_LIMIT),
        name="band_attention",
    )(*args)


def _t5_bucket(rel):
    half = T5_BUCKETS // 2
    exact = half // 2
    n = jnp.abs(rel)
    big = exact + (jnp.log(jnp.maximum(n, exact).astype(F32) / exact)
                   / math.log(T5_MAX_DIST / exact) * (half - exact)).astype(jnp.int32)
    big = jnp.minimum(big, half - 1)
    return jnp.where(rel > 0, half, 0) + jnp.where(n < exact, n, big)


def _toeplitz(v, tq, w):
    p = v.shape[-1] + 1
    assert p == w + tq
    ext = jnp.concatenate([v, jnp.zeros(v.shape[:-1] + (1,), v.dtype)], axis=-1)
    ext = jnp.roll(ext, -(tq - 1), axis=-1)
    flat = jnp.tile(ext, (1,) * (v.ndim - 1) + (tq,))[..., :tq * (p - 1)]
    return jnp.swapaxes(flat.reshape(v.shape[:-1] + (tq, p - 1))[..., :w], -1, -2)


def _t5_table(t5_table, head_lo, tq, r_lo, w_keys, mult):
    rel = np.arange(-r_lo - (tq - 1), w_keys - r_lo)
    m = mult(rel)
    b = t5_table.astype(F32)[:, head_lo:head_lo + GROUP_HEADS][_t5_bucket(jnp.asarray(rel, jnp.int32))]
    logm = np.log(np.maximum(m, 1)).astype(np.float32)
    vec = jnp.where(jnp.asarray(m > 0)[:, None], (b + logm[:, None]) * LOG2E, NEG).T
    return _toeplitz(vec, tq, w_keys)[None]


def _window_mult(rel):
    return (np.abs(rel) <= A_RADIUS).astype(np.int32)


def _dilated_mult(rel):
    m = np.zeros(rel.shape, np.int32)
    for window, dil in D_CONFIGS:
        m += ((rel % dil == 0) & (np.abs(rel) <= window // 2)).astype(np.int32)
    return m


NA_TILE_ROWS = 4
NA_WIN_ROWS = NA_TILE_ROWS + NA_ROWS
NA_TQ = NA_TILE_ROWS * GRID_W
NA_W = NA_WIN_ROWS * GRID_W


def _na_table(rpb):
    heads, n_dr, n_dc = rpb.shape
    half = NA_ROWS // 2
    assert n_dr == NA_WIN_ROWS + NA_TILE_ROWS - 1
    front = GRID_W - NA_COLS
    p = jnp.pad(rpb.astype(F32) * LOG2E, ((0, 0), (0, 0), (front, 2 * GRID_W - 1 - n_dc - front)))
    blocks = jnp.moveaxis(_toeplitz(p, GRID_W, GRID_W), 1, -1)
    blocks = jnp.pad(blocks, ((0, 0), (0, 0), (0, 0), (half, half)))
    j = np.arange(NA_WIN_ROWS)[:, None, None, None]
    kc = np.arange(GRID_W)[None, :, None, None]
    i = np.arange(NA_TILE_ROWS)[None, None, :, None]
    wq = np.arange(GRID_W)[None, None, None, :]
    col_start = np.clip(wq - NA_COLS // 2, 0, GRID_W - NA_COLS)
    col_ok = (kc >= col_start) & (kc < col_start + NA_COLS)
    tabs = []
    for shift, row_ok in ((half, (j < NA_ROWS) & (i >= 0)),
                          (0, (j >= i) & (j < i + NA_ROWS)),
                          (-half, (j >= NA_TILE_ROWS) & (i >= 0))):
        v = blocks[..., half + shift:half + shift + n_dr]
        t = _toeplitz(v, NA_TILE_ROWS, NA_WIN_ROWS)
        t = jnp.transpose(t, (0, 3, 1, 4, 2)).reshape(heads, NA_W, NA_TQ)
        valid = np.broadcast_to(col_ok & row_ok, (NA_WIN_ROWS, GRID_W, NA_TILE_ROWS, GRID_W))
        tabs.append(jnp.where(jnp.asarray(valid.reshape(NA_W, NA_TQ))[None], t, NEG))
    return jnp.stack(tabs, axis=0)


def _swap_pairs(x):
    n = x.shape[-1]
    lane = lax.broadcasted_iota(jnp.int32, x.shape, x.ndim - 1)
    return jnp.where(lane % 2 == 0, pltpu.roll(x, n - 1, x.ndim - 1), pltpu.roll(x, 1, x.ndim - 1))


def _head_rms(x, ones_bd, w):
    ms = jnp.dot(x * x, ones_bd, preferred_element_type=F32,
                 precision=lax.Precision.HIGHEST) * (1.0 / HEAD_DIM)
    return x * lax.rsqrt(ms + EPS) * w


def _cprep_kernel(q_ref, k_ref, v_ref, cos_ref, sin_ref, qw_ref, kw_ref, bd_ref,
                  qt_ref, k2_ref, vt_ref):
    cos = cos_ref[...]
    sin = sin_ref[...]
    q = _head_rms(q_ref[...].astype(F32), bd_ref[...], qw_ref[...])
    reps = q.shape[1] // LANES
    q = q * jnp.concatenate([cos] * reps, axis=1) + _swap_pairs(q) * jnp.concatenate([sin] * reps, axis=1)
    qt_ref[...] = (q * Q_SCALE).T.astype(qt_ref.dtype)
    k = _head_rms(k_ref[...].astype(F32), bd_ref[:LANES, :LANES], kw_ref[...])
    k = (k * cos + _swap_pairs(k) * sin).astype(k2_ref.dtype)
    vt = v_ref[...].astype(F32).T.astype(vt_ref.dtype)
    ones = jnp.ones((V_ROWS - HEAD_DIM, vt.shape[1]), vt_ref.dtype)
    for g in range(C_KV_HEADS):
        k2_ref[g] = k[:, g * HEAD_DIM:(g + 1) * HEAD_DIM]
        vt_ref[g, 0] = jnp.concatenate([vt[g * HEAD_DIM:(g + 1) * HEAD_DIM], ones], axis=0)


def _cprep(proj, cos2, sin2, qw, kw, ones_bd, t):
    L = proj.shape[0]
    vec = lambda width: pl.BlockSpec((1, width), lambda n: (0, 0))
    return pl.pallas_call(
        _cprep_kernel,
        grid=(L // t,),
        in_specs=[pl.BlockSpec((t, GROUP_WIDTH), lambda n: (n, _col("q_c", GROUP_WIDTH))),
                  pl.BlockSpec((t, LANES), lambda n: (n, _col("k_c", LANES))),
                  pl.BlockSpec((t, LANES), lambda n: (n, _col("v_c", LANES))),
                  pl.BlockSpec((t, LANES), lambda n: (n, 0)),
                  pl.BlockSpec((t, LANES), lambda n: (n, 0)),
                  vec(GROUP_WIDTH), vec(LANES),
                  pl.BlockSpec((GROUP_WIDTH, GROUP_WIDTH), lambda n: (0, 0))],
        out_specs=[pl.BlockSpec((GROUP_WIDTH, t), lambda n: (0, n)),
                   pl.BlockSpec((C_KV_HEADS, t, HEAD_DIM), lambda n: (0, n, 0)),
                   pl.BlockSpec((C_KV_HEADS, 1, V_ROWS, t), lambda n: (0, n, 0, 0))],
        out_shape=[jax.ShapeDtypeStruct((GROUP_WIDTH, L), BF16),
                   jax.ShapeDtypeStruct((C_KV_HEADS, L, HEAD_DIM), BF16),
                   jax.ShapeDtypeStruct((C_KV_HEADS, L // t, V_ROWS, t), BF16)],
        compiler_params=pltpu.CompilerParams(
            dimension_semantics=("arbitrary",), vmem_limit_bytes=VMEM_LIMIT),
        name="dense_prep",
    )(proj, proj, proj, cos2, sin2, qw, kw, ones_bd)


def _flash_kernel(qt_ref, k_ref, vt_ref, o_ref, m_scr, acc_scr, st_scr, *, tk, heads, unroll, ahead):
    n_kv = k_ref.shape[1] // tk
    m_scr[...] = jnp.full(m_scr.shape, NEG, F32)
    acc_scr[...] = jnp.zeros(acc_scr.shape, F32)

    def scores(j, h):
        k = k_ref[0, pl.ds(pl.multiple_of(j * tk, tk), tk), :]
        qt = qt_ref[h * HEAD_DIM:(h + 1) * HEAD_DIM, :]
        return jnp.dot(k, qt, preferred_element_type=F32)

    n_items = unroll * heads

    def item_scores(t, idx):
        j = t * unroll + idx // heads
        return scores(jnp.minimum(j, n_kv - 1), idx % heads)

    for a in range(ahead):
        st_scr[a] = item_scores(0, a)

    def body(t, carry):
        pending = [st_scr[a] for a in range(ahead)]
        for idx in range(n_items):
            pending.append(item_scores(t, idx + ahead))
            st = pending.pop(0)
            h = idx % heads
            m = m_scr[h]
            m_new = jnp.maximum(m, jnp.max(st, axis=0, keepdims=True))
            pt = jnp.exp2(st - m_new).astype(BF16)
            acc_scr[h] = (jnp.exp2(m - m_new) * acc_scr[h]
                          + jnp.dot(vt_ref[0, t * unroll + idx // heads], pt, preferred_element_type=F32))
            m_scr[h] = m_new
        for a in range(ahead):
            st_scr[a] = pending[a]
        return carry

    lax.fori_loop(0, n_kv // unroll, body, 0)
    outs = [acc_scr[h, :HEAD_DIM] / acc_scr[h, HEAD_DIM:HEAD_DIM + 1] for h in range(heads)]
    o_ref[...] = jnp.concatenate(outs, axis=0).T.astype(o_ref.dtype)


def _flash(qt, k2, vt, tq):
    L = qt.shape[1]
    _, n_kv, v_rows, tk = vt.shape
    heads = GROUP_HEADS // C_KV_HEADS
    qrows = heads * HEAD_DIM
    return pl.pallas_call(
        functools.partial(_flash_kernel, tk=tk, heads=heads, unroll=C_UNROLL, ahead=C_AHEAD),
        grid=(C_KV_HEADS, L // tq),
        in_specs=[pl.BlockSpec((qrows, tq), lambda g, n: (g, n)),
                  pl.BlockSpec((1, L, HEAD_DIM), lambda g, n: (g, 0, 0)),
                  pl.BlockSpec((1, n_kv, v_rows, tk), lambda g, n: (g, 0, 0, 0))],
        out_specs=pl.BlockSpec((tq, qrows), lambda g, n: (n, g)),
        out_shape=jax.ShapeDtypeStruct((L, GROUP_WIDTH), BF16),
        scratch_shapes=[pltpu.VMEM((heads, 1, tq), F32), pltpu.VMEM((heads, v_rows, tq), F32),
                        pltpu.VMEM((C_AHEAD, tk, tq), F32)],
        compiler_params=pltpu.CompilerParams(
            dimension_semantics=("arbitrary", "arbitrary"), vmem_limit_bytes=VMEM_LIMIT),
        name="dense_flash",
    )(qt, k2, vt)


def _outproj_kernel(*refs, final):
    if final:
        x_ref, ya, yb, yc, yd, z_ref, w_ref, g_ref, fw_ref, o_ref = refs
    else:
        x_ref, ya, yb, yc, yd, z_ref, w_ref, g_ref, o_ref = refs
    acc = jnp.zeros(x_ref.shape, F32)
    for gi, y_ref in enumerate((ya, yb, yc, yd)):
        cols = slice(gi * GROUP_WIDTH, (gi + 1) * GROUP_WIDTH)
        z = z_ref[:, cols].astype(F32)
        u = (y_ref[...].astype(F32) * (z * jax.nn.sigmoid(z))).astype(BF16)
        acc = acc + jnp.dot(u, w_ref[cols, :], preferred_element_type=F32)
    xn = x_ref[...] + g_ref[...] * acc
    if final:
        xn = xn * lax.rsqrt(jnp.mean(xn * xn, axis=-1, keepdims=True) + EPS) * fw_ref[...]
    o_ref[...] = xn


def _outproj(x, ys, proj, w_bf16, gate, final_w):
    L, d = x.shape
    tm = min(512, L)
    final = final_w is not None
    row = lambda width: pl.BlockSpec((tm, width), lambda i: (i, 0))
    vec = pl.BlockSpec((1, d), lambda i: (0, 0))
    in_specs = [row(d), row(GROUP_WIDTH), row(GROUP_WIDTH), row(GROUP_WIDTH), row(GROUP_WIDTH),
                pl.BlockSpec((tm, d), lambda i: (i, _col("z_a", d))),
                pl.BlockSpec(w_bf16.shape, lambda i: (0, 0)), vec]
    args = [x, *ys, proj, w_bf16, gate]
    if final:
        in_specs.append(vec)
        args.append(final_w)
    return pl.pallas_call(
        functools.partial(_outproj_kernel, final=final),
        grid=(L // tm,),
        in_specs=in_specs,
        out_specs=row(d),
        out_shape=jax.ShapeDtypeStruct((L, d), F32),
        compiler_params=pltpu.CompilerParams(
            dimension_semantics=("arbitrary",), vmem_limit_bytes=VMEM_LIMIT),
        name="outproj",
    )(*args)


def _rope_tables(L):
    t = np.arange(L)
    axis_dim = HEAD_DIM // 2
    inv = jnp.asarray(ROPE_THETA, F32) ** (-jnp.arange(0, axis_dim, 2, dtype=F32) / axis_dim)
    row = jnp.asarray(t // GRID_W, F32)
    col = jnp.asarray(t % GRID_W, F32)
    ang = jnp.concatenate([row[:, None] * inv[None], col[:, None] * inv[None]], axis=-1)
    cos = jnp.repeat(jnp.cos(ang), 2, axis=-1)
    sin = jnp.repeat(jnp.sin(ang), 2, axis=-1) * jnp.asarray(np.tile([-1.0, 1.0], HEAD_DIM // 2), F32)
    return jnp.tile(cos, (1, LANES // HEAD_DIM)), jnp.tile(sin, (1, LANES // HEAD_DIM))


A_TQ = 256
D_TQ = 256
D_REACH = D_CONFIGS[-1][0] // 2
C_TQ = 512
C_TK = 256
C_UNROLL = 8
C_AHEAD = 2


def _table_a(t5_table):
    return _t5_table(t5_table, 0, A_TQ, A_RADIUS, A_TQ + 2 * A_RADIUS, _window_mult)


def _table_d(t5_table):
    return _t5_table(t5_table, GROUP_HEADS, D_TQ, D_REACH, D_TQ + 2 * D_REACH, _dilated_mult)


def _seq_window(L, tq, pad):
    return lambda n: (n * tq, (n * tq - pad, 0, L))


def _mixer_a(proj, qt, tab_a, sink):
    L = proj.shape[0]
    k_pad, vt_pad = _kv_prep(proj, "k_a", "v_a", LANES, A_RADIUS, A_RADIUS)
    return _band_attention(qt, k_pad, vt_pad, tab_a, tq=A_TQ, heads=GROUP_HEADS,
                           group=GROUP_HEADS // A_KV_HEADS, window=_seq_window(L, A_TQ, A_RADIUS),
                           sink=sink.astype(F32))


def _mixer_b(proj, qt, tab_b):
    L = proj.shape[0]
    rows = L // GRID_W
    n_tiles = rows // NA_TILE_ROWS
    k, vt = _kv_prep(proj, "k_b", "v_b", GROUP_WIDTH, 0, 512)
    window = lambda n: (jnp.clip(n * NA_TILE_ROWS - NA_ROWS // 2, 0, rows - NA_WIN_ROWS) * GRID_W, None)
    variant = lambda n: jnp.where(n == 0, 0, jnp.where(n == n_tiles - 1, 2, 1))
    return _band_attention(qt, k, vt, tab_b, tq=NA_TQ, heads=2, group=1, window=window, variant=variant)


def _mixer_c(proj, q_norm_w, k_norm_w):
    L = proj.shape[0]
    cos2, sin2 = _rope_tables(L)
    ones_bd = jnp.asarray(np.kron(np.eye(GROUP_HEADS), np.ones((HEAD_DIM, HEAD_DIM))), F32)
    qt, k2, vt = _cprep(proj, cos2, sin2, jnp.tile(q_norm_w.astype(F32), GROUP_HEADS)[None],
                        jnp.tile(k_norm_w.astype(F32), LANES // HEAD_DIM)[None], ones_bd, C_TK)
    return _flash(qt, k2, vt, C_TQ)


def _mixer_d(proj, qt, tab_d):
    L = proj.shape[0]
    k_pad, vt_pad = _kv_prep(proj, "k_d", "v_d", GROUP_WIDTH, D_REACH, 512)
    return _band_attention(qt, k_pad, vt_pad, tab_d, tq=D_TQ, heads=2, group=1,
                           window=_seq_window(L, D_TQ, D_REACH))


def kernel(x, c, w_ada, b_ada, norm_w, w_in, w_out, attn_sink, na_rpb, q_norm_w, k_norm_w,
           t5_table, final_norm_w):
    B, L, D = x.shape
    assert B == 1 and L % 1024 == 0 and L // GRID_W >= NA_WIN_ROWS
    depth = w_ada.shape[0]
    x = x[0]

    mod = _ada_mod(jnp.broadcast_to(c, (8, D)), w_ada, b_ada)[:, 0:1, :]
    tab_a = _table_a(t5_table)
    tab_d = _table_d(t5_table)
    for i in range(depth):
        shift, scale, gate = jnp.split(mod[i], 3, axis=-1)
        w_i = w_in[i][:, _PERM].astype(BF16)
        proj = _inproj(x, norm_w[i][None], scale, shift, w_i)
        qt_a, qt_b, qt_d = _qt_prep(proj, ("q_a", "q_b", "q_d"))
        ys = (_mixer_a(proj, qt_a, tab_a, attn_sink[i]),
              _mixer_b(proj, qt_b, _na_table(na_rpb[i])),
              _mixer_c(proj, q_norm_w[i], k_norm_w[i]),
              _mixer_d(proj, qt_d, tab_d))
        x = _outproj(x, ys, proj, w_out[i].astype(BF16), gate,
                     final_norm_w[None] if i == depth - 1 else None)
    return x[None]
```

```python
import functools
import math

import numpy as np
import jax
import jax.numpy as jnp
from jax import lax
from jax.experimental import pallas as pl
from jax.experimental.pallas import tpu as pltpu

HEAD_DIM = 64
GROUP_WIDTH = 512
GROUP_HEADS = 8
A_KV_HEADS = 2
A_RADIUS = 128
C_KV_HEADS = 2
ROPE_THETA = 10000.0
NA_ROWS = 8
NA_COLS = 16
D_CONFIGS = ((128, 1), (512, 4), (2048, 16))
GRID_W = 64
T5_BUCKETS = 32
T5_MAX_DIST = 1024
EPS = 1e-6
NEG = -1e30

V_ROWS = HEAD_DIM + 16
LANES = 128
VMEM_LIMIT = 56 * 1024 * 1024

F32 = jnp.float32
BF16 = jnp.bfloat16

_SRC = {}
_off = 0
for _name, _w in (("q_a", 512), ("k_a", 128), ("v_a", 128), ("z_a", 512),
                  ("q_b", 512), ("k_b", 512), ("v_b", 512), ("z_b", 512),
                  ("q_c", 512), ("k_c", 128), ("v_c", 128), ("z_c", 512),
                  ("q_d", 512), ("k_d", 512), ("v_d", 512), ("z_d", 512)):
    _SRC[_name] = (_off, _w)
    _off += _w
IN_WIDTH = _off
_ORDER = ("q_a", "q_b", "k_b", "v_b", "q_c", "q_d", "k_d", "v_d",
          "z_a", "z_b", "z_c", "z_d", "k_a", "v_a", "k_c", "v_c")
_DST = {}
_off = 0
for _name in _ORDER:
    _DST[_name] = _off
    _off += _SRC[_name][1]
_PERM = np.concatenate([np.arange(_SRC[n][0], _SRC[n][0] + _SRC[n][1]) for n in _ORDER])


def _col(name, width):
    assert _DST[name] % width == 0
    return _DST[name] // width


def _ada_kernel(c_ref, w_ref, b_ref, o_ref):
    c = c_ref[...]
    cond = c * jax.nn.sigmoid(c)
    o_ref[0] = jnp.dot(cond, w_ref[0], preferred_element_type=F32,
                       precision=lax.Precision.HIGHEST) + b_ref[0]


def _ada_mod(c8, w_ada, b_ada):
    depth, d, n3 = w_ada.shape
    tn = 1024
    return pl.pallas_call(
        _ada_kernel,
        grid=(depth, n3 // tn),
        in_specs=[pl.BlockSpec((8, d), lambda i, j: (0, 0)),
                  pl.BlockSpec((1, d, tn), lambda i, j: (i, 0, j)),
                  pl.BlockSpec((1, 1, tn), lambda i, j: (i, 0, j))],
        out_specs=pl.BlockSpec((1, 8, tn), lambda i, j: (i, 0, j)),
        out_shape=jax.ShapeDtypeStruct((depth, 8, n3), F32),
        compiler_params=pltpu.CompilerParams(
            dimension_semantics=("arbitrary", "arbitrary"), vmem_limit_bytes=VMEM_LIMIT),
        name="ada_mod",
    )(c8, w_ada, b_ada.reshape(depth, 1, n3))


def _inproj_kernel(x_ref, nw_ref, sc_ref, sh_ref, w_ref, o_ref, h_scr):
    @pl.when(pl.program_id(1) == 0)
    def _():
        x = x_ref[...]
        y = x * lax.rsqrt(jnp.mean(x * x, axis=-1, keepdims=True) + EPS)
        h = (y * nw_ref[...]) * (1.0 + sc_ref[...]) + sh_ref[...]
        h_scr[...] = h.astype(BF16)

    o_ref[...] = jnp.dot(h_scr[...], w_ref[...], preferred_element_type=F32).astype(o_ref.dtype)


def _inproj(x, nw, scale, shift, w_bf16):
    L, d = x.shape
    n = w_bf16.shape[1]
    tm = min(1024, L)
    tn = 512
    vec = pl.BlockSpec((1, d), lambda i, j: (0, 0))
    return pl.pallas_call(
        _inproj_kernel,
        grid=(L // tm, n // tn),
        in_specs=[pl.BlockSpec((tm, d), lambda i, j: (i, 0)), vec, vec, vec,
                  pl.BlockSpec((d, tn), lambda i, j: (0, j))],
        out_specs=pl.BlockSpec((tm, tn), lambda i, j: (i, j)),
        out_shape=jax.ShapeDtypeStruct((L, n), BF16),
        scratch_shapes=[pltpu.VMEM((tm, d), BF16)],
        compiler_params=pltpu.CompilerParams(
            dimension_semantics=("arbitrary", "arbitrary"), vmem_limit_bytes=VMEM_LIMIT),
        name="inproj",
    )(x, nw, scale, shift, w_bf16)


LOG2E = math.log2(math.e)
Q_SCALE = HEAD_DIM ** -0.5 * LOG2E


def _qt_prep_kernel(*refs):
    n = len(refs) // 2
    for q_ref, qt_ref in zip(refs[:n], refs[n:]):
        qt_ref[...] = (q_ref[...].astype(F32) * Q_SCALE).T.astype(qt_ref.dtype)


def _qt_prep(proj, names, t=512):
    L = proj.shape[0]
    return pl.pallas_call(
        _qt_prep_kernel,
        grid=(L // t,),
        in_specs=[pl.BlockSpec((t, GROUP_WIDTH), functools.partial(lambda c, n: (n, c), _col(name, GROUP_WIDTH)))
                  for name in names],
        out_specs=[pl.BlockSpec((GROUP_WIDTH, t), lambda n: (0, n)) for _ in names],
        out_shape=[jax.ShapeDtypeStruct((GROUP_WIDTH, L), BF16) for _ in names],
        compiler_params=pltpu.CompilerParams(
            dimension_semantics=("arbitrary",), vmem_limit_bytes=VMEM_LIMIT),
        name="qt_prep",
    )(*([proj] * len(names)))


def _kv_prep_kernel(k_ref, v_ref, kp_ref, vt_ref, *, pad_blocks, tok_blocks):
    b = pl.program_id(0)
    is_token = (b >= pad_blocks) & (b < pad_blocks + tok_blocks)

    @pl.when(is_token)
    def _():
        kp_ref[...] = k_ref[...]
        vt = v_ref[...].astype(F32).T.astype(vt_ref.dtype)
        ones = jnp.ones((V_ROWS - HEAD_DIM, vt.shape[1]), vt_ref.dtype)
        for g in range(vt.shape[0] // HEAD_DIM):
            vt_ref[g * V_ROWS:(g + 1) * V_ROWS, :] = jnp.concatenate(
                [vt[g * HEAD_DIM:(g + 1) * HEAD_DIM], ones], axis=0)

    @pl.when(jnp.logical_not(is_token))
    def _():
        kp_ref[...] = jnp.zeros(kp_ref.shape, kp_ref.dtype)
        vt_ref[...] = jnp.zeros(vt_ref.shape, vt_ref.dtype)


def _kv_prep(proj, k_name, v_name, width, pad, t):
    L = proj.shape[0]
    assert pad % t == 0 and L % t == 0 and width % LANES == 0
    pad_blocks, tok_blocks = pad // t, L // t
    lp = L + 2 * pad
    kc, vc = _col(k_name, width), _col(v_name, width)
    tok = lambda b: jnp.clip(b - pad_blocks, 0, tok_blocks - 1)
    return pl.pallas_call(
        functools.partial(_kv_prep_kernel, pad_blocks=pad_blocks, tok_blocks=tok_blocks),
        grid=(lp // t,),
        in_specs=[pl.BlockSpec((t, width), lambda b: (tok(b), kc)),
                  pl.BlockSpec((t, width), lambda b: (tok(b), vc))],
        out_specs=[pl.BlockSpec((t, width), lambda b: (b, 0)),
                   pl.BlockSpec((width // HEAD_DIM * V_ROWS, t), lambda b: (0, b))],
        out_shape=[jax.ShapeDtypeStruct((lp, width), BF16),
                   jax.ShapeDtypeStruct((width // HEAD_DIM * V_ROWS, lp), BF16)],
        compiler_params=pltpu.CompilerParams(
            dimension_semantics=("arbitrary",), vmem_limit_bytes=VMEM_LIMIT),
        name="kv_prep",
    )(proj, proj)


BAND_CHUNK = 256


def _band_kernel(*refs, w_keys, heads, group, use_sink, window, variant):
    if use_sink:
        sink_ref, qt_ref, k_ref, vt_ref, tab_ref, o_ref = refs
    else:
        qt_ref, k_ref, vt_ref, tab_ref, o_ref = refs
    n = pl.program_id(0)
    _, pos = window(n)
    kwin = k_ref[...]
    vwin = vt_ref[...]
    var = variant(n)

    n_chunks = w_keys // BAND_CHUNK

    def attend(kbias):
        def scores(h, c):
            kv = h // group
            rows = slice(c * BAND_CHUNK, (c + 1) * BAND_CHUNK)
            st = jnp.dot(kwin[rows, kv * HEAD_DIM:(kv + 1) * HEAD_DIM],
                         qt_ref[h * HEAD_DIM:(h + 1) * HEAD_DIM, :],
                         preferred_element_type=F32) + tab_ref[var, h, rows, :]
            return st if kbias is None else st + kbias[rows]

        def col_max(m, st):
            cm = jnp.max(st, axis=0, keepdims=True)
            return cm if m is None else jnp.maximum(m, cm)

        outs = []
        cur, m = [], None
        for c in range(n_chunks):
            cur.append(scores(0, c))
            m = col_max(m, cur[-1])
        for h in range(heads):
            kv = h // group
            nxt, m_next, acc = [], None, None
            for c in range(n_chunks):
                if h + 1 < heads:
                    nxt.append(scores(h + 1, c))
                    m_next = col_max(m_next, nxt[-1])
                pt = jnp.exp2(cur[c] - m).astype(BF16)
                pv = jnp.dot(vwin[kv * V_ROWS:(kv + 1) * V_ROWS, c * BAND_CHUNK:(c + 1) * BAND_CHUNK], pt,
                             preferred_element_type=F32)
                acc = pv if acc is None else acc + pv
            o, l = acc[:HEAD_DIM], acc[HEAD_DIM:HEAD_DIM + 1]
            if use_sink:
                sk = sink_ref[h] * LOG2E
                m2 = jnp.maximum(m, sk)
                a = jnp.exp2(m - m2)
                o = o * (a / (l * a + jnp.exp2(sk - m2)))
            else:
                o = o / l
            outs.append(o)
            cur, m = nxt, m_next
        o_ref[...] = jnp.concatenate(outs, axis=0).T.astype(o_ref.dtype)

    if pos is None:
        attend(None)
    else:
        first, lo, hi = pos
        inside = (first >= lo) & (first + w_keys <= hi)

        @pl.when(inside)
        def _():
            attend(None)

        @pl.when(jnp.logical_not(inside))
        def _():
            kpos = first + lax.broadcasted_iota(jnp.int32, (w_keys, 1), 0)
            attend(jnp.where((kpos >= lo) & (kpos < hi), 0.0, NEG).astype(F32))


def _band_attention(qt, k_pad, vt_pad, table, *, group, window, variant=lambda n: 0, sink=None):
    qw, L = qt.shape
    heads = qw // HEAD_DIM
    _, _, w_keys, tq = table.shape
    kern = functools.partial(_band_kernel, w_keys=w_keys, heads=heads, group=group,
                             use_sink=sink is not None, window=window, variant=variant)
    start = lambda n: pl.multiple_of(window(n)[0], LANES)
    in_specs = [pl.BlockSpec((qw, tq), lambda n: (0, n)),
                pl.BlockSpec((pl.Element(w_keys), pl.Element(k_pad.shape[1])), lambda n: (start(n), 0)),
                pl.BlockSpec((pl.Element(vt_pad.shape[0]), pl.Element(w_keys)), lambda n: (0, start(n))),
                pl.BlockSpec(table.shape, lambda n: (0, 0, 0, 0), pipeline_mode=pl.Buffered(1))]
    args = [qt, k_pad, vt_pad, table]
    if sink is not None:
        in_specs = [pl.BlockSpec(memory_space=pltpu.SMEM)] + in_specs
        args = [sink] + args
    return pl.pallas_call(
        kern,
        grid=(L // tq,),
        in_specs=in_specs,
        out_specs=pl.BlockSpec((tq, qw), lambda n: (n, 0)),
        out_shape=jax.ShapeDtypeStruct((L, qw), BF16),
        compiler_params=pltpu.CompilerParams(
            dimension_semantics=("arbitrary",), vmem_limit_bytes=VMEM_LIMIT),
        name="band_attention",
    )(*args)


def _t5_bucket(rel):
    half = T5_BUCKETS // 2
    exact = half // 2
    n = jnp.abs(rel)
    big = exact + (jnp.log(jnp.maximum(n, exact).astype(F32) / exact)
                   / math.log(T5_MAX_DIST / exact) * (half - exact)).astype(jnp.int32)
    big = jnp.minimum(big, half - 1)
    return jnp.where(rel > 0, half, 0) + jnp.where(n < exact, n, big)


def _toeplitz(v, tq, w):
    p = v.shape[-1] + 1
    assert p == w + tq
    ext = jnp.concatenate([v, jnp.zeros(v.shape[:-1] + (1,), v.dtype)], axis=-1)
    ext = jnp.roll(ext, -(tq - 1), axis=-1)
    flat = jnp.tile(ext, (1,) * (v.ndim - 1) + (tq,))[..., :tq * (p - 1)]
    return jnp.swapaxes(flat.reshape(v.shape[:-1] + (tq, p - 1))[..., :w], -1, -2)


def _t5_table(t5_table, head_lo, tq, r_lo, w_keys, mult):
    rel = np.arange(-r_lo - (tq - 1), w_keys - r_lo)
    m = mult(rel)
    b = t5_table.astype(F32)[:, head_lo:head_lo + GROUP_HEADS][_t5_bucket(jnp.asarray(rel, jnp.int32))]
    logm = np.log(np.maximum(m, 1)).astype(np.float32)
    vec = jnp.where(jnp.asarray(m > 0)[:, None], (b + logm[:, None]) * LOG2E, NEG).T
    return _toeplitz(vec, tq, w_keys)[None]


def _window_mult(rel):
    return (np.abs(rel) <= A_RADIUS).astype(np.int32)


def _dilated_mult(rel):
    m = np.zeros(rel.shape, np.int32)
    for window, dil in D_CONFIGS:
        m += ((rel % dil == 0) & (np.abs(rel) <= window // 2)).astype(np.int32)
    return m


NA_TILE_ROWS = 4
NA_WIN_ROWS = NA_TILE_ROWS + NA_ROWS
NA_TQ = NA_TILE_ROWS * GRID_W
NA_W = NA_WIN_ROWS * GRID_W


def _na_table(rpb):
    heads, n_dr, n_dc = rpb.shape
    half = NA_ROWS // 2
    assert n_dr == NA_WIN_ROWS + NA_TILE_ROWS - 1
    front = GRID_W - NA_COLS
    p = jnp.pad(rpb.astype(F32) * LOG2E, ((0, 0), (0, 0), (front, 2 * GRID_W - 1 - n_dc - front)))
    blocks = jnp.moveaxis(_toeplitz(p, GRID_W, GRID_W), 1, -1)
    blocks = jnp.pad(blocks, ((0, 0), (0, 0), (0, 0), (half, half)))
    j = np.arange(NA_WIN_ROWS)[:, None, None, None]
    kc = np.arange(GRID_W)[None, :, None, None]
    i = np.arange(NA_TILE_ROWS)[None, None, :, None]
    wq = np.arange(GRID_W)[None, None, None, :]
    col_start = np.clip(wq - NA_COLS // 2, 0, GRID_W - NA_COLS)
    col_ok = (kc >= col_start) & (kc < col_start + NA_COLS)
    tabs = []
    for shift, row_ok in ((half, (j < NA_ROWS) & (i >= 0)),
                          (0, (j >= i) & (j < i + NA_ROWS)),
                          (-half, (j >= NA_TILE_ROWS) & (i >= 0))):
        v = blocks[..., half + shift:half + shift + n_dr]
        t = _toeplitz(v, NA_TILE_ROWS, NA_WIN_ROWS)
        t = jnp.transpose(t, (0, 3, 1, 4, 2)).reshape(heads, NA_W, NA_TQ)
        valid = np.broadcast_to(col_ok & row_ok, (NA_WIN_ROWS, GRID_W, NA_TILE_ROWS, GRID_W))
        tabs.append(jnp.where(jnp.asarray(valid.reshape(NA_W, NA_TQ))[None], t, NEG))
    return jnp.stack(tabs, axis=0)


def _swap_pairs(x):
    n = x.shape[-1]
    lane = lax.broadcasted_iota(jnp.int32, x.shape, x.ndim - 1)
    return jnp.where(lane % 2 == 0, pltpu.roll(x, n - 1, x.ndim - 1), pltpu.roll(x, 1, x.ndim - 1))


def _head_rms(x, ones_bd, w):
    ms = jnp.dot(x * x, ones_bd, preferred_element_type=F32,
                 precision=lax.Precision.HIGHEST) * (1.0 / HEAD_DIM)
    return x * lax.rsqrt(ms + EPS) * w


def _cprep_kernel(q_ref, k_ref, v_ref, cos_ref, sin_ref, qw_ref, kw_ref, bd_ref,
                  qt_ref, k2_ref, vt_ref):
    cos = cos_ref[...]
    sin = sin_ref[...]
    q = _head_rms(q_ref[...].astype(F32), bd_ref[...], qw_ref[...])
    reps = q.shape[1] // LANES
    q = q * jnp.concatenate([cos] * reps, axis=1) + _swap_pairs(q) * jnp.concatenate([sin] * reps, axis=1)
    qt_ref[...] = (q * Q_SCALE).T.astype(qt_ref.dtype)
    k = _head_rms(k_ref[...].astype(F32), bd_ref[:LANES, :LANES], kw_ref[...])
    k = (k * cos + _swap_pairs(k) * sin).astype(k2_ref.dtype)
    vt = v_ref[...].astype(F32).T.astype(vt_ref.dtype)
    ones = jnp.ones((V_ROWS - HEAD_DIM, vt.shape[1]), vt_ref.dtype)
    for g in range(C_KV_HEADS):
        k2_ref[g] = k[:, g * HEAD_DIM:(g + 1) * HEAD_DIM]
        vt_ref[g, 0] = jnp.concatenate([vt[g * HEAD_DIM:(g + 1) * HEAD_DIM], ones], axis=0)


def _cprep(proj, cos2, sin2, qw, kw, ones_bd, t):
    L = proj.shape[0]
    vec = lambda width: pl.BlockSpec((1, width), lambda n: (0, 0))
    return pl.pallas_call(
        _cprep_kernel,
        grid=(L // t,),
        in_specs=[pl.BlockSpec((t, GROUP_WIDTH), lambda n: (n, _col("q_c", GROUP_WIDTH))),
                  pl.BlockSpec((t, LANES), lambda n: (n, _col("k_c", LANES))),
                  pl.BlockSpec((t, LANES), lambda n: (n, _col("v_c", LANES))),
                  pl.BlockSpec((t, LANES), lambda n: (n, 0)),
                  pl.BlockSpec((t, LANES), lambda n: (n, 0)),
                  vec(GROUP_WIDTH), vec(LANES),
                  pl.BlockSpec((GROUP_WIDTH, GROUP_WIDTH), lambda n: (0, 0))],
        out_specs=[pl.BlockSpec((GROUP_WIDTH, t), lambda n: (0, n)),
                   pl.BlockSpec((C_KV_HEADS, t, HEAD_DIM), lambda n: (0, n, 0)),
                   pl.BlockSpec((C_KV_HEADS, 1, V_ROWS, t), lambda n: (0, n, 0, 0))],
        out_shape=[jax.ShapeDtypeStruct((GROUP_WIDTH, L), BF16),
                   jax.ShapeDtypeStruct((C_KV_HEADS, L, HEAD_DIM), BF16),
                   jax.ShapeDtypeStruct((C_KV_HEADS, L // t, V_ROWS, t), BF16)],
        compiler_params=pltpu.CompilerParams(
            dimension_semantics=("arbitrary",), vmem_limit_bytes=VMEM_LIMIT),
        name="dense_prep",
    )(proj, proj, proj, cos2, sin2, qw, kw, ones_bd)


def _flash_kernel(qt_ref, k_ref, vt_ref, o_ref, m_scr, acc_scr, st_scr, *, tk, heads, unroll, ahead):
    n_kv = k_ref.shape[1] // tk
    m_scr[...] = jnp.full(m_scr.shape, NEG, F32)
    acc_scr[...] = jnp.zeros(acc_scr.shape, F32)

    def scores(j, h):
        k = k_ref[0, pl.ds(pl.multiple_of(j * tk, tk), tk), :]
        qt = qt_ref[h * HEAD_DIM:(h + 1) * HEAD_DIM, :]
        return jnp.dot(k, qt, preferred_element_type=F32)

    n_items = unroll * heads

    def item_scores(t, idx):
        j = t * unroll + idx // heads
        return scores(jnp.minimum(j, n_kv - 1), idx % heads)

    for a in range(ahead):
        st_scr[a] = item_scores(0, a)

    def body(t, carry):
        pending = [st_scr[a] for a in range(ahead)]
        for idx in range(n_items):
            pending.append(item_scores(t, idx + ahead))
            st = pending.pop(0)
            h = idx % heads
            m = m_scr[h]
            m_new = jnp.maximum(m, jnp.max(st, axis=0, keepdims=True))
            pt = jnp.exp2(st - m_new).astype(BF16)
            acc_scr[h] = (jnp.exp2(m - m_new) * acc_scr[h]
                          + jnp.dot(vt_ref[0, t * unroll + idx // heads], pt, preferred_element_type=F32))
            m_scr[h] = m_new
        for a in range(ahead):
            st_scr[a] = pending[a]
        return carry

    lax.fori_loop(0, n_kv // unroll, body, 0)
    outs = [acc_scr[h, :HEAD_DIM] / acc_scr[h, HEAD_DIM:HEAD_DIM + 1] for h in range(heads)]
    o_ref[...] = jnp.concatenate(outs, axis=0).T.astype(o_ref.dtype)


def _flash(qt, k2, vt, tq):
    L = qt.shape[1]
    _, n_kv, v_rows, tk = vt.shape
    heads = GROUP_HEADS // C_KV_HEADS
    qrows = heads * HEAD_DIM
    return pl.pallas_call(
        functools.partial(_flash_kernel, tk=tk, heads=heads, unroll=C_UNROLL, ahead=C_AHEAD),
        grid=(C_KV_HEADS, L // tq),
        in_specs=[pl.BlockSpec((qrows, tq), lambda g, n: (g, n)),
                  pl.BlockSpec((1, L, HEAD_DIM), lambda g, n: (g, 0, 0)),
                  pl.BlockSpec((1, n_kv, v_rows, tk), lambda g, n: (g, 0, 0, 0))],
        out_specs=pl.BlockSpec((tq, qrows), lambda g, n: (n, g)),
        out_shape=jax.ShapeDtypeStruct((L, GROUP_WIDTH), BF16),
        scratch_shapes=[pltpu.VMEM((heads, 1, tq), F32), pltpu.VMEM((heads, v_rows, tq), F32),
                        pltpu.VMEM((C_AHEAD, tk, tq), F32)],
        compiler_params=pltpu.CompilerParams(
            dimension_semantics=("arbitrary", "arbitrary"), vmem_limit_bytes=VMEM_LIMIT),
        name="dense_flash",
    )(qt, k2, vt)


def _outproj_kernel(*refs, final):
    if final:
        x_ref, ya, yb, yc, yd, z_ref, w_ref, g_ref, fw_ref, o_ref = refs
    else:
        x_ref, ya, yb, yc, yd, z_ref, w_ref, g_ref, o_ref = refs
    acc = jnp.zeros(x_ref.shape, F32)
    for gi, y_ref in enumerate((ya, yb, yc, yd)):
        cols = slice(gi * GROUP_WIDTH, (gi + 1) * GROUP_WIDTH)
        z = z_ref[:, cols].astype(F32)
        u = (y_ref[...].astype(F32) * (z * jax.nn.sigmoid(z))).astype(BF16)
        acc = acc + jnp.dot(u, w_ref[cols, :], preferred_element_type=F32)
    xn = x_ref[...] + g_ref[...] * acc
    if final:
        xn = xn * lax.rsqrt(jnp.mean(xn * xn, axis=-1, keepdims=True) + EPS) * fw_ref[...]
    o_ref[...] = xn


def _outproj(x, ys, proj, w_bf16, gate, final_w):
    L, d = x.shape
    tm = min(512, L)
    final = final_w is not None
    row = lambda width: pl.BlockSpec((tm, width), lambda i: (i, 0))
    vec = pl.BlockSpec((1, d), lambda i: (0, 0))
    in_specs = [row(d), row(GROUP_WIDTH), row(GROUP_WIDTH), row(GROUP_WIDTH), row(GROUP_WIDTH),
                pl.BlockSpec((tm, d), lambda i: (i, _col("z_a", d))),
                pl.BlockSpec(w_bf16.shape, lambda i: (0, 0)), vec]
    args = [x, *ys, proj, w_bf16, gate]
    if final:
        in_specs.append(vec)
        args.append(final_w)
    return pl.pallas_call(
        functools.partial(_outproj_kernel, final=final),
        grid=(L // tm,),
        in_specs=in_specs,
        out_specs=row(d),
        out_shape=jax.ShapeDtypeStruct((L, d), F32),
        compiler_params=pltpu.CompilerParams(
            dimension_semantics=("arbitrary",), vmem_limit_bytes=VMEM_LIMIT),
        name="outproj",
    )(*args)


def _rope_tables(L):
    t = np.arange(L)
    axis_dim = HEAD_DIM // 2
    inv = jnp.asarray(ROPE_THETA, F32) ** (-jnp.arange(0, axis_dim, 2, dtype=F32) / axis_dim)
    row = jnp.asarray(t // GRID_W, F32)
    col = jnp.asarray(t % GRID_W, F32)
    ang = jnp.concatenate([row[:, None] * inv[None], col[:, None] * inv[None]], axis=-1)
    cos = jnp.repeat(jnp.cos(ang), 2, axis=-1)
    sin = jnp.repeat(jnp.sin(ang), 2, axis=-1) * jnp.asarray(np.tile([-1.0, 1.0], HEAD_DIM // 2), F32)
    return jnp.tile(cos, (1, LANES // HEAD_DIM)), jnp.tile(sin, (1, LANES // HEAD_DIM))


A_TQ = 256
D_TQ = 256
D_REACH = D_CONFIGS[-1][0] // 2
C_TQ = 512
C_TK = 256
C_UNROLL = 8
C_AHEAD = 2


def _table_a(t5_table):
    return _t5_table(t5_table, 0, A_TQ, A_RADIUS, A_TQ + 2 * A_RADIUS, _window_mult)


def _table_d(t5_table):
    return _t5_table(t5_table, GROUP_HEADS, D_TQ, D_REACH, D_TQ + 2 * D_REACH, _dilated_mult)


def _seq_window(L, tq, pad):
    return lambda n: (n * tq, (n * tq - pad, 0, L))


def _mixer_a(proj, qt, tab_a, sink):
    L = proj.shape[0]
    k_pad, vt_pad = _kv_prep(proj, "k_a", "v_a", LANES, A_RADIUS, A_RADIUS)
    return _band_attention(qt, k_pad, vt_pad, tab_a, group=GROUP_HEADS // A_KV_HEADS,
                           window=_seq_window(L, A_TQ, A_RADIUS), sink=sink.astype(F32))


def _mixer_b(proj, qt, tab_b):
    L = proj.shape[0]
    rows = L // GRID_W
    n_tiles = rows // NA_TILE_ROWS
    k, vt = _kv_prep(proj, "k_b", "v_b", GROUP_WIDTH, 0, 512)
    window = lambda n: (jnp.clip(n * NA_TILE_ROWS - NA_ROWS // 2, 0, rows - NA_WIN_ROWS) * GRID_W, None)
    variant = lambda n: jnp.where(n == 0, 0, jnp.where(n == n_tiles - 1, 2, 1))
    return _band_attention(qt, k, vt, tab_b, group=1, window=window, variant=variant)


def _mixer_c(proj, q_norm_w, k_norm_w):
    L = proj.shape[0]
    cos2, sin2 = _rope_tables(L)
    ones_bd = jnp.asarray(np.kron(np.eye(GROUP_HEADS), np.ones((HEAD_DIM, HEAD_DIM))), F32)
    qt, k2, vt = _cprep(proj, cos2, sin2, jnp.tile(q_norm_w.astype(F32), GROUP_HEADS)[None],
                        jnp.tile(k_norm_w.astype(F32), LANES // HEAD_DIM)[None], ones_bd, C_TK)
    return _flash(qt, k2, vt, C_TQ)


def _mixer_d(proj, qt, tab_d):
    L = proj.shape[0]
    k_pad, vt_pad = _kv_prep(proj, "k_d", "v_d", GROUP_WIDTH, D_REACH, 512)
    return _band_attention(qt, k_pad, vt_pad, tab_d, group=1, window=_seq_window(L, D_TQ, D_REACH))


def kernel(x, c, w_ada, b_ada, norm_w, w_in, w_out, attn_sink, na_rpb, q_norm_w, k_norm_w,
           t5_table, final_norm_w):
    B, L, D = x.shape
    assert B == 1 and L % 1024 == 0 and L // GRID_W >= NA_WIN_ROWS
    depth = w_ada.shape[0]
    x = x[0]

    mod = _ada_mod(jnp.broadcast_to(c, (8, D)), w_ada, b_ada)[:, 0:1, :]
    tab_a = _table_a(t5_table)
    tab_d = _table_d(t5_table)
    for i in range(depth):
        shift, scale, gate = jnp.split(mod[i], 3, axis=-1)
        w_i = w_in[i][:, _PERM].astype(BF16)
        proj = _inproj(x, norm_w[i][None], scale, shift, w_i)
        qt_a, qt_b, qt_d = _qt_prep(proj, ("q_a", "q_b", "q_d"))
        ys = (_mixer_a(proj, qt_a, tab_a, attn_sink[i]),
              _mixer_b(proj, qt_b, _na_table(na_rpb[i])),
              _mixer_c(proj, q_norm_w[i], k_norm_w[i]),
              _mixer_d(proj, qt_d, tab_d))
        x = _outproj(x, ys, proj, w_out[i].astype(BF16), gate,
                     final_norm_w[None] if i == depth - 1 else None)
    return x[None]
```

```python
import functools
import math

import numpy as np
import jax
import jax.numpy as jnp
from jax import lax
from jax.experimental import pallas as pl
from jax.experimental.pallas import tpu as pltpu

HEAD_DIM = 64
GROUP_WIDTH = 512
GROUP_HEADS = 8
A_KV_HEADS = 2
A_RADIUS = 128
C_KV_HEADS = 2
ROPE_THETA = 10000.0
NA_ROWS = 8
NA_COLS = 16
D_CONFIGS = ((128, 1), (512, 4), (2048, 16))
GRID_W = 64
T5_BUCKETS = 32
T5_MAX_DIST = 1024
EPS = 1e-6
NEG = -1e30

V_ROWS = HEAD_DIM + 16
LANES = 128
VMEM_LIMIT = 56 * 1024 * 1024

F32 = jnp.float32
BF16 = jnp.bfloat16

_SRC = {}
_off = 0
for _name, _w in (("q_a", 512), ("k_a", 128), ("v_a", 128), ("z_a", 512),
                  ("q_b", 512), ("k_b", 512), ("v_b", 512), ("z_b", 512),
                  ("q_c", 512), ("k_c", 128), ("v_c", 128), ("z_c", 512),
                  ("q_d", 512), ("k_d", 512), ("v_d", 512), ("z_d", 512)):
    _SRC[_name] = (_off, _w)
    _off += _w
IN_WIDTH = _off
_ORDER = ("q_a", "q_b", "k_b", "v_b", "q_c", "q_d", "k_d", "v_d",
          "z_a", "z_b", "z_c", "z_d", "k_a", "v_a", "k_c", "v_c")
_DST = {}
_off = 0
for _name in _ORDER:
    _DST[_name] = _off
    _off += _SRC[_name][1]
_PERM = np.concatenate([np.arange(_SRC[n][0], _SRC[n][0] + _SRC[n][1]) for n in _ORDER])


def _col(name, width):
    assert _DST[name] % width == 0
    return _DST[name] // width


def _ada_kernel(c_ref, w_ref, b_ref, o_ref):
    c = c_ref[...]
    cond = c * jax.nn.sigmoid(c)
    o_ref[0] = jnp.dot(cond, w_ref[0], preferred_element_type=F32,
                       precision=lax.Precision.HIGHEST) + b_ref[0]


def _ada_mod(c8, w_ada, b_ada):
    depth, d, n3 = w_ada.shape
    tn = 1024
    return pl.pallas_call(
        _ada_kernel,
        grid=(depth, n3 // tn),
        in_specs=[pl.BlockSpec((8, d), lambda i, j: (0, 0)),
                  pl.BlockSpec((1, d, tn), lambda i, j: (i, 0, j)),
                  pl.BlockSpec((1, 1, tn), lambda i, j: (i, 0, j))],
        out_specs=pl.BlockSpec((1, 8, tn), lambda i, j: (i, 0, j)),
        out_shape=jax.ShapeDtypeStruct((depth, 8, n3), F32),
        compiler_params=pltpu.CompilerParams(
            dimension_semantics=("arbitrary", "arbitrary"), vmem_limit_bytes=VMEM_LIMIT),
        name="ada_mod",
    )(c8, w_ada, b_ada.reshape(depth, 1, n3))


def _inproj_kernel(x_ref, nw_ref, sc_ref, sh_ref, w_ref, o_ref, h_scr):
    @pl.when(pl.program_id(1) == 0)
    def _():
        x = x_ref[...]
        y = x * lax.rsqrt(jnp.mean(x * x, axis=-1, keepdims=True) + EPS)
        h = (y * nw_ref[...]) * (1.0 + sc_ref[...]) + sh_ref[...]
        h_scr[...] = h.astype(BF16)

    o_ref[...] = jnp.dot(h_scr[...], w_ref[...], preferred_element_type=F32).astype(o_ref.dtype)


def _inproj(x, nw, scale, shift, w_bf16):
    L, d = x.shape
    n = w_bf16.shape[1]
    tm = min(1024, L)
    tn = 512
    vec = pl.BlockSpec((1, d), lambda i, j: (0, 0))
    return pl.pallas_call(
        _inproj_kernel,
        grid=(L // tm, n // tn),
        in_specs=[pl.BlockSpec((tm, d), lambda i, j: (i, 0)), vec, vec, vec,
                  pl.BlockSpec((d, tn), lambda i, j: (0, j))],
        out_specs=pl.BlockSpec((tm, tn), lambda i, j: (i, j)),
        out_shape=jax.ShapeDtypeStruct((L, n), BF16),
        scratch_shapes=[pltpu.VMEM((tm, d), BF16)],
        compiler_params=pltpu.CompilerParams(
            dimension_semantics=("arbitrary", "arbitrary"), vmem_limit_bytes=VMEM_LIMIT),
        name="inproj",
    )(x, nw, scale, shift, w_bf16)


LOG2E = math.log2(math.e)
Q_SCALE = HEAD_DIM ** -0.5 * LOG2E


def _qt_prep_kernel(*refs):
    n = len(refs) // 2
    for q_ref, qt_ref in zip(refs[:n], refs[n:]):
        qt_ref[...] = (q_ref[...].astype(F32) * Q_SCALE).T.astype(qt_ref.dtype)


def _qt_prep(proj, names, t=1024):
    L = proj.shape[0]
    return pl.pallas_call(
        _qt_prep_kernel,
        grid=(L // t,),
        in_specs=[pl.BlockSpec((t, GROUP_WIDTH), functools.partial(lambda c, n: (n, c), _col(name, GROUP_WIDTH)))
                  for name in names],
        out_specs=[pl.BlockSpec((GROUP_WIDTH, t), lambda n: (0, n)) for _ in names],
        out_shape=[jax.ShapeDtypeStruct((GROUP_WIDTH, L), BF16) for _ in names],
        compiler_params=pltpu.CompilerParams(
            dimension_semantics=("arbitrary",), vmem_limit_bytes=VMEM_LIMIT),
        name="qt_prep",
    )(*([proj] * len(names)))


def _kv_prep_kernel(k_ref, v_ref, kp_ref, vt_ref, *, pad_blocks, tok_blocks):
    b = pl.program_id(0)
    is_token = (b >= pad_blocks) & (b < pad_blocks + tok_blocks)

    @pl.when(is_token)
    def _():
        kp_ref[...] = k_ref[...]
        vt = v_ref[...].astype(F32).T.astype(vt_ref.dtype)
        ones = jnp.ones((V_ROWS - HEAD_DIM, vt.shape[1]), vt_ref.dtype)
        for g in range(vt.shape[0] // HEAD_DIM):
            vt_ref[g * V_ROWS:(g + 1) * V_ROWS, :] = jnp.concatenate(
                [vt[g * HEAD_DIM:(g + 1) * HEAD_DIM], ones], axis=0)

    @pl.when(jnp.logical_not(is_token))
    def _():
        kp_ref[...] = jnp.zeros(kp_ref.shape, kp_ref.dtype)
        vt_ref[...] = jnp.zeros(vt_ref.shape, vt_ref.dtype)


def _kv_prep(proj, k_name, v_name, width, pad, t):
    L = proj.shape[0]
    assert pad % t == 0 and L % t == 0 and width % LANES == 0
    pad_blocks, tok_blocks = pad // t, L // t
    lp = L + 2 * pad
    kc, vc = _col(k_name, width), _col(v_name, width)
    tok = lambda b: jnp.clip(b - pad_blocks, 0, tok_blocks - 1)
    return pl.pallas_call(
        functools.partial(_kv_prep_kernel, pad_blocks=pad_blocks, tok_blocks=tok_blocks),
        grid=(lp // t,),
        in_specs=[pl.BlockSpec((t, width), lambda b: (tok(b), kc)),
                  pl.BlockSpec((t, width), lambda b: (tok(b), vc))],
        out_specs=[pl.BlockSpec((t, width), lambda b: (b, 0)),
                   pl.BlockSpec((width // HEAD_DIM * V_ROWS, t), lambda b: (0, b))],
        out_shape=[jax.ShapeDtypeStruct((lp, width), BF16),
                   jax.ShapeDtypeStruct((width // HEAD_DIM * V_ROWS, lp), BF16)],
        compiler_params=pltpu.CompilerParams(
            dimension_semantics=("arbitrary",), vmem_limit_bytes=VMEM_LIMIT),
        name="kv_prep",
    )(proj, proj)


BAND_CHUNK = 256


def _band_kernel(*refs, w_keys, heads, group, use_sink, window, variant):
    if use_sink:
        sink_ref, qt_ref, k_ref, vt_ref, tab_ref, o_ref = refs
    else:
        qt_ref, k_ref, vt_ref, tab_ref, o_ref = refs
    n = pl.program_id(0)
    _, pos = window(n)
    kwin = k_ref[...]
    vwin = vt_ref[...]
    var = variant(n)

    n_chunks = w_keys // BAND_CHUNK

    def attend(kbias):
        def scores(h, c):
            kv = h // group
            rows = slice(c * BAND_CHUNK, (c + 1) * BAND_CHUNK)
            st = jnp.dot(kwin[rows, kv * HEAD_DIM:(kv + 1) * HEAD_DIM],
                         qt_ref[h * HEAD_DIM:(h + 1) * HEAD_DIM, :],
                         preferred_element_type=F32) + tab_ref[var, h, rows, :]
            return st if kbias is None else st + kbias[rows]

        def col_max(m, st):
            cm = jnp.max(st, axis=0, keepdims=True)
            return cm if m is None else jnp.maximum(m, cm)

        outs = []
        cur, m = [], None
        for c in range(n_chunks):
            cur.append(scores(0, c))
            m = col_max(m, cur[-1])
        for h in range(heads):
            kv = h // group
            nxt, m_next, acc = [], None, None
            for c in range(n_chunks):
                if h + 1 < heads:
                    nxt.append(scores(h + 1, c))
                    m_next = col_max(m_next, nxt[-1])
                pt = jnp.exp2(cur[c] - m).astype(BF16)
                pv = jnp.dot(vwin[kv * V_ROWS:(kv + 1) * V_ROWS, c * BAND_CHUNK:(c + 1) * BAND_CHUNK], pt,
                             preferred_element_type=F32)
                acc = pv if acc is None else acc + pv
            o, l = acc[:HEAD_DIM], acc[HEAD_DIM:HEAD_DIM + 1]
            if use_sink:
                sk = sink_ref[h] * LOG2E
                m2 = jnp.maximum(m, sk)
                a = jnp.exp2(m - m2)
                o = o * (a / (l * a + jnp.exp2(sk - m2)))
            else:
                o = o / l
            outs.append(o)
            cur, m = nxt, m_next
        o_ref[...] = jnp.concatenate(outs, axis=0).T.astype(o_ref.dtype)

    if pos is None:
        attend(None)
    else:
        first, lo, hi = pos
        inside = (first >= lo) & (first + w_keys <= hi)

        @pl.when(inside)
        def _():
            attend(None)

        @pl.when(jnp.logical_not(inside))
        def _():
            kpos = first + lax.broadcasted_iota(jnp.int32, (w_keys, 1), 0)
            attend(jnp.where((kpos >= lo) & (kpos < hi), 0.0, NEG).astype(F32))


def _band_attention(qt, k_pad, vt_pad, table, *, group, window, variant=lambda n: 0, sink=None):
    qw, L = qt.shape
    heads = qw // HEAD_DIM
    _, _, w_keys, tq = table.shape
    kern = functools.partial(_band_kernel, w_keys=w_keys, heads=heads, group=group,
                             use_sink=sink is not None, window=window, variant=variant)
    start = lambda n: pl.multiple_of(window(n)[0], LANES)
    in_specs = [pl.BlockSpec((qw, tq), lambda n: (0, n)),
                pl.BlockSpec((pl.Element(w_keys), pl.Element(k_pad.shape[1])), lambda n: (start(n), 0)),
                pl.BlockSpec((pl.Element(vt_pad.shape[0]), pl.Element(w_keys)), lambda n: (0, start(n))),
                pl.BlockSpec(table.shape, lambda n: (0, 0, 0, 0), pipeline_mode=pl.Buffered(1))]
    args = [qt, k_pad, vt_pad, table]
    if sink is not None:
        in_specs = [pl.BlockSpec(memory_space=pltpu.SMEM)] + in_specs
        args = [sink] + args
    return pl.pallas_call(
        kern,
        grid=(L // tq,),
        in_specs=in_specs,
        out_specs=pl.BlockSpec((tq, qw), lambda n: (n, 0)),
        out_shape=jax.ShapeDtypeStruct((L, qw), BF16),
        compiler_params=pltpu.CompilerParams(
            dimension_semantics=("arbitrary",), vmem_limit_bytes=VMEM_LIMIT),
        name="band_attention",
    )(*args)


def _t5_bucket(rel):
    half = T5_BUCKETS // 2
    exact = half // 2
    n = jnp.abs(rel)
    big = exact + (jnp.log(jnp.maximum(n, exact).astype(F32) / exact)
                   / math.log(T5_MAX_DIST / exact) * (half - exact)).astype(jnp.int32)
    big = jnp.minimum(big, half - 1)
    return jnp.where(rel > 0, half, 0) + jnp.where(n < exact, n, big)


def _toeplitz_kernel(rv_ref, o_ref, *, n_diag):
    w, tq = o_ref.shape[1:]
    blocks = []
    for d in range(n_diag):
        s = LANES * (n_diag - 1 - d)
        x = jnp.broadcast_to(rv_ref[0, :, s:s + 2 * LANES], (LANES, 2 * LANES))
        blocks.append(pltpu.roll(x, LANES, 1, stride=1, stride_axis=0)[:, :LANES])
    for cb in range(w // LANES):
        for ib in range(tq // LANES):
            o_ref[0, cb * LANES:(cb + 1) * LANES, ib * LANES:(ib + 1) * LANES] = (
                blocks[cb - ib + tq // LANES - 1])


def _toeplitz(v, tq, w):
    heads, n_rel = v.shape
    n_diag = w // LANES + tq // LANES - 1
    assert w % LANES == 0 and tq % LANES == 0 and n_rel == LANES * n_diag + LANES - 1
    rv = jnp.pad(v, ((0, 0), (0, 1)))[:, None, ::-1]
    return pl.pallas_call(
        functools.partial(_toeplitz_kernel, n_diag=n_diag),
        grid=(heads,),
        in_specs=[pl.BlockSpec((1, 1, rv.shape[-1]), lambda h: (h, 0, 0))],
        out_specs=pl.BlockSpec((1, w, tq), lambda h: (h, 0, 0)),
        out_shape=jax.ShapeDtypeStruct((heads, w, tq), F32),
        compiler_params=pltpu.CompilerParams(
            dimension_semantics=("arbitrary",), vmem_limit_bytes=VMEM_LIMIT),
        name="toeplitz",
    )(rv)


def _t5_table(t5_table, head_lo, tq, r_lo, w_keys, mult):
    rel = np.arange(-r_lo - (tq - 1), w_keys - r_lo)
    m = mult(rel)
    b = t5_table.astype(F32)[:, head_lo:head_lo + GROUP_HEADS][_t5_bucket(jnp.asarray(rel, jnp.int32))]
    logm = np.log(np.maximum(m, 1)).astype(np.float32)
    vec = jnp.where(jnp.asarray(m > 0)[:, None], (b + logm[:, None]) * LOG2E, NEG).T
    return _toeplitz(vec, tq, w_keys)[None]


def _window_mult(rel):
    return (np.abs(rel) <= A_RADIUS).astype(np.int32)


def _dilated_mult(rel):
    m = np.zeros(rel.shape, np.int32)
    for window, dil in D_CONFIGS:
        m += ((rel % dil == 0) & (np.abs(rel) <= window // 2)).astype(np.int32)
    return m


NA_TILE_ROWS = 4
NA_WIN_ROWS = NA_TILE_ROWS + NA_ROWS
NA_TQ = NA_TILE_ROWS * GRID_W
NA_W = NA_WIN_ROWS * GRID_W


def _na_table(rpb):
    return _na_tables(rpb[None])[0]


def _na_row_index(variant, j, i):
    dr, ok = ((j - i, j < NA_ROWS),
              (j - NA_ROWS // 2 - i, i <= j < i + NA_ROWS),
              (j - NA_ROWS - i, j >= NA_TILE_ROWS))[variant]
    return dr + NA_ROWS - 1 if ok else None


def _na_table_kernel(rp_ref, o_ref):
    kc = lax.broadcasted_iota(jnp.int32, (GRID_W, LANES), 0)
    lane = lax.broadcasted_iota(jnp.int32, (GRID_W, LANES), 1)
    col_start = jnp.clip(lane % GRID_W - NA_COLS // 2, 0, GRID_W - NA_COLS)
    col_ok = (kc >= col_start) & (kc < col_start + NA_COLS)
    neg = jnp.full((GRID_W, LANES), NEG, F32)
    cache = {}

    def half_block(d, side):
        if d is None:
            return neg
        if (d, side) not in cache:
            x = jnp.broadcast_to(rp_ref[0, 0, d:d + 1, :], (GRID_W, LANES))
            y = pltpu.roll(x, GRID_W * (1 - side), 1, stride=1, stride_axis=0)
            cache[d, side] = jnp.where(col_ok, y, NEG)
        return cache[d, side]

    for variant in range(3):
        for j in range(NA_WIN_ROWS):
            for ip in range(NA_TILE_ROWS // 2):
                left = half_block(_na_row_index(variant, j, 2 * ip), 0)
                right = half_block(_na_row_index(variant, j, 2 * ip + 1), 1)
                o_ref[0, variant, 0, j * GRID_W:(j + 1) * GRID_W, ip * LANES:(ip + 1) * LANES] = (
                    jnp.where(lane < GRID_W, left, right))


def _na_tables(rpb):
    depth, heads, n_dr, n_dc = rpb.shape
    assert n_dr == NA_WIN_ROWS + NA_TILE_ROWS - 1 and 2 * GRID_W == LANES
    front = GRID_W - NA_COLS
    rp = jnp.pad(rpb.astype(F32) * LOG2E, ((0, 0), (0, 0), (0, 1), (front, LANES - n_dc - front)))
    rp = rp[..., ::-1]
    return pl.pallas_call(
        _na_table_kernel,
        grid=(depth, heads),
        in_specs=[pl.BlockSpec((1, 1, n_dr + 1, LANES), lambda l, h: (l, h, 0, 0))],
        out_specs=pl.BlockSpec((1, 3, 1, NA_W, NA_TQ), lambda l, h: (l, 0, h, 0, 0)),
        out_shape=jax.ShapeDtypeStruct((depth, 3, heads, NA_W, NA_TQ), F32),
        compiler_params=pltpu.CompilerParams(
            dimension_semantics=("arbitrary", "arbitrary"), vmem_limit_bytes=VMEM_LIMIT),
        name="na_tables",
    )(rp)


def _swap_pairs(x):
    n = x.shape[-1]
    lane = lax.broadcasted_iota(jnp.int32, x.shape, x.ndim - 1)
    return jnp.where(lane % 2 == 0, pltpu.roll(x, n - 1, x.ndim - 1), pltpu.roll(x, 1, x.ndim - 1))


def _head_rms(x, ones_bd, w):
    sq = x * x
    hi = sq.astype(BF16)
    lo = (sq - hi.astype(F32)).astype(BF16)
    ms = (jnp.dot(hi, ones_bd, preferred_element_type=F32)
          + jnp.dot(lo, ones_bd, preferred_element_type=F32)) * (1.0 / HEAD_DIM)
    return x * lax.rsqrt(ms + EPS) * w


def _cprep_kernel(q_ref, k_ref, v_ref, cos_ref, sin_ref, qw_ref, kw_ref, bd_ref,
                  qt_ref, k2_ref, vt_ref):
    cos = cos_ref[...]
    sin = sin_ref[...]
    q = _head_rms(q_ref[...].astype(F32), bd_ref[...], qw_ref[...])
    reps = q.shape[1] // LANES
    q = q * jnp.concatenate([cos] * reps, axis=1) + _swap_pairs(q) * jnp.concatenate([sin] * reps, axis=1)
    qt_ref[...] = (q * Q_SCALE).T.astype(qt_ref.dtype)
    k = _head_rms(k_ref[...].astype(F32), bd_ref[:LANES, :LANES], kw_ref[...])
    k = (k * cos + _swap_pairs(k) * sin).astype(k2_ref.dtype)
    vt = v_ref[...].astype(F32).T.astype(vt_ref.dtype)
    ones = jnp.ones((V_ROWS - HEAD_DIM, vt.shape[1]), vt_ref.dtype)
    tk = vt_ref.shape[-1]
    for g in range(C_KV_HEADS):
        k2_ref[g] = k[:, g * HEAD_DIM:(g + 1) * HEAD_DIM]
        vg = jnp.concatenate([vt[g * HEAD_DIM:(g + 1) * HEAD_DIM], ones], axis=0)
        for u in range(vt_ref.shape[1]):
            vt_ref[g, u] = vg[:, u * tk:(u + 1) * tk]


def _cprep(proj, cos2, sin2, qw, kw, ones_bd, tk, t=1024):
    L = proj.shape[0]
    vec = lambda width: pl.BlockSpec((1, width), lambda n: (0, 0))
    return pl.pallas_call(
        _cprep_kernel,
        grid=(L // t,),
        in_specs=[pl.BlockSpec((t, GROUP_WIDTH), lambda n: (n, _col("q_c", GROUP_WIDTH))),
                  pl.BlockSpec((t, LANES), lambda n: (n, _col("k_c", LANES))),
                  pl.BlockSpec((t, LANES), lambda n: (n, _col("v_c", LANES))),
                  pl.BlockSpec((t, LANES), lambda n: (n, 0)),
                  pl.BlockSpec((t, LANES), lambda n: (n, 0)),
                  vec(GROUP_WIDTH), vec(LANES),
                  pl.BlockSpec((GROUP_WIDTH, GROUP_WIDTH), lambda n: (0, 0))],
        out_specs=[pl.BlockSpec((GROUP_WIDTH, t), lambda n: (0, n)),
                   pl.BlockSpec((C_KV_HEADS, t, HEAD_DIM), lambda n: (0, n, 0)),
                   pl.BlockSpec((C_KV_HEADS, t // tk, V_ROWS, tk), lambda n: (0, n, 0, 0))],
        out_shape=[jax.ShapeDtypeStruct((GROUP_WIDTH, L), BF16),
                   jax.ShapeDtypeStruct((C_KV_HEADS, L, HEAD_DIM), BF16),
                   jax.ShapeDtypeStruct((C_KV_HEADS, L // tk, V_ROWS, tk), BF16)],
        compiler_params=pltpu.CompilerParams(
            dimension_semantics=("arbitrary",), vmem_limit_bytes=VMEM_LIMIT),
        name="dense_prep",
    )(proj, proj, proj, cos2, sin2, qw, kw, ones_bd)


def _flash_kernel(qt_ref, k_ref, vt_ref, o_ref, m_scr, acc_scr, st_scr, *, tk, heads, unroll, ahead):
    n_kv = k_ref.shape[1] // tk
    m_scr[...] = jnp.full(m_scr.shape, NEG, F32)
    acc_scr[...] = jnp.zeros(acc_scr.shape, F32)

    def scores(j, h):
        k = k_ref[0, pl.ds(pl.multiple_of(j * tk, tk), tk), :]
        qt = qt_ref[h * HEAD_DIM:(h + 1) * HEAD_DIM, :]
        return jnp.dot(k, qt, preferred_element_type=F32)

    n_items = unroll * heads

    def item_scores(t, idx):
        j = t * unroll + idx // heads
        return scores(jnp.minimum(j, n_kv - 1), idx % heads)

    for a in range(ahead):
        st_scr[a] = item_scores(0, a)

    def body(t, carry):
        pending = [st_scr[a] for a in range(ahead)]
        for idx in range(n_items):
            pending.append(item_scores(t, idx + ahead))
            st = pending.pop(0)
            h = idx % heads
            m = m_scr[h]
            m_new = jnp.maximum(m, jnp.max(st, axis=0, keepdims=True))
            pt = jnp.exp2(st - m_new).astype(BF16)
            acc_scr[h] = (jnp.exp2(m - m_new) * acc_scr[h]
                          + jnp.dot(vt_ref[0, t * unroll + idx // heads], pt, preferred_element_type=F32))
            m_scr[h] = m_new
        for a in range(ahead):
            st_scr[a] = pending[a]
        return carry

    lax.fori_loop(0, n_kv // unroll, body, 0)
    outs = [acc_scr[h, :HEAD_DIM] / acc_scr[h, HEAD_DIM:HEAD_DIM + 1] for h in range(heads)]
    o_ref[...] = jnp.concatenate(outs, axis=0).T.astype(o_ref.dtype)


def _flash(qt, k2, vt, tq):
    L = qt.shape[1]
    _, n_kv, v_rows, tk = vt.shape
    heads = GROUP_HEADS // C_KV_HEADS
    qrows = heads * HEAD_DIM
    return pl.pallas_call(
        functools.partial(_flash_kernel, tk=tk, heads=heads, unroll=C_UNROLL, ahead=C_AHEAD),
        grid=(C_KV_HEADS, L // tq),
        in_specs=[pl.BlockSpec((qrows, tq), lambda g, n: (g, n)),
                  pl.BlockSpec((1, L, HEAD_DIM), lambda g, n: (g, 0, 0)),
                  pl.BlockSpec((1, n_kv, v_rows, tk), lambda g, n: (g, 0, 0, 0))],
        out_specs=pl.BlockSpec((tq, qrows), lambda g, n: (n, g)),
        out_shape=jax.ShapeDtypeStruct((L, GROUP_WIDTH), BF16),
        scratch_shapes=[pltpu.VMEM((heads, 1, tq), F32), pltpu.VMEM((heads, v_rows, tq), F32),
                        pltpu.VMEM((C_AHEAD, tk, tq), F32)],
        compiler_params=pltpu.CompilerParams(
            dimension_semantics=("arbitrary", "arbitrary"), vmem_limit_bytes=VMEM_LIMIT),
        name="dense_flash",
    )(qt, k2, vt)


def _outproj_kernel(*refs, final):
    if final:
        x_ref, ya, yb, yc, yd, z_ref, w_ref, g_ref, fw_ref, o_ref = refs
    else:
        x_ref, ya, yb, yc, yd, z_ref, w_ref, g_ref, o_ref = refs
    acc = jnp.zeros(x_ref.shape, F32)
    for gi, y_ref in enumerate((ya, yb, yc, yd)):
        cols = slice(gi * GROUP_WIDTH, (gi + 1) * GROUP_WIDTH)
        z = z_ref[:, cols].astype(F32)
        u = (y_ref[...].astype(F32) * (z * jax.nn.sigmoid(z))).astype(BF16)
        acc = acc + jnp.dot(u, w_ref[cols, :], preferred_element_type=F32)
    xn = x_ref[...] + g_ref[...] * acc
    if final:
        xn = xn * lax.rsqrt(jnp.mean(xn * xn, axis=-1, keepdims=True) + EPS) * fw_ref[...]
    o_ref[...] = xn


def _outproj(x, ys, proj, w_bf16, gate, final_w):
    L, d = x.shape
    tm = min(512, L)
    final = final_w is not None
    row = lambda width: pl.BlockSpec((tm, width), lambda i: (i, 0))
    vec = pl.BlockSpec((1, d), lambda i: (0, 0))
    in_specs = [row(d), row(GROUP_WIDTH), row(GROUP_WIDTH), row(GROUP_WIDTH), row(GROUP_WIDTH),
                pl.BlockSpec((tm, d), lambda i: (i, _col("z_a", d))),
                pl.BlockSpec(w_bf16.shape, lambda i: (0, 0)), vec]
    args = [x, *ys, proj, w_bf16, gate]
    if final:
        in_specs.append(vec)
        args.append(final_w)
    return pl.pallas_call(
        functools.partial(_outproj_kernel, final=final),
        grid=(L // tm,),
        in_specs=in_specs,
        out_specs=row(d),
        out_shape=jax.ShapeDtypeStruct((L, d), F32),
        compiler_params=pltpu.CompilerParams(
            dimension_semantics=("arbitrary",), vmem_limit_bytes=VMEM_LIMIT),
        name="outproj",
    )(*args)


def _rope_tables(L):
    t = np.arange(L)
    axis_dim = HEAD_DIM // 2
    inv = jnp.asarray(ROPE_THETA, F32) ** (-jnp.arange(0, axis_dim, 2, dtype=F32) / axis_dim)
    row = jnp.asarray(t // GRID_W, F32)
    col = jnp.asarray(t % GRID_W, F32)
    ang = jnp.concatenate([row[:, None] * inv[None], col[:, None] * inv[None]], axis=-1)
    cos = jnp.repeat(jnp.cos(ang), 2, axis=-1)
    sin = jnp.repeat(jnp.sin(ang), 2, axis=-1) * jnp.asarray(np.tile([-1.0, 1.0], HEAD_DIM // 2), F32)
    return jnp.tile(cos, (1, LANES // HEAD_DIM)), jnp.tile(sin, (1, LANES // HEAD_DIM))


A_TQ = 256
D_TQ = 256
D_REACH = D_CONFIGS[-1][0] // 2
C_TQ = 512
C_TK = 256
C_UNROLL = 8
C_AHEAD = 2


def _table_a(t5_table):
    return _t5_table(t5_table, 0, A_TQ, A_RADIUS, A_TQ + 2 * A_RADIUS, _window_mult)


def _table_d(t5_table):
    return _t5_table(t5_table, GROUP_HEADS, D_TQ, D_REACH, D_TQ + 2 * D_REACH, _dilated_mult)


PREP_T = 1024
assert PREP_T >= D_REACH


def _seq_window(L, tq, r_lo):
    return lambda n: (n * tq + PREP_T - r_lo, (n * tq - r_lo, 0, L))


def _mixer_a(proj, qt, tab_a, sink):
    L = proj.shape[0]
    k_pad, vt_pad = _kv_prep(proj, "k_a", "v_a", LANES, PREP_T, PREP_T)
    return _band_attention(qt, k_pad, vt_pad, tab_a, group=GROUP_HEADS // A_KV_HEADS,
                           window=_seq_window(L, A_TQ, A_RADIUS), sink=sink.astype(F32))


def _mixer_b(proj, qt, tab_b):
    L = proj.shape[0]
    rows = L // GRID_W
    n_tiles = rows // NA_TILE_ROWS
    k, vt = _kv_prep(proj, "k_b", "v_b", GROUP_WIDTH, 0, PREP_T)
    window = lambda n: (jnp.clip(n * NA_TILE_ROWS - NA_ROWS // 2, 0, rows - NA_WIN_ROWS) * GRID_W, None)
    variant = lambda n: jnp.where(n == 0, 0, jnp.where(n == n_tiles - 1, 2, 1))
    return _band_attention(qt, k, vt, tab_b, group=1, window=window, variant=variant)


def _mixer_c(proj, q_norm_w, k_norm_w):
    L = proj.shape[0]
    cos2, sin2 = _rope_tables(L)
    ones_bd = jnp.asarray(np.kron(np.eye(GROUP_HEADS), np.ones((HEAD_DIM, HEAD_DIM))), BF16)
    qt, k2, vt = _cprep(proj, cos2, sin2, jnp.tile(q_norm_w.astype(F32), GROUP_HEADS)[None],
                        jnp.tile(k_norm_w.astype(F32), LANES // HEAD_DIM)[None], ones_bd, C_TK)
    return _flash(qt, k2, vt, C_TQ)


def _mixer_d(proj, qt, tab_d):
    L = proj.shape[0]
    k_pad, vt_pad = _kv_prep(proj, "k_d", "v_d", GROUP_WIDTH, PREP_T, PREP_T)
    return _band_attention(qt, k_pad, vt_pad, tab_d, group=1, window=_seq_window(L, D_TQ, D_REACH))


def kernel(x, c, w_ada, b_ada, norm_w, w_in, w_out, attn_sink, na_rpb, q_norm_w, k_norm_w,
           t5_table, final_norm_w):
    B, L, D = x.shape
    assert B == 1 and L % 1024 == 0 and L // GRID_W >= NA_WIN_ROWS
    depth = w_ada.shape[0]
    x = x[0]

    mod = _ada_mod(jnp.broadcast_to(c, (8, D)), w_ada, b_ada)[:, 0:1, :]
    tab_a = _table_a(t5_table)
    tab_d = _table_d(t5_table)
    tabs_b = _na_tables(na_rpb)
    for i in range(depth):
        shift, scale, gate = jnp.split(mod[i], 3, axis=-1)
        w_i = w_in[i][:, _PERM].astype(BF16)
        proj = _inproj(x, norm_w[i][None], scale, shift, w_i)
        qt_a, qt_b, qt_d = _qt_prep(proj, ("q_a", "q_b", "q_d"))
        ys = (_mixer_a(proj, qt_a, tab_a, attn_sink[i]),
              _mixer_b(proj, qt_b, tabs_b[i]),
              _mixer_c(proj, q_norm_w[i], k_norm_w[i]),
              _mixer_d(proj, qt_d, tab_d))
        x = _outproj(x, ys, proj, w_out[i].astype(BF16), gate,
                     final_norm_w[None] if i == depth - 1 else None)
    return x[None]
```

```python
import functools
import math

import numpy as np
import jax
import jax.numpy as jnp
from jax import lax
from jax.experimental import pallas as pl
from jax.experimental.pallas import tpu as pltpu

HEAD_DIM = 64
GROUP_WIDTH = 512
GROUP_HEADS = 8
A_KV_HEADS = 2
A_RADIUS = 128
C_KV_HEADS = 2
ROPE_THETA = 10000.0
NA_ROWS = 8
NA_COLS = 16
D_CONFIGS = ((128, 1), (512, 4), (2048, 16))
GRID_W = 64
T5_BUCKETS = 32
T5_MAX_DIST = 1024
EPS = 1e-6
NEG = -1e30

V_ROWS = HEAD_DIM + 16
LANES = 128
VMEM_LIMIT = 56 * 1024 * 1024

F32 = jnp.float32
BF16 = jnp.bfloat16

_SRC = {}
_off = 0
for _name, _w in (("q_a", 512), ("k_a", 128), ("v_a", 128), ("z_a", 512),
                  ("q_b", 512), ("k_b", 512), ("v_b", 512), ("z_b", 512),
                  ("q_c", 512), ("k_c", 128), ("v_c", 128), ("z_c", 512),
                  ("q_d", 512), ("k_d", 512), ("v_d", 512), ("z_d", 512)):
    _SRC[_name] = (_off, _w)
    _off += _w
IN_WIDTH = _off
_ORDER = ("q_a", "q_b", "k_b", "v_b", "q_c", "q_d", "k_d", "v_d",
          "z_a", "z_b", "z_c", "z_d", "k_a", "v_a", "k_c", "v_c")
_DST = {}
_off = 0
for _name in _ORDER:
    _DST[_name] = _off
    _off += _SRC[_name][1]


def _col(name, width):
    assert _DST[name] % width == 0
    return _DST[name] // width


def _ada_kernel(c_ref, w_ref, b_ref, o_ref):
    c = c_ref[...]
    cond = c * jax.nn.sigmoid(c)
    o_ref[0] = jnp.dot(cond, w_ref[0], preferred_element_type=F32,
                       precision=lax.Precision.HIGHEST) + b_ref[0]


def _ada_mod(c8, w_ada, b_ada):
    depth, d, n3 = w_ada.shape
    tn = 1024
    return pl.pallas_call(
        _ada_kernel,
        grid=(depth, n3 // tn),
        in_specs=[pl.BlockSpec((8, d), lambda i, j: (0, 0)),
                  pl.BlockSpec((1, d, tn), lambda i, j: (i, 0, j)),
                  pl.BlockSpec((1, 1, tn), lambda i, j: (i, 0, j))],
        out_specs=pl.BlockSpec((1, 8, tn), lambda i, j: (i, 0, j)),
        out_shape=jax.ShapeDtypeStruct((depth, 8, n3), F32),
        compiler_params=pltpu.CompilerParams(
            dimension_semantics=("arbitrary", "arbitrary"), vmem_limit_bytes=VMEM_LIMIT),
        name="ada_mod",
    )(c8, w_ada, b_ada.reshape(depth, 1, n3))


def _modulated_norm(x, nw, scale, shift):
    y = x * lax.rsqrt(jnp.mean(x * x, axis=-1, keepdims=True) + EPS)
    return ((y * nw) * (1.0 + scale) + shift).astype(BF16)


def _norm_kernel(x_ref, nw_ref, sc_ref, sh_ref, h_ref):
    h_ref[...] = _modulated_norm(x_ref[...], nw_ref[...], sc_ref[...], sh_ref[...])


def _norm(x, nw, scale, shift):
    L, d = x.shape
    tm = min(512, L)
    vec = pl.BlockSpec((1, d), lambda i: (0, 0))
    return pl.pallas_call(
        _norm_kernel,
        grid=(L // tm,),
        in_specs=[pl.BlockSpec((tm, d), lambda i: (i, 0)), vec, vec, vec],
        out_specs=pl.BlockSpec((tm, d), lambda i: (i, 0)),
        out_shape=jax.ShapeDtypeStruct((L, d), BF16),
        compiler_params=pltpu.CompilerParams(
            dimension_semantics=("arbitrary",), vmem_limit_bytes=VMEM_LIMIT),
        name="norm",
    )(x, nw, scale, shift)


def _inproj_kernel(h_ref, w_ref, o_ref):
    o_ref[...] = jnp.dot(h_ref[...], w_ref[...], preferred_element_type=F32).astype(o_ref.dtype)


def _inproj(h, w_bf16):
    L, d = h.shape
    n = w_bf16.shape[1]
    tm = min(1024, L)
    tn = 512
    return pl.pallas_call(
        _inproj_kernel,
        grid=(L // tm, n // tn),
        in_specs=[pl.BlockSpec((tm, d), lambda i, j: (i, 0)),
                  pl.BlockSpec((d, tn), lambda i, j: (0, j))],
        out_specs=pl.BlockSpec((tm, tn), lambda i, j: (i, j)),
        out_shape=jax.ShapeDtypeStruct((L, n), BF16),
        compiler_params=pltpu.CompilerParams(
            dimension_semantics=("arbitrary", "arbitrary"), vmem_limit_bytes=VMEM_LIMIT),
        name="inproj",
    )(h, w_bf16)


LOG2E = math.log2(math.e)
Q_SCALE = HEAD_DIM ** -0.5 * LOG2E


def _qt_prep_kernel(*refs):
    n = len(refs) // 2
    for q_ref, qt_ref in zip(refs[:n], refs[n:]):
        qt_ref[...] = (q_ref[...].astype(F32) * Q_SCALE).T.astype(qt_ref.dtype)


def _qt_prep(proj, names, t=1024):
    L = proj.shape[0]
    return pl.pallas_call(
        _qt_prep_kernel,
        grid=(L // t,),
        in_specs=[pl.BlockSpec((t, GROUP_WIDTH), functools.partial(lambda c, n: (n, c), _col(name, GROUP_WIDTH)))
                  for name in names],
        out_specs=[pl.BlockSpec((GROUP_WIDTH, t), lambda n: (0, n)) for _ in names],
        out_shape=[jax.ShapeDtypeStruct((GROUP_WIDTH, L), BF16) for _ in names],
        compiler_params=pltpu.CompilerParams(
            dimension_semantics=("arbitrary",), vmem_limit_bytes=VMEM_LIMIT),
        name="qt_prep",
    )(*([proj] * len(names)))


def _kv_prep_kernel(k_ref, v_ref, kp_ref, vt_ref, *, pad_blocks, tok_blocks):
    b = pl.program_id(0)
    is_token = (b >= pad_blocks) & (b < pad_blocks + tok_blocks)

    @pl.when(is_token)
    def _():
        kp_ref[...] = k_ref[...]
        vt = v_ref[...].astype(F32).T.astype(vt_ref.dtype)
        ones = jnp.ones((V_ROWS - HEAD_DIM, vt.shape[1]), vt_ref.dtype)
        for g in range(vt.shape[0] // HEAD_DIM):
            vt_ref[g * V_ROWS:(g + 1) * V_ROWS, :] = jnp.concatenate(
                [vt[g * HEAD_DIM:(g + 1) * HEAD_DIM], ones], axis=0)

    @pl.when(jnp.logical_not(is_token))
    def _():
        kp_ref[...] = jnp.zeros(kp_ref.shape, kp_ref.dtype)
        vt_ref[...] = jnp.zeros(vt_ref.shape, vt_ref.dtype)


def _kv_prep(proj, k_name, v_name, width, pad, t):
    L = proj.shape[0]
    assert pad % t == 0 and L % t == 0 and width % LANES == 0
    pad_blocks, tok_blocks = pad // t, L // t
    lp = L + 2 * pad
    kc, vc = _col(k_name, width), _col(v_name, width)
    tok = lambda b: jnp.clip(b - pad_blocks, 0, tok_blocks - 1)
    return pl.pallas_call(
        functools.partial(_kv_prep_kernel, pad_blocks=pad_blocks, tok_blocks=tok_blocks),
        grid=(lp // t,),
        in_specs=[pl.BlockSpec((t, width), lambda b: (tok(b), kc)),
                  pl.BlockSpec((t, width), lambda b: (tok(b), vc))],
        out_specs=[pl.BlockSpec((t, width), lambda b: (b, 0)),
                   pl.BlockSpec((width // HEAD_DIM * V_ROWS, t), lambda b: (0, b))],
        out_shape=[jax.ShapeDtypeStruct((lp, width), BF16),
                   jax.ShapeDtypeStruct((width // HEAD_DIM * V_ROWS, lp), BF16)],
        compiler_params=pltpu.CompilerParams(
            dimension_semantics=("arbitrary",), vmem_limit_bytes=VMEM_LIMIT),
        name="kv_prep",
    )(proj, proj)


BAND_CHUNK = 256


def _band_kernel(*refs, w_keys, heads, group, use_sink, window, variant):
    if use_sink:
        sink_ref, qt_ref, k_ref, vt_ref, tab_ref, o_ref = refs
    else:
        qt_ref, k_ref, vt_ref, tab_ref, o_ref = refs
    n = pl.program_id(0)
    _, pos = window(n)
    kwin = k_ref[...]
    vwin = vt_ref[...]
    var = variant(n)

    n_chunks = w_keys // BAND_CHUNK

    def attend(kbias):
        def scores(h, c):
            kv = h // group
            rows = slice(c * BAND_CHUNK, (c + 1) * BAND_CHUNK)
            st = jnp.dot(kwin[rows, kv * HEAD_DIM:(kv + 1) * HEAD_DIM],
                         qt_ref[h * HEAD_DIM:(h + 1) * HEAD_DIM, :],
                         preferred_element_type=F32) + tab_ref[var, h, rows, :]
            return st if kbias is None else st + kbias[rows]

        def col_max(m, st):
            cm = jnp.max(st, axis=0, keepdims=True)
            return cm if m is None else jnp.maximum(m, cm)

        outs = []
        cur, m = [], None
        for c in range(n_chunks):
            cur.append(scores(0, c))
            m = col_max(m, cur[-1])
        for h in range(heads):
            kv = h // group
            nxt, m_next, acc = [], None, None
            for c in range(n_chunks):
                if h + 1 < heads:
                    nxt.append(scores(h + 1, c))
                    m_next = col_max(m_next, nxt[-1])
                pt = jnp.exp2(cur[c] - m).astype(BF16)
                pv = jnp.dot(vwin[kv * V_ROWS:(kv + 1) * V_ROWS, c * BAND_CHUNK:(c + 1) * BAND_CHUNK], pt,
                             preferred_element_type=F32)
                acc = pv if acc is None else acc + pv
            o, l = acc[:HEAD_DIM], acc[HEAD_DIM:HEAD_DIM + 1]
            if use_sink:
                sk = sink_ref[h] * LOG2E
                m2 = jnp.maximum(m, sk)
                a = jnp.exp2(m - m2)
                o = o * (a / (l * a + jnp.exp2(sk - m2)))
            else:
                o = o / l
            outs.append(o)
            cur, m = nxt, m_next
        o_ref[...] = jnp.concatenate(outs, axis=0).T.astype(o_ref.dtype)

    if pos is None:
        attend(None)
    else:
        first, lo, hi = pos
        inside = (first >= lo) & (first + w_keys <= hi)

        @pl.when(inside)
        def _():
            attend(None)

        @pl.when(jnp.logical_not(inside))
        def _():
            kpos = first + lax.broadcasted_iota(jnp.int32, (w_keys, 1), 0)
            attend(jnp.where((kpos >= lo) & (kpos < hi), 0.0, NEG).astype(F32))


def _band_attention(qt, k_pad, vt_pad, table, *, group, window, variant=lambda n: 0, sink=None):
    qw, L = qt.shape
    heads = qw // HEAD_DIM
    _, _, w_keys, tq = table.shape
    kern = functools.partial(_band_kernel, w_keys=w_keys, heads=heads, group=group,
                             use_sink=sink is not None, window=window, variant=variant)
    start = lambda n: pl.multiple_of(window(n)[0], LANES)
    in_specs = [pl.BlockSpec((qw, tq), lambda n: (0, n)),
                pl.BlockSpec((pl.Element(w_keys), pl.Element(k_pad.shape[1])), lambda n: (start(n), 0)),
                pl.BlockSpec((pl.Element(vt_pad.shape[0]), pl.Element(w_keys)), lambda n: (0, start(n))),
                pl.BlockSpec(table.shape, lambda n: (0, 0, 0, 0), pipeline_mode=pl.Buffered(1))]
    args = [qt, k_pad, vt_pad, table]
    if sink is not None:
        in_specs = [pl.BlockSpec(memory_space=pltpu.SMEM)] + in_specs
        args = [sink] + args
    return pl.pallas_call(
        kern,
        grid=(L // tq,),
        in_specs=in_specs,
        out_specs=pl.BlockSpec((tq, qw), lambda n: (n, 0)),
        out_shape=jax.ShapeDtypeStruct((L, qw), BF16),
        compiler_params=pltpu.CompilerParams(
            dimension_semantics=("arbitrary",), vmem_limit_bytes=VMEM_LIMIT),
        name="band_attention",
    )(*args)


def _t5_bucket(rel):
    half = T5_BUCKETS // 2
    exact = half // 2
    n = jnp.abs(rel)
    big = exact + (jnp.log(jnp.maximum(n, exact).astype(F32) / exact)
                   / math.log(T5_MAX_DIST / exact) * (half - exact)).astype(jnp.int32)
    big = jnp.minimum(big, half - 1)
    return jnp.where(rel > 0, half, 0) + jnp.where(n < exact, n, big)


def _toeplitz_kernel(rv_ref, o_ref, *, n_diag):
    w, tq = o_ref.shape[1:]
    blocks = []
    for d in range(n_diag):
        s = LANES * (n_diag - 1 - d)
        x = jnp.broadcast_to(rv_ref[0, :, s:s + 2 * LANES], (LANES, 2 * LANES))
        blocks.append(pltpu.roll(x, LANES, 1, stride=1, stride_axis=0)[:, :LANES])
    for cb in range(w // LANES):
        for ib in range(tq // LANES):
            o_ref[0, cb * LANES:(cb + 1) * LANES, ib * LANES:(ib + 1) * LANES] = (
                blocks[cb - ib + tq // LANES - 1])


def _toeplitz(v, tq, w):
    heads, n_rel = v.shape
    n_diag = w // LANES + tq // LANES - 1
    assert w % LANES == 0 and tq % LANES == 0 and n_rel == LANES * n_diag + LANES - 1
    rv = jnp.pad(v, ((0, 0), (0, 1)))[:, None, ::-1]
    return pl.pallas_call(
        functools.partial(_toeplitz_kernel, n_diag=n_diag),
        grid=(heads,),
        in_specs=[pl.BlockSpec((1, 1, rv.shape[-1]), lambda h: (h, 0, 0))],
        out_specs=pl.BlockSpec((1, w, tq), lambda h: (h, 0, 0)),
        out_shape=jax.ShapeDtypeStruct((heads, w, tq), F32),
        compiler_params=pltpu.CompilerParams(
            dimension_semantics=("arbitrary",), vmem_limit_bytes=VMEM_LIMIT),
        name="toeplitz",
    )(rv)


def _t5_table(t5_table, head_lo, tq, r_lo, w_keys, mult):
    rel = np.arange(-r_lo - (tq - 1), w_keys - r_lo)
    m = mult(rel)
    b = t5_table.astype(F32)[:, head_lo:head_lo + GROUP_HEADS][_t5_bucket(jnp.asarray(rel, jnp.int32))]
    logm = np.log(np.maximum(m, 1)).astype(np.float32)
    vec = jnp.where(jnp.asarray(m > 0)[:, None], (b + logm[:, None]) * LOG2E, NEG).T
    return _toeplitz(vec, tq, w_keys)[None]


def _window_mult(rel):
    return (np.abs(rel) <= A_RADIUS).astype(np.int32)


def _dilated_mult(rel):
    m = np.zeros(rel.shape, np.int32)
    for window, dil in D_CONFIGS:
        m += ((rel % dil == 0) & (np.abs(rel) <= window // 2)).astype(np.int32)
    return m


NA_TILE_ROWS = 4
NA_WIN_ROWS = NA_TILE_ROWS + NA_ROWS
NA_TQ = NA_TILE_ROWS * GRID_W
NA_W = NA_WIN_ROWS * GRID_W


def _na_table(rpb):
    return _na_tables(rpb[None])[0]


def _na_row_index(variant, j, i):
    dr, ok = ((j - i, j < NA_ROWS),
              (j - NA_ROWS // 2 - i, i <= j < i + NA_ROWS),
              (j - NA_ROWS - i, j >= NA_TILE_ROWS))[variant]
    return dr + NA_ROWS - 1 if ok else None


def _na_table_kernel(rp_ref, o_ref):
    kc = lax.broadcasted_iota(jnp.int32, (GRID_W, LANES), 0)
    lane = lax.broadcasted_iota(jnp.int32, (GRID_W, LANES), 1)
    col_start = jnp.clip(lane % GRID_W - NA_COLS // 2, 0, GRID_W - NA_COLS)
    col_ok = (kc >= col_start) & (kc < col_start + NA_COLS)
    neg = jnp.full((GRID_W, LANES), NEG, F32)
    cache = {}

    def half_block(d, side):
        if d is None:
            return neg
        if (d, side) not in cache:
            x = jnp.broadcast_to(rp_ref[0, 0, d:d + 1, :], (GRID_W, LANES))
            y = pltpu.roll(x, GRID_W * (1 - side), 1, stride=1, stride_axis=0)
            cache[d, side] = jnp.where(col_ok, y, NEG)
        return cache[d, side]

    for variant in range(3):
        for j in range(NA_WIN_ROWS):
            for ip in range(NA_TILE_ROWS // 2):
                left = half_block(_na_row_index(variant, j, 2 * ip), 0)
                right = half_block(_na_row_index(variant, j, 2 * ip + 1), 1)
                o_ref[0, variant, 0, j * GRID_W:(j + 1) * GRID_W, ip * LANES:(ip + 1) * LANES] = (
                    jnp.where(lane < GRID_W, left, right))


def _na_tables(rpb):
    depth, heads, n_dr, n_dc = rpb.shape
    assert n_dr == NA_WIN_ROWS + NA_TILE_ROWS - 1 and 2 * GRID_W == LANES
    front = GRID_W - NA_COLS
    rp = jnp.pad(rpb.astype(F32) * LOG2E, ((0, 0), (0, 0), (0, 1), (front, LANES - n_dc - front)))
    rp = rp[..., ::-1]
    return pl.pallas_call(
        _na_table_kernel,
        grid=(depth, heads),
        in_specs=[pl.BlockSpec((1, 1, n_dr + 1, LANES), lambda l, h: (l, h, 0, 0))],
        out_specs=pl.BlockSpec((1, 3, 1, NA_W, NA_TQ), lambda l, h: (l, 0, h, 0, 0)),
        out_shape=jax.ShapeDtypeStruct((depth, 3, heads, NA_W, NA_TQ), F32),
        compiler_params=pltpu.CompilerParams(
            dimension_semantics=("arbitrary", "arbitrary"), vmem_limit_bytes=VMEM_LIMIT),
        name="na_tables",
    )(rp)


def _swap_pairs(x):
    n = x.shape[-1]
    lane = lax.broadcasted_iota(jnp.int32, x.shape, x.ndim - 1)
    return jnp.where(lane % 2 == 0, pltpu.roll(x, n - 1, x.ndim - 1), pltpu.roll(x, 1, x.ndim - 1))


def _head_rms(x, ones_bd, w):
    sq = x * x
    hi = sq.astype(BF16)
    lo = (sq - hi.astype(F32)).astype(BF16)
    ms = (jnp.dot(hi, ones_bd, preferred_element_type=F32)
          + jnp.dot(lo, ones_bd, preferred_element_type=F32)) * (1.0 / HEAD_DIM)
    return x * lax.rsqrt(ms + EPS) * w


def _cprep_kernel(q_ref, k_ref, v_ref, cos_ref, sin_ref, qw_ref, kw_ref, bd_ref,
                  qt_ref, k2_ref, vt_ref):
    cos = cos_ref[...]
    sin = sin_ref[...]
    q = _head_rms(q_ref[...].astype(F32), bd_ref[...], qw_ref[...])
    reps = q.shape[1] // LANES
    q = q * jnp.concatenate([cos] * reps, axis=1) + _swap_pairs(q) * jnp.concatenate([sin] * reps, axis=1)
    qt_ref[...] = (q * Q_SCALE).T.astype(qt_ref.dtype)
    k = _head_rms(k_ref[...].astype(F32), bd_ref[:LANES, :LANES], kw_ref[...])
    k = (k * cos + _swap_pairs(k) * sin).astype(k2_ref.dtype)
    vt = v_ref[...].astype(F32).T.astype(vt_ref.dtype)
    ones = jnp.ones((V_ROWS - HEAD_DIM, vt.shape[1]), vt_ref.dtype)
    tk = vt_ref.shape[-1]
    for g in range(C_KV_HEADS):
        k2_ref[g] = k[:, g * HEAD_DIM:(g + 1) * HEAD_DIM]
        vg = jnp.concatenate([vt[g * HEAD_DIM:(g + 1) * HEAD_DIM], ones], axis=0)
        for u in range(vt_ref.shape[1]):
            vt_ref[g, u] = vg[:, u * tk:(u + 1) * tk]


def _cprep(proj, cos2, sin2, qw, kw, ones_bd, tk, t=1024):
    L = proj.shape[0]
    vec = lambda width: pl.BlockSpec((1, width), lambda n: (0, 0))
    return pl.pallas_call(
        _cprep_kernel,
        grid=(L // t,),
        in_specs=[pl.BlockSpec((t, GROUP_WIDTH), lambda n: (n, _col("q_c", GROUP_WIDTH))),
                  pl.BlockSpec((t, LANES), lambda n: (n, _col("k_c", LANES))),
                  pl.BlockSpec((t, LANES), lambda n: (n, _col("v_c", LANES))),
                  pl.BlockSpec((t, LANES), lambda n: (n, 0)),
                  pl.BlockSpec((t, LANES), lambda n: (n, 0)),
                  vec(GROUP_WIDTH), vec(LANES),
                  pl.BlockSpec((GROUP_WIDTH, GROUP_WIDTH), lambda n: (0, 0))],
        out_specs=[pl.BlockSpec((GROUP_WIDTH, t), lambda n: (0, n)),
                   pl.BlockSpec((C_KV_HEADS, t, HEAD_DIM), lambda n: (0, n, 0)),
                   pl.BlockSpec((C_KV_HEADS, t // tk, V_ROWS, tk), lambda n: (0, n, 0, 0))],
        out_shape=[jax.ShapeDtypeStruct((GROUP_WIDTH, L), BF16),
                   jax.ShapeDtypeStruct((C_KV_HEADS, L, HEAD_DIM), BF16),
                   jax.ShapeDtypeStruct((C_KV_HEADS, L // tk, V_ROWS, tk), BF16)],
        compiler_params=pltpu.CompilerParams(
            dimension_semantics=("arbitrary",), vmem_limit_bytes=VMEM_LIMIT),
        name="dense_prep",
    )(proj, proj, proj, cos2, sin2, qw, kw, ones_bd)


def _flash_kernel(qt_ref, k_ref, vt_ref, o_ref, m_scr, acc_scr, st_scr, *, tk, heads, unroll, ahead):
    n_kv = k_ref.shape[1] // tk
    m_scr[...] = jnp.full(m_scr.shape, NEG, F32)
    acc_scr[...] = jnp.zeros(acc_scr.shape, F32)

    def scores(j, h):
        k = k_ref[0, pl.ds(pl.multiple_of(j * tk, tk), tk), :]
        qt = qt_ref[h * HEAD_DIM:(h + 1) * HEAD_DIM, :]
        return jnp.dot(k, qt, preferred_element_type=F32)

    n_items = unroll * heads

    def item_scores(t, idx):
        j = t * unroll + idx // heads
        return scores(jnp.minimum(j, n_kv - 1), idx % heads)

    for a in range(ahead):
        st_scr[a] = item_scores(0, a)

    def body(t, carry):
        pending = [st_scr[a] for a in range(ahead)]
        for idx in range(n_items):
            pending.append(item_scores(t, idx + ahead))
            st = pending.pop(0)
            h = idx % heads
            m = m_scr[h]
            m_new = jnp.maximum(m, jnp.max(st, axis=0, keepdims=True))
            pt = jnp.exp2(st - m_new).astype(BF16)
            acc_scr[h] = (jnp.exp2(m - m_new) * acc_scr[h]
                          + jnp.dot(vt_ref[0, t * unroll + idx // heads], pt, preferred_element_type=F32))
            m_scr[h] = m_new
        for a in range(ahead):
            st_scr[a] = pending[a]
        return carry

    lax.fori_loop(0, n_kv // unroll, body, 0)
    outs = [acc_scr[h, :HEAD_DIM] / acc_scr[h, HEAD_DIM:HEAD_DIM + 1] for h in range(heads)]
    o_ref[...] = jnp.concatenate(outs, axis=0).T.astype(o_ref.dtype)


def _flash(qt, k2, vt, tq):
    L = qt.shape[1]
    _, n_kv, v_rows, tk = vt.shape
    heads = GROUP_HEADS // C_KV_HEADS
    qrows = heads * HEAD_DIM
    return pl.pallas_call(
        functools.partial(_flash_kernel, tk=tk, heads=heads, unroll=C_UNROLL, ahead=C_AHEAD),
        grid=(C_KV_HEADS, L // tq),
        in_specs=[pl.BlockSpec((qrows, tq), lambda g, n: (g, n)),
                  pl.BlockSpec((1, L, HEAD_DIM), lambda g, n: (g, 0, 0)),
                  pl.BlockSpec((1, n_kv, v_rows, tk), lambda g, n: (g, 0, 0, 0))],
        out_specs=pl.BlockSpec((tq, qrows), lambda g, n: (n, g)),
        out_shape=jax.ShapeDtypeStruct((L, GROUP_WIDTH), BF16),
        scratch_shapes=[pltpu.VMEM((heads, 1, tq), F32), pltpu.VMEM((heads, v_rows, tq), F32),
                        pltpu.VMEM((C_AHEAD, tk, tq), F32)],
        compiler_params=pltpu.CompilerParams(
            dimension_semantics=("arbitrary", "arbitrary"), vmem_limit_bytes=VMEM_LIMIT),
        name="dense_flash",
    )(qt, k2, vt)


def _outproj_kernel(*refs, final):
    if final:
        x_ref, ya, yb, yc, yd, z_ref, w_ref, g_ref, fw_ref, o_ref = refs
    else:
        x_ref, ya, yb, yc, yd, z_ref, w_ref, g_ref, nw_ref, sc_ref, sh_ref, o_ref, h_ref = refs
    acc = jnp.zeros(x_ref.shape, F32)
    for gi, y_ref in enumerate((ya, yb, yc, yd)):
        cols = slice(gi * GROUP_WIDTH, (gi + 1) * GROUP_WIDTH)
        z = z_ref[:, cols].astype(F32)
        u = (y_ref[...].astype(F32) * (z * jax.nn.sigmoid(z))).astype(BF16)
        acc = acc + jnp.dot(u, w_ref[cols, :], preferred_element_type=F32)
    xn = x_ref[...] + g_ref[...] * acc
    if final:
        o_ref[...] = xn * lax.rsqrt(jnp.mean(xn * xn, axis=-1, keepdims=True) + EPS) * fw_ref[...]
    else:
        o_ref[...] = xn
        h_ref[...] = _modulated_norm(xn, nw_ref[...], sc_ref[...], sh_ref[...])


def _outproj(x, ys, proj, w_bf16, gate, vecs):
    L, d = x.shape
    tm = min(512, L)
    final = len(vecs) == 1
    row = lambda width: pl.BlockSpec((tm, width), lambda i: (i, 0))
    vec = pl.BlockSpec((1, d), lambda i: (0, 0))
    in_specs = [row(d), row(GROUP_WIDTH), row(GROUP_WIDTH), row(GROUP_WIDTH), row(GROUP_WIDTH),
                pl.BlockSpec((tm, d), lambda i: (i, _col("z_a", d))),
                pl.BlockSpec(w_bf16.shape, lambda i: (0, 0)), vec] + [vec] * len(vecs)
    x_out = jax.ShapeDtypeStruct((L, d), F32)
    return pl.pallas_call(
        functools.partial(_outproj_kernel, final=final),
        grid=(L // tm,),
        in_specs=in_specs,
        out_specs=row(d) if final else [row(d), row(d)],
        out_shape=x_out if final else [x_out, jax.ShapeDtypeStruct((L, d), BF16)],
        compiler_params=pltpu.CompilerParams(
            dimension_semantics=("arbitrary",), vmem_limit_bytes=VMEM_LIMIT),
        name="outproj",
    )(x, *ys, proj, w_bf16, gate, *vecs)


def _rope_tables(L):
    t = np.arange(L)
    axis_dim = HEAD_DIM // 2
    inv = jnp.asarray(ROPE_THETA, F32) ** (-jnp.arange(0, axis_dim, 2, dtype=F32) / axis_dim)
    row = jnp.asarray(t // GRID_W, F32)
    col = jnp.asarray(t % GRID_W, F32)
    ang = jnp.concatenate([row[:, None] * inv[None], col[:, None] * inv[None]], axis=-1)
    cos = jnp.repeat(jnp.cos(ang), 2, axis=-1)
    sin = jnp.repeat(jnp.sin(ang), 2, axis=-1) * jnp.asarray(np.tile([-1.0, 1.0], HEAD_DIM // 2), F32)
    return jnp.tile(cos, (1, LANES // HEAD_DIM)), jnp.tile(sin, (1, LANES // HEAD_DIM))


A_TQ = 256
D_TQ = 256
D_REACH = D_CONFIGS[-1][0] // 2
C_TQ = 512
C_TK = 256
C_UNROLL = 8
C_AHEAD = 2


def _table_a(t5_table):
    return _t5_table(t5_table, 0, A_TQ, A_RADIUS, A_TQ + 2 * A_RADIUS, _window_mult)


def _table_d(t5_table):
    return _t5_table(t5_table, GROUP_HEADS, D_TQ, D_REACH, D_TQ + 2 * D_REACH, _dilated_mult)


PREP_T = 1024
assert PREP_T >= D_REACH


def _seq_window(L, tq, r_lo):
    return lambda n: (n * tq + PREP_T - r_lo, (n * tq - r_lo, 0, L))


def _mixer_a(proj, qt, tab_a, sink):
    L = proj.shape[0]
    k_pad, vt_pad = _kv_prep(proj, "k_a", "v_a", LANES, PREP_T, PREP_T)
    return _band_attention(qt, k_pad, vt_pad, tab_a, group=GROUP_HEADS // A_KV_HEADS,
                           window=_seq_window(L, A_TQ, A_RADIUS), sink=sink.astype(F32))


def _mixer_b(proj, qt, tab_b):
    L = proj.shape[0]
    rows = L // GRID_W
    n_tiles = rows // NA_TILE_ROWS
    k, vt = _kv_prep(proj, "k_b", "v_b", GROUP_WIDTH, 0, PREP_T)
    window = lambda n: (jnp.clip(n * NA_TILE_ROWS - NA_ROWS // 2, 0, rows - NA_WIN_ROWS) * GRID_W, None)
    variant = lambda n: jnp.where(n == 0, 0, jnp.where(n == n_tiles - 1, 2, 1))
    return _band_attention(qt, k, vt, tab_b, group=1, window=window, variant=variant)


def _mixer_c(proj, q_norm_w, k_norm_w):
    L = proj.shape[0]
    cos2, sin2 = _rope_tables(L)
    ones_bd = jnp.asarray(np.kron(np.eye(GROUP_HEADS), np.ones((HEAD_DIM, HEAD_DIM))), BF16)
    qt, k2, vt = _cprep(proj, cos2, sin2, jnp.tile(q_norm_w.astype(F32), GROUP_HEADS)[None],
                        jnp.tile(k_norm_w.astype(F32), LANES // HEAD_DIM)[None], ones_bd, C_TK)
    return _flash(qt, k2, vt, C_TQ)


def _mixer_d(proj, qt, tab_d):
    L = proj.shape[0]
    k_pad, vt_pad = _kv_prep(proj, "k_d", "v_d", GROUP_WIDTH, PREP_T, PREP_T)
    return _band_attention(qt, k_pad, vt_pad, tab_d, group=1, window=_seq_window(L, D_TQ, D_REACH))


def kernel(x, c, w_ada, b_ada, norm_w, w_in, w_out, attn_sink, na_rpb, q_norm_w, k_norm_w,
           t5_table, final_norm_w):
    B, L, D = x.shape
    assert B == 1 and L % 1024 == 0 and L // GRID_W >= NA_WIN_ROWS
    depth = w_ada.shape[0]
    x = x[0]

    mod = _ada_mod(jnp.broadcast_to(c, (8, D)), w_ada, b_ada)[:, 0:1, :]
    tab_a = _table_a(t5_table)
    tab_d = _table_d(t5_table)
    tabs_b = _na_tables(na_rpb)
    w_in_b = jnp.concatenate([w_in[:, :, _SRC[n][0]:_SRC[n][0] + _SRC[n][1]] for n in _ORDER],
                             axis=-1).astype(BF16)
    w_out_b = w_out.astype(BF16)
    shift, scale, gate = jnp.split(mod, 3, axis=-1)
    h = _norm(x, norm_w[0][None], scale[0], shift[0])
    for i in range(depth):
        proj = _inproj(h, w_in_b[i])
        qt_a, qt_b, qt_d = _qt_prep(proj, ("q_a", "q_b", "q_d"))
        ys = (_mixer_a(proj, qt_a, tab_a, attn_sink[i]),
              _mixer_b(proj, qt_b, tabs_b[i]),
              _mixer_c(proj, q_norm_w[i], k_norm_w[i]),
              _mixer_d(proj, qt_d, tab_d))
        if i + 1 < depth:
            x, h = _outproj(x, ys, proj, w_out_b[i], gate[i],
                            (norm_w[i + 1][None], scale[i + 1], shift[i + 1]))
        else:
            x = _outproj(x, ys, proj, w_out_b[i], gate[i], (final_norm_w[None],))
    return x[None]
```

```python
import functools
import math

import numpy as np
import jax
import jax.numpy as jnp
from jax import lax
from jax.experimental import pallas as pl
from jax.experimental.pallas import tpu as pltpu

HEAD_DIM = 64
GROUP_WIDTH = 512
GROUP_HEADS = 8
A_KV_HEADS = 2
A_RADIUS = 128
C_KV_HEADS = 2
ROPE_THETA = 10000.0
NA_ROWS = 8
NA_COLS = 16
D_CONFIGS = ((128, 1), (512, 4), (2048, 16))
GRID_W = 64
T5_BUCKETS = 32
T5_MAX_DIST = 1024
EPS = 1e-6
NEG = -1e30

V_ROWS = HEAD_DIM + 16
LANES = 128
VMEM_LIMIT = 56 * 1024 * 1024

F32 = jnp.float32
BF16 = jnp.bfloat16

_SRC = {}
_off = 0
for _name, _w in (("q_a", 512), ("k_a", 128), ("v_a", 128), ("z_a", 512),
                  ("q_b", 512), ("k_b", 512), ("v_b", 512), ("z_b", 512),
                  ("q_c", 512), ("k_c", 128), ("v_c", 128), ("z_c", 512),
                  ("q_d", 512), ("k_d", 512), ("v_d", 512), ("z_d", 512)):
    _SRC[_name] = (_off, _w)
    _off += _w
IN_WIDTH = _off


def _piece(name, t, row_block=lambda n: n):
    off, width = _SRC[name]
    return pl.BlockSpec((pl.Element(t), pl.Element(width)),
                        lambda n: (pl.multiple_of(row_block(n) * t, t), off))


def _ada_kernel(c_ref, w_ref, b_ref, o_ref):
    c = c_ref[...]
    cond = c * jax.nn.sigmoid(c)
    o_ref[0] = jnp.dot(cond, w_ref[0], preferred_element_type=F32,
                       precision=lax.Precision.HIGHEST) + b_ref[0]


def _ada_mod(c8, w_ada, b_ada):
    depth, d, n3 = w_ada.shape
    tn = 1024
    return pl.pallas_call(
        _ada_kernel,
        grid=(depth, n3 // tn),
        in_specs=[pl.BlockSpec((8, d), lambda i, j: (0, 0)),
                  pl.BlockSpec((1, d, tn), lambda i, j: (i, 0, j)),
                  pl.BlockSpec((1, 1, tn), lambda i, j: (i, 0, j))],
        out_specs=pl.BlockSpec((1, 8, tn), lambda i, j: (i, 0, j)),
        out_shape=jax.ShapeDtypeStruct((depth, 8, n3), F32),
        compiler_params=pltpu.CompilerParams(
            dimension_semantics=("arbitrary", "arbitrary"), vmem_limit_bytes=VMEM_LIMIT),
        name="ada_mod",
    )(c8, w_ada, b_ada.reshape(depth, 1, n3))


def _modulated_norm(x, nw, scale, shift):
    y = x * lax.rsqrt(jnp.mean(x * x, axis=-1, keepdims=True) + EPS)
    return ((y * nw) * (1.0 + scale) + shift).astype(BF16)


def _norm_kernel(x_ref, nw_ref, sc_ref, sh_ref, h_ref):
    h_ref[...] = _modulated_norm(x_ref[...], nw_ref[...], sc_ref[...], sh_ref[...])


def _norm(x, nw, scale, shift):
    L, d = x.shape
    tm = min(512, L)
    vec = pl.BlockSpec((1, d), lambda i: (0, 0))
    return pl.pallas_call(
        _norm_kernel,
        grid=(L // tm,),
        in_specs=[pl.BlockSpec((tm, d), lambda i: (i, 0)), vec, vec, vec],
        out_specs=pl.BlockSpec((tm, d), lambda i: (i, 0)),
        out_shape=jax.ShapeDtypeStruct((L, d), BF16),
        compiler_params=pltpu.CompilerParams(
            dimension_semantics=("arbitrary",), vmem_limit_bytes=VMEM_LIMIT),
        name="norm",
    )(x, nw, scale, shift)


def _inproj_kernel(h_ref, w_ref, o_ref):
    o_ref[...] = jnp.dot(h_ref[...], w_ref[...], preferred_element_type=F32).astype(o_ref.dtype)


def _inproj(h, w_bf16):
    L, d = h.shape
    n = w_bf16.shape[1]
    tm = min(1024, L)
    tn = 512
    return pl.pallas_call(
        _inproj_kernel,
        grid=(L // tm, n // tn),
        in_specs=[pl.BlockSpec((tm, d), lambda i, j: (i, 0)),
                  pl.BlockSpec((d, tn), lambda i, j: (0, j))],
        out_specs=pl.BlockSpec((tm, tn), lambda i, j: (i, j)),
        out_shape=jax.ShapeDtypeStruct((L, n), BF16),
        compiler_params=pltpu.CompilerParams(
            dimension_semantics=("arbitrary", "arbitrary"), vmem_limit_bytes=VMEM_LIMIT),
        name="inproj",
    )(h, w_bf16)


LOG2E = math.log2(math.e)
Q_SCALE = HEAD_DIM ** -0.5 * LOG2E


def _qt_prep_kernel(*refs):
    n = len(refs) // 2
    for q_ref, qt_ref in zip(refs[:n], refs[n:]):
        qt_ref[...] = (q_ref[...].astype(F32) * Q_SCALE).T.astype(qt_ref.dtype)


def _qt_prep(proj, names, t=1024):
    L = proj.shape[0]
    return pl.pallas_call(
        _qt_prep_kernel,
        grid=(L // t,),
        in_specs=[_piece(name, t) for name in names],
        out_specs=[pl.BlockSpec((GROUP_WIDTH, t), lambda n: (0, n)) for _ in names],
        out_shape=[jax.ShapeDtypeStruct((GROUP_WIDTH, L), BF16) for _ in names],
        compiler_params=pltpu.CompilerParams(
            dimension_semantics=("arbitrary",), vmem_limit_bytes=VMEM_LIMIT),
        name="qt_prep",
    )(*([proj] * len(names)))


def _kv_prep_kernel(k_ref, v_ref, kp_ref, vt_ref, *, pad_blocks, tok_blocks):
    b = pl.program_id(0)
    is_token = (b >= pad_blocks) & (b < pad_blocks + tok_blocks)

    @pl.when(is_token)
    def _():
        kp_ref[...] = k_ref[...]
        vt = v_ref[...].astype(F32).T.astype(vt_ref.dtype)
        ones = jnp.ones((V_ROWS - HEAD_DIM, vt.shape[1]), vt_ref.dtype)
        for g in range(vt.shape[0] // HEAD_DIM):
            vt_ref[g * V_ROWS:(g + 1) * V_ROWS, :] = jnp.concatenate(
                [vt[g * HEAD_DIM:(g + 1) * HEAD_DIM], ones], axis=0)

    @pl.when(jnp.logical_not(is_token))
    def _():
        kp_ref[...] = jnp.zeros(kp_ref.shape, kp_ref.dtype)
        vt_ref[...] = jnp.zeros(vt_ref.shape, vt_ref.dtype)


def _kv_prep(proj, k_name, v_name, width, pad, t):
    L = proj.shape[0]
    assert pad % t == 0 and L % t == 0 and width % LANES == 0
    pad_blocks, tok_blocks = pad // t, L // t
    lp = L + 2 * pad
    assert _SRC[k_name][1] == width and _SRC[v_name][1] == width
    tok = lambda b: jnp.clip(b - pad_blocks, 0, tok_blocks - 1)
    return pl.pallas_call(
        functools.partial(_kv_prep_kernel, pad_blocks=pad_blocks, tok_blocks=tok_blocks),
        grid=(lp // t,),
        in_specs=[_piece(k_name, t, tok), _piece(v_name, t, tok)],
        out_specs=[pl.BlockSpec((t, width), lambda b: (b, 0)),
                   pl.BlockSpec((width // HEAD_DIM * V_ROWS, t), lambda b: (0, b))],
        out_shape=[jax.ShapeDtypeStruct((lp, width), BF16),
                   jax.ShapeDtypeStruct((width // HEAD_DIM * V_ROWS, lp), BF16)],
        compiler_params=pltpu.CompilerParams(
            dimension_semantics=("arbitrary",), vmem_limit_bytes=VMEM_LIMIT),
        name="kv_prep",
    )(proj, proj)


BAND_CHUNK = 256


def _band_kernel(*refs, w_keys, heads, group, use_sink, window, variant):
    if use_sink:
        sink_ref, qt_ref, k_ref, vt_ref, tab_ref, o_ref = refs
    else:
        qt_ref, k_ref, vt_ref, tab_ref, o_ref = refs
    n = pl.program_id(0)
    _, pos = window(n)
    kwin = k_ref[...]
    vwin = vt_ref[...]
    var = variant(n)

    n_chunks = w_keys // BAND_CHUNK

    def attend(kbias):
        def scores(h, c):
            kv = h // group
            rows = slice(c * BAND_CHUNK, (c + 1) * BAND_CHUNK)
            st = jnp.dot(kwin[rows, kv * HEAD_DIM:(kv + 1) * HEAD_DIM],
                         qt_ref[h * HEAD_DIM:(h + 1) * HEAD_DIM, :],
                         preferred_element_type=F32) + tab_ref[var, h, rows, :]
            return st if kbias is None else st + kbias[rows]

        def col_max(m, st):
            cm = jnp.max(st, axis=0, keepdims=True)
            return cm if m is None else jnp.maximum(m, cm)

        outs = []
        cur, m = [], None
        for c in range(n_chunks):
            cur.append(scores(0, c))
            m = col_max(m, cur[-1])
        for h in range(heads):
            kv = h // group
            nxt, m_next, acc = [], None, None
            for c in range(n_chunks):
                if h + 1 < heads:
                    nxt.append(scores(h + 1, c))
                    m_next = col_max(m_next, nxt[-1])
                pt = jnp.exp2(cur[c] - m).astype(BF16)
                pv = jnp.dot(vwin[kv * V_ROWS:(kv + 1) * V_ROWS, c * BAND_CHUNK:(c + 1) * BAND_CHUNK], pt,
                             preferred_element_type=F32)
                acc = pv if acc is None else acc + pv
            o, l = acc[:HEAD_DIM], acc[HEAD_DIM:HEAD_DIM + 1]
            if use_sink:
                sk = sink_ref[h] * LOG2E
                m2 = jnp.maximum(m, sk)
                a = jnp.exp2(m - m2)
                o = o * (a / (l * a + jnp.exp2(sk - m2)))
            else:
                o = o / l
            outs.append(o)
            cur, m = nxt, m_next
        o_ref[...] = jnp.concatenate(outs, axis=0).T.astype(o_ref.dtype)

    if pos is None:
        attend(None)
    else:
        first, lo, hi = pos
        inside = (first >= lo) & (first + w_keys <= hi)

        @pl.when(inside)
        def _():
            attend(None)

        @pl.when(jnp.logical_not(inside))
        def _():
            kpos = first + lax.broadcasted_iota(jnp.int32, (w_keys, 1), 0)
            attend(jnp.where((kpos >= lo) & (kpos < hi), 0.0, NEG).astype(F32))


def _band_attention(qt, k_pad, vt_pad, table, *, group, window, variant=lambda n: 0, sink=None):
    qw, L = qt.shape
    heads = qw // HEAD_DIM
    _, _, w_keys, tq = table.shape
    kern = functools.partial(_band_kernel, w_keys=w_keys, heads=heads, group=group,
                             use_sink=sink is not None, window=window, variant=variant)
    start = lambda n: pl.multiple_of(window(n)[0], LANES)
    in_specs = [pl.BlockSpec((qw, tq), lambda n: (0, n)),
                pl.BlockSpec((pl.Element(w_keys), pl.Element(k_pad.shape[1])), lambda n: (start(n), 0)),
                pl.BlockSpec((pl.Element(vt_pad.shape[0]), pl.Element(w_keys)), lambda n: (0, start(n))),
                pl.BlockSpec(table.shape, lambda n: (0, 0, 0, 0), pipeline_mode=pl.Buffered(1))]
    args = [qt, k_pad, vt_pad, table]
    if sink is not None:
        in_specs = [pl.BlockSpec(memory_space=pltpu.SMEM)] + in_specs
        args = [sink] + args
    return pl.pallas_call(
        kern,
        grid=(L // tq,),
        in_specs=in_specs,
        out_specs=pl.BlockSpec((tq, qw), lambda n: (n, 0)),
        out_shape=jax.ShapeDtypeStruct((L, qw), BF16),
        compiler_params=pltpu.CompilerParams(
            dimension_semantics=("arbitrary",), vmem_limit_bytes=VMEM_LIMIT),
        name="band_attention",
    )(*args)


def _t5_bucket(rel):
    half = T5_BUCKETS // 2
    exact = half // 2
    n = jnp.abs(rel)
    big = exact + (jnp.log(jnp.maximum(n, exact).astype(F32) / exact)
                   / math.log(T5_MAX_DIST / exact) * (half - exact)).astype(jnp.int32)
    big = jnp.minimum(big, half - 1)
    return jnp.where(rel > 0, half, 0) + jnp.where(n < exact, n, big)


def _toeplitz_kernel(rv_ref, o_ref, *, n_diag):
    w, tq = o_ref.shape[1:]
    blocks = []
    for d in range(n_diag):
        s = LANES * (n_diag - 1 - d)
        x = jnp.broadcast_to(rv_ref[0, :, s:s + 2 * LANES], (LANES, 2 * LANES))
        blocks.append(pltpu.roll(x, LANES, 1, stride=1, stride_axis=0)[:, :LANES])
    for cb in range(w // LANES):
        for ib in range(tq // LANES):
            o_ref[0, cb * LANES:(cb + 1) * LANES, ib * LANES:(ib + 1) * LANES] = (
                blocks[cb - ib + tq // LANES - 1])


def _toeplitz(v, tq, w):
    heads, n_rel = v.shape
    n_diag = w // LANES + tq // LANES - 1
    assert w % LANES == 0 and tq % LANES == 0 and n_rel == LANES * n_diag + LANES - 1
    rv = jnp.pad(v, ((0, 0), (0, 1)))[:, None, ::-1]
    return pl.pallas_call(
        functools.partial(_toeplitz_kernel, n_diag=n_diag),
        grid=(heads,),
        in_specs=[pl.BlockSpec((1, 1, rv.shape[-1]), lambda h: (h, 0, 0))],
        out_specs=pl.BlockSpec((1, w, tq), lambda h: (h, 0, 0)),
        out_shape=jax.ShapeDtypeStruct((heads, w, tq), F32),
        compiler_params=pltpu.CompilerParams(
            dimension_semantics=("arbitrary",), vmem_limit_bytes=VMEM_LIMIT),
        name="toeplitz",
    )(rv)


def _t5_table(t5_table, head_lo, tq, r_lo, w_keys, mult):
    rel = np.arange(-r_lo - (tq - 1), w_keys - r_lo)
    m = mult(rel)
    b = t5_table.astype(F32)[:, head_lo:head_lo + GROUP_HEADS][_t5_bucket(jnp.asarray(rel, jnp.int32))]
    logm = np.log(np.maximum(m, 1)).astype(np.float32)
    vec = jnp.where(jnp.asarray(m > 0)[:, None], (b + logm[:, None]) * LOG2E, NEG).T
    return _toeplitz(vec, tq, w_keys)[None]


def _window_mult(rel):
    return (np.abs(rel) <= A_RADIUS).astype(np.int32)


def _dilated_mult(rel):
    m = np.zeros(rel.shape, np.int32)
    for window, dil in D_CONFIGS:
        m += ((rel % dil == 0) & (np.abs(rel) <= window // 2)).astype(np.int32)
    return m


NA_TILE_ROWS = 4
NA_WIN_ROWS = NA_TILE_ROWS + NA_ROWS
NA_TQ = NA_TILE_ROWS * GRID_W
NA_W = NA_WIN_ROWS * GRID_W


def _na_table(rpb):
    return _na_tables(rpb[None])[0]


def _na_row_index(variant, j, i):
    dr, ok = ((j - i, j < NA_ROWS),
              (j - NA_ROWS // 2 - i, i <= j < i + NA_ROWS),
              (j - NA_ROWS - i, j >= NA_TILE_ROWS))[variant]
    return dr + NA_ROWS - 1 if ok else None


def _na_table_kernel(rp_ref, o_ref):
    kc = lax.broadcasted_iota(jnp.int32, (GRID_W, LANES), 0)
    lane = lax.broadcasted_iota(jnp.int32, (GRID_W, LANES), 1)
    col_start = jnp.clip(lane % GRID_W - NA_COLS // 2, 0, GRID_W - NA_COLS)
    col_ok = (kc >= col_start) & (kc < col_start + NA_COLS)
    neg = jnp.full((GRID_W, LANES), NEG, F32)
    cache = {}

    def half_block(d, side):
        if d is None:
            return neg
        if (d, side) not in cache:
            x = jnp.broadcast_to(rp_ref[0, 0, d:d + 1, :], (GRID_W, LANES))
            y = pltpu.roll(x, GRID_W * (1 - side), 1, stride=1, stride_axis=0)
            cache[d, side] = jnp.where(col_ok, y, NEG)
        return cache[d, side]

    for variant in range(3):
        for j in range(NA_WIN_ROWS):
            for ip in range(NA_TILE_ROWS // 2):
                left = half_block(_na_row_index(variant, j, 2 * ip), 0)
                right = half_block(_na_row_index(variant, j, 2 * ip + 1), 1)
                o_ref[0, variant, 0, j * GRID_W:(j + 1) * GRID_W, ip * LANES:(ip + 1) * LANES] = (
                    jnp.where(lane < GRID_W, left, right))


def _na_tables(rpb):
    depth, heads, n_dr, n_dc = rpb.shape
    assert n_dr == NA_WIN_ROWS + NA_TILE_ROWS - 1 and 2 * GRID_W == LANES
    front = GRID_W - NA_COLS
    rp = jnp.pad(rpb.astype(F32) * LOG2E, ((0, 0), (0, 0), (0, 1), (front, LANES - n_dc - front)))
    rp = rp[..., ::-1]
    return pl.pallas_call(
        _na_table_kernel,
        grid=(depth, heads),
        in_specs=[pl.BlockSpec((1, 1, n_dr + 1, LANES), lambda l, h: (l, h, 0, 0))],
        out_specs=pl.BlockSpec((1, 3, 1, NA_W, NA_TQ), lambda l, h: (l, 0, h, 0, 0)),
        out_shape=jax.ShapeDtypeStruct((depth, 3, heads, NA_W, NA_TQ), F32),
        compiler_params=pltpu.CompilerParams(
            dimension_semantics=("arbitrary", "arbitrary"), vmem_limit_bytes=VMEM_LIMIT),
        name="na_tables",
    )(rp)


def _swap_pairs(x):
    n = x.shape[-1]
    lane = lax.broadcasted_iota(jnp.int32, x.shape, x.ndim - 1)
    return jnp.where(lane % 2 == 0, pltpu.roll(x, n - 1, x.ndim - 1), pltpu.roll(x, 1, x.ndim - 1))


def _head_rms(x, ones_bd, w):
    sq = x * x
    hi = sq.astype(BF16)
    lo = (sq - hi.astype(F32)).astype(BF16)
    ms = (jnp.dot(hi, ones_bd, preferred_element_type=F32)
          + jnp.dot(lo, ones_bd, preferred_element_type=F32)) * (1.0 / HEAD_DIM)
    return x * lax.rsqrt(ms + EPS) * w


def _cprep_kernel(q_ref, k_ref, v_ref, cos_ref, sin_ref, qw_ref, kw_ref, bd_ref,
                  qt_ref, k2_ref, vt_ref):
    cos = cos_ref[...]
    sin = sin_ref[...]
    q = _head_rms(q_ref[...].astype(F32), bd_ref[...], qw_ref[...])
    reps = q.shape[1] // LANES
    q = q * jnp.concatenate([cos] * reps, axis=1) + _swap_pairs(q) * jnp.concatenate([sin] * reps, axis=1)
    qt_ref[...] = (q * Q_SCALE).T.astype(qt_ref.dtype)
    k = _head_rms(k_ref[...].astype(F32), bd_ref[:LANES, :LANES], kw_ref[...])
    k = (k * cos + _swap_pairs(k) * sin).astype(k2_ref.dtype)
    vt = v_ref[...].astype(F32).T.astype(vt_ref.dtype)
    ones = jnp.ones((V_ROWS - HEAD_DIM, vt.shape[1]), vt_ref.dtype)
    tk = vt_ref.shape[-1]
    for g in range(C_KV_HEADS):
        k2_ref[g] = k[:, g * HEAD_DIM:(g + 1) * HEAD_DIM]
        vg = jnp.concatenate([vt[g * HEAD_DIM:(g + 1) * HEAD_DIM], ones], axis=0)
        for u in range(vt_ref.shape[1]):
            vt_ref[g, u] = vg[:, u * tk:(u + 1) * tk]


def _cprep(proj, cos2, sin2, qw, kw, ones_bd, tk, t=1024):
    L = proj.shape[0]
    vec = lambda width: pl.BlockSpec((1, width), lambda n: (0, 0))
    return pl.pallas_call(
        _cprep_kernel,
        grid=(L // t,),
        in_specs=[_piece("q_c", t), _piece("k_c", t), _piece("v_c", t),
                  pl.BlockSpec((t, LANES), lambda n: (n, 0)),
                  pl.BlockSpec((t, LANES), lambda n: (n, 0)),
                  vec(GROUP_WIDTH), vec(LANES),
                  pl.BlockSpec((GROUP_WIDTH, GROUP_WIDTH), lambda n: (0, 0))],
        out_specs=[pl.BlockSpec((GROUP_WIDTH, t), lambda n: (0, n)),
                   pl.BlockSpec((C_KV_HEADS, t, HEAD_DIM), lambda n: (0, n, 0)),
                   pl.BlockSpec((C_KV_HEADS, t // tk, V_ROWS, tk), lambda n: (0, n, 0, 0))],
        out_shape=[jax.ShapeDtypeStruct((GROUP_WIDTH, L), BF16),
                   jax.ShapeDtypeStruct((C_KV_HEADS, L, HEAD_DIM), BF16),
                   jax.ShapeDtypeStruct((C_KV_HEADS, L // tk, V_ROWS, tk), BF16)],
        compiler_params=pltpu.CompilerParams(
            dimension_semantics=("arbitrary",), vmem_limit_bytes=VMEM_LIMIT),
        name="dense_prep",
    )(proj, proj, proj, cos2, sin2, qw, kw, ones_bd)


def _flash_kernel(qt_ref, k_ref, vt_ref, o_ref, m_scr, acc_scr, st_scr, *, tk, heads, unroll, ahead):
    n_kv = k_ref.shape[1] // tk
    m_scr[...] = jnp.full(m_scr.shape, NEG, F32)
    acc_scr[...] = jnp.zeros(acc_scr.shape, F32)

    def scores(j, h):
        k = k_ref[0, pl.ds(pl.multiple_of(j * tk, tk), tk), :]
        qt = qt_ref[h * HEAD_DIM:(h + 1) * HEAD_DIM, :]
        return jnp.dot(k, qt, preferred_element_type=F32)

    n_items = unroll * heads

    def item_scores(t, idx):
        j = t * unroll + idx // heads
        return scores(jnp.minimum(j, n_kv - 1), idx % heads)

    for a in range(ahead):
        st_scr[a] = item_scores(0, a)

    def body(t, carry):
        pending = [st_scr[a] for a in range(ahead)]
        for idx in range(n_items):
            pending.append(item_scores(t, idx + ahead))
            st = pending.pop(0)
            h = idx % heads
            m = m_scr[h]
            m_new = jnp.maximum(m, jnp.max(st, axis=0, keepdims=True))
            pt = jnp.exp2(st - m_new).astype(BF16)
            acc_scr[h] = (jnp.exp2(m - m_new) * acc_scr[h]
                          + jnp.dot(vt_ref[0, t * unroll + idx // heads], pt, preferred_element_type=F32))
            m_scr[h] = m_new
        for a in range(ahead):
            st_scr[a] = pending[a]
        return carry

    lax.fori_loop(0, n_kv // unroll, body, 0)
    outs = [acc_scr[h, :HEAD_DIM] / acc_scr[h, HEAD_DIM:HEAD_DIM + 1] for h in range(heads)]
    o_ref[...] = jnp.concatenate(outs, axis=0).T.astype(o_ref.dtype)


def _flash(qt, k2, vt, tq):
    L = qt.shape[1]
    _, n_kv, v_rows, tk = vt.shape
    heads = GROUP_HEADS // C_KV_HEADS
    qrows = heads * HEAD_DIM
    return pl.pallas_call(
        functools.partial(_flash_kernel, tk=tk, heads=heads, unroll=C_UNROLL, ahead=C_AHEAD),
        grid=(C_KV_HEADS, L // tq),
        in_specs=[pl.BlockSpec((qrows, tq), lambda g, n: (g, n)),
                  pl.BlockSpec((1, L, HEAD_DIM), lambda g, n: (g, 0, 0)),
                  pl.BlockSpec((1, n_kv, v_rows, tk), lambda g, n: (g, 0, 0, 0))],
        out_specs=pl.BlockSpec((tq, qrows), lambda g, n: (n, g)),
        out_shape=jax.ShapeDtypeStruct((L, GROUP_WIDTH), BF16),
        scratch_shapes=[pltpu.VMEM((heads, 1, tq), F32), pltpu.VMEM((heads, v_rows, tq), F32),
                        pltpu.VMEM((C_AHEAD, tk, tq), F32)],
        compiler_params=pltpu.CompilerParams(
            dimension_semantics=("arbitrary", "arbitrary"), vmem_limit_bytes=VMEM_LIMIT),
        name="dense_flash",
    )(qt, k2, vt)


def _outproj_kernel(*refs, final):
    if final:
        x_ref, ya, yb, yc, yd, za, zb, zc, zd, w_ref, g_ref, fw_ref, o_ref = refs
    else:
        x_ref, ya, yb, yc, yd, za, zb, zc, zd, w_ref, g_ref, nw_ref, sc_ref, sh_ref, o_ref, h_ref = refs
    acc = jnp.zeros(x_ref.shape, F32)
    for gi, (y_ref, z_ref) in enumerate(zip((ya, yb, yc, yd), (za, zb, zc, zd))):
        z = z_ref[...].astype(F32)
        u = (y_ref[...].astype(F32) * (z * jax.nn.sigmoid(z))).astype(BF16)
        acc = acc + jnp.dot(u, w_ref[gi * GROUP_WIDTH:(gi + 1) * GROUP_WIDTH, :], preferred_element_type=F32)
    xn = x_ref[...] + g_ref[...] * acc
    if final:
        o_ref[...] = xn * lax.rsqrt(jnp.mean(xn * xn, axis=-1, keepdims=True) + EPS) * fw_ref[...]
    else:
        o_ref[...] = xn
        h_ref[...] = _modulated_norm(xn, nw_ref[...], sc_ref[...], sh_ref[...])


def _outproj(x, ys, proj, w_bf16, gate, vecs):
    L, d = x.shape
    tm = min(512, L)
    final = len(vecs) == 1
    row = lambda width: pl.BlockSpec((tm, width), lambda i: (i, 0))
    vec = pl.BlockSpec((1, d), lambda i: (0, 0))
    in_specs = ([row(d)] + [row(GROUP_WIDTH)] * 4 + [_piece(z, tm) for z in ("z_a", "z_b", "z_c", "z_d")]
                + [pl.BlockSpec(w_bf16.shape, lambda i: (0, 0)), vec] + [vec] * len(vecs))
    x_out = jax.ShapeDtypeStruct((L, d), F32)
    return pl.pallas_call(
        functools.partial(_outproj_kernel, final=final),
        grid=(L // tm,),
        in_specs=in_specs,
        out_specs=row(d) if final else [row(d), row(d)],
        out_shape=x_out if final else [x_out, jax.ShapeDtypeStruct((L, d), BF16)],
        compiler_params=pltpu.CompilerParams(
            dimension_semantics=("arbitrary",), vmem_limit_bytes=VMEM_LIMIT),
        name="outproj",
    )(x, *ys, proj, proj, proj, proj, w_bf16, gate, *vecs)


def _rope_tables(L):
    t = np.arange(L)
    axis_dim = HEAD_DIM // 2
    inv = jnp.asarray(ROPE_THETA, F32) ** (-jnp.arange(0, axis_dim, 2, dtype=F32) / axis_dim)
    row = jnp.asarray(t // GRID_W, F32)
    col = jnp.asarray(t % GRID_W, F32)
    ang = jnp.concatenate([row[:, None] * inv[None], col[:, None] * inv[None]], axis=-1)
    cos = jnp.repeat(jnp.cos(ang), 2, axis=-1)
    sin = jnp.repeat(jnp.sin(ang), 2, axis=-1) * jnp.asarray(np.tile([-1.0, 1.0], HEAD_DIM // 2), F32)
    return jnp.tile(cos, (1, LANES // HEAD_DIM)), jnp.tile(sin, (1, LANES // HEAD_DIM))


A_TQ = 256
D_TQ = 256
D_REACH = D_CONFIGS[-1][0] // 2
C_TQ = 512
C_TK = 256
C_UNROLL = 8
C_AHEAD = 2


def _table_a(t5_table):
    return _t5_table(t5_table, 0, A_TQ, A_RADIUS, A_TQ + 2 * A_RADIUS, _window_mult)


def _table_d(t5_table):
    return _t5_table(t5_table, GROUP_HEADS, D_TQ, D_REACH, D_TQ + 2 * D_REACH, _dilated_mult)


PREP_T = 1024
assert PREP_T >= D_REACH


def _seq_window(L, tq, r_lo):
    return lambda n: (n * tq + PREP_T - r_lo, (n * tq - r_lo, 0, L))


def _mixer_a(proj, qt, tab_a, sink):
    L = proj.shape[0]
    k_pad, vt_pad = _kv_prep(proj, "k_a", "v_a", LANES, PREP_T, PREP_T)
    return _band_attention(qt, k_pad, vt_pad, tab_a, group=GROUP_HEADS // A_KV_HEADS,
                           window=_seq_window(L, A_TQ, A_RADIUS), sink=sink.astype(F32))


def _mixer_b(proj, qt, tab_b):
    L = proj.shape[0]
    rows = L // GRID_W
    n_tiles = rows // NA_TILE_ROWS
    k, vt = _kv_prep(proj, "k_b", "v_b", GROUP_WIDTH, 0, PREP_T)
    window = lambda n: (jnp.clip(n * NA_TILE_ROWS - NA_ROWS // 2, 0, rows - NA_WIN_ROWS) * GRID_W, None)
    variant = lambda n: jnp.where(n == 0, 0, jnp.where(n == n_tiles - 1, 2, 1))
    return _band_attention(qt, k, vt, tab_b, group=1, window=window, variant=variant)


def _mixer_c(proj, q_norm_w, k_norm_w):
    L = proj.shape[0]
    cos2, sin2 = _rope_tables(L)
    ones_bd = jnp.asarray(np.kron(np.eye(GROUP_HEADS), np.ones((HEAD_DIM, HEAD_DIM))), BF16)
    qt, k2, vt = _cprep(proj, cos2, sin2, jnp.tile(q_norm_w.astype(F32), GROUP_HEADS)[None],
                        jnp.tile(k_norm_w.astype(F32), LANES // HEAD_DIM)[None], ones_bd, C_TK)
    return _flash(qt, k2, vt, C_TQ)


def _mixer_d(proj, qt, tab_d):
    L = proj.shape[0]
    k_pad, vt_pad = _kv_prep(proj, "k_d", "v_d", GROUP_WIDTH, PREP_T, PREP_T)
    return _band_attention(qt, k_pad, vt_pad, tab_d, group=1, window=_seq_window(L, D_TQ, D_REACH))


def kernel(x, c, w_ada, b_ada, norm_w, w_in, w_out, attn_sink, na_rpb, q_norm_w, k_norm_w,
           t5_table, final_norm_w):
    B, L, D = x.shape
    assert B == 1 and L % 1024 == 0 and L // GRID_W >= NA_WIN_ROWS
    depth = w_ada.shape[0]
    x = x[0]

    mod = _ada_mod(jnp.broadcast_to(c, (8, D)), w_ada, b_ada)[:, 0:1, :]
    tab_a = _table_a(t5_table)
    tab_d = _table_d(t5_table)
    tabs_b = _na_tables(na_rpb)
    w_in_b = w_in.astype(BF16)
    w_out_b = w_out.astype(BF16)
    shift, scale, gate = jnp.split(mod, 3, axis=-1)
    h = _norm(x, norm_w[0][None], scale[0], shift[0])
    for i in range(depth):
        proj = _inproj(h, w_in_b[i])
        qt_a, qt_b, qt_d = _qt_prep(proj, ("q_a", "q_b", "q_d"))
        ys = (_mixer_a(proj, qt_a, tab_a, attn_sink[i]),
              _mixer_b(proj, qt_b, tabs_b[i]),
              _mixer_c(proj, q_norm_w[i], k_norm_w[i]),
              _mixer_d(proj, qt_d, tab_d))
        if i + 1 < depth:
            x, h = _outproj(x, ys, proj, w_out_b[i], gate[i],
                            (norm_w[i + 1][None], scale[i + 1], shift[i + 1]))
        else:
            x = _outproj(x, ys, proj, w_out_b[i], gate[i], (final_norm_w[None],))
    return x[None]
```

```python
import functools
import math

import numpy as np
import jax
import jax.numpy as jnp
from jax import lax
from jax.experimental import pallas as pl
from jax.experimental.pallas import tpu as pltpu

HEAD_DIM = 64
GROUP_WIDTH = 512
GROUP_HEADS = 8
A_KV_HEADS = 2
A_RADIUS = 128
C_KV_HEADS = 2
ROPE_THETA = 10000.0
NA_ROWS = 8
NA_COLS = 16
D_CONFIGS = ((128, 1), (512, 4), (2048, 16))
GRID_W = 64
T5_BUCKETS = 32
T5_MAX_DIST = 1024
EPS = 1e-6
NEG = -1e30

V_ROWS = HEAD_DIM + 16
LANES = 128
VMEM_LIMIT = 56 * 1024 * 1024

F32 = jnp.float32
BF16 = jnp.bfloat16

_SRC = {}
_off = 0
for _name, _w in (("q_a", 512), ("k_a", 128), ("v_a", 128), ("z_a", 512),
                  ("q_b", 512), ("k_b", 512), ("v_b", 512), ("z_b", 512),
                  ("q_c", 512), ("k_c", 128), ("v_c", 128), ("z_c", 512),
                  ("q_d", 512), ("k_d", 512), ("v_d", 512), ("z_d", 512)):
    _SRC[_name] = (_off, _w)
    _off += _w
IN_WIDTH = _off


def _piece(name, t, row_block=lambda n: n):
    off, width = _SRC[name]
    return pl.BlockSpec((pl.Element(t), pl.Element(width)),
                        lambda n: (pl.multiple_of(row_block(n) * t, t), off))


def _ada_kernel(c_ref, w_ref, b_ref, o_ref):
    c = c_ref[...]
    cond = c * jax.nn.sigmoid(c)
    o_ref[0] = jnp.dot(cond, w_ref[0], preferred_element_type=F32,
                       precision=lax.Precision.HIGHEST) + b_ref[0]


def _ada_mod(c8, w_ada, b_ada):
    depth, d, n3 = w_ada.shape
    tn = 1024
    return pl.pallas_call(
        _ada_kernel,
        grid=(depth, n3 // tn),
        in_specs=[pl.BlockSpec((8, d), lambda i, j: (0, 0)),
                  pl.BlockSpec((1, d, tn), lambda i, j: (i, 0, j)),
                  pl.BlockSpec((1, 1, tn), lambda i, j: (i, 0, j))],
        out_specs=pl.BlockSpec((1, 8, tn), lambda i, j: (i, 0, j)),
        out_shape=jax.ShapeDtypeStruct((depth, 8, n3), F32),
        compiler_params=pltpu.CompilerParams(
            dimension_semantics=("arbitrary", "arbitrary"), vmem_limit_bytes=VMEM_LIMIT),
        name="ada_mod",
    )(c8, w_ada, b_ada.reshape(depth, 1, n3))


def _modulated_norm(x, nw, scale, shift):
    y = x * lax.rsqrt(jnp.mean(x * x, axis=-1, keepdims=True) + EPS)
    return ((y * nw) * (1.0 + scale) + shift).astype(BF16)


def _norm_kernel(x_ref, nw_ref, sc_ref, sh_ref, h_ref):
    h_ref[...] = _modulated_norm(x_ref[...], nw_ref[...], sc_ref[...], sh_ref[...])


def _norm(x, nw, scale, shift):
    L, d = x.shape
    tm = min(512, L)
    vec = pl.BlockSpec((1, d), lambda i: (0, 0))
    return pl.pallas_call(
        _norm_kernel,
        grid=(L // tm,),
        in_specs=[pl.BlockSpec((tm, d), lambda i: (i, 0)), vec, vec, vec],
        out_specs=pl.BlockSpec((tm, d), lambda i: (i, 0)),
        out_shape=jax.ShapeDtypeStruct((L, d), BF16),
        compiler_params=pltpu.CompilerParams(
            dimension_semantics=("arbitrary",), vmem_limit_bytes=VMEM_LIMIT),
        name="norm",
    )(x, nw, scale, shift)


def _inproj_kernel(h_ref, w_ref, o_ref):
    o_ref[...] = jnp.dot(h_ref[...], w_ref[...], preferred_element_type=F32).astype(o_ref.dtype)


def _inproj(h, w_bf16, layer):
    L, d = h.shape
    n = w_bf16.shape[2]
    tm = min(1024, L)
    tn = 512
    return pl.pallas_call(
        _inproj_kernel,
        grid=(L // tm, n // tn),
        in_specs=[pl.BlockSpec((tm, d), lambda i, j: (i, 0)),
                  pl.BlockSpec((None, d, tn), lambda i, j: (layer, 0, j))],
        out_specs=pl.BlockSpec((tm, tn), lambda i, j: (i, j)),
        out_shape=jax.ShapeDtypeStruct((L, n), BF16),
        compiler_params=pltpu.CompilerParams(
            dimension_semantics=("arbitrary", "arbitrary"), vmem_limit_bytes=VMEM_LIMIT),
        name="inproj",
    )(h, w_bf16)


LOG2E = math.log2(math.e)
Q_SCALE = HEAD_DIM ** -0.5 * LOG2E


def _qt_prep_kernel(*refs):
    n = len(refs) // 2
    for q_ref, qt_ref in zip(refs[:n], refs[n:]):
        qt_ref[...] = (q_ref[...].astype(F32) * Q_SCALE).T.astype(qt_ref.dtype)


def _qt_prep(proj, names, t=1024):
    L = proj.shape[0]
    return pl.pallas_call(
        _qt_prep_kernel,
        grid=(L // t,),
        in_specs=[_piece(name, t) for name in names],
        out_specs=[pl.BlockSpec((GROUP_WIDTH, t), lambda n: (0, n)) for _ in names],
        out_shape=[jax.ShapeDtypeStruct((GROUP_WIDTH, L), BF16) for _ in names],
        compiler_params=pltpu.CompilerParams(
            dimension_semantics=("arbitrary",), vmem_limit_bytes=VMEM_LIMIT),
        name="qt_prep",
    )(*([proj] * len(names)))


def _kv_prep_kernel(k_ref, v_ref, kp_ref, vt_ref, *, pad_blocks, tok_blocks):
    b = pl.program_id(0)
    is_token = (b >= pad_blocks) & (b < pad_blocks + tok_blocks)

    @pl.when(is_token)
    def _():
        kp_ref[...] = k_ref[...]
        vt = v_ref[...].astype(F32).T.astype(vt_ref.dtype)
        ones = jnp.ones((V_ROWS - HEAD_DIM, vt.shape[1]), vt_ref.dtype)
        for g in range(vt.shape[0] // HEAD_DIM):
            vt_ref[g * V_ROWS:(g + 1) * V_ROWS, :] = jnp.concatenate(
                [vt[g * HEAD_DIM:(g + 1) * HEAD_DIM], ones], axis=0)

    @pl.when(jnp.logical_not(is_token))
    def _():
        kp_ref[...] = jnp.zeros(kp_ref.shape, kp_ref.dtype)
        vt_ref[...] = jnp.zeros(vt_ref.shape, vt_ref.dtype)


def _kv_prep(proj, k_name, v_name, width, pad, t):
    L = proj.shape[0]
    assert pad % t == 0 and L % t == 0 and width % LANES == 0
    pad_blocks, tok_blocks = pad // t, L // t
    lp = L + 2 * pad
    assert _SRC[k_name][1] == width and _SRC[v_name][1] == width
    tok = lambda b: jnp.clip(b - pad_blocks, 0, tok_blocks - 1)
    return pl.pallas_call(
        functools.partial(_kv_prep_kernel, pad_blocks=pad_blocks, tok_blocks=tok_blocks),
        grid=(lp // t,),
        in_specs=[_piece(k_name, t, tok), _piece(v_name, t, tok)],
        out_specs=[pl.BlockSpec((t, width), lambda b: (b, 0)),
                   pl.BlockSpec((width // HEAD_DIM * V_ROWS, t), lambda b: (0, b))],
        out_shape=[jax.ShapeDtypeStruct((lp, width), BF16),
                   jax.ShapeDtypeStruct((width // HEAD_DIM * V_ROWS, lp), BF16)],
        compiler_params=pltpu.CompilerParams(
            dimension_semantics=("arbitrary",), vmem_limit_bytes=VMEM_LIMIT),
        name="kv_prep",
    )(proj, proj)


BAND_CHUNK = 256


def _band_kernel(*refs, w_keys, heads, group, use_sink, window, variant):
    if use_sink:
        sink_ref, qt_ref, k_ref, vt_ref, tab_ref, o_ref = refs
    else:
        qt_ref, k_ref, vt_ref, tab_ref, o_ref = refs
    n = pl.program_id(0)
    _, pos = window(n)
    kwin = k_ref[...]
    vwin = vt_ref[...]
    var = variant(n)

    n_chunks = w_keys // BAND_CHUNK

    def attend(kbias):
        def scores(h, c):
            kv = h // group
            rows = slice(c * BAND_CHUNK, (c + 1) * BAND_CHUNK)
            st = jnp.dot(kwin[rows, kv * HEAD_DIM:(kv + 1) * HEAD_DIM],
                         qt_ref[h * HEAD_DIM:(h + 1) * HEAD_DIM, :],
                         preferred_element_type=F32) + tab_ref[var, h, rows, :]
            return st if kbias is None else st + kbias[rows]

        def col_max(m, st):
            cm = jnp.max(st, axis=0, keepdims=True)
            return cm if m is None else jnp.maximum(m, cm)

        outs = []
        cur, m = [], None
        for c in range(n_chunks):
            cur.append(scores(0, c))
            m = col_max(m, cur[-1])
        for h in range(heads):
            kv = h // group
            nxt, m_next, acc = [], None, None
            for c in range(n_chunks):
                if h + 1 < heads:
                    nxt.append(scores(h + 1, c))
                    m_next = col_max(m_next, nxt[-1])
                pt = jnp.exp2(cur[c] - m).astype(BF16)
                pv = jnp.dot(vwin[kv * V_ROWS:(kv + 1) * V_ROWS, c * BAND_CHUNK:(c + 1) * BAND_CHUNK], pt,
                             preferred_element_type=F32)
                acc = pv if acc is None else acc + pv
            o, l = acc[:HEAD_DIM], acc[HEAD_DIM:HEAD_DIM + 1]
            if use_sink:
                sk = sink_ref[h] * LOG2E
                m2 = jnp.maximum(m, sk)
                a = jnp.exp2(m - m2)
                o = o * (a / (l * a + jnp.exp2(sk - m2)))
            else:
                o = o / l
            outs.append(o)
            cur, m = nxt, m_next
        o_ref[...] = jnp.concatenate(outs, axis=0).T.astype(o_ref.dtype)

    if pos is None:
        attend(None)
    else:
        first, lo, hi = pos
        inside = (first >= lo) & (first + w_keys <= hi)

        @pl.when(inside)
        def _():
            attend(None)

        @pl.when(jnp.logical_not(inside))
        def _():
            kpos = first + lax.broadcasted_iota(jnp.int32, (w_keys, 1), 0)
            attend(jnp.where((kpos >= lo) & (kpos < hi), 0.0, NEG).astype(F32))


def _band_attention(qt, k_pad, vt_pad, table, *, group, window, variant=lambda n: 0, sink=None,
                    layer=None):
    qw, L = qt.shape
    heads = qw // HEAD_DIM
    w_keys, tq = table.shape[-2:]
    kern = functools.partial(_band_kernel, w_keys=w_keys, heads=heads, group=group,
                             use_sink=sink is not None, window=window, variant=variant)
    start = lambda n: pl.multiple_of(window(n)[0], LANES)
    if layer is None:
        table_spec = pl.BlockSpec(table.shape, lambda n: (0, 0, 0, 0), pipeline_mode=pl.Buffered(1))
    else:
        table_spec = pl.BlockSpec((None,) + table.shape[1:], lambda n: (layer, 0, 0, 0, 0),
                                  pipeline_mode=pl.Buffered(1))
    in_specs = [pl.BlockSpec((qw, tq), lambda n: (0, n)),
                pl.BlockSpec((pl.Element(w_keys), pl.Element(k_pad.shape[1])), lambda n: (start(n), 0)),
                pl.BlockSpec((pl.Element(vt_pad.shape[0]), pl.Element(w_keys)), lambda n: (0, start(n))),
                table_spec]
    args = [qt, k_pad, vt_pad, table]
    if sink is not None:
        in_specs = [pl.BlockSpec(memory_space=pltpu.SMEM)] + in_specs
        args = [sink] + args
    return pl.pallas_call(
        kern,
        grid=(L // tq,),
        in_specs=in_specs,
        out_specs=pl.BlockSpec((tq, qw), lambda n: (n, 0)),
        out_shape=jax.ShapeDtypeStruct((L, qw), BF16),
        compiler_params=pltpu.CompilerParams(
            dimension_semantics=("arbitrary",), vmem_limit_bytes=VMEM_LIMIT),
        name="band_attention",
    )(*args)


def _t5_bucket(rel):
    half = T5_BUCKETS // 2
    exact = half // 2
    n = jnp.abs(rel)
    big = exact + (jnp.log(jnp.maximum(n, exact).astype(F32) / exact)
                   / math.log(T5_MAX_DIST / exact) * (half - exact)).astype(jnp.int32)
    big = jnp.minimum(big, half - 1)
    return jnp.where(rel > 0, half, 0) + jnp.where(n < exact, n, big)


def _toeplitz_kernel(rv_ref, o_ref, *, n_diag):
    w, tq = o_ref.shape[1:]
    blocks = []
    for d in range(n_diag):
        s = LANES * (n_diag - 1 - d)
        x = jnp.broadcast_to(rv_ref[0, :, s:s + 2 * LANES], (LANES, 2 * LANES))
        blocks.append(pltpu.roll(x, LANES, 1, stride=1, stride_axis=0)[:, :LANES])
    for cb in range(w // LANES):
        for ib in range(tq // LANES):
            o_ref[0, cb * LANES:(cb + 1) * LANES, ib * LANES:(ib + 1) * LANES] = (
                blocks[cb - ib + tq // LANES - 1])


def _toeplitz(v, tq, w):
    heads, n_rel = v.shape
    n_diag = w // LANES + tq // LANES - 1
    assert w % LANES == 0 and tq % LANES == 0 and n_rel == LANES * n_diag + LANES - 1
    rv = jnp.pad(v, ((0, 0), (0, 1)))[:, None, ::-1]
    return pl.pallas_call(
        functools.partial(_toeplitz_kernel, n_diag=n_diag),
        grid=(heads,),
        in_specs=[pl.BlockSpec((1, 1, rv.shape[-1]), lambda h: (h, 0, 0))],
        out_specs=pl.BlockSpec((1, w, tq), lambda h: (h, 0, 0)),
        out_shape=jax.ShapeDtypeStruct((heads, w, tq), F32),
        compiler_params=pltpu.CompilerParams(
            dimension_semantics=("arbitrary",), vmem_limit_bytes=VMEM_LIMIT),
        name="toeplitz",
    )(rv)


def _t5_table(t5_table, head_lo, tq, r_lo, w_keys, mult):
    rel = np.arange(-r_lo - (tq - 1), w_keys - r_lo)
    m = mult(rel)
    b = t5_table.astype(F32)[:, head_lo:head_lo + GROUP_HEADS][_t5_bucket(jnp.asarray(rel, jnp.int32))]
    logm = np.log(np.maximum(m, 1)).astype(np.float32)
    vec = jnp.where(jnp.asarray(m > 0)[:, None], (b + logm[:, None]) * LOG2E, NEG).T
    return _toeplitz(vec, tq, w_keys)[None]


def _window_mult(rel):
    return (np.abs(rel) <= A_RADIUS).astype(np.int32)


def _dilated_mult(rel):
    m = np.zeros(rel.shape, np.int32)
    for window, dil in D_CONFIGS:
        m += ((rel % dil == 0) & (np.abs(rel) <= window // 2)).astype(np.int32)
    return m


NA_TILE_ROWS = 4
NA_WIN_ROWS = NA_TILE_ROWS + NA_ROWS
NA_TQ = NA_TILE_ROWS * GRID_W
NA_W = NA_WIN_ROWS * GRID_W


def _na_table(rpb):
    return _na_tables(rpb[None])[0]


def _na_row_index(variant, j, i):
    dr, ok = ((j - i, j < NA_ROWS),
              (j - NA_ROWS // 2 - i, i <= j < i + NA_ROWS),
              (j - NA_ROWS - i, j >= NA_TILE_ROWS))[variant]
    return dr + NA_ROWS - 1 if ok else None


def _na_table_kernel(rp_ref, o_ref):
    kc = lax.broadcasted_iota(jnp.int32, (GRID_W, LANES), 0)
    lane = lax.broadcasted_iota(jnp.int32, (GRID_W, LANES), 1)
    col_start = jnp.clip(lane % GRID_W - NA_COLS // 2, 0, GRID_W - NA_COLS)
    col_ok = (kc >= col_start) & (kc < col_start + NA_COLS)
    neg = jnp.full((GRID_W, LANES), NEG, F32)
    cache = {}

    def half_block(d, side):
        if d is None:
            return neg
        if (d, side) not in cache:
            x = jnp.broadcast_to(rp_ref[0, 0, d:d + 1, :], (GRID_W, LANES))
            y = pltpu.roll(x, GRID_W * (1 - side), 1, stride=1, stride_axis=0)
            cache[d, side] = jnp.where(col_ok, y, NEG)
        return cache[d, side]

    for variant in range(3):
        for j in range(NA_WIN_ROWS):
            for ip in range(NA_TILE_ROWS // 2):
                left = half_block(_na_row_index(variant, j, 2 * ip), 0)
                right = half_block(_na_row_index(variant, j, 2 * ip + 1), 1)
                o_ref[0, variant, 0, j * GRID_W:(j + 1) * GRID_W, ip * LANES:(ip + 1) * LANES] = (
                    jnp.where(lane < GRID_W, left, right))


def _na_tables(rpb):
    depth, heads, n_dr, n_dc = rpb.shape
    assert n_dr == NA_WIN_ROWS + NA_TILE_ROWS - 1 and 2 * GRID_W == LANES
    front = GRID_W - NA_COLS
    rp = jnp.pad(rpb.astype(F32) * LOG2E, ((0, 0), (0, 0), (0, 1), (front, LANES - n_dc - front)))
    rp = rp[..., ::-1]
    return pl.pallas_call(
        _na_table_kernel,
        grid=(depth, heads),
        in_specs=[pl.BlockSpec((1, 1, n_dr + 1, LANES), lambda l, h: (l, h, 0, 0))],
        out_specs=pl.BlockSpec((1, 3, 1, NA_W, NA_TQ), lambda l, h: (l, 0, h, 0, 0)),
        out_shape=jax.ShapeDtypeStruct((depth, 3, heads, NA_W, NA_TQ), F32),
        compiler_params=pltpu.CompilerParams(
            dimension_semantics=("arbitrary", "arbitrary"), vmem_limit_bytes=VMEM_LIMIT),
        name="na_tables",
    )(rp)


def _swap_pairs(x):
    n = x.shape[-1]
    lane = lax.broadcasted_iota(jnp.int32, x.shape, x.ndim - 1)
    return jnp.where(lane % 2 == 0, pltpu.roll(x, n - 1, x.ndim - 1), pltpu.roll(x, 1, x.ndim - 1))


def _head_rms(x, ones_bd, w):
    sq = x * x
    hi = sq.astype(BF16)
    lo = (sq - hi.astype(F32)).astype(BF16)
    ms = (jnp.dot(hi, ones_bd, preferred_element_type=F32)
          + jnp.dot(lo, ones_bd, preferred_element_type=F32)) * (1.0 / HEAD_DIM)
    return x * lax.rsqrt(ms + EPS) * w


def _cprep_kernel(q_ref, k_ref, v_ref, cos_ref, sin_ref, qw_ref, kw_ref, bd_ref,
                  qt_ref, k2_ref, vt_ref):
    cos = cos_ref[...]
    sin = sin_ref[...]
    q = _head_rms(q_ref[...].astype(F32), bd_ref[...], qw_ref[...])
    reps = q.shape[1] // LANES
    q = q * jnp.concatenate([cos] * reps, axis=1) + _swap_pairs(q) * jnp.concatenate([sin] * reps, axis=1)
    qt_ref[...] = (q * Q_SCALE).T.astype(qt_ref.dtype)
    k = _head_rms(k_ref[...].astype(F32), bd_ref[:LANES, :LANES], kw_ref[...])
    k = (k * cos + _swap_pairs(k) * sin).astype(k2_ref.dtype)
    vt = v_ref[...].astype(F32).T.astype(vt_ref.dtype)
    ones = jnp.ones((V_ROWS - HEAD_DIM, vt.shape[1]), vt_ref.dtype)
    tk = vt_ref.shape[-1]
    for g in range(C_KV_HEADS):
        k2_ref[g] = k[:, g * HEAD_DIM:(g + 1) * HEAD_DIM]
        vg = jnp.concatenate([vt[g * HEAD_DIM:(g + 1) * HEAD_DIM], ones], axis=0)
        for u in range(vt_ref.shape[1]):
            vt_ref[g, u] = vg[:, u * tk:(u + 1) * tk]


def _cprep(proj, cos2, sin2, qw, kw, ones_bd, tk, t=1024):
    L = proj.shape[0]
    vec = lambda width: pl.BlockSpec((1, width), lambda n: (0, 0))
    return pl.pallas_call(
        _cprep_kernel,
        grid=(L // t,),
        in_specs=[_piece("q_c", t), _piece("k_c", t), _piece("v_c", t),
                  pl.BlockSpec((t, LANES), lambda n: (n, 0)),
                  pl.BlockSpec((t, LANES), lambda n: (n, 0)),
                  vec(GROUP_WIDTH), vec(LANES),
                  pl.BlockSpec((GROUP_WIDTH, GROUP_WIDTH), lambda n: (0, 0))],
        out_specs=[pl.BlockSpec((GROUP_WIDTH, t), lambda n: (0, n)),
                   pl.BlockSpec((C_KV_HEADS, t, HEAD_DIM), lambda n: (0, n, 0)),
                   pl.BlockSpec((C_KV_HEADS, t // tk, V_ROWS, tk), lambda n: (0, n, 0, 0))],
        out_shape=[jax.ShapeDtypeStruct((GROUP_WIDTH, L), BF16),
                   jax.ShapeDtypeStruct((C_KV_HEADS, L, HEAD_DIM), BF16),
                   jax.ShapeDtypeStruct((C_KV_HEADS, L // tk, V_ROWS, tk), BF16)],
        compiler_params=pltpu.CompilerParams(
            dimension_semantics=("arbitrary",), vmem_limit_bytes=VMEM_LIMIT),
        name="dense_prep",
    )(proj, proj, proj, cos2, sin2, qw, kw, ones_bd)


def _flash_kernel(qt_ref, k_ref, vt_ref, o_ref, m_scr, acc_scr, st_scr, *, tk, heads, unroll, ahead):
    n_kv = k_ref.shape[1] // tk
    m_scr[...] = jnp.full(m_scr.shape, NEG, F32)
    acc_scr[...] = jnp.zeros(acc_scr.shape, F32)

    def scores(j, h):
        k = k_ref[0, pl.ds(pl.multiple_of(j * tk, tk), tk), :]
        qt = qt_ref[h * HEAD_DIM:(h + 1) * HEAD_DIM, :]
        return jnp.dot(k, qt, preferred_element_type=F32)

    n_items = unroll * heads

    def item_scores(t, idx):
        j = t * unroll + idx // heads
        return scores(jnp.minimum(j, n_kv - 1), idx % heads)

    for a in range(ahead):
        st_scr[a] = item_scores(0, a)

    def body(t, carry):
        pending = [st_scr[a] for a in range(ahead)]
        for idx in range(n_items):
            pending.append(item_scores(t, idx + ahead))
            st = pending.pop(0)
            h = idx % heads
            m = m_scr[h]
            m_new = jnp.maximum(m, jnp.max(st, axis=0, keepdims=True))
            pt = jnp.exp2(st - m_new).astype(BF16)
            acc_scr[h] = (jnp.exp2(m - m_new) * acc_scr[h]
                          + jnp.dot(vt_ref[0, t * unroll + idx // heads], pt, preferred_element_type=F32))
            m_scr[h] = m_new
        for a in range(ahead):
            st_scr[a] = pending[a]
        return carry

    lax.fori_loop(0, n_kv // unroll, body, 0)
    outs = [acc_scr[h, :HEAD_DIM] / acc_scr[h, HEAD_DIM:HEAD_DIM + 1] for h in range(heads)]
    o_ref[...] = jnp.concatenate(outs, axis=0).T.astype(o_ref.dtype)


def _flash(qt, k2, vt, tq):
    L = qt.shape[1]
    _, n_kv, v_rows, tk = vt.shape
    heads = GROUP_HEADS // C_KV_HEADS
    qrows = heads * HEAD_DIM
    return pl.pallas_call(
        functools.partial(_flash_kernel, tk=tk, heads=heads, unroll=C_UNROLL, ahead=C_AHEAD),
        grid=(C_KV_HEADS, L // tq),
        in_specs=[pl.BlockSpec((qrows, tq), lambda g, n: (g, n)),
                  pl.BlockSpec((1, L, HEAD_DIM), lambda g, n: (g, 0, 0)),
                  pl.BlockSpec((1, n_kv, v_rows, tk), lambda g, n: (g, 0, 0, 0))],
        out_specs=pl.BlockSpec((tq, qrows), lambda g, n: (n, g)),
        out_shape=jax.ShapeDtypeStruct((L, GROUP_WIDTH), BF16),
        scratch_shapes=[pltpu.VMEM((heads, 1, tq), F32), pltpu.VMEM((heads, v_rows, tq), F32),
                        pltpu.VMEM((C_AHEAD, tk, tq), F32)],
        compiler_params=pltpu.CompilerParams(
            dimension_semantics=("arbitrary", "arbitrary"), vmem_limit_bytes=VMEM_LIMIT),
        name="dense_flash",
    )(qt, k2, vt)


def _outproj_kernel(*refs, final):
    if final:
        x_ref, ya, yb, yc, yd, za, zb, zc, zd, w_ref, g_ref, fw_ref, o_ref = refs
    else:
        x_ref, ya, yb, yc, yd, za, zb, zc, zd, w_ref, g_ref, nw_ref, sc_ref, sh_ref, o_ref, h_ref = refs
    acc = jnp.zeros(x_ref.shape, F32)
    for gi, (y_ref, z_ref) in enumerate(zip((ya, yb, yc, yd), (za, zb, zc, zd))):
        z = z_ref[...].astype(F32)
        u = (y_ref[...].astype(F32) * (z * jax.nn.sigmoid(z))).astype(BF16)
        acc = acc + jnp.dot(u, w_ref[gi * GROUP_WIDTH:(gi + 1) * GROUP_WIDTH, :], preferred_element_type=F32)
    xn = x_ref[...] + g_ref[...] * acc
    if final:
        o_ref[...] = xn * lax.rsqrt(jnp.mean(xn * xn, axis=-1, keepdims=True) + EPS) * fw_ref[...]
    else:
        o_ref[...] = xn
        h_ref[...] = _modulated_norm(xn, nw_ref[...], sc_ref[...], sh_ref[...])


def _outproj(x, ys, proj, w_bf16, layer, gate, vecs):
    L, d = x.shape
    tm = min(512, L)
    final = len(vecs) == 1
    row = lambda width: pl.BlockSpec((tm, width), lambda i: (i, 0))
    vec = pl.BlockSpec((1, d), lambda i: (0, 0))
    in_specs = ([row(d)] + [row(GROUP_WIDTH)] * 4 + [_piece(z, tm) for z in ("z_a", "z_b", "z_c", "z_d")]
                + [pl.BlockSpec((None,) + w_bf16.shape[1:], lambda i: (layer, 0, 0)), vec]
                + [vec] * len(vecs))
    x_out = jax.ShapeDtypeStruct((L, d), F32)
    return pl.pallas_call(
        functools.partial(_outproj_kernel, final=final),
        grid=(L // tm,),
        in_specs=in_specs,
        out_specs=row(d) if final else [row(d), row(d)],
        out_shape=x_out if final else [x_out, jax.ShapeDtypeStruct((L, d), BF16)],
        compiler_params=pltpu.CompilerParams(
            dimension_semantics=("arbitrary",), vmem_limit_bytes=VMEM_LIMIT),
        name="outproj",
    )(x, *ys, proj, proj, proj, proj, w_bf16, gate, *vecs)


def _rope_tables(L):
    t = np.arange(L)
    axis_dim = HEAD_DIM // 2
    inv = jnp.asarray(ROPE_THETA, F32) ** (-jnp.arange(0, axis_dim, 2, dtype=F32) / axis_dim)
    row = jnp.asarray(t // GRID_W, F32)
    col = jnp.asarray(t % GRID_W, F32)
    ang = jnp.concatenate([row[:, None] * inv[None], col[:, None] * inv[None]], axis=-1)
    cos = jnp.repeat(jnp.cos(ang), 2, axis=-1)
    sin = jnp.repeat(jnp.sin(ang), 2, axis=-1) * jnp.asarray(np.tile([-1.0, 1.0], HEAD_DIM // 2), F32)
    return jnp.tile(cos, (1, LANES // HEAD_DIM)), jnp.tile(sin, (1, LANES // HEAD_DIM))


A_TQ = 256
D_TQ = 256
D_REACH = D_CONFIGS[-1][0] // 2
C_TQ = 512
C_TK = 256
C_UNROLL = 8
C_AHEAD = 2


def _table_a(t5_table):
    return _t5_table(t5_table, 0, A_TQ, A_RADIUS, A_TQ + 2 * A_RADIUS, _window_mult)


def _table_d(t5_table):
    return _t5_table(t5_table, GROUP_HEADS, D_TQ, D_REACH, D_TQ + 2 * D_REACH, _dilated_mult)


PREP_T = 1024
assert PREP_T >= D_REACH


def _seq_window(L, tq, r_lo):
    return lambda n: (n * tq + PREP_T - r_lo, (n * tq - r_lo, 0, L))


def _mixer_a(proj, qt, tab_a, sink):
    L = proj.shape[0]
    k_pad, vt_pad = _kv_prep(proj, "k_a", "v_a", LANES, PREP_T, PREP_T)
    return _band_attention(qt, k_pad, vt_pad, tab_a, group=GROUP_HEADS // A_KV_HEADS,
                           window=_seq_window(L, A_TQ, A_RADIUS), sink=sink.astype(F32))


def _mixer_b(proj, qt, tabs_b, layer):
    L = proj.shape[0]
    rows = L // GRID_W
    n_tiles = rows // NA_TILE_ROWS
    k, vt = _kv_prep(proj, "k_b", "v_b", GROUP_WIDTH, 0, PREP_T)
    window = lambda n: (jnp.clip(n * NA_TILE_ROWS - NA_ROWS // 2, 0, rows - NA_WIN_ROWS) * GRID_W, None)
    variant = lambda n: jnp.where(n == 0, 0, jnp.where(n == n_tiles - 1, 2, 1))
    return _band_attention(qt, k, vt, tabs_b, group=1, window=window, variant=variant, layer=layer)


def _mixer_c(proj, q_norm_w, k_norm_w):
    L = proj.shape[0]
    cos2, sin2 = _rope_tables(L)
    ones_bd = jnp.asarray(np.kron(np.eye(GROUP_HEADS), np.ones((HEAD_DIM, HEAD_DIM))), BF16)
    qt, k2, vt = _cprep(proj, cos2, sin2, jnp.tile(q_norm_w.astype(F32), GROUP_HEADS)[None],
                        jnp.tile(k_norm_w.astype(F32), LANES // HEAD_DIM)[None], ones_bd, C_TK)
    return _flash(qt, k2, vt, C_TQ)


def _mixer_d(proj, qt, tab_d):
    L = proj.shape[0]
    k_pad, vt_pad = _kv_prep(proj, "k_d", "v_d", GROUP_WIDTH, PREP_T, PREP_T)
    return _band_attention(qt, k_pad, vt_pad, tab_d, group=1, window=_seq_window(L, D_TQ, D_REACH))


def kernel(x, c, w_ada, b_ada, norm_w, w_in, w_out, attn_sink, na_rpb, q_norm_w, k_norm_w,
           t5_table, final_norm_w):
    B, L, D = x.shape
    assert B == 1 and L % 1024 == 0 and L // GRID_W >= NA_WIN_ROWS
    depth = w_ada.shape[0]
    x = x[0]

    mod = _ada_mod(jnp.broadcast_to(c, (8, D)), w_ada, b_ada)[:, 0:1, :]
    tab_a = _table_a(t5_table)
    tab_d = _table_d(t5_table)
    tabs_b = _na_tables(na_rpb)
    w_in_b = w_in.astype(BF16)
    w_out_b = w_out.astype(BF16)
    shift, scale, gate = jnp.split(mod, 3, axis=-1)
    h = _norm(x, norm_w[0][None], scale[0], shift[0])
    for i in range(depth):
        proj = _inproj(h, w_in_b, i)
        qt_a, qt_b, qt_d = _qt_prep(proj, ("q_a", "q_b", "q_d"))
        ys = (_mixer_a(proj, qt_a, tab_a, attn_sink[i]),
              _mixer_b(proj, qt_b, tabs_b, i),
              _mixer_c(proj, q_norm_w[i], k_norm_w[i]),
              _mixer_d(proj, qt_d, tab_d))
        if i + 1 < depth:
            x, h = _outproj(x, ys, proj, w_out_b, i, gate[i],
                            (norm_w[i + 1][None], scale[i + 1], shift[i + 1]))
        else:
            x = _outproj(x, ys, proj, w_out_b, i, gate[i], (final_norm_w[None],))
    return x[None]
```

```python
import functools
import math

import numpy as np
import jax
import jax.numpy as jnp
from jax import lax
from jax.experimental import pallas as pl
from jax.experimental.pallas import tpu as pltpu

HEAD_DIM = 64
GROUP_WIDTH = 512
GROUP_HEADS = 8
A_KV_HEADS = 2
A_RADIUS = 128
C_KV_HEADS = 2
ROPE_THETA = 10000.0
NA_ROWS = 8
NA_COLS = 16
D_CONFIGS = ((128, 1), (512, 4), (2048, 16))
GRID_W = 64
T5_BUCKETS = 32
T5_MAX_DIST = 1024
EPS = 1e-6
NEG = -1e30

V_ROWS = HEAD_DIM + 16
LANES = 128
VMEM_LIMIT = 56 * 1024 * 1024

F32 = jnp.float32
BF16 = jnp.bfloat16

_SRC = {}
_off = 0
for _name, _w in (("q_a", 512), ("k_a", 128), ("v_a", 128), ("z_a", 512),
                  ("q_b", 512), ("k_b", 512), ("v_b", 512), ("z_b", 512),
                  ("q_c", 512), ("k_c", 128), ("v_c", 128), ("z_c", 512),
                  ("q_d", 512), ("k_d", 512), ("v_d", 512), ("z_d", 512)):
    _SRC[_name] = (_off, _w)
    _off += _w
IN_WIDTH = _off


def _piece(name, t, row_block=lambda n: n):
    off, width = _SRC[name]
    return pl.BlockSpec((pl.Element(t), pl.Element(width)),
                        lambda n: (pl.multiple_of(row_block(n) * t, t), off))


def _ada_kernel(c_ref, w_ref, b_ref, o_ref):
    c = c_ref[...]
    cond = c * jax.nn.sigmoid(c)
    o_ref[0] = jnp.dot(cond, w_ref[0], preferred_element_type=F32,
                       precision=lax.Precision.HIGHEST) + b_ref[0]


def _ada_mod(c8, w_ada, b_ada):
    depth, d, n3 = w_ada.shape
    tn = 1024
    return pl.pallas_call(
        _ada_kernel,
        grid=(depth, n3 // tn),
        in_specs=[pl.BlockSpec((8, d), lambda i, j: (0, 0)),
                  pl.BlockSpec((1, d, tn), lambda i, j: (i, 0, j)),
                  pl.BlockSpec((1, 1, tn), lambda i, j: (i, 0, j))],
        out_specs=pl.BlockSpec((1, 8, tn), lambda i, j: (i, 0, j)),
        out_shape=jax.ShapeDtypeStruct((depth, 8, n3), F32),
        compiler_params=pltpu.CompilerParams(
            dimension_semantics=("arbitrary", "arbitrary"), vmem_limit_bytes=VMEM_LIMIT),
        name="ada_mod",
    )(c8, w_ada, b_ada.reshape(depth, 1, n3))


def _modulated_norm(x, nw, scale, shift):
    y = x * lax.rsqrt(jnp.mean(x * x, axis=-1, keepdims=True) + EPS)
    return ((y * nw) * (1.0 + scale) + shift).astype(BF16)


def _norm_kernel(x_ref, nw_ref, sc_ref, sh_ref, h_ref):
    h_ref[...] = _modulated_norm(x_ref[...], nw_ref[...], sc_ref[...], sh_ref[...])


def _norm(x, nw, scale, shift):
    L, d = x.shape
    tm = min(512, L)
    vec = pl.BlockSpec((1, d), lambda i: (0, 0))
    return pl.pallas_call(
        _norm_kernel,
        grid=(L // tm,),
        in_specs=[pl.BlockSpec((tm, d), lambda i: (i, 0)), vec, vec, vec],
        out_specs=pl.BlockSpec((tm, d), lambda i: (i, 0)),
        out_shape=jax.ShapeDtypeStruct((L, d), BF16),
        compiler_params=pltpu.CompilerParams(
            dimension_semantics=("arbitrary",), vmem_limit_bytes=VMEM_LIMIT),
        name="norm",
    )(x, nw, scale, shift)


def _inproj_kernel(h_ref, w_ref, o_ref):
    o_ref[...] = jnp.dot(h_ref[...], w_ref[...].astype(BF16),
                         preferred_element_type=F32).astype(o_ref.dtype)


def _inproj(h, w, layer):
    L, d = h.shape
    n = w.shape[2]
    tm = min(2048, L)
    tn = 512
    return pl.pallas_call(
        _inproj_kernel,
        grid=(L // tm, n // tn),
        in_specs=[pl.BlockSpec((tm, d), lambda i, j: (i, 0)),
                  pl.BlockSpec((None, d, tn), lambda i, j: (layer, 0, j))],
        out_specs=pl.BlockSpec((tm, tn), lambda i, j: (i, j)),
        out_shape=jax.ShapeDtypeStruct((L, n), BF16),
        compiler_params=pltpu.CompilerParams(
            dimension_semantics=("arbitrary", "arbitrary"), vmem_limit_bytes=VMEM_LIMIT),
        name="inproj",
    )(h, w)


LOG2E = math.log2(math.e)
Q_SCALE = HEAD_DIM ** -0.5 * LOG2E


def _qt_prep_kernel(*refs):
    n = len(refs) // 2
    for q_ref, qt_ref in zip(refs[:n], refs[n:]):
        qt_ref[...] = (q_ref[...].astype(F32) * Q_SCALE).T.astype(qt_ref.dtype)


def _qt_prep(proj, names, t=1024):
    L = proj.shape[0]
    return pl.pallas_call(
        _qt_prep_kernel,
        grid=(L // t,),
        in_specs=[_piece(name, t) for name in names],
        out_specs=[pl.BlockSpec((GROUP_WIDTH, t), lambda n: (0, n)) for _ in names],
        out_shape=[jax.ShapeDtypeStruct((GROUP_WIDTH, L), BF16) for _ in names],
        compiler_params=pltpu.CompilerParams(
            dimension_semantics=("arbitrary",), vmem_limit_bytes=VMEM_LIMIT),
        name="qt_prep",
    )(*([proj] * len(names)))


def _kv_prep_kernel(k_ref, v_ref, kp_ref, vt_ref, *, pad_blocks, tok_blocks):
    b = pl.program_id(0)
    is_token = (b >= pad_blocks) & (b < pad_blocks + tok_blocks)

    @pl.when(is_token)
    def _():
        kp_ref[...] = k_ref[...]
        vt = v_ref[...].astype(F32).T.astype(vt_ref.dtype)
        ones = jnp.ones((V_ROWS - HEAD_DIM, vt.shape[1]), vt_ref.dtype)
        for g in range(vt.shape[0] // HEAD_DIM):
            vt_ref[g * V_ROWS:(g + 1) * V_ROWS, :] = jnp.concatenate(
                [vt[g * HEAD_DIM:(g + 1) * HEAD_DIM], ones], axis=0)

    @pl.when(jnp.logical_not(is_token))
    def _():
        kp_ref[...] = jnp.zeros(kp_ref.shape, kp_ref.dtype)
        vt_ref[...] = jnp.zeros(vt_ref.shape, vt_ref.dtype)


def _kv_prep(proj, k_name, v_name, width, pad, t):
    L = proj.shape[0]
    assert pad % t == 0 and L % t == 0 and width % LANES == 0
    pad_blocks, tok_blocks = pad // t, L // t
    lp = L + 2 * pad
    assert _SRC[k_name][1] == width and _SRC[v_name][1] == width
    tok = lambda b: jnp.clip(b - pad_blocks, 0, tok_blocks - 1)
    return pl.pallas_call(
        functools.partial(_kv_prep_kernel, pad_blocks=pad_blocks, tok_blocks=tok_blocks),
        grid=(lp // t,),
        in_specs=[_piece(k_name, t, tok), _piece(v_name, t, tok)],
        out_specs=[pl.BlockSpec((t, width), lambda b: (b, 0)),
                   pl.BlockSpec((width // HEAD_DIM * V_ROWS, t), lambda b: (0, b))],
        out_shape=[jax.ShapeDtypeStruct((lp, width), BF16),
                   jax.ShapeDtypeStruct((width // HEAD_DIM * V_ROWS, lp), BF16)],
        compiler_params=pltpu.CompilerParams(
            dimension_semantics=("arbitrary",), vmem_limit_bytes=VMEM_LIMIT),
        name="kv_prep",
    )(proj, proj)


def _band_kernel(*refs, w_keys, chunk, heads, group, use_sink, window, variant):
    if use_sink:
        sink_ref, qt_ref, k_ref, vt_ref, tab_ref, o_ref = refs
    else:
        qt_ref, k_ref, vt_ref, tab_ref, o_ref = refs
    n = pl.program_id(0)
    _, pos = window(n)
    kwin = k_ref[...]
    vwin = vt_ref[...]
    var = variant(n)

    n_chunks = w_keys // chunk

    def attend(kbias):
        def scores(h, c):
            kv = h // group
            rows = slice(c * chunk, (c + 1) * chunk)
            st = jnp.dot(kwin[rows, kv * HEAD_DIM:(kv + 1) * HEAD_DIM],
                         qt_ref[h * HEAD_DIM:(h + 1) * HEAD_DIM, :],
                         preferred_element_type=F32) + tab_ref[var, h, rows, :]
            return st if kbias is None else st + kbias[rows]

        def col_max(m, st):
            cm = jnp.max(st, axis=0, keepdims=True)
            return cm if m is None else jnp.maximum(m, cm)

        outs = []
        cur, m = [], None
        for c in range(n_chunks):
            cur.append(scores(0, c))
            m = col_max(m, cur[-1])
        for h in range(heads):
            kv = h // group
            nxt, m_next, acc = [], None, None
            for c in range(n_chunks):
                if h + 1 < heads:
                    nxt.append(scores(h + 1, c))
                    m_next = col_max(m_next, nxt[-1])
                pt = jnp.exp2(cur[c] - m).astype(BF16)
                pv = jnp.dot(vwin[kv * V_ROWS:(kv + 1) * V_ROWS, c * chunk:(c + 1) * chunk], pt,
                             preferred_element_type=F32)
                acc = pv if acc is None else acc + pv
            o, l = acc[:HEAD_DIM], acc[HEAD_DIM:HEAD_DIM + 1]
            if use_sink:
                sk = sink_ref[h] * LOG2E
                m2 = jnp.maximum(m, sk)
                a = jnp.exp2(m - m2)
                o = o * (a / (l * a + jnp.exp2(sk - m2)))
            else:
                o = o / l
            outs.append(o)
            cur, m = nxt, m_next
        o_ref[...] = jnp.concatenate(outs, axis=0).T.astype(o_ref.dtype)

    if pos is None:
        attend(None)
    else:
        first, lo, hi = pos
        inside = (first >= lo) & (first + w_keys <= hi)

        @pl.when(inside)
        def _():
            attend(None)

        @pl.when(jnp.logical_not(inside))
        def _():
            kpos = first + lax.broadcasted_iota(jnp.int32, (w_keys, 1), 0)
            attend(jnp.where((kpos >= lo) & (kpos < hi), 0.0, NEG).astype(F32))


def _band_attention(qt, k_pad, vt_pad, table, *, chunk, group, window, variant=lambda n: 0,
                    sink=None, layer=None):
    qw, L = qt.shape
    heads = qw // HEAD_DIM
    w_keys, tq = table.shape[-2:]
    assert w_keys % chunk == 0
    kern = functools.partial(_band_kernel, w_keys=w_keys, chunk=chunk, heads=heads, group=group,
                             use_sink=sink is not None, window=window, variant=variant)
    start = lambda n: pl.multiple_of(window(n)[0], LANES)
    if layer is None:
        table_spec = pl.BlockSpec(table.shape, lambda n: (0, 0, 0, 0), pipeline_mode=pl.Buffered(1))
    else:
        table_spec = pl.BlockSpec((None,) + table.shape[1:], lambda n: (layer, 0, 0, 0, 0),
                                  pipeline_mode=pl.Buffered(1))
    in_specs = [pl.BlockSpec((qw, tq), lambda n: (0, n)),
                pl.BlockSpec((pl.Element(w_keys), pl.Element(k_pad.shape[1])), lambda n: (start(n), 0)),
                pl.BlockSpec((pl.Element(vt_pad.shape[0]), pl.Element(w_keys)), lambda n: (0, start(n))),
                table_spec]
    args = [qt, k_pad, vt_pad, table]
    if sink is not None:
        in_specs = [pl.BlockSpec(memory_space=pltpu.SMEM)] + in_specs
        args = [sink] + args
    return pl.pallas_call(
        kern,
        grid=(L // tq,),
        in_specs=in_specs,
        out_specs=pl.BlockSpec((tq, qw), lambda n: (n, 0)),
        out_shape=jax.ShapeDtypeStruct((L, qw), BF16),
        compiler_params=pltpu.CompilerParams(
            dimension_semantics=("arbitrary",), vmem_limit_bytes=VMEM_LIMIT),
        name="band_attention",
    )(*args)


def _t5_bucket(rel):
    half = T5_BUCKETS // 2
    exact = half // 2
    n = jnp.abs(rel)
    big = exact + (jnp.log(jnp.maximum(n, exact).astype(F32) / exact)
                   / math.log(T5_MAX_DIST / exact) * (half - exact)).astype(jnp.int32)
    big = jnp.minimum(big, half - 1)
    return jnp.where(rel > 0, half, 0) + jnp.where(n < exact, n, big)


def _toeplitz_kernel(rv_ref, o_ref, *, n_diag):
    w, tq = o_ref.shape[1:]
    blocks = []
    for d in range(n_diag):
        s = LANES * (n_diag - 1 - d)
        x = jnp.broadcast_to(rv_ref[0, :, s:s + 2 * LANES], (LANES, 2 * LANES))
        blocks.append(pltpu.roll(x, LANES, 1, stride=1, stride_axis=0)[:, :LANES])
    for cb in range(w // LANES):
        for ib in range(tq // LANES):
            o_ref[0, cb * LANES:(cb + 1) * LANES, ib * LANES:(ib + 1) * LANES] = (
                blocks[cb - ib + tq // LANES - 1])


def _toeplitz(v, tq, w):
    heads, n_rel = v.shape
    n_diag = w // LANES + tq // LANES - 1
    assert w % LANES == 0 and tq % LANES == 0 and n_rel == LANES * n_diag + LANES - 1
    rv = jnp.pad(v, ((0, 0), (0, 1)))[:, None, ::-1]
    return pl.pallas_call(
        functools.partial(_toeplitz_kernel, n_diag=n_diag),
        grid=(heads,),
        in_specs=[pl.BlockSpec((1, 1, rv.shape[-1]), lambda h: (h, 0, 0))],
        out_specs=pl.BlockSpec((1, w, tq), lambda h: (h, 0, 0)),
        out_shape=jax.ShapeDtypeStruct((heads, w, tq), F32),
        compiler_params=pltpu.CompilerParams(
            dimension_semantics=("arbitrary",), vmem_limit_bytes=VMEM_LIMIT),
        name="toeplitz",
    )(rv)


def _t5_table(t5_table, head_lo, tq, r_lo, w_keys, mult):
    rel = np.arange(-r_lo - (tq - 1), w_keys - r_lo)
    m = mult(rel)
    b = t5_table.astype(F32)[:, head_lo:head_lo + GROUP_HEADS][_t5_bucket(jnp.asarray(rel, jnp.int32))]
    logm = np.log(np.maximum(m, 1)).astype(np.float32)
    vec = jnp.where(jnp.asarray(m > 0)[:, None], (b + logm[:, None]) * LOG2E, NEG).T
    return _toeplitz(vec, tq, w_keys)[None]


def _window_mult(rel):
    return (np.abs(rel) <= A_RADIUS).astype(np.int32)


def _dilated_mult(rel):
    m = np.zeros(rel.shape, np.int32)
    for window, dil in D_CONFIGS:
        m += ((rel % dil == 0) & (np.abs(rel) <= window // 2)).astype(np.int32)
    return m


NA_TILE_ROWS = 4
NA_WIN_ROWS = NA_TILE_ROWS + NA_ROWS
NA_TQ = NA_TILE_ROWS * GRID_W
NA_W = NA_WIN_ROWS * GRID_W


def _na_table(rpb):
    return _na_tables(rpb[None])[0]


def _na_row_index(variant, j, i):
    dr, ok = ((j - i, j < NA_ROWS),
              (j - NA_ROWS // 2 - i, i <= j < i + NA_ROWS),
              (j - NA_ROWS - i, j >= NA_TILE_ROWS))[variant]
    return dr + NA_ROWS - 1 if ok else None


def _na_table_kernel(rp_ref, o_ref):
    kc = lax.broadcasted_iota(jnp.int32, (GRID_W, LANES), 0)
    lane = lax.broadcasted_iota(jnp.int32, (GRID_W, LANES), 1)
    col_start = jnp.clip(lane % GRID_W - NA_COLS // 2, 0, GRID_W - NA_COLS)
    col_ok = (kc >= col_start) & (kc < col_start + NA_COLS)
    neg = jnp.full((GRID_W, LANES), NEG, F32)
    cache = {}

    def half_block(d, side):
        if d is None:
            return neg
        if (d, side) not in cache:
            x = jnp.broadcast_to(rp_ref[0, 0, d:d + 1, :], (GRID_W, LANES))
            y = pltpu.roll(x, GRID_W * (1 - side), 1, stride=1, stride_axis=0)
            cache[d, side] = jnp.where(col_ok, y, NEG)
        return cache[d, side]

    for variant in range(3):
        for j in range(NA_WIN_ROWS):
            for ip in range(NA_TILE_ROWS // 2):
                left = half_block(_na_row_index(variant, j, 2 * ip), 0)
                right = half_block(_na_row_index(variant, j, 2 * ip + 1), 1)
                o_ref[0, variant, 0, j * GRID_W:(j + 1) * GRID_W, ip * LANES:(ip + 1) * LANES] = (
                    jnp.where(lane < GRID_W, left, right))


def _na_tables(rpb):
    depth, heads, n_dr, n_dc = rpb.shape
    assert n_dr == NA_WIN_ROWS + NA_TILE_ROWS - 1 and 2 * GRID_W == LANES
    front = GRID_W - NA_COLS
    rp = jnp.pad(rpb.astype(F32) * LOG2E, ((0, 0), (0, 0), (0, 1), (front, LANES - n_dc - front)))
    rp = rp[..., ::-1]
    return pl.pallas_call(
        _na_table_kernel,
        grid=(depth, heads),
        in_specs=[pl.BlockSpec((1, 1, n_dr + 1, LANES), lambda l, h: (l, h, 0, 0))],
        out_specs=pl.BlockSpec((1, 3, 1, NA_W, NA_TQ), lambda l, h: (l, 0, h, 0, 0)),
        out_shape=jax.ShapeDtypeStruct((depth, 3, heads, NA_W, NA_TQ), F32),
        compiler_params=pltpu.CompilerParams(
            dimension_semantics=("arbitrary", "arbitrary"), vmem_limit_bytes=VMEM_LIMIT),
        name="na_tables",
    )(rp)


def _swap_pairs(x):
    n = x.shape[-1]
    lane = lax.broadcasted_iota(jnp.int32, x.shape, x.ndim - 1)
    return jnp.where(lane % 2 == 0, pltpu.roll(x, n - 1, x.ndim - 1), pltpu.roll(x, 1, x.ndim - 1))


def _head_rms(x, ones_bd, w):
    sq = x * x
    hi = sq.astype(BF16)
    lo = (sq - hi.astype(F32)).astype(BF16)
    ms = (jnp.dot(hi, ones_bd, preferred_element_type=F32)
          + jnp.dot(lo, ones_bd, preferred_element_type=F32)) * (1.0 / HEAD_DIM)
    return x * lax.rsqrt(ms + EPS) * w


def _cprep_kernel(q_ref, k_ref, v_ref, cos_ref, sin_ref, qw_ref, kw_ref, bd_ref,
                  qt_ref, k2_ref, vt_ref):
    cos = cos_ref[...]
    sin = sin_ref[...]
    q = _head_rms(q_ref[...].astype(F32), bd_ref[...], qw_ref[...])
    reps = q.shape[1] // LANES
    q = q * jnp.concatenate([cos] * reps, axis=1) + _swap_pairs(q) * jnp.concatenate([sin] * reps, axis=1)
    qt_ref[...] = (q * Q_SCALE).T.astype(qt_ref.dtype)
    k = _head_rms(k_ref[...].astype(F32), bd_ref[:LANES, :LANES], kw_ref[...])
    k = (k * cos + _swap_pairs(k) * sin).astype(k2_ref.dtype)
    vt = v_ref[...].astype(F32).T.astype(vt_ref.dtype)
    ones = jnp.ones((V_ROWS - HEAD_DIM, vt.shape[1]), vt_ref.dtype)
    tk = vt_ref.shape[-1]
    for g in range(C_KV_HEADS):
        k2_ref[g] = k[:, g * HEAD_DIM:(g + 1) * HEAD_DIM]
        vg = jnp.concatenate([vt[g * HEAD_DIM:(g + 1) * HEAD_DIM], ones], axis=0)
        for u in range(vt_ref.shape[1]):
            vt_ref[g, u] = vg[:, u * tk:(u + 1) * tk]


def _cprep(proj, cos2, sin2, qw, kw, ones_bd, tk, t=1024):
    L = proj.shape[0]
    vec = lambda width: pl.BlockSpec((1, width), lambda n: (0, 0))
    return pl.pallas_call(
        _cprep_kernel,
        grid=(L // t,),
        in_specs=[_piece("q_c", t), _piece("k_c", t), _piece("v_c", t),
                  pl.BlockSpec((t, LANES), lambda n: (n, 0)),
                  pl.BlockSpec((t, LANES), lambda n: (n, 0)),
                  vec(GROUP_WIDTH), vec(LANES),
                  pl.BlockSpec((GROUP_WIDTH, GROUP_WIDTH), lambda n: (0, 0))],
        out_specs=[pl.BlockSpec((GROUP_WIDTH, t), lambda n: (0, n)),
                   pl.BlockSpec((C_KV_HEADS, t, HEAD_DIM), lambda n: (0, n, 0)),
                   pl.BlockSpec((C_KV_HEADS, t // tk, V_ROWS, tk), lambda n: (0, n, 0, 0))],
        out_shape=[jax.ShapeDtypeStruct((GROUP_WIDTH, L), BF16),
                   jax.ShapeDtypeStruct((C_KV_HEADS, L, HEAD_DIM), BF16),
                   jax.ShapeDtypeStruct((C_KV_HEADS, L // tk, V_ROWS, tk), BF16)],
        compiler_params=pltpu.CompilerParams(
            dimension_semantics=("arbitrary",), vmem_limit_bytes=VMEM_LIMIT),
        name="dense_prep",
    )(proj, proj, proj, cos2, sin2, qw, kw, ones_bd)


def _flash_kernel(qt_ref, k_ref, vt_ref, o_ref, m_scr, acc_scr, st_scr, *, tk, heads, unroll, ahead):
    n_kv = k_ref.shape[1] // tk
    m_scr[...] = jnp.full(m_scr.shape, NEG, F32)
    acc_scr[...] = jnp.zeros(acc_scr.shape, F32)

    def scores(j, h):
        k = k_ref[0, pl.ds(pl.multiple_of(j * tk, tk), tk), :]
        qt = qt_ref[h * HEAD_DIM:(h + 1) * HEAD_DIM, :]
        return jnp.dot(k, qt, preferred_element_type=F32)

    n_items = unroll * heads

    def item_scores(t, idx):
        j = t * unroll + idx // heads
        return scores(jnp.minimum(j, n_kv - 1), idx % heads)

    for a in range(ahead):
        st_scr[a] = item_scores(0, a)

    def body(t, carry):
        pending = [st_scr[a] for a in range(ahead)]
        for idx in range(n_items):
            pending.append(item_scores(t, idx + ahead))
            st = pending.pop(0)
            h = idx % heads
            m = m_scr[h]
            m_new = jnp.maximum(m, jnp.max(st, axis=0, keepdims=True))
            pt = jnp.exp2(st - m_new).astype(BF16)
            acc_scr[h] = (jnp.exp2(m - m_new) * acc_scr[h]
                          + jnp.dot(vt_ref[0, t * unroll + idx // heads], pt, preferred_element_type=F32))
            m_scr[h] = m_new
        for a in range(ahead):
            st_scr[a] = pending[a]
        return carry

    lax.fori_loop(0, n_kv // unroll, body, 0)
    outs = [acc_scr[h, :HEAD_DIM] / acc_scr[h, HEAD_DIM:HEAD_DIM + 1] for h in range(heads)]
    o_ref[...] = jnp.concatenate(outs, axis=0).T.astype(o_ref.dtype)


def _flash(qt, k2, vt, tq):
    L = qt.shape[1]
    _, n_kv, v_rows, tk = vt.shape
    heads = GROUP_HEADS // C_KV_HEADS
    qrows = heads * HEAD_DIM
    return pl.pallas_call(
        functools.partial(_flash_kernel, tk=tk, heads=heads, unroll=C_UNROLL, ahead=C_AHEAD),
        grid=(C_KV_HEADS, L // tq),
        in_specs=[pl.BlockSpec((qrows, tq), lambda g, n: (g, n)),
                  pl.BlockSpec((1, L, HEAD_DIM), lambda g, n: (g, 0, 0)),
                  pl.BlockSpec((1, n_kv, v_rows, tk), lambda g, n: (g, 0, 0, 0))],
        out_specs=pl.BlockSpec((tq, qrows), lambda g, n: (n, g)),
        out_shape=jax.ShapeDtypeStruct((L, GROUP_WIDTH), BF16),
        scratch_shapes=[pltpu.VMEM((heads, 1, tq), F32), pltpu.VMEM((heads, v_rows, tq), F32),
                        pltpu.VMEM((C_AHEAD, tk, tq), F32)],
        compiler_params=pltpu.CompilerParams(
            dimension_semantics=("arbitrary", "arbitrary"), vmem_limit_bytes=VMEM_LIMIT),
        name="dense_flash",
    )(qt, k2, vt)


def _outproj_kernel(*refs, final):
    if final:
        x_ref, ya, yb, yc, yd, za, zb, zc, zd, w_ref, g_ref, fw_ref, o_ref = refs
    else:
        x_ref, ya, yb, yc, yd, za, zb, zc, zd, w_ref, g_ref, nw_ref, sc_ref, sh_ref, o_ref, h_ref = refs
    acc = jnp.zeros(x_ref.shape, F32)
    for gi, (y_ref, z_ref) in enumerate(zip((ya, yb, yc, yd), (za, zb, zc, zd))):
        z = z_ref[...].astype(F32)
        u = (y_ref[...].astype(F32) * (z * jax.nn.sigmoid(z))).astype(BF16)
        acc = acc + jnp.dot(u, w_ref[gi * GROUP_WIDTH:(gi + 1) * GROUP_WIDTH, :], preferred_element_type=F32)
    xn = x_ref[...] + g_ref[...] * acc
    if final:
        o_ref[...] = xn * lax.rsqrt(jnp.mean(xn * xn, axis=-1, keepdims=True) + EPS) * fw_ref[...]
    else:
        o_ref[...] = xn
        h_ref[...] = _modulated_norm(xn, nw_ref[...], sc_ref[...], sh_ref[...])


def _outproj(x, ys, proj, w_bf16, layer, gate, vecs):
    L, d = x.shape
    tm = min(512, L)
    final = len(vecs) == 1
    row = lambda width: pl.BlockSpec((tm, width), lambda i: (i, 0))
    vec = pl.BlockSpec((1, d), lambda i: (0, 0))
    in_specs = ([row(d)] + [row(GROUP_WIDTH)] * 4 + [_piece(z, tm) for z in ("z_a", "z_b", "z_c", "z_d")]
                + [pl.BlockSpec((None,) + w_bf16.shape[1:], lambda i: (layer, 0, 0)), vec]
                + [vec] * len(vecs))
    x_out = jax.ShapeDtypeStruct((L, d), F32)
    return pl.pallas_call(
        functools.partial(_outproj_kernel, final=final),
        grid=(L // tm,),
        in_specs=in_specs,
        out_specs=row(d) if final else [row(d), row(d)],
        out_shape=x_out if final else [x_out, jax.ShapeDtypeStruct((L, d), BF16)],
        compiler_params=pltpu.CompilerParams(
            dimension_semantics=("arbitrary",), vmem_limit_bytes=VMEM_LIMIT),
        name="outproj",
    )(x, *ys, proj, proj, proj, proj, w_bf16, gate, *vecs)


def _rope_tables(L):
    t = np.arange(L)
    axis_dim = HEAD_DIM // 2
    inv = jnp.asarray(ROPE_THETA, F32) ** (-jnp.arange(0, axis_dim, 2, dtype=F32) / axis_dim)
    row = jnp.asarray(t // GRID_W, F32)
    col = jnp.asarray(t % GRID_W, F32)
    ang = jnp.concatenate([row[:, None] * inv[None], col[:, None] * inv[None]], axis=-1)
    cos = jnp.repeat(jnp.cos(ang), 2, axis=-1)
    sin = jnp.repeat(jnp.sin(ang), 2, axis=-1) * jnp.asarray(np.tile([-1.0, 1.0], HEAD_DIM // 2), F32)
    return jnp.tile(cos, (1, LANES // HEAD_DIM)), jnp.tile(sin, (1, LANES // HEAD_DIM))


A_TQ = 256
A_CHUNK = 256
NA_CHUNK = 256
D_CHUNK = 768
D_TQ = 256
D_REACH = D_CONFIGS[-1][0] // 2
C_TQ = 512
C_TK = 256
C_UNROLL = 8
C_AHEAD = 2


def _table_a(t5_table):
    return _t5_table(t5_table, 0, A_TQ, A_RADIUS, A_TQ + 2 * A_RADIUS, _window_mult)


def _table_d(t5_table):
    return _t5_table(t5_table, GROUP_HEADS, D_TQ, D_REACH, D_TQ + 2 * D_REACH, _dilated_mult)


PREP_T = 1024
assert PREP_T >= D_REACH


def _seq_window(L, tq, r_lo):
    return lambda n: (n * tq + PREP_T - r_lo, (n * tq - r_lo, 0, L))


def _mixer_a(proj, qt, tab_a, sink):
    L = proj.shape[0]
    k_pad, vt_pad = _kv_prep(proj, "k_a", "v_a", LANES, PREP_T, PREP_T)
    return _band_attention(qt, k_pad, vt_pad, tab_a, chunk=A_CHUNK, group=GROUP_HEADS // A_KV_HEADS,
                           window=_seq_window(L, A_TQ, A_RADIUS), sink=sink.astype(F32))


def _mixer_b(proj, qt, tabs_b, layer):
    L = proj.shape[0]
    rows = L // GRID_W
    n_tiles = rows // NA_TILE_ROWS
    k, vt = _kv_prep(proj, "k_b", "v_b", GROUP_WIDTH, 0, PREP_T)
    window = lambda n: (jnp.clip(n * NA_TILE_ROWS - NA_ROWS // 2, 0, rows - NA_WIN_ROWS) * GRID_W, None)
    variant = lambda n: jnp.where(n == 0, 0, jnp.where(n == n_tiles - 1, 2, 1))
    return _band_attention(qt, k, vt, tabs_b, chunk=NA_CHUNK, group=1, window=window, variant=variant,
                           layer=layer)


def _mixer_c(proj, q_norm_w, k_norm_w):
    L = proj.shape[0]
    cos2, sin2 = _rope_tables(L)
    ones_bd = jnp.asarray(np.kron(np.eye(GROUP_HEADS), np.ones((HEAD_DIM, HEAD_DIM))), BF16)
    qt, k2, vt = _cprep(proj, cos2, sin2, jnp.tile(q_norm_w.astype(F32), GROUP_HEADS)[None],
                        jnp.tile(k_norm_w.astype(F32), LANES // HEAD_DIM)[None], ones_bd, C_TK)
    return _flash(qt, k2, vt, C_TQ)


def _mixer_d(proj, qt, tab_d):
    L = proj.shape[0]
    k_pad, vt_pad = _kv_prep(proj, "k_d", "v_d", GROUP_WIDTH, PREP_T, PREP_T)
    return _band_attention(qt, k_pad, vt_pad, tab_d, chunk=D_CHUNK, group=1,
                           window=_seq_window(L, D_TQ, D_REACH))


def kernel(x, c, w_ada, b_ada, norm_w, w_in, w_out, attn_sink, na_rpb, q_norm_w, k_norm_w,
           t5_table, final_norm_w):
    B, L, D = x.shape
    assert B == 1 and L % 1024 == 0 and L // GRID_W >= NA_WIN_ROWS
    depth = w_ada.shape[0]
    x = x[0]

    mod = _ada_mod(jnp.broadcast_to(c, (8, D)), w_ada, b_ada)[:, 0:1, :]
    tab_a = _table_a(t5_table)
    tab_d = _table_d(t5_table)
    tabs_b = _na_tables(na_rpb)
    w_out_b = w_out.astype(BF16)
    shift, scale, gate = jnp.split(mod, 3, axis=-1)
    h = _norm(x, norm_w[0][None], scale[0], shift[0])
    for i in range(depth):
        proj = _inproj(h, w_in, i)
        qt_a, qt_b, qt_d = _qt_prep(proj, ("q_a", "q_b", "q_d"))
        ys = (_mixer_a(proj, qt_a, tab_a, attn_sink[i]),
              _mixer_b(proj, qt_b, tabs_b, i),
              _mixer_c(proj, q_norm_w[i], k_norm_w[i]),
              _mixer_d(proj, qt_d, tab_d))
        if i + 1 < depth:
            x, h = _outproj(x, ys, proj, w_out_b, i, gate[i],
                            (norm_w[i + 1][None], scale[i + 1], shift[i + 1]))
        else:
            x = _outproj(x, ys, proj, w_out_b, i, gate[i], (final_norm_w[None],))
    return x[None]
```

```python
import functools
import math

import numpy as np
import jax
import jax.numpy as jnp
from jax import lax
from jax.experimental import pallas as pl
from jax.experimental.pallas import tpu as pltpu

HEAD_DIM = 64
GROUP_WIDTH = 512
GROUP_HEADS = 8
A_KV_HEADS = 2
A_RADIUS = 128
C_KV_HEADS = 2
ROPE_THETA = 10000.0
NA_ROWS = 8
NA_COLS = 16
D_CONFIGS = ((128, 1), (512, 4), (2048, 16))
GRID_W = 64
T5_BUCKETS = 32
T5_MAX_DIST = 1024
EPS = 1e-6
NEG = -1e30

V_ROWS = HEAD_DIM + 16
LANES = 128
VMEM_LIMIT = 56 * 1024 * 1024

F32 = jnp.float32
BF16 = jnp.bfloat16

_SRC = {}
_off = 0
for _name, _w in (("q_a", 512), ("k_a", 128), ("v_a", 128), ("z_a", 512),
                  ("q_b", 512), ("k_b", 512), ("v_b", 512), ("z_b", 512),
                  ("q_c", 512), ("k_c", 128), ("v_c", 128), ("z_c", 512),
                  ("q_d", 512), ("k_d", 512), ("v_d", 512), ("z_d", 512)):
    _SRC[_name] = (_off, _w)
    _off += _w
IN_WIDTH = _off


def _piece(name, t, row_block=lambda n: n):
    off, width = _SRC[name]
    return pl.BlockSpec((pl.Element(t), pl.Element(width)),
                        lambda n: (pl.multiple_of(row_block(n) * t, t), off))


def _ada_kernel(c_ref, w_ref, b_ref, o_ref):
    c = c_ref[...]
    cond = c * jax.nn.sigmoid(c)
    o_ref[0] = jnp.dot(cond, w_ref[0], preferred_element_type=F32,
                       precision=lax.Precision.HIGHEST) + b_ref[0]


def _ada_mod(c8, w_ada, b_ada):
    depth, d, n3 = w_ada.shape
    tn = 1024
    return pl.pallas_call(
        _ada_kernel,
        grid=(depth, n3 // tn),
        in_specs=[pl.BlockSpec((8, d), lambda i, j: (0, 0)),
                  pl.BlockSpec((1, d, tn), lambda i, j: (i, 0, j)),
                  pl.BlockSpec((1, 1, tn), lambda i, j: (i, 0, j))],
        out_specs=pl.BlockSpec((1, 8, tn), lambda i, j: (i, 0, j)),
        out_shape=jax.ShapeDtypeStruct((depth, 8, n3), F32),
        compiler_params=pltpu.CompilerParams(
            dimension_semantics=("arbitrary", "arbitrary"), vmem_limit_bytes=VMEM_LIMIT),
        name="ada_mod",
    )(c8, w_ada, b_ada.reshape(depth, 1, n3))


def _modulated_norm(x, nw, scale, shift):
    y = x * lax.rsqrt(jnp.mean(x * x, axis=-1, keepdims=True) + EPS)
    return ((y * nw) * (1.0 + scale) + shift).astype(BF16)


def _norm_kernel(x_ref, nw_ref, sc_ref, sh_ref, h_ref):
    h_ref[...] = _modulated_norm(x_ref[...], nw_ref[...], sc_ref[...], sh_ref[...])


def _norm(x, nw, scale, shift):
    L, d = x.shape
    tm = min(512, L)
    vec = pl.BlockSpec((1, d), lambda i: (0, 0))
    return pl.pallas_call(
        _norm_kernel,
        grid=(L // tm,),
        in_specs=[pl.BlockSpec((tm, d), lambda i: (i, 0)), vec, vec, vec],
        out_specs=pl.BlockSpec((tm, d), lambda i: (i, 0)),
        out_shape=jax.ShapeDtypeStruct((L, d), BF16),
        compiler_params=pltpu.CompilerParams(
            dimension_semantics=("arbitrary",), vmem_limit_bytes=VMEM_LIMIT),
        name="norm",
    )(x, nw, scale, shift)


def _inproj_kernel(h_ref, w_ref, o_ref):
    o_ref[...] = jnp.dot(h_ref[...], w_ref[...].astype(BF16),
                         preferred_element_type=F32).astype(o_ref.dtype)


def _inproj(h, w, layer):
    L, d = h.shape
    n = w.shape[2]
    tm = min(2048, L)
    tn = 512
    return pl.pallas_call(
        _inproj_kernel,
        grid=(L // tm, n // tn),
        in_specs=[pl.BlockSpec((tm, d), lambda i, j: (i, 0)),
                  pl.BlockSpec((None, d, tn), lambda i, j: (layer, 0, j))],
        out_specs=pl.BlockSpec((tm, tn), lambda i, j: (i, j)),
        out_shape=jax.ShapeDtypeStruct((L, n), BF16),
        compiler_params=pltpu.CompilerParams(
            dimension_semantics=("arbitrary", "arbitrary"), vmem_limit_bytes=VMEM_LIMIT),
        name="inproj",
    )(h, w)


LOG2E = math.log2(math.e)
Q_SCALE = HEAD_DIM ** -0.5 * LOG2E


def _qt_prep_kernel(*refs):
    n = len(refs) // 2
    for q_ref, qt_ref in zip(refs[:n], refs[n:]):
        qt_ref[...] = (q_ref[...].astype(F32) * Q_SCALE).T.astype(qt_ref.dtype)


def _qt_prep(proj, names, t=1024):
    L = proj.shape[0]
    return pl.pallas_call(
        _qt_prep_kernel,
        grid=(L // t,),
        in_specs=[_piece(name, t) for name in names],
        out_specs=[pl.BlockSpec((GROUP_WIDTH, t), lambda n: (0, n)) for _ in names],
        out_shape=[jax.ShapeDtypeStruct((GROUP_WIDTH, L), BF16) for _ in names],
        compiler_params=pltpu.CompilerParams(
            dimension_semantics=("arbitrary",), vmem_limit_bytes=VMEM_LIMIT),
        name="qt_prep",
    )(*([proj] * len(names)))


def _kv_prep_kernel(k_ref, v_ref, kp_ref, vt_ref, *, pad_blocks, tok_blocks):
    b = pl.program_id(0)
    is_token = (b >= pad_blocks) & (b < pad_blocks + tok_blocks)

    @pl.when(is_token)
    def _():
        kp_ref[...] = k_ref[...]
        vt = v_ref[...].astype(F32).T.astype(vt_ref.dtype)
        ones = jnp.ones((V_ROWS - HEAD_DIM, vt.shape[1]), vt_ref.dtype)
        for g in range(vt.shape[0] // HEAD_DIM):
            vt_ref[g * V_ROWS:(g + 1) * V_ROWS, :] = jnp.concatenate(
                [vt[g * HEAD_DIM:(g + 1) * HEAD_DIM], ones], axis=0)

    @pl.when(jnp.logical_not(is_token))
    def _():
        kp_ref[...] = jnp.zeros(kp_ref.shape, kp_ref.dtype)
        vt_ref[...] = jnp.zeros(vt_ref.shape, vt_ref.dtype)


def _kv_prep(proj, k_name, v_name, width, pad, t):
    L = proj.shape[0]
    assert pad % t == 0 and L % t == 0 and width % LANES == 0
    pad_blocks, tok_blocks = pad // t, L // t
    lp = L + 2 * pad
    assert _SRC[k_name][1] == width and _SRC[v_name][1] == width
    tok = lambda b: jnp.clip(b - pad_blocks, 0, tok_blocks - 1)
    return pl.pallas_call(
        functools.partial(_kv_prep_kernel, pad_blocks=pad_blocks, tok_blocks=tok_blocks),
        grid=(lp // t,),
        in_specs=[_piece(k_name, t, tok), _piece(v_name, t, tok)],
        out_specs=[pl.BlockSpec((t, width), lambda b: (b, 0)),
                   pl.BlockSpec((width // HEAD_DIM * V_ROWS, t), lambda b: (0, b))],
        out_shape=[jax.ShapeDtypeStruct((lp, width), BF16),
                   jax.ShapeDtypeStruct((width // HEAD_DIM * V_ROWS, lp), BF16)],
        compiler_params=pltpu.CompilerParams(
            dimension_semantics=("arbitrary",), vmem_limit_bytes=VMEM_LIMIT),
        name="kv_prep",
    )(proj, proj)


def _band_kernel(*refs, w_keys, chunk, heads, group, use_sink, window, variant):
    if use_sink:
        sink_ref, qt_ref, k_ref, vt_ref, tab_ref, o_ref = refs
    else:
        qt_ref, k_ref, vt_ref, tab_ref, o_ref = refs
    n = pl.program_id(0)
    _, pos = window(n)
    kwin = k_ref[...]
    vwin = vt_ref[...]
    var = variant(n)

    n_chunks = w_keys // chunk

    def attend(kbias):
        def scores(h, c):
            kv = h // group
            rows = slice(c * chunk, (c + 1) * chunk)
            st = jnp.dot(kwin[rows, kv * HEAD_DIM:(kv + 1) * HEAD_DIM],
                         qt_ref[h * HEAD_DIM:(h + 1) * HEAD_DIM, :],
                         preferred_element_type=F32) + tab_ref[var, h, rows, :]
            return st if kbias is None else st + kbias[rows]

        def col_max(m, st):
            cm = jnp.max(st, axis=0, keepdims=True)
            return cm if m is None else jnp.maximum(m, cm)

        outs = []
        cur, m = [], None
        for c in range(n_chunks):
            cur.append(scores(0, c))
            m = col_max(m, cur[-1])
        for h in range(heads):
            kv = h // group
            nxt, m_next, acc = [], None, None
            for c in range(n_chunks):
                if h + 1 < heads:
                    nxt.append(scores(h + 1, c))
                    m_next = col_max(m_next, nxt[-1])
                pt = jnp.exp2(cur[c] - m).astype(BF16)
                pv = jnp.dot(vwin[kv * V_ROWS:(kv + 1) * V_ROWS, c * chunk:(c + 1) * chunk], pt,
                             preferred_element_type=F32)
                acc = pv if acc is None else acc + pv
            o, l = acc[:HEAD_DIM], acc[HEAD_DIM:HEAD_DIM + 1]
            if use_sink:
                sk = sink_ref[h] * LOG2E
                m2 = jnp.maximum(m, sk)
                a = jnp.exp2(m - m2)
                o = o * (a / (l * a + jnp.exp2(sk - m2)))
            else:
                o = o / l
            outs.append(o)
            cur, m = nxt, m_next
        o_ref[...] = jnp.concatenate(outs, axis=0).T.astype(o_ref.dtype)

    if pos is None:
        attend(None)
    else:
        first, lo, hi = pos
        inside = (first >= lo) & (first + w_keys <= hi)

        @pl.when(inside)
        def _():
            attend(None)

        @pl.when(jnp.logical_not(inside))
        def _():
            kpos = first + lax.broadcasted_iota(jnp.int32, (w_keys, 1), 0)
            attend(jnp.where((kpos >= lo) & (kpos < hi), 0.0, NEG).astype(F32))


def _band_attention(qt, k_pad, vt_pad, table, *, chunk, group, window, variant=lambda n: 0,
                    sink=None, layer=None):
    qw, L = qt.shape
    heads = qw // HEAD_DIM
    w_keys, tq = table.shape[-2:]
    assert w_keys % chunk == 0
    kern = functools.partial(_band_kernel, w_keys=w_keys, chunk=chunk, heads=heads, group=group,
                             use_sink=sink is not None, window=window, variant=variant)
    start = lambda n: pl.multiple_of(window(n)[0], LANES)
    if layer is None:
        table_spec = pl.BlockSpec(table.shape, lambda n: (0, 0, 0, 0), pipeline_mode=pl.Buffered(1))
    else:
        table_spec = pl.BlockSpec((None,) + table.shape[1:], lambda n: (layer, 0, 0, 0, 0),
                                  pipeline_mode=pl.Buffered(1))
    in_specs = [pl.BlockSpec((qw, tq), lambda n: (0, n)),
                pl.BlockSpec((pl.Element(w_keys), pl.Element(k_pad.shape[1])), lambda n: (start(n), 0)),
                pl.BlockSpec((pl.Element(vt_pad.shape[0]), pl.Element(w_keys)), lambda n: (0, start(n))),
                table_spec]
    args = [qt, k_pad, vt_pad, table]
    if sink is not None:
        in_specs = [pl.BlockSpec(memory_space=pltpu.SMEM)] + in_specs
        args = [sink] + args
    return pl.pallas_call(
        kern,
        grid=(L // tq,),
        in_specs=in_specs,
        out_specs=pl.BlockSpec((tq, qw), lambda n: (n, 0)),
        out_shape=jax.ShapeDtypeStruct((L, qw), BF16),
        compiler_params=pltpu.CompilerParams(
            dimension_semantics=("arbitrary",), vmem_limit_bytes=VMEM_LIMIT),
        name="band_attention",
    )(*args)


def _t5_bucket(rel):
    half = T5_BUCKETS // 2
    exact = half // 2
    n = jnp.abs(rel)
    big = exact + (jnp.log(jnp.maximum(n, exact).astype(F32) / exact)
                   / math.log(T5_MAX_DIST / exact) * (half - exact)).astype(jnp.int32)
    big = jnp.minimum(big, half - 1)
    return jnp.where(rel > 0, half, 0) + jnp.where(n < exact, n, big)


def _toeplitz_kernel(rv_ref, o_ref, *, n_diag):
    w, tq = o_ref.shape[1:]
    blocks = []
    for d in range(n_diag):
        s = LANES * (n_diag - 1 - d)
        x = jnp.broadcast_to(rv_ref[0, :, s:s + 2 * LANES], (LANES, 2 * LANES))
        blocks.append(pltpu.roll(x, LANES, 1, stride=1, stride_axis=0)[:, :LANES])
    for cb in range(w // LANES):
        for ib in range(tq // LANES):
            o_ref[0, cb * LANES:(cb + 1) * LANES, ib * LANES:(ib + 1) * LANES] = (
                blocks[cb - ib + tq // LANES - 1])


def _toeplitz(v, tq, w):
    heads, n_rel = v.shape
    n_diag = w // LANES + tq // LANES - 1
    assert w % LANES == 0 and tq % LANES == 0 and n_rel == LANES * n_diag + LANES - 1
    rv = jnp.pad(v, ((0, 0), (0, 1)))[:, None, ::-1]
    return pl.pallas_call(
        functools.partial(_toeplitz_kernel, n_diag=n_diag),
        grid=(heads,),
        in_specs=[pl.BlockSpec((1, 1, rv.shape[-1]), lambda h: (h, 0, 0))],
        out_specs=pl.BlockSpec((1, w, tq), lambda h: (h, 0, 0)),
        out_shape=jax.ShapeDtypeStruct((heads, w, tq), F32),
        compiler_params=pltpu.CompilerParams(
            dimension_semantics=("arbitrary",), vmem_limit_bytes=VMEM_LIMIT),
        name="toeplitz",
    )(rv)


def _t5_table(t5_table, head_lo, tq, r_lo, w_keys, mult):
    rel = np.arange(-r_lo - (tq - 1), w_keys - r_lo)
    m = mult(rel)
    b = t5_table.astype(F32)[:, head_lo:head_lo + GROUP_HEADS][_t5_bucket(jnp.asarray(rel, jnp.int32))]
    logm = np.log(np.maximum(m, 1)).astype(np.float32)
    vec = jnp.where(jnp.asarray(m > 0)[:, None], (b + logm[:, None]) * LOG2E, NEG).T
    return _toeplitz(vec, tq, w_keys)[None]


def _window_mult(rel):
    return (np.abs(rel) <= A_RADIUS).astype(np.int32)


def _dilated_mult(rel):
    m = np.zeros(rel.shape, np.int32)
    for window, dil in D_CONFIGS:
        m += ((rel % dil == 0) & (np.abs(rel) <= window // 2)).astype(np.int32)
    return m


NA_TILE_ROWS = 4
NA_WIN_ROWS = NA_TILE_ROWS + NA_ROWS
NA_TQ = NA_TILE_ROWS * GRID_W
NA_W = NA_WIN_ROWS * GRID_W


def _na_table(rpb):
    return _na_tables(rpb[None])[0]


def _na_row_index(variant, j, i):
    dr, ok = ((j - i, j < NA_ROWS),
              (j - NA_ROWS // 2 - i, i <= j < i + NA_ROWS),
              (j - NA_ROWS - i, j >= NA_TILE_ROWS))[variant]
    return dr + NA_ROWS - 1 if ok else None


def _na_table_kernel(rp_ref, o_ref):
    kc = lax.broadcasted_iota(jnp.int32, (GRID_W, LANES), 0)
    lane = lax.broadcasted_iota(jnp.int32, (GRID_W, LANES), 1)
    col_start = jnp.clip(lane % GRID_W - NA_COLS // 2, 0, GRID_W - NA_COLS)
    col_ok = (kc >= col_start) & (kc < col_start + NA_COLS)
    neg = jnp.full((GRID_W, LANES), NEG, F32)
    cache = {}

    def half_block(d, side):
        if d is None:
            return neg
        if (d, side) not in cache:
            x = jnp.broadcast_to(rp_ref[0, 0, d:d + 1, :], (GRID_W, LANES))
            y = pltpu.roll(x, GRID_W * (1 - side), 1, stride=1, stride_axis=0)
            cache[d, side] = jnp.where(col_ok, y, NEG)
        return cache[d, side]

    for variant in range(3):
        for j in range(NA_WIN_ROWS):
            for ip in range(NA_TILE_ROWS // 2):
                left = half_block(_na_row_index(variant, j, 2 * ip), 0)
                right = half_block(_na_row_index(variant, j, 2 * ip + 1), 1)
                o_ref[0, variant, 0, j * GRID_W:(j + 1) * GRID_W, ip * LANES:(ip + 1) * LANES] = (
                    jnp.where(lane < GRID_W, left, right))


def _na_tables(rpb):
    depth, heads, n_dr, n_dc = rpb.shape
    assert n_dr == NA_WIN_ROWS + NA_TILE_ROWS - 1 and 2 * GRID_W == LANES
    front = GRID_W - NA_COLS
    rp = jnp.pad(rpb.astype(F32) * LOG2E, ((0, 0), (0, 0), (0, 1), (front, LANES - n_dc - front)))
    rp = rp[..., ::-1]
    return pl.pallas_call(
        _na_table_kernel,
        grid=(depth, heads),
        in_specs=[pl.BlockSpec((1, 1, n_dr + 1, LANES), lambda l, h: (l, h, 0, 0))],
        out_specs=pl.BlockSpec((1, 3, 1, NA_W, NA_TQ), lambda l, h: (l, 0, h, 0, 0)),
        out_shape=jax.ShapeDtypeStruct((depth, 3, heads, NA_W, NA_TQ), F32),
        compiler_params=pltpu.CompilerParams(
            dimension_semantics=("arbitrary", "arbitrary"), vmem_limit_bytes=VMEM_LIMIT),
        name="na_tables",
    )(rp)


def _swap_pairs(x):
    n = x.shape[-1]
    lane = lax.broadcasted_iota(jnp.int32, x.shape, x.ndim - 1)
    return jnp.where(lane % 2 == 0, pltpu.roll(x, n - 1, x.ndim - 1), pltpu.roll(x, 1, x.ndim - 1))


def _head_rms(x, ones_bd, w):
    sq = x * x
    hi = sq.astype(BF16)
    lo = (sq - hi.astype(F32)).astype(BF16)
    ms = (jnp.dot(hi, ones_bd, preferred_element_type=F32)
          + jnp.dot(lo, ones_bd, preferred_element_type=F32)) * (1.0 / HEAD_DIM)
    return x * lax.rsqrt(ms + EPS) * w


def _cprep_kernel(q_ref, k_ref, v_ref, cos_ref, sin_ref, qw_ref, kw_ref, bd_ref,
                  qt_ref, k2_ref, vt_ref):
    cos = cos_ref[...]
    sin = sin_ref[...]
    q = _head_rms(q_ref[...].astype(F32), bd_ref[...], qw_ref[...])
    reps = q.shape[1] // LANES
    q = q * jnp.concatenate([cos] * reps, axis=1) + _swap_pairs(q) * jnp.concatenate([sin] * reps, axis=1)
    qt_ref[...] = (q * Q_SCALE).T.astype(qt_ref.dtype)
    k = _head_rms(k_ref[...].astype(F32), bd_ref[:LANES, :LANES], kw_ref[...])
    k = (k * cos + _swap_pairs(k) * sin).astype(k2_ref.dtype)
    vt = v_ref[...].astype(F32).T.astype(vt_ref.dtype)
    ones = jnp.ones((V_ROWS - HEAD_DIM, vt.shape[1]), vt_ref.dtype)
    tk = vt_ref.shape[-1]
    for g in range(C_KV_HEADS):
        k2_ref[g] = k[:, g * HEAD_DIM:(g + 1) * HEAD_DIM]
        vg = jnp.concatenate([vt[g * HEAD_DIM:(g + 1) * HEAD_DIM], ones], axis=0)
        for u in range(vt_ref.shape[1]):
            vt_ref[g, u] = vg[:, u * tk:(u + 1) * tk]


def _cprep(proj, cos2, sin2, qw, kw, ones_bd, tk, t=1024):
    L = proj.shape[0]
    vec = lambda width: pl.BlockSpec((1, width), lambda n: (0, 0))
    return pl.pallas_call(
        _cprep_kernel,
        grid=(L // t,),
        in_specs=[_piece("q_c", t), _piece("k_c", t), _piece("v_c", t),
                  pl.BlockSpec((t, LANES), lambda n: (n, 0)),
                  pl.BlockSpec((t, LANES), lambda n: (n, 0)),
                  vec(GROUP_WIDTH), vec(LANES),
                  pl.BlockSpec((GROUP_WIDTH, GROUP_WIDTH), lambda n: (0, 0))],
        out_specs=[pl.BlockSpec((GROUP_WIDTH, t), lambda n: (0, n)),
                   pl.BlockSpec((C_KV_HEADS, t, HEAD_DIM), lambda n: (0, n, 0)),
                   pl.BlockSpec((C_KV_HEADS, t // tk, V_ROWS, tk), lambda n: (0, n, 0, 0))],
        out_shape=[jax.ShapeDtypeStruct((GROUP_WIDTH, L), BF16),
                   jax.ShapeDtypeStruct((C_KV_HEADS, L, HEAD_DIM), BF16),
                   jax.ShapeDtypeStruct((C_KV_HEADS, L // tk, V_ROWS, tk), BF16)],
        compiler_params=pltpu.CompilerParams(
            dimension_semantics=("arbitrary",), vmem_limit_bytes=VMEM_LIMIT),
        name="dense_prep",
    )(proj, proj, proj, cos2, sin2, qw, kw, ones_bd)


def _flash_kernel(qt_ref, k_ref, vt_ref, o_ref, m_scr, acc_scr, st_scr, *, tk, heads, unroll, ahead):
    n_kv = k_ref.shape[1] // tk
    m_scr[...] = jnp.full(m_scr.shape, NEG, F32)
    acc_scr[...] = jnp.zeros(acc_scr.shape, F32)

    def scores(j, h):
        k = k_ref[0, pl.ds(pl.multiple_of(j * tk, tk), tk), :]
        qt = qt_ref[h * HEAD_DIM:(h + 1) * HEAD_DIM, :]
        return jnp.dot(k, qt, preferred_element_type=F32)

    n_items = unroll * heads

    def item_scores(t, idx):
        j = t * unroll + idx // heads
        return scores(jnp.minimum(j, n_kv - 1), idx % heads)

    for a in range(ahead):
        st_scr[a] = item_scores(0, a)

    def body(t, carry):
        pending = [st_scr[a] for a in range(ahead)]
        for idx in range(n_items):
            pending.append(item_scores(t, idx + ahead))
            st = pending.pop(0)
            h = idx % heads
            vt = vt_ref[0, t * unroll + idx // heads]
            for q0 in range(0, st.shape[1], C_SPLIT):
                cols = slice(q0, q0 + C_SPLIT)
                m = m_scr[h, :, cols]
                m_new = jnp.maximum(m, jnp.max(st[:, cols], axis=0, keepdims=True))
                pt = jnp.exp2(st[:, cols] - m_new).astype(BF16)
                acc_scr[h, :, cols] = (jnp.exp2(m - m_new) * acc_scr[h, :, cols]
                                       + jnp.dot(vt, pt, preferred_element_type=F32))
                m_scr[h, :, cols] = m_new
        for a in range(ahead):
            st_scr[a] = pending[a]
        return carry

    lax.fori_loop(0, n_kv // unroll, body, 0)
    outs = [acc_scr[h, :HEAD_DIM] / acc_scr[h, HEAD_DIM:HEAD_DIM + 1] for h in range(heads)]
    o_ref[...] = jnp.concatenate(outs, axis=0).T.astype(o_ref.dtype)


def _flash(qt, k2, vt, tq):
    L = qt.shape[1]
    _, n_kv, v_rows, tk = vt.shape
    heads = GROUP_HEADS // C_KV_HEADS
    qrows = heads * HEAD_DIM
    return pl.pallas_call(
        functools.partial(_flash_kernel, tk=tk, heads=heads, unroll=C_UNROLL, ahead=C_AHEAD),
        grid=(C_KV_HEADS, L // tq),
        in_specs=[pl.BlockSpec((qrows, tq), lambda g, n: (g, n)),
                  pl.BlockSpec((1, L, HEAD_DIM), lambda g, n: (g, 0, 0)),
                  pl.BlockSpec((1, n_kv, v_rows, tk), lambda g, n: (g, 0, 0, 0))],
        out_specs=pl.BlockSpec((tq, qrows), lambda g, n: (n, g)),
        out_shape=jax.ShapeDtypeStruct((L, GROUP_WIDTH), BF16),
        scratch_shapes=[pltpu.VMEM((heads, 1, tq), F32), pltpu.VMEM((heads, v_rows, tq), F32),
                        pltpu.VMEM((C_AHEAD, tk, tq), F32)],
        compiler_params=pltpu.CompilerParams(
            dimension_semantics=("arbitrary", "arbitrary"), vmem_limit_bytes=VMEM_LIMIT),
        name="dense_flash",
    )(qt, k2, vt)


def _outproj_kernel(*refs, final):
    if final:
        x_ref, ya, yb, yc, yd, za, zb, zc, zd, w_ref, g_ref, fw_ref, o_ref = refs
    else:
        x_ref, ya, yb, yc, yd, za, zb, zc, zd, w_ref, g_ref, nw_ref, sc_ref, sh_ref, o_ref, h_ref = refs
    acc = jnp.zeros(x_ref.shape, F32)
    for gi, (y_ref, z_ref) in enumerate(zip((ya, yb, yc, yd), (za, zb, zc, zd))):
        z = z_ref[...].astype(F32)
        u = (y_ref[...].astype(F32) * (z * jax.nn.sigmoid(z))).astype(BF16)
        acc = acc + jnp.dot(u, w_ref[gi * GROUP_WIDTH:(gi + 1) * GROUP_WIDTH, :], preferred_element_type=F32)
    xn = x_ref[...] + g_ref[...] * acc
    if final:
        o_ref[...] = xn * lax.rsqrt(jnp.mean(xn * xn, axis=-1, keepdims=True) + EPS) * fw_ref[...]
    else:
        o_ref[...] = xn
        h_ref[...] = _modulated_norm(xn, nw_ref[...], sc_ref[...], sh_ref[...])


def _outproj(x, ys, proj, w_bf16, layer, gate, vecs):
    L, d = x.shape
    tm = min(512, L)
    final = len(vecs) == 1
    row = lambda width: pl.BlockSpec((tm, width), lambda i: (i, 0))
    vec = pl.BlockSpec((1, d), lambda i: (0, 0))
    in_specs = ([row(d)] + [row(GROUP_WIDTH)] * 4 + [_piece(z, tm) for z in ("z_a", "z_b", "z_c", "z_d")]
                + [pl.BlockSpec((None,) + w_bf16.shape[1:], lambda i: (layer, 0, 0)), vec]
                + [vec] * len(vecs))
    x_out = jax.ShapeDtypeStruct((L, d), F32)
    return pl.pallas_call(
        functools.partial(_outproj_kernel, final=final),
        grid=(L // tm,),
        in_specs=in_specs,
        out_specs=row(d) if final else [row(d), row(d)],
        out_shape=x_out if final else [x_out, jax.ShapeDtypeStruct((L, d), BF16)],
        compiler_params=pltpu.CompilerParams(
            dimension_semantics=("arbitrary",), vmem_limit_bytes=VMEM_LIMIT),
        name="outproj",
    )(x, *ys, proj, proj, proj, proj, w_bf16, gate, *vecs)


def _rope_tables(L):
    t = np.arange(L)
    axis_dim = HEAD_DIM // 2
    inv = jnp.asarray(ROPE_THETA, F32) ** (-jnp.arange(0, axis_dim, 2, dtype=F32) / axis_dim)
    row = jnp.asarray(t // GRID_W, F32)
    col = jnp.asarray(t % GRID_W, F32)
    ang = jnp.concatenate([row[:, None] * inv[None], col[:, None] * inv[None]], axis=-1)
    cos = jnp.repeat(jnp.cos(ang), 2, axis=-1)
    sin = jnp.repeat(jnp.sin(ang), 2, axis=-1) * jnp.asarray(np.tile([-1.0, 1.0], HEAD_DIM // 2), F32)
    return jnp.tile(cos, (1, LANES // HEAD_DIM)), jnp.tile(sin, (1, LANES // HEAD_DIM))


A_TQ = 256
A_CHUNK = 256
NA_CHUNK = 256
D_CHUNK = 768
D_TQ = 256
D_REACH = D_CONFIGS[-1][0] // 2
C_TQ = 512
C_TK = 256
C_UNROLL = 8
C_AHEAD = 2
C_SPLIT = 256


def _table_a(t5_table):
    return _t5_table(t5_table, 0, A_TQ, A_RADIUS, A_TQ + 2 * A_RADIUS, _window_mult)


def _table_d(t5_table):
    return _t5_table(t5_table, GROUP_HEADS, D_TQ, D_REACH, D_TQ + 2 * D_REACH, _dilated_mult)


PREP_T = 1024
assert PREP_T >= D_REACH


def _seq_window(L, tq, r_lo):
    return lambda n: (n * tq + PREP_T - r_lo, (n * tq - r_lo, 0, L))


def _mixer_a(proj, qt, tab_a, sink):
    L = proj.shape[0]
    k_pad, vt_pad = _kv_prep(proj, "k_a", "v_a", LANES, PREP_T, PREP_T)
    return _band_attention(qt, k_pad, vt_pad, tab_a, chunk=A_CHUNK, group=GROUP_HEADS // A_KV_HEADS,
                           window=_seq_window(L, A_TQ, A_RADIUS), sink=sink.astype(F32))


def _mixer_b(proj, qt, tabs_b, layer):
    L = proj.shape[0]
    rows = L // GRID_W
    n_tiles = rows // NA_TILE_ROWS
    k, vt = _kv_prep(proj, "k_b", "v_b", GROUP_WIDTH, 0, PREP_T)
    window = lambda n: (jnp.clip(n * NA_TILE_ROWS - NA_ROWS // 2, 0, rows - NA_WIN_ROWS) * GRID_W, None)
    variant = lambda n: jnp.where(n == 0, 0, jnp.where(n == n_tiles - 1, 2, 1))
    return _band_attention(qt, k, vt, tabs_b, chunk=NA_CHUNK, group=1, window=window, variant=variant,
                           layer=layer)


def _mixer_c(proj, q_norm_w, k_norm_w):
    L = proj.shape[0]
    cos2, sin2 = _rope_tables(L)
    ones_bd = jnp.asarray(np.kron(np.eye(GROUP_HEADS), np.ones((HEAD_DIM, HEAD_DIM))), BF16)
    qt, k2, vt = _cprep(proj, cos2, sin2, jnp.tile(q_norm_w.astype(F32), GROUP_HEADS)[None],
                        jnp.tile(k_norm_w.astype(F32), LANES // HEAD_DIM)[None], ones_bd, C_TK)
    return _flash(qt, k2, vt, C_TQ)


def _mixer_d(proj, qt, tab_d):
    L = proj.shape[0]
    k_pad, vt_pad = _kv_prep(proj, "k_d", "v_d", GROUP_WIDTH, PREP_T, PREP_T)
    return _band_attention(qt, k_pad, vt_pad, tab_d, chunk=D_CHUNK, group=1,
                           window=_seq_window(L, D_TQ, D_REACH))


def kernel(x, c, w_ada, b_ada, norm_w, w_in, w_out, attn_sink, na_rpb, q_norm_w, k_norm_w,
           t5_table, final_norm_w):
    B, L, D = x.shape
    assert B == 1 and L % 1024 == 0 and L // GRID_W >= NA_WIN_ROWS
    depth = w_ada.shape[0]
    x = x[0]

    mod = _ada_mod(jnp.broadcast_to(c, (8, D)), w_ada, b_ada)[:, 0:1, :]
    tab_a = _table_a(t5_table)
    tab_d = _table_d(t5_table)
    tabs_b = _na_tables(na_rpb)
    w_out_b = w_out.astype(BF16)
    shift, scale, gate = jnp.split(mod, 3, axis=-1)
    h = _norm(x, norm_w[0][None], scale[0], shift[0])
    for i in range(depth):
        proj = _inproj(h, w_in, i)
        qt_a, qt_b, qt_d = _qt_prep(proj, ("q_a", "q_b", "q_d"))
        ys = (_mixer_a(proj, qt_a, tab_a, attn_sink[i]),
              _mixer_b(proj, qt_b, tabs_b, i),
              _mixer_c(proj, q_norm_w[i], k_norm_w[i]),
              _mixer_d(proj, qt_d, tab_d))
        if i + 1 < depth:
            x, h = _outproj(x, ys, proj, w_out_b, i, gate[i],
                            (norm_w[i + 1][None], scale[i + 1], shift[i + 1]))
        else:
            x = _outproj(x, ys, proj, w_out_b, i, gate[i], (final_norm_w[None],))
    return x[None]
```

```python
import functools
import math

import numpy as np
import jax
import jax.numpy as jnp
from jax import lax
from jax.experimental import pallas as pl
from jax.experimental.pallas import tpu as pltpu

HEAD_DIM = 64
GROUP_WIDTH = 512
GROUP_HEADS = 8
A_KV_HEADS = 2
A_RADIUS = 128
C_KV_HEADS = 2
ROPE_THETA = 10000.0
NA_ROWS = 8
NA_COLS = 16
D_CONFIGS = ((128, 1), (512, 4), (2048, 16))
GRID_W = 64
T5_BUCKETS = 32
T5_MAX_DIST = 1024
EPS = 1e-6
NEG = -1e30

V_ROWS = HEAD_DIM + 16
LANES = 128
VMEM_LIMIT = 56 * 1024 * 1024

F32 = jnp.float32
BF16 = jnp.bfloat16

_SRC = {}
_off = 0
for _name, _w in (("q_a", 512), ("k_a", 128), ("v_a", 128), ("z_a", 512),
                  ("q_b", 512), ("k_b", 512), ("v_b", 512), ("z_b", 512),
                  ("q_c", 512), ("k_c", 128), ("v_c", 128), ("z_c", 512),
                  ("q_d", 512), ("k_d", 512), ("v_d", 512), ("z_d", 512)):
    _SRC[_name] = (_off, _w)
    _off += _w
IN_WIDTH = _off


def _piece(name, t, row_block=lambda n: n):
    off, width = _SRC[name]
    return pl.BlockSpec((pl.Element(t), pl.Element(width)),
                        lambda n: (pl.multiple_of(row_block(n) * t, t), off))


def _ada_kernel(c_ref, w_ref, b_ref, o_ref):
    c = c_ref[...]
    cond = c * jax.nn.sigmoid(c)
    o_ref[0] = jnp.dot(cond, w_ref[0], preferred_element_type=F32,
                       precision=lax.Precision.HIGHEST) + b_ref[0]


def _ada_mod(c8, w_ada, b_ada):
    depth, d, n3 = w_ada.shape
    tn = 1024
    return pl.pallas_call(
        _ada_kernel,
        grid=(depth, n3 // tn),
        in_specs=[pl.BlockSpec((8, d), lambda i, j: (0, 0)),
                  pl.BlockSpec((1, d, tn), lambda i, j: (i, 0, j)),
                  pl.BlockSpec((1, 1, tn), lambda i, j: (i, 0, j))],
        out_specs=pl.BlockSpec((1, 8, tn), lambda i, j: (i, 0, j)),
        out_shape=jax.ShapeDtypeStruct((depth, 8, n3), F32),
        compiler_params=pltpu.CompilerParams(
            dimension_semantics=("arbitrary", "arbitrary"), vmem_limit_bytes=VMEM_LIMIT),
        name="ada_mod",
    )(c8, w_ada, b_ada.reshape(depth, 1, n3))


def _modulated_norm(x, nw, scale, shift):
    y = x * lax.rsqrt(jnp.mean(x * x, axis=-1, keepdims=True) + EPS)
    return ((y * nw) * (1.0 + scale) + shift).astype(BF16)


def _norm_kernel(x_ref, nw_ref, sc_ref, sh_ref, h_ref):
    h_ref[...] = _modulated_norm(x_ref[...], nw_ref[...], sc_ref[...], sh_ref[...])


def _norm(x, nw, scale, shift):
    L, d = x.shape
    tm = min(512, L)
    vec = pl.BlockSpec((1, d), lambda i: (0, 0))
    return pl.pallas_call(
        _norm_kernel,
        grid=(L // tm,),
        in_specs=[pl.BlockSpec((tm, d), lambda i: (i, 0)), vec, vec, vec],
        out_specs=pl.BlockSpec((tm, d), lambda i: (i, 0)),
        out_shape=jax.ShapeDtypeStruct((L, d), BF16),
        compiler_params=pltpu.CompilerParams(
            dimension_semantics=("arbitrary",), vmem_limit_bytes=VMEM_LIMIT),
        name="norm",
    )(x, nw, scale, shift)


def _inproj_kernel(h_ref, w_ref, o_ref):
    o_ref[...] = jnp.dot(h_ref[...], w_ref[...].astype(BF16),
                         preferred_element_type=F32).astype(o_ref.dtype)


def _inproj(h, w, layer):
    L, d = h.shape
    n = w.shape[2]
    tm = min(2048, L)
    tn = 512
    return pl.pallas_call(
        _inproj_kernel,
        grid=(L // tm, n // tn),
        in_specs=[pl.BlockSpec((tm, d), lambda i, j: (i, 0)),
                  pl.BlockSpec((None, d, tn), lambda i, j: (layer, 0, j))],
        out_specs=pl.BlockSpec((tm, tn), lambda i, j: (i, j)),
        out_shape=jax.ShapeDtypeStruct((L, n), BF16),
        compiler_params=pltpu.CompilerParams(
            dimension_semantics=("arbitrary", "arbitrary"), vmem_limit_bytes=VMEM_LIMIT),
        name="inproj",
    )(h, w)


LOG2E = math.log2(math.e)
Q_SCALE = HEAD_DIM ** -0.5 * LOG2E


def _qt_prep_kernel(*refs):
    n = len(refs) // 2
    for q_ref, qt_ref in zip(refs[:n], refs[n:]):
        qt_ref[...] = (q_ref[...].astype(F32) * Q_SCALE).T.astype(qt_ref.dtype)


def _qt_prep(proj, names, t=1024):
    L = proj.shape[0]
    return pl.pallas_call(
        _qt_prep_kernel,
        grid=(L // t,),
        in_specs=[_piece(name, t) for name in names],
        out_specs=[pl.BlockSpec((GROUP_WIDTH, t), lambda n: (0, n)) for _ in names],
        out_shape=[jax.ShapeDtypeStruct((GROUP_WIDTH, L), BF16) for _ in names],
        compiler_params=pltpu.CompilerParams(
            dimension_semantics=("arbitrary",), vmem_limit_bytes=VMEM_LIMIT),
        name="qt_prep",
    )(*([proj] * len(names)))


def _kv_prep_kernel(k_ref, v_ref, kp_ref, vt_ref, *, pad_blocks, tok_blocks):
    b = pl.program_id(0)
    is_token = (b >= pad_blocks) & (b < pad_blocks + tok_blocks)

    @pl.when(is_token)
    def _():
        kp_ref[...] = k_ref[...]
        vt = v_ref[...].astype(F32).T.astype(vt_ref.dtype)
        ones = jnp.ones((V_ROWS - HEAD_DIM, vt.shape[1]), vt_ref.dtype)
        for g in range(vt.shape[0] // HEAD_DIM):
            vt_ref[g * V_ROWS:(g + 1) * V_ROWS, :] = jnp.concatenate(
                [vt[g * HEAD_DIM:(g + 1) * HEAD_DIM], ones], axis=0)

    @pl.when(jnp.logical_not(is_token))
    def _():
        kp_ref[...] = jnp.zeros(kp_ref.shape, kp_ref.dtype)
        vt_ref[...] = jnp.zeros(vt_ref.shape, vt_ref.dtype)


def _kv_prep(proj, k_name, v_name, width, pad, t):
    L = proj.shape[0]
    assert pad % t == 0 and L % t == 0 and width % LANES == 0
    pad_blocks, tok_blocks = pad // t, L // t
    lp = L + 2 * pad
    assert _SRC[k_name][1] == width and _SRC[v_name][1] == width
    tok = lambda b: jnp.clip(b - pad_blocks, 0, tok_blocks - 1)
    return pl.pallas_call(
        functools.partial(_kv_prep_kernel, pad_blocks=pad_blocks, tok_blocks=tok_blocks),
        grid=(lp // t,),
        in_specs=[_piece(k_name, t, tok), _piece(v_name, t, tok)],
        out_specs=[pl.BlockSpec((t, width), lambda b: (b, 0)),
                   pl.BlockSpec((width // HEAD_DIM * V_ROWS, t), lambda b: (0, b))],
        out_shape=[jax.ShapeDtypeStruct((lp, width), BF16),
                   jax.ShapeDtypeStruct((width // HEAD_DIM * V_ROWS, lp), BF16)],
        compiler_params=pltpu.CompilerParams(
            dimension_semantics=("arbitrary",), vmem_limit_bytes=VMEM_LIMIT),
        name="kv_prep",
    )(proj, proj)


def _band_kernel(*refs, w_keys, chunk, lanes, heads, group, use_sink, window, variant):
    if use_sink:
        sink_ref, qt_ref, k_ref, vt_ref, tab_ref, o_ref = refs
    else:
        qt_ref, k_ref, vt_ref, tab_ref, o_ref = refs
    n = pl.program_id(0)
    _, pos = window(n)
    kwin = k_ref[...]
    vwin = vt_ref[...]
    var = variant(n)

    n_chunks = w_keys // chunk

    def attend(kbias):
        def scores(h, c):
            kv = h // group
            rows = slice(c * chunk, (c + 1) * chunk)
            st = jnp.dot(kwin[rows, kv * HEAD_DIM:(kv + 1) * HEAD_DIM],
                         qt_ref[h * HEAD_DIM:(h + 1) * HEAD_DIM, :],
                         preferred_element_type=F32) + tab_ref[var, h, rows, :]
            return st if kbias is None else st + kbias[rows]

        def col_max(m, st):
            cm = jnp.max(st, axis=0, keepdims=True)
            return cm if m is None else jnp.maximum(m, cm)

        def finish(h, acc, m):
            o, l = acc[:HEAD_DIM], acc[HEAD_DIM:HEAD_DIM + 1]
            if use_sink:
                sk = sink_ref[h] * LOG2E
                m2 = jnp.maximum(m, sk)
                a = jnp.exp2(m - m2)
                return o * (a / (l * a + jnp.exp2(sk - m2)))
            return o / l

        outs = []
        groups = [range(g, g + lanes) for g in range(0, heads, lanes)]
        cur = {h: [] for h in groups[0]}
        m = {h: None for h in groups[0]}
        for c in range(n_chunks):
            for h in groups[0]:
                cur[h].append(scores(h, c))
                m[h] = col_max(m[h], cur[h][-1])
        for gi, grp in enumerate(groups):
            following = groups[gi + 1] if gi + 1 < len(groups) else ()
            nxt = {h: [] for h in following}
            m_next = {h: None for h in following}
            acc = {h: None for h in grp}
            for c in range(n_chunks):
                for h in following:
                    nxt[h].append(scores(h, c))
                    m_next[h] = col_max(m_next[h], nxt[h][-1])
                for h in grp:
                    kv = h // group
                    pt = jnp.exp2(cur[h][c] - m[h]).astype(BF16)
                    pv = jnp.dot(vwin[kv * V_ROWS:(kv + 1) * V_ROWS, c * chunk:(c + 1) * chunk], pt,
                                 preferred_element_type=F32)
                    acc[h] = pv if acc[h] is None else acc[h] + pv
            outs.extend(finish(h, acc[h], m[h]) for h in grp)
            cur, m = nxt, m_next
        o_ref[...] = jnp.concatenate(outs, axis=0).T.astype(o_ref.dtype)

    if pos is None:
        attend(None)
    else:
        first, lo, hi = pos
        inside = (first >= lo) & (first + w_keys <= hi)

        @pl.when(inside)
        def _():
            attend(None)

        @pl.when(jnp.logical_not(inside))
        def _():
            kpos = first + lax.broadcasted_iota(jnp.int32, (w_keys, 1), 0)
            attend(jnp.where((kpos >= lo) & (kpos < hi), 0.0, NEG).astype(F32))


def _band_attention(qt, k_pad, vt_pad, table, *, chunk, lanes, group, window, variant=lambda n: 0,
                    sink=None, layer=None):
    qw, L = qt.shape
    heads = qw // HEAD_DIM
    w_keys, tq = table.shape[-2:]
    assert w_keys % chunk == 0
    kern = functools.partial(_band_kernel, w_keys=w_keys, chunk=chunk, lanes=lanes, heads=heads, group=group,
                             use_sink=sink is not None, window=window, variant=variant)
    start = lambda n: pl.multiple_of(window(n)[0], LANES)
    if layer is None:
        table_spec = pl.BlockSpec(table.shape, lambda n: (0, 0, 0, 0), pipeline_mode=pl.Buffered(1))
    else:
        table_spec = pl.BlockSpec((None,) + table.shape[1:], lambda n: (layer, 0, 0, 0, 0),
                                  pipeline_mode=pl.Buffered(1))
    in_specs = [pl.BlockSpec((qw, tq), lambda n: (0, n)),
                pl.BlockSpec((pl.Element(w_keys), pl.Element(k_pad.shape[1])), lambda n: (start(n), 0)),
                pl.BlockSpec((pl.Element(vt_pad.shape[0]), pl.Element(w_keys)), lambda n: (0, start(n))),
                table_spec]
    args = [qt, k_pad, vt_pad, table]
    if sink is not None:
        in_specs = [pl.BlockSpec(memory_space=pltpu.SMEM)] + in_specs
        args = [sink] + args
    return pl.pallas_call(
        kern,
        grid=(L // tq,),
        in_specs=in_specs,
        out_specs=pl.BlockSpec((tq, qw), lambda n: (n, 0)),
        out_shape=jax.ShapeDtypeStruct((L, qw), BF16),
        compiler_params=pltpu.CompilerParams(
            dimension_semantics=("arbitrary",), vmem_limit_bytes=VMEM_LIMIT),
        name="band_attention",
    )(*args)


def _t5_bucket(rel):
    half = T5_BUCKETS // 2
    exact = half // 2
    n = jnp.abs(rel)
    big = exact + (jnp.log(jnp.maximum(n, exact).astype(F32) / exact)
                   / math.log(T5_MAX_DIST / exact) * (half - exact)).astype(jnp.int32)
    big = jnp.minimum(big, half - 1)
    return jnp.where(rel > 0, half, 0) + jnp.where(n < exact, n, big)


def _toeplitz_kernel(rv_ref, o_ref, *, n_diag):
    w, tq = o_ref.shape[1:]
    blocks = []
    for d in range(n_diag):
        s = LANES * (n_diag - 1 - d)
        x = jnp.broadcast_to(rv_ref[0, :, s:s + 2 * LANES], (LANES, 2 * LANES))
        blocks.append(pltpu.roll(x, LANES, 1, stride=1, stride_axis=0)[:, :LANES])
    for cb in range(w // LANES):
        for ib in range(tq // LANES):
            o_ref[0, cb * LANES:(cb + 1) * LANES, ib * LANES:(ib + 1) * LANES] = (
                blocks[cb - ib + tq // LANES - 1])


def _toeplitz(v, tq, w):
    heads, n_rel = v.shape
    n_diag = w // LANES + tq // LANES - 1
    assert w % LANES == 0 and tq % LANES == 0 and n_rel == LANES * n_diag + LANES - 1
    rv = jnp.pad(v, ((0, 0), (0, 1)))[:, None, ::-1]
    return pl.pallas_call(
        functools.partial(_toeplitz_kernel, n_diag=n_diag),
        grid=(heads,),
        in_specs=[pl.BlockSpec((1, 1, rv.shape[-1]), lambda h: (h, 0, 0))],
        out_specs=pl.BlockSpec((1, w, tq), lambda h: (h, 0, 0)),
        out_shape=jax.ShapeDtypeStruct((heads, w, tq), F32),
        compiler_params=pltpu.CompilerParams(
            dimension_semantics=("arbitrary",), vmem_limit_bytes=VMEM_LIMIT),
        name="toeplitz",
    )(rv)


def _t5_table(t5_table, head_lo, tq, r_lo, w_keys, mult):
    rel = np.arange(-r_lo - (tq - 1), w_keys - r_lo)
    m = mult(rel)
    b = t5_table.astype(F32)[:, head_lo:head_lo + GROUP_HEADS][_t5_bucket(jnp.asarray(rel, jnp.int32))]
    logm = np.log(np.maximum(m, 1)).astype(np.float32)
    vec = jnp.where(jnp.asarray(m > 0)[:, None], (b + logm[:, None]) * LOG2E, NEG).T
    return _toeplitz(vec, tq, w_keys)[None]


def _window_mult(rel):
    return (np.abs(rel) <= A_RADIUS).astype(np.int32)


def _dilated_mult(rel):
    m = np.zeros(rel.shape, np.int32)
    for window, dil in D_CONFIGS:
        m += ((rel % dil == 0) & (np.abs(rel) <= window // 2)).astype(np.int32)
    return m


NA_TILE_ROWS = 4
NA_WIN_ROWS = NA_TILE_ROWS + NA_ROWS
NA_TQ = NA_TILE_ROWS * GRID_W
NA_W = NA_WIN_ROWS * GRID_W


def _na_table(rpb):
    return _na_tables(rpb[None])[0]


def _na_row_index(variant, j, i):
    dr, ok = ((j - i, j < NA_ROWS),
              (j - NA_ROWS // 2 - i, i <= j < i + NA_ROWS),
              (j - NA_ROWS - i, j >= NA_TILE_ROWS))[variant]
    return dr + NA_ROWS - 1 if ok else None


def _na_table_kernel(rp_ref, o_ref):
    kc = lax.broadcasted_iota(jnp.int32, (GRID_W, LANES), 0)
    lane = lax.broadcasted_iota(jnp.int32, (GRID_W, LANES), 1)
    col_start = jnp.clip(lane % GRID_W - NA_COLS // 2, 0, GRID_W - NA_COLS)
    col_ok = (kc >= col_start) & (kc < col_start + NA_COLS)
    neg = jnp.full((GRID_W, LANES), NEG, F32)
    cache = {}

    def half_block(d, side):
        if d is None:
            return neg
        if (d, side) not in cache:
            x = jnp.broadcast_to(rp_ref[0, 0, d:d + 1, :], (GRID_W, LANES))
            y = pltpu.roll(x, GRID_W * (1 - side), 1, stride=1, stride_axis=0)
            cache[d, side] = jnp.where(col_ok, y, NEG)
        return cache[d, side]

    for variant in range(3):
        for j in range(NA_WIN_ROWS):
            for ip in range(NA_TILE_ROWS // 2):
                left = half_block(_na_row_index(variant, j, 2 * ip), 0)
                right = half_block(_na_row_index(variant, j, 2 * ip + 1), 1)
                o_ref[0, variant, 0, j * GRID_W:(j + 1) * GRID_W, ip * LANES:(ip + 1) * LANES] = (
                    jnp.where(lane < GRID_W, left, right))


def _na_tables(rpb):
    depth, heads, n_dr, n_dc = rpb.shape
    assert n_dr == NA_WIN_ROWS + NA_TILE_ROWS - 1 and 2 * GRID_W == LANES
    front = GRID_W - NA_COLS
    rp = jnp.pad(rpb.astype(F32) * LOG2E, ((0, 0), (0, 0), (0, 1), (front, LANES - n_dc - front)))
    rp = rp[..., ::-1]
    return pl.pallas_call(
        _na_table_kernel,
        grid=(depth, heads),
        in_specs=[pl.BlockSpec((1, 1, n_dr + 1, LANES), lambda l, h: (l, h, 0, 0))],
        out_specs=pl.BlockSpec((1, 3, 1, NA_W, NA_TQ), lambda l, h: (l, 0, h, 0, 0)),
        out_shape=jax.ShapeDtypeStruct((depth, 3, heads, NA_W, NA_TQ), F32),
        compiler_params=pltpu.CompilerParams(
            dimension_semantics=("arbitrary", "arbitrary"), vmem_limit_bytes=VMEM_LIMIT),
        name="na_tables",
    )(rp)


def _swap_pairs(x):
    n = x.shape[-1]
    lane = lax.broadcasted_iota(jnp.int32, x.shape, x.ndim - 1)
    return jnp.where(lane % 2 == 0, pltpu.roll(x, n - 1, x.ndim - 1), pltpu.roll(x, 1, x.ndim - 1))


def _head_rms(x, ones_bd, w):
    sq = x * x
    hi = sq.astype(BF16)
    lo = (sq - hi.astype(F32)).astype(BF16)
    ms = (jnp.dot(hi, ones_bd, preferred_element_type=F32)
          + jnp.dot(lo, ones_bd, preferred_element_type=F32)) * (1.0 / HEAD_DIM)
    return x * lax.rsqrt(ms + EPS) * w


def _cprep_kernel(q_ref, k_ref, v_ref, cos_ref, sin_ref, qw_ref, kw_ref, bd_ref,
                  qt_ref, k2_ref, vt_ref):
    cos = cos_ref[...]
    sin = sin_ref[...]
    q = _head_rms(q_ref[...].astype(F32), bd_ref[...], qw_ref[...])
    reps = q.shape[1] // LANES
    q = q * jnp.concatenate([cos] * reps, axis=1) + _swap_pairs(q) * jnp.concatenate([sin] * reps, axis=1)
    qt_ref[...] = (q * Q_SCALE).T.astype(qt_ref.dtype)
    k = _head_rms(k_ref[...].astype(F32), bd_ref[:LANES, :LANES], kw_ref[...])
    k = (k * cos + _swap_pairs(k) * sin).astype(k2_ref.dtype)
    vt = v_ref[...].astype(F32).T.astype(vt_ref.dtype)
    ones = jnp.ones((V_ROWS - HEAD_DIM, vt.shape[1]), vt_ref.dtype)
    tk = vt_ref.shape[-1]
    for g in range(C_KV_HEADS):
        k2_ref[g] = k[:, g * HEAD_DIM:(g + 1) * HEAD_DIM]
        vg = jnp.concatenate([vt[g * HEAD_DIM:(g + 1) * HEAD_DIM], ones], axis=0)
        for u in range(vt_ref.shape[1]):
            vt_ref[g, u] = vg[:, u * tk:(u + 1) * tk]


def _cprep(proj, cos2, sin2, qw, kw, ones_bd, tk, t=1024):
    L = proj.shape[0]
    vec = lambda width: pl.BlockSpec((1, width), lambda n: (0, 0))
    return pl.pallas_call(
        _cprep_kernel,
        grid=(L // t,),
        in_specs=[_piece("q_c", t), _piece("k_c", t), _piece("v_c", t),
                  pl.BlockSpec((t, LANES), lambda n: (n, 0)),
                  pl.BlockSpec((t, LANES), lambda n: (n, 0)),
                  vec(GROUP_WIDTH), vec(LANES),
                  pl.BlockSpec((GROUP_WIDTH, GROUP_WIDTH), lambda n: (0, 0))],
        out_specs=[pl.BlockSpec((GROUP_WIDTH, t), lambda n: (0, n)),
                   pl.BlockSpec((C_KV_HEADS, t, HEAD_DIM), lambda n: (0, n, 0)),
                   pl.BlockSpec((C_KV_HEADS, t // tk, V_ROWS, tk), lambda n: (0, n, 0, 0))],
        out_shape=[jax.ShapeDtypeStruct((GROUP_WIDTH, L), BF16),
                   jax.ShapeDtypeStruct((C_KV_HEADS, L, HEAD_DIM), BF16),
                   jax.ShapeDtypeStruct((C_KV_HEADS, L // tk, V_ROWS, tk), BF16)],
        compiler_params=pltpu.CompilerParams(
            dimension_semantics=("arbitrary",), vmem_limit_bytes=VMEM_LIMIT),
        name="dense_prep",
    )(proj, proj, proj, cos2, sin2, qw, kw, ones_bd)


def _flash_kernel(qt_ref, k_ref, vt_ref, o_ref, m_scr, acc_scr, st_scr, *, tk, heads, unroll, ahead):
    n_kv = k_ref.shape[1] // tk
    m_scr[...] = jnp.full(m_scr.shape, NEG, F32)
    acc_scr[...] = jnp.zeros(acc_scr.shape, F32)

    def scores(j, h):
        k = k_ref[0, pl.ds(pl.multiple_of(j * tk, tk), tk), :]
        qt = qt_ref[h * HEAD_DIM:(h + 1) * HEAD_DIM, :]
        return jnp.dot(k, qt, preferred_element_type=F32)

    n_items = unroll * heads

    def item_scores(t, idx):
        j = t * unroll + idx // heads
        return scores(jnp.minimum(j, n_kv - 1), idx % heads)

    for a in range(ahead):
        st_scr[a] = item_scores(0, a)

    def body(t, carry):
        pending = [st_scr[a] for a in range(ahead)]
        for idx in range(n_items):
            pending.append(item_scores(t, idx + ahead))
            st = pending.pop(0)
            h = idx % heads
            vt = vt_ref[0, t * unroll + idx // heads]
            for q0 in range(0, st.shape[1], C_SPLIT):
                cols = slice(q0, q0 + C_SPLIT)
                m = m_scr[h, :, cols]
                m_new = jnp.maximum(m, jnp.max(st[:, cols], axis=0, keepdims=True))
                pt = jnp.exp2(st[:, cols] - m_new).astype(BF16)
                acc_scr[h, :, cols] = (jnp.exp2(m - m_new) * acc_scr[h, :, cols]
                                       + jnp.dot(vt, pt, preferred_element_type=F32))
                m_scr[h, :, cols] = m_new
        for a in range(ahead):
            st_scr[a] = pending[a]
        return carry

    lax.fori_loop(0, n_kv // unroll, body, 0)
    outs = [acc_scr[h, :HEAD_DIM] / acc_scr[h, HEAD_DIM:HEAD_DIM + 1] for h in range(heads)]
    o_ref[...] = jnp.concatenate(outs, axis=0).T.astype(o_ref.dtype)


def _flash(qt, k2, vt, tq):
    L = qt.shape[1]
    _, n_kv, v_rows, tk = vt.shape
    heads = GROUP_HEADS // C_KV_HEADS
    qrows = heads * HEAD_DIM
    return pl.pallas_call(
        functools.partial(_flash_kernel, tk=tk, heads=heads, unroll=C_UNROLL, ahead=C_AHEAD),
        grid=(C_KV_HEADS, L // tq),
        in_specs=[pl.BlockSpec((qrows, tq), lambda g, n: (g, n)),
                  pl.BlockSpec((1, L, HEAD_DIM), lambda g, n: (g, 0, 0)),
                  pl.BlockSpec((1, n_kv, v_rows, tk), lambda g, n: (g, 0, 0, 0))],
        out_specs=pl.BlockSpec((tq, qrows), lambda g, n: (n, g)),
        out_shape=jax.ShapeDtypeStruct((L, GROUP_WIDTH), BF16),
        scratch_shapes=[pltpu.VMEM((heads, 1, tq), F32), pltpu.VMEM((heads, v_rows, tq), F32),
                        pltpu.VMEM((C_AHEAD, tk, tq), F32)],
        compiler_params=pltpu.CompilerParams(
            dimension_semantics=("arbitrary", "arbitrary"), vmem_limit_bytes=VMEM_LIMIT),
        name="dense_flash",
    )(qt, k2, vt)


def _outproj_kernel(*refs, final):
    if final:
        x_ref, ya, yb, yc, yd, za, zb, zc, zd, w_ref, g_ref, fw_ref, o_ref = refs
    else:
        x_ref, ya, yb, yc, yd, za, zb, zc, zd, w_ref, g_ref, nw_ref, sc_ref, sh_ref, o_ref, h_ref = refs
    acc = jnp.zeros(x_ref.shape, F32)
    for gi, (y_ref, z_ref) in enumerate(zip((ya, yb, yc, yd), (za, zb, zc, zd))):
        z = z_ref[...].astype(F32)
        u = (y_ref[...].astype(F32) * (z * jax.nn.sigmoid(z))).astype(BF16)
        acc = acc + jnp.dot(u, w_ref[gi * GROUP_WIDTH:(gi + 1) * GROUP_WIDTH, :], preferred_element_type=F32)
    xn = x_ref[...] + g_ref[...] * acc
    if final:
        o_ref[...] = xn * lax.rsqrt(jnp.mean(xn * xn, axis=-1, keepdims=True) + EPS) * fw_ref[...]
    else:
        o_ref[...] = xn
        h_ref[...] = _modulated_norm(xn, nw_ref[...], sc_ref[...], sh_ref[...])


def _outproj(x, ys, proj, w_bf16, layer, gate, vecs):
    L, d = x.shape
    tm = min(512, L)
    final = len(vecs) == 1
    row = lambda width: pl.BlockSpec((tm, width), lambda i: (i, 0))
    vec = pl.BlockSpec((1, d), lambda i: (0, 0))
    in_specs = ([row(d)] + [row(GROUP_WIDTH)] * 4 + [_piece(z, tm) for z in ("z_a", "z_b", "z_c", "z_d")]
                + [pl.BlockSpec((None,) + w_bf16.shape[1:], lambda i: (layer, 0, 0)), vec]
                + [vec] * len(vecs))
    x_out = jax.ShapeDtypeStruct((L, d), F32)
    return pl.pallas_call(
        functools.partial(_outproj_kernel, final=final),
        grid=(L // tm,),
        in_specs=in_specs,
        out_specs=row(d) if final else [row(d), row(d)],
        out_shape=x_out if final else [x_out, jax.ShapeDtypeStruct((L, d), BF16)],
        compiler_params=pltpu.CompilerParams(
            dimension_semantics=("arbitrary",), vmem_limit_bytes=VMEM_LIMIT),
        name="outproj",
    )(x, *ys, proj, proj, proj, proj, w_bf16, gate, *vecs)


def _rope_tables(L):
    t = np.arange(L)
    axis_dim = HEAD_DIM // 2
    inv = jnp.asarray(ROPE_THETA, F32) ** (-jnp.arange(0, axis_dim, 2, dtype=F32) / axis_dim)
    row = jnp.asarray(t // GRID_W, F32)
    col = jnp.asarray(t % GRID_W, F32)
    ang = jnp.concatenate([row[:, None] * inv[None], col[:, None] * inv[None]], axis=-1)
    cos = jnp.repeat(jnp.cos(ang), 2, axis=-1)
    sin = jnp.repeat(jnp.sin(ang), 2, axis=-1) * jnp.asarray(np.tile([-1.0, 1.0], HEAD_DIM // 2), F32)
    return jnp.tile(cos, (1, LANES // HEAD_DIM)), jnp.tile(sin, (1, LANES // HEAD_DIM))


A_TQ = 256
A_CHUNK = 256
NA_CHUNK = 256
D_CHUNK = 768
BAND_LANES = 2
D_TQ = 256
D_REACH = D_CONFIGS[-1][0] // 2
C_TQ = 512
C_TK = 256
C_UNROLL = 16
C_AHEAD = 2
C_SPLIT = 256


def _table_a(t5_table):
    return _t5_table(t5_table, 0, A_TQ, A_RADIUS, A_TQ + 2 * A_RADIUS, _window_mult)


def _table_d(t5_table):
    return _t5_table(t5_table, GROUP_HEADS, D_TQ, D_REACH, D_TQ + 2 * D_REACH, _dilated_mult)


PREP_T = 1024
assert PREP_T >= D_REACH


def _seq_window(L, tq, r_lo):
    return lambda n: (n * tq + PREP_T - r_lo, (n * tq - r_lo, 0, L))


def _mixer_a(proj, qt, tab_a, sink):
    L = proj.shape[0]
    k_pad, vt_pad = _kv_prep(proj, "k_a", "v_a", LANES, PREP_T, PREP_T)
    return _band_attention(qt, k_pad, vt_pad, tab_a, chunk=A_CHUNK, lanes=BAND_LANES, group=GROUP_HEADS // A_KV_HEADS,
                           window=_seq_window(L, A_TQ, A_RADIUS), sink=sink.astype(F32))


def _mixer_b(proj, qt, tabs_b, layer):
    L = proj.shape[0]
    rows = L // GRID_W
    n_tiles = rows // NA_TILE_ROWS
    k, vt = _kv_prep(proj, "k_b", "v_b", GROUP_WIDTH, 0, PREP_T)
    window = lambda n: (jnp.clip(n * NA_TILE_ROWS - NA_ROWS // 2, 0, rows - NA_WIN_ROWS) * GRID_W, None)
    variant = lambda n: jnp.where(n == 0, 0, jnp.where(n == n_tiles - 1, 2, 1))
    return _band_attention(qt, k, vt, tabs_b, chunk=NA_CHUNK, lanes=BAND_LANES, group=1, window=window, variant=variant,
                           layer=layer)


def _mixer_c(proj, q_norm_w, k_norm_w):
    L = proj.shape[0]
    cos2, sin2 = _rope_tables(L)
    ones_bd = jnp.asarray(np.kron(np.eye(GROUP_HEADS), np.ones((HEAD_DIM, HEAD_DIM))), BF16)
    qt, k2, vt = _cprep(proj, cos2, sin2, jnp.tile(q_norm_w.astype(F32), GROUP_HEADS)[None],
                        jnp.tile(k_norm_w.astype(F32), LANES // HEAD_DIM)[None], ones_bd, C_TK)
    return _flash(qt, k2, vt, C_TQ)


def _mixer_d(proj, qt, tab_d):
    L = proj.shape[0]
    k_pad, vt_pad = _kv_prep(proj, "k_d", "v_d", GROUP_WIDTH, PREP_T, PREP_T)
    return _band_attention(qt, k_pad, vt_pad, tab_d, chunk=D_CHUNK, lanes=BAND_LANES, group=1,
                           window=_seq_window(L, D_TQ, D_REACH))


def kernel(x, c, w_ada, b_ada, norm_w, w_in, w_out, attn_sink, na_rpb, q_norm_w, k_norm_w,
           t5_table, final_norm_w):
    B, L, D = x.shape
    assert B == 1 and L % 1024 == 0 and L // GRID_W >= NA_WIN_ROWS
    depth = w_ada.shape[0]
    x = x[0]

    mod = _ada_mod(jnp.broadcast_to(c, (8, D)), w_ada, b_ada)[:, 0:1, :]
    tab_a = _table_a(t5_table)
    tab_d = _table_d(t5_table)
    tabs_b = _na_tables(na_rpb)
    w_out_b = w_out.astype(BF16)
    shift, scale, gate = jnp.split(mod, 3, axis=-1)
    h = _norm(x, norm_w[0][None], scale[0], shift[0])
    for i in range(depth):
        proj = _inproj(h, w_in, i)
        qt_a, qt_b, qt_d = _qt_prep(proj, ("q_a", "q_b", "q_d"))
        ys = (_mixer_a(proj, qt_a, tab_a, attn_sink[i]),
              _mixer_b(proj, qt_b, tabs_b, i),
              _mixer_c(proj, q_norm_w[i], k_norm_w[i]),
              _mixer_d(proj, qt_d, tab_d))
        if i + 1 < depth:
            x, h = _outproj(x, ys, proj, w_out_b, i, gate[i],
                            (norm_w[i + 1][None], scale[i + 1], shift[i + 1]))
        else:
            x = _outproj(x, ys, proj, w_out_b, i, gate[i], (final_norm_w[None],))
    return x[None]
```

```python
import functools
import math

import numpy as np
import jax
import jax.numpy as jnp
from jax import lax
from jax.experimental import pallas as pl
from jax.experimental.pallas import tpu as pltpu

HEAD_DIM = 64
GROUP_WIDTH = 512
GROUP_HEADS = 8
A_KV_HEADS = 2
A_RADIUS = 128
C_KV_HEADS = 2
ROPE_THETA = 10000.0
NA_ROWS = 8
NA_COLS = 16
D_CONFIGS = ((128, 1), (512, 4), (2048, 16))
GRID_W = 64
T5_BUCKETS = 32
T5_MAX_DIST = 1024
EPS = 1e-6
NEG = -1e30

LANES = 128
BF16_SUBLANES = 16
V_ROWS = HEAD_DIM + BF16_SUBLANES
VMEM_LIMIT = 56 * 1024 * 1024

F32 = jnp.float32
BF16 = jnp.bfloat16

_SRC = {}
_off = 0
for _name, _w in (("q_a", 512), ("k_a", 128), ("v_a", 128), ("z_a", 512),
                  ("q_b", 512), ("k_b", 512), ("v_b", 512), ("z_b", 512),
                  ("q_c", 512), ("k_c", 128), ("v_c", 128), ("z_c", 512),
                  ("q_d", 512), ("k_d", 512), ("v_d", 512), ("z_d", 512)):
    _SRC[_name] = (_off, _w)
    _off += _w
IN_WIDTH = _off


def _piece(name, t, row_block=lambda n: n):
    off, width = _SRC[name]
    return pl.BlockSpec((pl.Element(t), pl.Element(width)),
                        lambda n: (pl.multiple_of(row_block(n) * t, t), off))


def _ada_kernel(c_ref, w_ref, b_ref, o_ref):
    c = c_ref[...]
    cond = c * jax.nn.sigmoid(c)
    o_ref[0] = jnp.dot(cond, w_ref[0], preferred_element_type=F32,
                       precision=lax.Precision.HIGHEST) + b_ref[0]


def _ada_mod(c8, w_ada, b_ada):
    depth, d, n3 = w_ada.shape
    tn = 1024
    return pl.pallas_call(
        _ada_kernel,
        grid=(depth, n3 // tn),
        in_specs=[pl.BlockSpec((8, d), lambda i, j: (0, 0)),
                  pl.BlockSpec((1, d, tn), lambda i, j: (i, 0, j)),
                  pl.BlockSpec((1, 1, tn), lambda i, j: (i, 0, j))],
        out_specs=pl.BlockSpec((1, 8, tn), lambda i, j: (i, 0, j)),
        out_shape=jax.ShapeDtypeStruct((depth, 8, n3), F32),
        compiler_params=pltpu.CompilerParams(
            dimension_semantics=("arbitrary", "arbitrary"), vmem_limit_bytes=VMEM_LIMIT),
        name="ada_mod",
    )(c8, w_ada, b_ada.reshape(depth, 1, n3))


def _modulated_norm(x, nw, scale, shift):
    y = x * lax.rsqrt(jnp.mean(x * x, axis=-1, keepdims=True) + EPS)
    return ((y * nw) * (1.0 + scale) + shift).astype(BF16)


def _norm_kernel(x_ref, nw_ref, sc_ref, sh_ref, h_ref):
    h_ref[...] = _modulated_norm(x_ref[...], nw_ref[...], sc_ref[...], sh_ref[...])


def _norm(x, nw, scale, shift):
    L, d = x.shape
    tm = min(512, L)
    vec = pl.BlockSpec((1, d), lambda i: (0, 0))
    return pl.pallas_call(
        _norm_kernel,
        grid=(L // tm,),
        in_specs=[pl.BlockSpec((tm, d), lambda i: (i, 0)), vec, vec, vec],
        out_specs=pl.BlockSpec((tm, d), lambda i: (i, 0)),
        out_shape=jax.ShapeDtypeStruct((L, d), BF16),
        compiler_params=pltpu.CompilerParams(
            dimension_semantics=("arbitrary",), vmem_limit_bytes=VMEM_LIMIT),
        name="norm",
    )(x, nw, scale, shift)


def _inproj_kernel(h_ref, w_ref, o_ref):
    o_ref[...] = jnp.dot(h_ref[...], w_ref[...].astype(BF16),
                         preferred_element_type=F32).astype(o_ref.dtype)


def _inproj(h, w, layer):
    L, d = h.shape
    n = w.shape[2]
    tm = min(2048, L)
    tn = 512
    return pl.pallas_call(
        _inproj_kernel,
        grid=(L // tm, n // tn),
        in_specs=[pl.BlockSpec((tm, d), lambda i, j: (i, 0)),
                  pl.BlockSpec((None, d, tn), lambda i, j: (layer, 0, j))],
        out_specs=pl.BlockSpec((tm, tn), lambda i, j: (i, j)),
        out_shape=jax.ShapeDtypeStruct((L, n), BF16),
        compiler_params=pltpu.CompilerParams(
            dimension_semantics=("arbitrary", "arbitrary"), vmem_limit_bytes=VMEM_LIMIT),
        name="inproj",
    )(h, w)


LOG2E = math.log2(math.e)
Q_SCALE = HEAD_DIM ** -0.5 * LOG2E


def _qt_prep_kernel(*refs):
    n = len(refs) // 2
    for q_ref, qt_ref in zip(refs[:n], refs[n:]):
        qt_ref[...] = (q_ref[...].astype(F32) * Q_SCALE).T.astype(qt_ref.dtype)


def _qt_prep(proj, names, t=1024):
    L = proj.shape[0]
    return pl.pallas_call(
        _qt_prep_kernel,
        grid=(L // t,),
        in_specs=[_piece(name, t) for name in names],
        out_specs=[pl.BlockSpec((GROUP_WIDTH, t), lambda n: (0, n)) for _ in names],
        out_shape=[jax.ShapeDtypeStruct((GROUP_WIDTH, L), BF16) for _ in names],
        compiler_params=pltpu.CompilerParams(
            dimension_semantics=("arbitrary",), vmem_limit_bytes=VMEM_LIMIT),
        name="qt_prep",
    )(*([proj] * len(names)))


def _kv_prep_kernel(k_ref, v_ref, kp_ref, vt_ref, *, pad_blocks, tok_blocks):
    b = pl.program_id(0)
    is_token = (b >= pad_blocks) & (b < pad_blocks + tok_blocks)

    @pl.when(is_token)
    def _():
        kp_ref[...] = k_ref[...]
        vt = v_ref[...].astype(F32).T.astype(vt_ref.dtype)
        ones = jnp.ones((V_ROWS - HEAD_DIM, vt.shape[1]), vt_ref.dtype)
        for g in range(vt.shape[0] // HEAD_DIM):
            vt_ref[g * V_ROWS:(g + 1) * V_ROWS, :] = jnp.concatenate(
                [vt[g * HEAD_DIM:(g + 1) * HEAD_DIM], ones], axis=0)

    @pl.when(jnp.logical_not(is_token))
    def _():
        kp_ref[...] = jnp.zeros(kp_ref.shape, kp_ref.dtype)
        vt_ref[...] = jnp.zeros(vt_ref.shape, vt_ref.dtype)


def _kv_prep(proj, k_name, v_name, width, pad, t):
    L = proj.shape[0]
    assert pad % t == 0 and L % t == 0 and width % LANES == 0
    pad_blocks, tok_blocks = pad // t, L // t
    lp = L + 2 * pad
    assert _SRC[k_name][1] == width and _SRC[v_name][1] == width
    tok = lambda b: jnp.clip(b - pad_blocks, 0, tok_blocks - 1)
    return pl.pallas_call(
        functools.partial(_kv_prep_kernel, pad_blocks=pad_blocks, tok_blocks=tok_blocks),
        grid=(lp // t,),
        in_specs=[_piece(k_name, t, tok), _piece(v_name, t, tok)],
        out_specs=[pl.BlockSpec((t, width), lambda b: (b, 0)),
                   pl.BlockSpec((width // HEAD_DIM * V_ROWS, t), lambda b: (0, b))],
        out_shape=[jax.ShapeDtypeStruct((lp, width), BF16),
                   jax.ShapeDtypeStruct((width // HEAD_DIM * V_ROWS, lp), BF16)],
        compiler_params=pltpu.CompilerParams(
            dimension_semantics=("arbitrary",), vmem_limit_bytes=VMEM_LIMIT),
        name="kv_prep",
    )(proj, proj)


def _band_kernel(*refs, w_keys, chunk, lanes, heads, group, use_sink, window, variant):
    if use_sink:
        sink_ref, qt_ref, k_ref, vt_ref, tab_ref, o_ref = refs
    else:
        qt_ref, k_ref, vt_ref, tab_ref, o_ref = refs
    n = pl.program_id(0)
    _, pos = window(n)
    kwin = k_ref[...]
    vwin = vt_ref[...]
    var = variant(n)

    n_chunks = w_keys // chunk

    def attend(kbias):
        def scores(h, c):
            kv = h // group
            rows = slice(c * chunk, (c + 1) * chunk)
            st = jnp.dot(kwin[rows, kv * HEAD_DIM:(kv + 1) * HEAD_DIM],
                         qt_ref[h * HEAD_DIM:(h + 1) * HEAD_DIM, :],
                         preferred_element_type=F32) + tab_ref[var, h, rows, :]
            return st if kbias is None else st + kbias[rows]

        def col_max(m, st):
            cm = jnp.max(st, axis=0, keepdims=True)
            return cm if m is None else jnp.maximum(m, cm)

        def finish(h, acc, m):
            o, l = acc[:HEAD_DIM], acc[HEAD_DIM:HEAD_DIM + 1]
            if use_sink:
                sk = sink_ref[h] * LOG2E
                m2 = jnp.maximum(m, sk)
                a = jnp.exp2(m - m2)
                return o * (a / (l * a + jnp.exp2(sk - m2)))
            return o / l

        outs = []
        groups = [range(g, g + lanes) for g in range(0, heads, lanes)]
        cur = {h: [] for h in groups[0]}
        m = {h: None for h in groups[0]}
        for c in range(n_chunks):
            for h in groups[0]:
                cur[h].append(scores(h, c))
                m[h] = col_max(m[h], cur[h][-1])
        for gi, grp in enumerate(groups):
            following = groups[gi + 1] if gi + 1 < len(groups) else ()
            nxt = {h: [] for h in following}
            m_next = {h: None for h in following}
            acc = {h: None for h in grp}
            for c in range(n_chunks):
                for h in following:
                    nxt[h].append(scores(h, c))
                    m_next[h] = col_max(m_next[h], nxt[h][-1])
                for h in grp:
                    kv = h // group
                    pt = jnp.exp2(cur[h][c] - m[h]).astype(BF16)
                    pv = jnp.dot(vwin[kv * V_ROWS:(kv + 1) * V_ROWS, c * chunk:(c + 1) * chunk], pt,
                                 preferred_element_type=F32)
                    acc[h] = pv if acc[h] is None else acc[h] + pv
            outs.extend(finish(h, acc[h], m[h]) for h in grp)
            cur, m = nxt, m_next
        o_ref[...] = jnp.concatenate(outs, axis=0).T.astype(o_ref.dtype)

    if pos is None:
        attend(None)
    else:
        first, lo, hi = pos
        inside = (first >= lo) & (first + w_keys <= hi)

        @pl.when(inside)
        def _():
            attend(None)

        @pl.when(jnp.logical_not(inside))
        def _():
            kpos = first + lax.broadcasted_iota(jnp.int32, (w_keys, 1), 0)
            attend(jnp.where((kpos >= lo) & (kpos < hi), 0.0, NEG).astype(F32))


def _band_attention(qt, k_pad, vt_pad, table, *, chunk, lanes, group, window, variant=lambda n: 0,
                    sink=None, layer=None):
    qw, L = qt.shape
    heads = qw // HEAD_DIM
    w_keys, tq = table.shape[-2:]
    assert w_keys % chunk == 0
    kern = functools.partial(_band_kernel, w_keys=w_keys, chunk=chunk, lanes=lanes, heads=heads, group=group,
                             use_sink=sink is not None, window=window, variant=variant)
    start = lambda n: pl.multiple_of(window(n)[0], LANES)
    if layer is None:
        table_spec = pl.BlockSpec(table.shape, lambda n: (0, 0, 0, 0), pipeline_mode=pl.Buffered(1))
    else:
        table_spec = pl.BlockSpec((None,) + table.shape[1:], lambda n: (layer, 0, 0, 0, 0),
                                  pipeline_mode=pl.Buffered(1))
    in_specs = [pl.BlockSpec((qw, tq), lambda n: (0, n)),
                pl.BlockSpec((pl.Element(w_keys), pl.Element(k_pad.shape[1])), lambda n: (start(n), 0)),
                pl.BlockSpec((pl.Element(vt_pad.shape[0]), pl.Element(w_keys)), lambda n: (0, start(n))),
                table_spec]
    args = [qt, k_pad, vt_pad, table]
    if sink is not None:
        in_specs = [pl.BlockSpec(memory_space=pltpu.SMEM)] + in_specs
        args = [sink] + args
    return pl.pallas_call(
        kern,
        grid=(L // tq,),
        in_specs=in_specs,
        out_specs=pl.BlockSpec((tq, qw), lambda n: (n, 0)),
        out_shape=jax.ShapeDtypeStruct((L, qw), BF16),
        compiler_params=pltpu.CompilerParams(
            dimension_semantics=("arbitrary",), vmem_limit_bytes=VMEM_LIMIT),
        name="band_attention",
    )(*args)


def _t5_bucket(rel):
    half = T5_BUCKETS // 2
    exact = half // 2
    n = jnp.abs(rel)
    big = exact + (jnp.log(jnp.maximum(n, exact).astype(F32) / exact)
                   / math.log(T5_MAX_DIST / exact) * (half - exact)).astype(jnp.int32)
    big = jnp.minimum(big, half - 1)
    return jnp.where(rel > 0, half, 0) + jnp.where(n < exact, n, big)


def _toeplitz_kernel(rv_ref, o_ref, *, n_diag):
    w, tq = o_ref.shape[1:]
    blocks = []
    for d in range(n_diag):
        s = LANES * (n_diag - 1 - d)
        x = jnp.broadcast_to(rv_ref[0, :, s:s + 2 * LANES], (LANES, 2 * LANES))
        blocks.append(pltpu.roll(x, LANES, 1, stride=1, stride_axis=0)[:, :LANES])
    for cb in range(w // LANES):
        for ib in range(tq // LANES):
            o_ref[0, cb * LANES:(cb + 1) * LANES, ib * LANES:(ib + 1) * LANES] = (
                blocks[cb - ib + tq // LANES - 1])


def _toeplitz(v, tq, w):
    heads, n_rel = v.shape
    n_diag = w // LANES + tq // LANES - 1
    assert w % LANES == 0 and tq % LANES == 0 and n_rel == LANES * n_diag + LANES - 1
    rv = jnp.pad(v, ((0, 0), (0, 1)))[:, None, ::-1]
    return pl.pallas_call(
        functools.partial(_toeplitz_kernel, n_diag=n_diag),
        grid=(heads,),
        in_specs=[pl.BlockSpec((1, 1, rv.shape[-1]), lambda h: (h, 0, 0))],
        out_specs=pl.BlockSpec((1, w, tq), lambda h: (h, 0, 0)),
        out_shape=jax.ShapeDtypeStruct((heads, w, tq), F32),
        compiler_params=pltpu.CompilerParams(
            dimension_semantics=("arbitrary",), vmem_limit_bytes=VMEM_LIMIT),
        name="toeplitz",
    )(rv)


def _t5_table(t5_table, head_lo, tq, r_lo, w_keys, mult):
    rel = np.arange(-r_lo - (tq - 1), w_keys - r_lo)
    m = mult(rel)
    b = t5_table.astype(F32)[:, head_lo:head_lo + GROUP_HEADS][_t5_bucket(jnp.asarray(rel, jnp.int32))]
    logm = np.log(np.maximum(m, 1)).astype(np.float32)
    vec = jnp.where(jnp.asarray(m > 0)[:, None], (b + logm[:, None]) * LOG2E, NEG).T
    return _toeplitz(vec, tq, w_keys)[None]


def _window_mult(rel):
    return (np.abs(rel) <= A_RADIUS).astype(np.int32)


def _dilated_mult(rel):
    m = np.zeros(rel.shape, np.int32)
    for window, dil in D_CONFIGS:
        m += ((rel % dil == 0) & (np.abs(rel) <= window // 2)).astype(np.int32)
    return m


NA_TILE_ROWS = 4
NA_WIN_ROWS = NA_TILE_ROWS + NA_ROWS
NA_TQ = NA_TILE_ROWS * GRID_W
NA_W = NA_WIN_ROWS * GRID_W


def _na_row_index(variant, j, i):
    dr, ok = ((j - i, j < NA_ROWS),
              (j - NA_ROWS // 2 - i, i <= j < i + NA_ROWS),
              (j - NA_ROWS - i, j >= NA_TILE_ROWS))[variant]
    return dr + NA_ROWS - 1 if ok else None


def _na_table_kernel(rp_ref, o_ref):
    kc = lax.broadcasted_iota(jnp.int32, (GRID_W, LANES), 0)
    lane = lax.broadcasted_iota(jnp.int32, (GRID_W, LANES), 1)
    col_start = jnp.clip(lane % GRID_W - NA_COLS // 2, 0, GRID_W - NA_COLS)
    col_ok = (kc >= col_start) & (kc < col_start + NA_COLS)
    neg = jnp.full((GRID_W, LANES), NEG, F32)
    cache = {}

    def half_block(d, side):
        if d is None:
            return neg
        if (d, side) not in cache:
            x = jnp.broadcast_to(rp_ref[0, 0, d:d + 1, :], (GRID_W, LANES))
            y = pltpu.roll(x, GRID_W * (1 - side), 1, stride=1, stride_axis=0)
            cache[d, side] = jnp.where(col_ok, y, NEG)
        return cache[d, side]

    for variant in range(3):
        for j in range(NA_WIN_ROWS):
            for ip in range(NA_TILE_ROWS // 2):
                left = half_block(_na_row_index(variant, j, 2 * ip), 0)
                right = half_block(_na_row_index(variant, j, 2 * ip + 1), 1)
                o_ref[0, variant, 0, j * GRID_W:(j + 1) * GRID_W, ip * LANES:(ip + 1) * LANES] = (
                    jnp.where(lane < GRID_W, left, right))


def _na_tables(rpb):
    depth, heads, n_dr, n_dc = rpb.shape
    assert n_dr == NA_WIN_ROWS + NA_TILE_ROWS - 1 and 2 * GRID_W == LANES
    front = GRID_W - NA_COLS
    rp = jnp.pad(rpb.astype(F32) * LOG2E, ((0, 0), (0, 0), (0, 1), (front, LANES - n_dc - front)))
    rp = rp[..., ::-1]
    return pl.pallas_call(
        _na_table_kernel,
        grid=(depth, heads),
        in_specs=[pl.BlockSpec((1, 1, n_dr + 1, LANES), lambda l, h: (l, h, 0, 0))],
        out_specs=pl.BlockSpec((1, 3, 1, NA_W, NA_TQ), lambda l, h: (l, 0, h, 0, 0)),
        out_shape=jax.ShapeDtypeStruct((depth, 3, heads, NA_W, NA_TQ), F32),
        compiler_params=pltpu.CompilerParams(
            dimension_semantics=("arbitrary", "arbitrary"), vmem_limit_bytes=VMEM_LIMIT),
        name="na_tables",
    )(rp)


def _swap_pairs(x):
    n = x.shape[-1]
    lane = lax.broadcasted_iota(jnp.int32, x.shape, x.ndim - 1)
    return jnp.where(lane % 2 == 0, pltpu.roll(x, n - 1, x.ndim - 1), pltpu.roll(x, 1, x.ndim - 1))


def _head_rms(x, ones_bd, w):
    sq = x * x
    hi = sq.astype(BF16)
    lo = (sq - hi.astype(F32)).astype(BF16)
    ms = (jnp.dot(hi, ones_bd, preferred_element_type=F32)
          + jnp.dot(lo, ones_bd, preferred_element_type=F32)) * (1.0 / HEAD_DIM)
    return x * lax.rsqrt(ms + EPS) * w


def _cprep_kernel(q_ref, k_ref, v_ref, cos_ref, sin_ref, qw_ref, kw_ref, bd_ref,
                  qt_ref, k2_ref, vt_ref):
    cos = cos_ref[...]
    sin = sin_ref[...]
    q = _head_rms(q_ref[...].astype(F32), bd_ref[...], qw_ref[...])
    reps = q.shape[1] // LANES
    q = q * jnp.concatenate([cos] * reps, axis=1) + _swap_pairs(q) * jnp.concatenate([sin] * reps, axis=1)
    qt_ref[...] = (q * Q_SCALE).T.astype(qt_ref.dtype)
    k = _head_rms(k_ref[...].astype(F32), bd_ref[:LANES, :LANES], kw_ref[...])
    k = (k * cos + _swap_pairs(k) * sin).astype(k2_ref.dtype)
    vt = v_ref[...].astype(F32).T.astype(vt_ref.dtype)
    ones = jnp.ones((V_ROWS - HEAD_DIM, vt.shape[1]), vt_ref.dtype)
    tk = vt_ref.shape[-1]
    for g in range(C_KV_HEADS):
        k2_ref[g] = k[:, g * HEAD_DIM:(g + 1) * HEAD_DIM]
        vg = jnp.concatenate([vt[g * HEAD_DIM:(g + 1) * HEAD_DIM], ones], axis=0)
        for u in range(vt_ref.shape[1]):
            vt_ref[g, u] = vg[:, u * tk:(u + 1) * tk]


def _cprep(proj, cos2, sin2, qw, kw, ones_bd, tk, t=1024):
    L = proj.shape[0]
    vec = lambda width: pl.BlockSpec((1, width), lambda n: (0, 0))
    return pl.pallas_call(
        _cprep_kernel,
        grid=(L // t,),
        in_specs=[_piece("q_c", t), _piece("k_c", t), _piece("v_c", t),
                  pl.BlockSpec((t, LANES), lambda n: (n, 0)),
                  pl.BlockSpec((t, LANES), lambda n: (n, 0)),
                  vec(GROUP_WIDTH), vec(LANES),
                  pl.BlockSpec((GROUP_WIDTH, GROUP_WIDTH), lambda n: (0, 0))],
        out_specs=[pl.BlockSpec((GROUP_WIDTH, t), lambda n: (0, n)),
                   pl.BlockSpec((C_KV_HEADS, t, HEAD_DIM), lambda n: (0, n, 0)),
                   pl.BlockSpec((C_KV_HEADS, t // tk, V_ROWS, tk), lambda n: (0, n, 0, 0))],
        out_shape=[jax.ShapeDtypeStruct((GROUP_WIDTH, L), BF16),
                   jax.ShapeDtypeStruct((C_KV_HEADS, L, HEAD_DIM), BF16),
                   jax.ShapeDtypeStruct((C_KV_HEADS, L // tk, V_ROWS, tk), BF16)],
        compiler_params=pltpu.CompilerParams(
            dimension_semantics=("arbitrary",), vmem_limit_bytes=VMEM_LIMIT),
        name="dense_prep",
    )(proj, proj, proj, cos2, sin2, qw, kw, ones_bd)


def _flash_kernel(qt_ref, k_ref, vt_ref, o_ref, m_scr, acc_scr, st_scr, *, tk, heads, unroll, ahead):
    n_kv = k_ref.shape[1] // tk
    m_scr[...] = jnp.full(m_scr.shape, NEG, F32)
    acc_scr[...] = jnp.zeros(acc_scr.shape, F32)

    def scores(j, h):
        k = k_ref[0, pl.ds(pl.multiple_of(j * tk, tk), tk), :]
        qt = qt_ref[h * HEAD_DIM:(h + 1) * HEAD_DIM, :]
        return jnp.dot(k, qt, preferred_element_type=F32)

    n_items = unroll * heads

    def item_scores(t, idx):
        j = t * unroll + idx // heads
        return scores(jnp.minimum(j, n_kv - 1), idx % heads)

    for a in range(ahead):
        st_scr[a] = item_scores(0, a)

    def body(t, carry):
        pending = [st_scr[a] for a in range(ahead)]
        for idx in range(n_items):
            pending.append(item_scores(t, idx + ahead))
            st = pending.pop(0)
            h = idx % heads
            vt = vt_ref[0, t * unroll + idx // heads]
            for q0 in range(0, st.shape[1], C_SPLIT):
                cols = slice(q0, q0 + C_SPLIT)
                m = m_scr[h, :, cols]
                m_new = jnp.maximum(m, jnp.max(st[:, cols], axis=0, keepdims=True))
                pt = jnp.exp2(st[:, cols] - m_new).astype(BF16)
                acc_scr[h, :, cols] = (jnp.exp2(m - m_new) * acc_scr[h, :, cols]
                                       + jnp.dot(vt, pt, preferred_element_type=F32))
                m_scr[h, :, cols] = m_new
        for a in range(ahead):
            st_scr[a] = pending[a]
        return carry

    lax.fori_loop(0, n_kv // unroll, body, 0)
    outs = [acc_scr[h, :HEAD_DIM] / acc_scr[h, HEAD_DIM:HEAD_DIM + 1] for h in range(heads)]
    o_ref[...] = jnp.concatenate(outs, axis=0).T.astype(o_ref.dtype)


def _flash(qt, k2, vt, tq):
    L = qt.shape[1]
    _, n_kv, v_rows, tk = vt.shape
    heads = GROUP_HEADS // C_KV_HEADS
    qrows = heads * HEAD_DIM
    return pl.pallas_call(
        functools.partial(_flash_kernel, tk=tk, heads=heads, unroll=math.gcd(C_UNROLL, n_kv),
                          ahead=C_AHEAD),
        grid=(C_KV_HEADS, L // tq),
        in_specs=[pl.BlockSpec((qrows, tq), lambda g, n: (g, n)),
                  pl.BlockSpec((1, L, HEAD_DIM), lambda g, n: (g, 0, 0)),
                  pl.BlockSpec((1, n_kv, v_rows, tk), lambda g, n: (g, 0, 0, 0))],
        out_specs=pl.BlockSpec((tq, qrows), lambda g, n: (n, g)),
        out_shape=jax.ShapeDtypeStruct((L, GROUP_WIDTH), BF16),
        scratch_shapes=[pltpu.VMEM((heads, 1, tq), F32), pltpu.VMEM((heads, v_rows, tq), F32),
                        pltpu.VMEM((C_AHEAD, tk, tq), F32)],
        compiler_params=pltpu.CompilerParams(
            dimension_semantics=("arbitrary", "arbitrary"), vmem_limit_bytes=VMEM_LIMIT),
        name="dense_flash",
    )(qt, k2, vt)


def _outproj_kernel(*refs, final):
    if final:
        x_ref, ya, yb, yc, yd, za, zb, zc, zd, w_ref, g_ref, fw_ref, o_ref = refs
    else:
        x_ref, ya, yb, yc, yd, za, zb, zc, zd, w_ref, g_ref, nw_ref, sc_ref, sh_ref, o_ref, h_ref = refs
    acc = jnp.zeros(x_ref.shape, F32)
    for gi, (y_ref, z_ref) in enumerate(zip((ya, yb, yc, yd), (za, zb, zc, zd))):
        z = z_ref[...].astype(F32)
        u = (y_ref[...].astype(F32) * (z * jax.nn.sigmoid(z))).astype(BF16)
        acc = acc + jnp.dot(u, w_ref[gi * GROUP_WIDTH:(gi + 1) * GROUP_WIDTH, :], preferred_element_type=F32)
    xn = x_ref[...] + g_ref[...] * acc
    if final:
        o_ref[...] = xn * lax.rsqrt(jnp.mean(xn * xn, axis=-1, keepdims=True) + EPS) * fw_ref[...]
    else:
        o_ref[...] = xn
        h_ref[...] = _modulated_norm(xn, nw_ref[...], sc_ref[...], sh_ref[...])


def _outproj(x, ys, proj, w_bf16, layer, gate, vecs):
    L, d = x.shape
    tm = min(512, L)
    final = len(vecs) == 1
    row = lambda width: pl.BlockSpec((tm, width), lambda i: (i, 0))
    vec = pl.BlockSpec((1, d), lambda i: (0, 0))
    in_specs = ([row(d)] + [row(GROUP_WIDTH)] * 4 + [_piece(z, tm) for z in ("z_a", "z_b", "z_c", "z_d")]
                + [pl.BlockSpec((None,) + w_bf16.shape[1:], lambda i: (layer, 0, 0)), vec]
                + [vec] * len(vecs))
    x_out = jax.ShapeDtypeStruct((L, d), F32)
    return pl.pallas_call(
        functools.partial(_outproj_kernel, final=final),
        grid=(L // tm,),
        in_specs=in_specs,
        out_specs=row(d) if final else [row(d), row(d)],
        out_shape=x_out if final else [x_out, jax.ShapeDtypeStruct((L, d), BF16)],
        compiler_params=pltpu.CompilerParams(
            dimension_semantics=("arbitrary",), vmem_limit_bytes=VMEM_LIMIT),
        name="outproj",
    )(x, *ys, proj, proj, proj, proj, w_bf16, gate, *vecs)


def _rope_tables(L):
    t = np.arange(L)
    axis_dim = HEAD_DIM // 2
    inv = jnp.asarray(ROPE_THETA, F32) ** (-jnp.arange(0, axis_dim, 2, dtype=F32) / axis_dim)
    row = jnp.asarray(t // GRID_W, F32)
    col = jnp.asarray(t % GRID_W, F32)
    ang = jnp.concatenate([row[:, None] * inv[None], col[:, None] * inv[None]], axis=-1)
    cos = jnp.repeat(jnp.cos(ang), 2, axis=-1)
    sin = jnp.repeat(jnp.sin(ang), 2, axis=-1) * jnp.asarray(np.tile([-1.0, 1.0], HEAD_DIM // 2), F32)
    return jnp.tile(cos, (1, LANES // HEAD_DIM)), jnp.tile(sin, (1, LANES // HEAD_DIM))


A_TQ = 256
A_CHUNK = 256
NA_CHUNK = 256
D_CHUNK = 768
BAND_LANES = 2
NA_LANES = 4
D_TQ = 256
D_REACH = D_CONFIGS[-1][0] // 2
C_TQ = 512
C_TK = 256
C_UNROLL = 16
C_AHEAD = 2
C_SPLIT = 256


def _table_a(t5_table):
    return _t5_table(t5_table, 0, A_TQ, A_RADIUS, A_TQ + 2 * A_RADIUS, _window_mult)


def _table_d(t5_table):
    return _t5_table(t5_table, GROUP_HEADS, D_TQ, D_REACH, D_TQ + 2 * D_REACH, _dilated_mult)


PREP_T = 1024
assert PREP_T >= D_REACH


def _seq_window(L, tq, r_lo):
    return lambda n: (n * tq + PREP_T - r_lo, (n * tq - r_lo, 0, L))


def _mixer_a(proj, qt, tab_a, sink):
    L = proj.shape[0]
    k_pad, vt_pad = _kv_prep(proj, "k_a", "v_a", LANES, PREP_T, PREP_T)
    return _band_attention(qt, k_pad, vt_pad, tab_a, chunk=A_CHUNK, lanes=BAND_LANES,
                           group=GROUP_HEADS // A_KV_HEADS, window=_seq_window(L, A_TQ, A_RADIUS),
                           sink=sink.astype(F32))


def _mixer_b(proj, qt, tabs_b, layer):
    L = proj.shape[0]
    rows = L // GRID_W
    n_tiles = rows // NA_TILE_ROWS
    k, vt = _kv_prep(proj, "k_b", "v_b", GROUP_WIDTH, 0, PREP_T)
    window = lambda n: (jnp.clip(n * NA_TILE_ROWS - NA_ROWS // 2, 0, rows - NA_WIN_ROWS) * GRID_W, None)
    variant = lambda n: jnp.where(n == 0, 0, jnp.where(n == n_tiles - 1, 2, 1))
    return _band_attention(qt, k, vt, tabs_b, chunk=NA_CHUNK, lanes=NA_LANES, group=1,
                           window=window, variant=variant, layer=layer)


def _mixer_c(proj, q_norm_w, k_norm_w):
    L = proj.shape[0]
    cos2, sin2 = _rope_tables(L)
    ones_bd = jnp.asarray(np.kron(np.eye(GROUP_HEADS), np.ones((HEAD_DIM, HEAD_DIM))), BF16)
    qt, k2, vt = _cprep(proj, cos2, sin2, jnp.tile(q_norm_w.astype(F32), GROUP_HEADS)[None],
                        jnp.tile(k_norm_w.astype(F32), LANES // HEAD_DIM)[None], ones_bd, C_TK)
    return _flash(qt, k2, vt, C_TQ)


def _mixer_d(proj, qt, tab_d):
    L = proj.shape[0]
    k_pad, vt_pad = _kv_prep(proj, "k_d", "v_d", GROUP_WIDTH, PREP_T, PREP_T)
    return _band_attention(qt, k_pad, vt_pad, tab_d, chunk=D_CHUNK, lanes=BAND_LANES, group=1,
                           window=_seq_window(L, D_TQ, D_REACH))


def kernel(x, c, w_ada, b_ada, norm_w, w_in, w_out, attn_sink, na_rpb, q_norm_w, k_norm_w,
           t5_table, final_norm_w):
    B, L, D = x.shape
    assert B == 1 and L % 1024 == 0 and L // GRID_W >= NA_WIN_ROWS
    depth = w_ada.shape[0]
    x = x[0]

    mod = _ada_mod(jnp.broadcast_to(c, (8, D)), w_ada, b_ada)[:, 0:1, :]
    tab_a = _table_a(t5_table)
    tab_d = _table_d(t5_table)
    tabs_b = _na_tables(na_rpb)
    w_out_b = w_out.astype(BF16)
    shift, scale, gate = jnp.split(mod, 3, axis=-1)
    h = _norm(x, norm_w[0][None], scale[0], shift[0])
    for i in range(depth):
        proj = _inproj(h, w_in, i)
        qt_a, qt_b, qt_d = _qt_prep(proj, ("q_a", "q_b", "q_d"))
        ys = (_mixer_a(proj, qt_a, tab_a, attn_sink[i]),
              _mixer_b(proj, qt_b, tabs_b, i),
              _mixer_c(proj, q_norm_w[i], k_norm_w[i]),
              _mixer_d(proj, qt_d, tab_d))
        if i + 1 < depth:
            x, h = _outproj(x, ys, proj, w_out_b, i, gate[i],
                            (norm_w[i + 1][None], scale[i + 1], shift[i + 1]))
        else:
            x = _outproj(x, ys, proj, w_out_b, i, gate[i], (final_norm_w[None],))
    return x[None]
```

```python
import functools
import math

import numpy as np
import jax
import jax.numpy as jnp
from jax import lax
from jax.experimental import pallas as pl
from jax.experimental.pallas import tpu as pltpu

HEAD_DIM = 64
GROUP_WIDTH = 512
GROUP_HEADS = 8
A_KV_HEADS = 2
A_RADIUS = 128
C_KV_HEADS = 2
ROPE_THETA = 10000.0
NA_ROWS = 8
NA_COLS = 16
D_CONFIGS = ((128, 1), (512, 4), (2048, 16))
GRID_W = 64
T5_BUCKETS = 32
T5_MAX_DIST = 1024
EPS = 1e-6
NEG = -1e30

LANES = 128
BF16_SUBLANES = 16
V_ROWS = HEAD_DIM + BF16_SUBLANES
VMEM_LIMIT = 56 * 1024 * 1024

F32 = jnp.float32
BF16 = jnp.bfloat16

_SRC = {}
_off = 0
for _name, _w in (("q_a", 512), ("k_a", 128), ("v_a", 128), ("z_a", 512),
                  ("q_b", 512), ("k_b", 512), ("v_b", 512), ("z_b", 512),
                  ("q_c", 512), ("k_c", 128), ("v_c", 128), ("z_c", 512),
                  ("q_d", 512), ("k_d", 512), ("v_d", 512), ("z_d", 512)):
    _SRC[_name] = (_off, _w)
    _off += _w
IN_WIDTH = _off


def _piece(name, t, row_block=lambda n: n):
    off, width = _SRC[name]
    return pl.BlockSpec((pl.Element(t), pl.Element(width)),
                        lambda n: (pl.multiple_of(row_block(n) * t, t), off))


def _ada_kernel(c_ref, w_ref, b_ref, o_ref):
    c = c_ref[...]
    cond = c * jax.nn.sigmoid(c)
    o_ref[0] = jnp.dot(cond, w_ref[0], preferred_element_type=F32,
                       precision=lax.Precision.HIGHEST) + b_ref[0]


def _ada_mod(c8, w_ada, b_ada):
    depth, d, n3 = w_ada.shape
    tn = 1024
    return pl.pallas_call(
        _ada_kernel,
        grid=(depth, n3 // tn),
        in_specs=[pl.BlockSpec((8, d), lambda i, j: (0, 0)),
                  pl.BlockSpec((1, d, tn), lambda i, j: (i, 0, j)),
                  pl.BlockSpec((1, 1, tn), lambda i, j: (i, 0, j))],
        out_specs=pl.BlockSpec((1, 8, tn), lambda i, j: (i, 0, j)),
        out_shape=jax.ShapeDtypeStruct((depth, 8, n3), F32),
        compiler_params=pltpu.CompilerParams(
            dimension_semantics=("arbitrary", "arbitrary"), vmem_limit_bytes=VMEM_LIMIT),
        name="ada_mod",
    )(c8, w_ada, b_ada.reshape(depth, 1, n3))


def _modulated_norm(x, nw, scale, shift):
    y = x * lax.rsqrt(jnp.mean(x * x, axis=-1, keepdims=True) + EPS)
    return ((y * nw) * (1.0 + scale) + shift).astype(BF16)


def _norm_kernel(x_ref, nw_ref, sc_ref, sh_ref, h_ref):
    h_ref[...] = _modulated_norm(x_ref[...], nw_ref[...], sc_ref[...], sh_ref[...])


def _norm(x, nw, scale, shift):
    L, d = x.shape
    tm = min(512, L)
    vec = pl.BlockSpec((1, d), lambda i: (0, 0))
    return pl.pallas_call(
        _norm_kernel,
        grid=(L // tm,),
        in_specs=[pl.BlockSpec((tm, d), lambda i: (i, 0)), vec, vec, vec],
        out_specs=pl.BlockSpec((tm, d), lambda i: (i, 0)),
        out_shape=jax.ShapeDtypeStruct((L, d), BF16),
        compiler_params=pltpu.CompilerParams(
            dimension_semantics=("arbitrary",), vmem_limit_bytes=VMEM_LIMIT),
        name="norm",
    )(x, nw, scale, shift)


def _inproj_kernel(h_ref, w_ref, o_ref):
    o_ref[...] = jnp.dot(h_ref[...], w_ref[...].astype(BF16),
                         preferred_element_type=F32).astype(o_ref.dtype)


def _inproj(h, w, layer):
    L, d = h.shape
    n = w.shape[2]
    tm = min(2048, L)
    tn = 512
    return pl.pallas_call(
        _inproj_kernel,
        grid=(L // tm, n // tn),
        in_specs=[pl.BlockSpec((tm, d), lambda i, j: (i, 0)),
                  pl.BlockSpec((None, d, tn), lambda i, j: (layer, 0, j))],
        out_specs=pl.BlockSpec((tm, tn), lambda i, j: (i, j)),
        out_shape=jax.ShapeDtypeStruct((L, n), BF16),
        compiler_params=pltpu.CompilerParams(
            dimension_semantics=("arbitrary", "arbitrary"), vmem_limit_bytes=VMEM_LIMIT),
        name="inproj",
    )(h, w)


LOG2E = math.log2(math.e)
Q_SCALE = HEAD_DIM ** -0.5 * LOG2E


def _qt_prep_kernel(*refs):
    n = len(refs) // 2
    for q_ref, qt_ref in zip(refs[:n], refs[n:]):
        qt_ref[...] = (q_ref[...].astype(F32) * Q_SCALE).T.astype(qt_ref.dtype)


def _qt_prep(proj, names, t=1024):
    L = proj.shape[0]
    return pl.pallas_call(
        _qt_prep_kernel,
        grid=(L // t,),
        in_specs=[_piece(name, t) for name in names],
        out_specs=[pl.BlockSpec((GROUP_WIDTH, t), lambda n: (0, n)) for _ in names],
        out_shape=[jax.ShapeDtypeStruct((GROUP_WIDTH, L), BF16) for _ in names],
        compiler_params=pltpu.CompilerParams(
            dimension_semantics=("arbitrary",), vmem_limit_bytes=VMEM_LIMIT),
        name="qt_prep",
    )(*([proj] * len(names)))


def _kv_prep_kernel(k_ref, v_ref, kp_ref, vt_ref, *, pad_blocks, tok_blocks):
    b = pl.program_id(0)
    is_token = (b >= pad_blocks) & (b < pad_blocks + tok_blocks)

    @pl.when(is_token)
    def _():
        kp_ref[...] = k_ref[...]
        vt = v_ref[...].astype(F32).T.astype(vt_ref.dtype)
        ones = jnp.ones((V_ROWS - HEAD_DIM, vt.shape[1]), vt_ref.dtype)
        for g in range(vt.shape[0] // HEAD_DIM):
            vt_ref[g * V_ROWS:(g + 1) * V_ROWS, :] = jnp.concatenate(
                [vt[g * HEAD_DIM:(g + 1) * HEAD_DIM], ones], axis=0)

    @pl.when(jnp.logical_not(is_token))
    def _():
        kp_ref[...] = jnp.zeros(kp_ref.shape, kp_ref.dtype)
        vt_ref[...] = jnp.zeros(vt_ref.shape, vt_ref.dtype)


def _kv_prep(proj, k_name, v_name, width, pad, t):
    L = proj.shape[0]
    assert pad % t == 0 and L % t == 0 and width % LANES == 0
    pad_blocks, tok_blocks = pad // t, L // t
    lp = L + 2 * pad
    assert _SRC[k_name][1] == width and _SRC[v_name][1] == width
    tok = lambda b: jnp.clip(b - pad_blocks, 0, tok_blocks - 1)
    return pl.pallas_call(
        functools.partial(_kv_prep_kernel, pad_blocks=pad_blocks, tok_blocks=tok_blocks),
        grid=(lp // t,),
        in_specs=[_piece(k_name, t, tok), _piece(v_name, t, tok)],
        out_specs=[pl.BlockSpec((t, width), lambda b: (b, 0)),
                   pl.BlockSpec((width // HEAD_DIM * V_ROWS, t), lambda b: (0, b))],
        out_shape=[jax.ShapeDtypeStruct((lp, width), BF16),
                   jax.ShapeDtypeStruct((width // HEAD_DIM * V_ROWS, lp), BF16)],
        compiler_params=pltpu.CompilerParams(
            dimension_semantics=("arbitrary",), vmem_limit_bytes=VMEM_LIMIT),
        name="kv_prep",
    )(proj, proj)


def _band_kernel(*refs, w_keys, chunk, lanes, heads, group, use_sink, window, variant):
    if use_sink:
        sink_ref, qt_ref, k_ref, vt_ref, tab_ref, o_ref = refs
    else:
        qt_ref, k_ref, vt_ref, tab_ref, o_ref = refs
    n = pl.program_id(0)
    _, pos = window(n)
    kwin = k_ref[...]
    vwin = vt_ref[...]
    var = variant(n)

    n_chunks = w_keys // chunk

    def attend(kbias):
        def scores(h, c):
            kv = h // group
            rows = slice(c * chunk, (c + 1) * chunk)
            st = jnp.dot(kwin[rows, kv * HEAD_DIM:(kv + 1) * HEAD_DIM],
                         qt_ref[h * HEAD_DIM:(h + 1) * HEAD_DIM, :],
                         preferred_element_type=F32) + tab_ref[var, h, rows, :]
            return st if kbias is None else st + kbias[rows]

        def col_max(m, st):
            cm = jnp.max(st, axis=0, keepdims=True)
            return cm if m is None else jnp.maximum(m, cm)

        def finish(h, acc, m):
            o, l = acc[:HEAD_DIM], acc[HEAD_DIM:HEAD_DIM + 1]
            if use_sink:
                sk = sink_ref[h] * LOG2E
                m2 = jnp.maximum(m, sk)
                a = jnp.exp2(m - m2)
                return o * (a / (l * a + jnp.exp2(sk - m2)))
            return o / l

        outs = []
        groups = [range(g, g + lanes) for g in range(0, heads, lanes)]
        cur = {h: [] for h in groups[0]}
        m = {h: None for h in groups[0]}
        for c in range(n_chunks):
            for h in groups[0]:
                cur[h].append(scores(h, c))
                m[h] = col_max(m[h], cur[h][-1])
        for gi, grp in enumerate(groups):
            following = groups[gi + 1] if gi + 1 < len(groups) else ()
            nxt = {h: [] for h in following}
            m_next = {h: None for h in following}
            acc = {h: None for h in grp}
            for c in range(n_chunks):
                for h in following:
                    nxt[h].append(scores(h, c))
                    m_next[h] = col_max(m_next[h], nxt[h][-1])
                for h in grp:
                    kv = h // group
                    pt = jnp.exp2(cur[h][c] - m[h]).astype(BF16)
                    pv = jnp.dot(vwin[kv * V_ROWS:(kv + 1) * V_ROWS, c * chunk:(c + 1) * chunk], pt,
                                 preferred_element_type=F32)
                    acc[h] = pv if acc[h] is None else acc[h] + pv
            outs.extend(finish(h, acc[h], m[h]) for h in grp)
            cur, m = nxt, m_next
        o_ref[...] = jnp.concatenate(outs, axis=0).T.astype(o_ref.dtype)

    if pos is None:
        attend(None)
    else:
        first, lo, hi = pos
        inside = (first >= lo) & (first + w_keys <= hi)

        @pl.when(inside)
        def _():
            attend(None)

        @pl.when(jnp.logical_not(inside))
        def _():
            kpos = first + lax.broadcasted_iota(jnp.int32, (w_keys, 1), 0)
            attend(jnp.where((kpos >= lo) & (kpos < hi), 0.0, NEG).astype(F32))


def _band_attention(qt, k_pad, vt_pad, table, *, chunk, lanes, group, window, variant=lambda n: 0,
                    sink=None, layer=None):
    qw, L = qt.shape
    heads = qw // HEAD_DIM
    w_keys, tq = table.shape[-2:]
    assert w_keys % chunk == 0
    kern = functools.partial(_band_kernel, w_keys=w_keys, chunk=chunk, lanes=lanes, heads=heads, group=group,
                             use_sink=sink is not None, window=window, variant=variant)
    start = lambda n: pl.multiple_of(window(n)[0], LANES)
    if layer is None:
        table_spec = pl.BlockSpec(table.shape, lambda n: (0, 0, 0, 0), pipeline_mode=pl.Buffered(1))
    else:
        table_spec = pl.BlockSpec((None,) + table.shape[1:], lambda n: (layer, 0, 0, 0, 0),
                                  pipeline_mode=pl.Buffered(1))
    in_specs = [pl.BlockSpec((qw, tq), lambda n: (0, n)),
                pl.BlockSpec((pl.Element(w_keys), pl.Element(k_pad.shape[1])), lambda n: (start(n), 0)),
                pl.BlockSpec((pl.Element(vt_pad.shape[0]), pl.Element(w_keys)), lambda n: (0, start(n))),
                table_spec]
    args = [qt, k_pad, vt_pad, table]
    if sink is not None:
        in_specs = [pl.BlockSpec(memory_space=pltpu.SMEM)] + in_specs
        args = [sink] + args
    return pl.pallas_call(
        kern,
        grid=(L // tq,),
        in_specs=in_specs,
        out_specs=pl.BlockSpec((tq, qw), lambda n: (n, 0)),
        out_shape=jax.ShapeDtypeStruct((L, qw), BF16),
        compiler_params=pltpu.CompilerParams(
            dimension_semantics=("arbitrary",), vmem_limit_bytes=VMEM_LIMIT),
        name="band_attention",
    )(*args)


def _t5_bucket(rel):
    half = T5_BUCKETS // 2
    exact = half // 2
    n = jnp.abs(rel)
    big = exact + (jnp.log(jnp.maximum(n, exact).astype(F32) / exact)
                   / math.log(T5_MAX_DIST / exact) * (half - exact)).astype(jnp.int32)
    big = jnp.minimum(big, half - 1)
    return jnp.where(rel > 0, half, 0) + jnp.where(n < exact, n, big)


def _toeplitz_kernel(rv_ref, o_ref, *, n_diag):
    w, tq = o_ref.shape[1:]
    blocks = []
    for d in range(n_diag):
        s = LANES * (n_diag - 1 - d)
        x = jnp.broadcast_to(rv_ref[0, :, s:s + 2 * LANES], (LANES, 2 * LANES))
        blocks.append(pltpu.roll(x, LANES, 1, stride=1, stride_axis=0)[:, :LANES])
    for cb in range(w // LANES):
        for ib in range(tq // LANES):
            o_ref[0, cb * LANES:(cb + 1) * LANES, ib * LANES:(ib + 1) * LANES] = (
                blocks[cb - ib + tq // LANES - 1])


def _toeplitz(v, tq, w):
    heads, n_rel = v.shape
    n_diag = w // LANES + tq // LANES - 1
    assert w % LANES == 0 and tq % LANES == 0 and n_rel == LANES * n_diag + LANES - 1
    rv = jnp.pad(v, ((0, 0), (0, 1)))[:, None, ::-1]
    return pl.pallas_call(
        functools.partial(_toeplitz_kernel, n_diag=n_diag),
        grid=(heads,),
        in_specs=[pl.BlockSpec((1, 1, rv.shape[-1]), lambda h: (h, 0, 0))],
        out_specs=pl.BlockSpec((1, w, tq), lambda h: (h, 0, 0)),
        out_shape=jax.ShapeDtypeStruct((heads, w, tq), F32),
        compiler_params=pltpu.CompilerParams(
            dimension_semantics=("arbitrary",), vmem_limit_bytes=VMEM_LIMIT),
        name="toeplitz",
    )(rv)


def _t5_table(t5_table, head_lo, tq, r_lo, w_keys, mult):
    rel = np.arange(-r_lo - (tq - 1), w_keys - r_lo)
    m = mult(rel)
    b = t5_table.astype(F32)[:, head_lo:head_lo + GROUP_HEADS][_t5_bucket(jnp.asarray(rel, jnp.int32))]
    logm = np.log(np.maximum(m, 1)).astype(np.float32)
    vec = jnp.where(jnp.asarray(m > 0)[:, None], (b + logm[:, None]) * LOG2E, NEG).T
    return _toeplitz(vec, tq, w_keys)[None]


def _window_mult(rel):
    return (np.abs(rel) <= A_RADIUS).astype(np.int32)


def _dilated_mult(rel):
    m = np.zeros(rel.shape, np.int32)
    for window, dil in D_CONFIGS:
        m += ((rel % dil == 0) & (np.abs(rel) <= window // 2)).astype(np.int32)
    return m


NA_TILE_ROWS = 4
NA_WIN_ROWS = NA_TILE_ROWS + NA_ROWS
NA_TQ = NA_TILE_ROWS * GRID_W
NA_W = NA_WIN_ROWS * GRID_W


def _na_row_index(variant, j, i):
    dr, ok = ((j - i, j < NA_ROWS),
              (j - NA_ROWS // 2 - i, i <= j < i + NA_ROWS),
              (j - NA_ROWS - i, j >= NA_TILE_ROWS))[variant]
    return dr + NA_ROWS - 1 if ok else None


def _na_table_kernel(rp_ref, o_ref):
    kc = lax.broadcasted_iota(jnp.int32, (GRID_W, LANES), 0)
    lane = lax.broadcasted_iota(jnp.int32, (GRID_W, LANES), 1)
    col_start = jnp.clip(lane % GRID_W - NA_COLS // 2, 0, GRID_W - NA_COLS)
    col_ok = (kc >= col_start) & (kc < col_start + NA_COLS)
    neg = jnp.full((GRID_W, LANES), NEG, F32)
    cache = {}

    def half_block(d, side):
        if d is None:
            return neg
        if (d, side) not in cache:
            x = jnp.broadcast_to(rp_ref[0, 0, d:d + 1, :], (GRID_W, LANES))
            y = pltpu.roll(x, GRID_W * (1 - side), 1, stride=1, stride_axis=0)
            cache[d, side] = jnp.where(col_ok, y, NEG)
        return cache[d, side]

    for variant in range(3):
        for j in range(NA_WIN_ROWS):
            for ip in range(NA_TILE_ROWS // 2):
                left = half_block(_na_row_index(variant, j, 2 * ip), 0)
                right = half_block(_na_row_index(variant, j, 2 * ip + 1), 1)
                o_ref[0, variant, 0, j * GRID_W:(j + 1) * GRID_W, ip * LANES:(ip + 1) * LANES] = (
                    jnp.where(lane < GRID_W, left, right))


def _na_tables(rpb):
    depth, heads, n_dr, n_dc = rpb.shape
    assert n_dr == NA_WIN_ROWS + NA_TILE_ROWS - 1 and 2 * GRID_W == LANES
    front = GRID_W - NA_COLS
    rp = jnp.pad(rpb.astype(F32) * LOG2E, ((0, 0), (0, 0), (0, 1), (front, LANES - n_dc - front)))
    rp = rp[..., ::-1]
    return pl.pallas_call(
        _na_table_kernel,
        grid=(depth, heads),
        in_specs=[pl.BlockSpec((1, 1, n_dr + 1, LANES), lambda l, h: (l, h, 0, 0))],
        out_specs=pl.BlockSpec((1, 3, 1, NA_W, NA_TQ), lambda l, h: (l, 0, h, 0, 0)),
        out_shape=jax.ShapeDtypeStruct((depth, 3, heads, NA_W, NA_TQ), F32),
        compiler_params=pltpu.CompilerParams(
            dimension_semantics=("arbitrary", "arbitrary"), vmem_limit_bytes=VMEM_LIMIT),
        name="na_tables",
    )(rp)


def _swap_pairs(x):
    n = x.shape[-1]
    lane = lax.broadcasted_iota(jnp.int32, x.shape, x.ndim - 1)
    return jnp.where(lane % 2 == 0, pltpu.roll(x, n - 1, x.ndim - 1), pltpu.roll(x, 1, x.ndim - 1))


def _head_rms(x, ones_bd, w):
    sq = x * x
    hi = sq.astype(BF16)
    lo = (sq - hi.astype(F32)).astype(BF16)
    ms = (jnp.dot(hi, ones_bd, preferred_element_type=F32)
          + jnp.dot(lo, ones_bd, preferred_element_type=F32)) * (1.0 / HEAD_DIM)
    return x * lax.rsqrt(ms + EPS) * w


def _cprep_kernel(q_ref, k_ref, v_ref, cos_ref, sin_ref, qw_ref, kw_ref, bd_ref,
                  qt_ref, k2_ref, vt_ref):
    cos = cos_ref[...]
    sin = sin_ref[...]
    q = _head_rms(q_ref[...].astype(F32), bd_ref[...], qw_ref[...])
    reps = q.shape[1] // LANES
    q = q * jnp.concatenate([cos] * reps, axis=1) + _swap_pairs(q) * jnp.concatenate([sin] * reps, axis=1)
    qt_ref[...] = (q * Q_SCALE).T.astype(qt_ref.dtype)
    k = _head_rms(k_ref[...].astype(F32), bd_ref[:LANES, :LANES], kw_ref[...])
    k = (k * cos + _swap_pairs(k) * sin).astype(k2_ref.dtype)
    vt = v_ref[...].astype(F32).T.astype(vt_ref.dtype)
    ones = jnp.ones((V_ROWS - HEAD_DIM, vt.shape[1]), vt_ref.dtype)
    tk = vt_ref.shape[-1]
    for g in range(C_KV_HEADS):
        k2_ref[g] = k[:, g * HEAD_DIM:(g + 1) * HEAD_DIM]
        vg = jnp.concatenate([vt[g * HEAD_DIM:(g + 1) * HEAD_DIM], ones], axis=0)
        for u in range(vt_ref.shape[1]):
            vt_ref[g, u] = vg[:, u * tk:(u + 1) * tk]


def _cprep(proj, cos2, sin2, qw, kw, ones_bd, tk, t=1024):
    L = proj.shape[0]
    vec = lambda width: pl.BlockSpec((1, width), lambda n: (0, 0))
    return pl.pallas_call(
        _cprep_kernel,
        grid=(L // t,),
        in_specs=[_piece("q_c", t), _piece("k_c", t), _piece("v_c", t),
                  pl.BlockSpec((t, LANES), lambda n: (n, 0)),
                  pl.BlockSpec((t, LANES), lambda n: (n, 0)),
                  vec(GROUP_WIDTH), vec(LANES),
                  pl.BlockSpec((GROUP_WIDTH, GROUP_WIDTH), lambda n: (0, 0))],
        out_specs=[pl.BlockSpec((GROUP_WIDTH, t), lambda n: (0, n)),
                   pl.BlockSpec((C_KV_HEADS, t, HEAD_DIM), lambda n: (0, n, 0)),
                   pl.BlockSpec((C_KV_HEADS, t // tk, V_ROWS, tk), lambda n: (0, n, 0, 0))],
        out_shape=[jax.ShapeDtypeStruct((GROUP_WIDTH, L), BF16),
                   jax.ShapeDtypeStruct((C_KV_HEADS, L, HEAD_DIM), BF16),
                   jax.ShapeDtypeStruct((C_KV_HEADS, L // tk, V_ROWS, tk), BF16)],
        compiler_params=pltpu.CompilerParams(
            dimension_semantics=("arbitrary",), vmem_limit_bytes=VMEM_LIMIT),
        name="dense_prep",
    )(proj, proj, proj, cos2, sin2, qw, kw, ones_bd)


def _flash_kernel(qt_ref, k_ref, vt_ref, o_ref, m_scr, acc_scr, st_scr, *, tk, heads, unroll, ahead):
    n_kv = k_ref.shape[1] // tk
    m_scr[...] = jnp.full(m_scr.shape, NEG, F32)
    acc_scr[...] = jnp.zeros(acc_scr.shape, F32)

    def scores(j, h):
        k = k_ref[0, pl.ds(pl.multiple_of(j * tk, tk), tk), :]
        qt = qt_ref[h * HEAD_DIM:(h + 1) * HEAD_DIM, :]
        return jnp.dot(k, qt, preferred_element_type=F32)

    n_items = unroll * heads

    def item_scores(t, idx):
        j = t * unroll + idx // heads
        return scores(jnp.minimum(j, n_kv - 1), idx % heads)

    for a in range(ahead):
        st_scr[a] = item_scores(0, a)

    def body(t, carry):
        pending = [st_scr[a] for a in range(ahead)]
        for idx in range(n_items):
            pending.append(item_scores(t, idx + ahead))
            st = pending.pop(0)
            h = idx % heads
            vt = vt_ref[0, t * unroll + idx // heads]
            for q0 in range(0, st.shape[1], C_SPLIT):
                cols = slice(q0, q0 + C_SPLIT)
                m = m_scr[h, :, cols]
                m_new = jnp.maximum(m, jnp.max(st[:, cols], axis=0, keepdims=True))
                pt = jnp.exp2(st[:, cols] - m_new).astype(BF16)
                acc_scr[h, :, cols] = (jnp.exp2(m - m_new) * acc_scr[h, :, cols]
                                       + jnp.dot(vt, pt, preferred_element_type=F32))
                m_scr[h, :, cols] = m_new
        for a in range(ahead):
            st_scr[a] = pending[a]
        return carry

    lax.fori_loop(0, n_kv // unroll, body, 0)
    outs = [acc_scr[h, :HEAD_DIM] / acc_scr[h, HEAD_DIM:HEAD_DIM + 1] for h in range(heads)]
    o_ref[...] = jnp.concatenate(outs, axis=0).T.astype(o_ref.dtype)


def _flash(qt, k2, vt, tq):
    L = qt.shape[1]
    _, n_kv, v_rows, tk = vt.shape
    heads = GROUP_HEADS // C_KV_HEADS
    qrows = heads * HEAD_DIM
    return pl.pallas_call(
        functools.partial(_flash_kernel, tk=tk, heads=heads, unroll=math.gcd(C_UNROLL, n_kv),
                          ahead=C_AHEAD),
        grid=(C_KV_HEADS, L // tq),
        in_specs=[pl.BlockSpec((qrows, tq), lambda g, n: (g, n)),
                  pl.BlockSpec((1, L, HEAD_DIM), lambda g, n: (g, 0, 0)),
                  pl.BlockSpec((1, n_kv, v_rows, tk), lambda g, n: (g, 0, 0, 0))],
        out_specs=pl.BlockSpec((tq, qrows), lambda g, n: (n, g)),
        out_shape=jax.ShapeDtypeStruct((L, GROUP_WIDTH), BF16),
        scratch_shapes=[pltpu.VMEM((heads, 1, tq), F32), pltpu.VMEM((heads, v_rows, tq), F32),
                        pltpu.VMEM((C_AHEAD, tk, tq), F32)],
        compiler_params=pltpu.CompilerParams(
            dimension_semantics=("arbitrary", "arbitrary"), vmem_limit_bytes=VMEM_LIMIT),
        name="dense_flash",
    )(qt, k2, vt)


def _outproj_kernel(*refs, final):
    if final:
        x_ref, ya, yb, yc, yd, za, zb, zc, zd, w_ref, g_ref, fw_ref, o_ref = refs
    else:
        x_ref, ya, yb, yc, yd, za, zb, zc, zd, w_ref, g_ref, nw_ref, sc_ref, sh_ref, o_ref, h_ref = refs
    sub = x_ref.shape[0] // OUT_SPLIT
    for r0 in range(0, x_ref.shape[0], sub):
        rows = slice(r0, r0 + sub)
        acc = jnp.zeros((sub, x_ref.shape[1]), F32)
        for gi, (y_ref, z_ref) in enumerate(zip((ya, yb, yc, yd), (za, zb, zc, zd))):
            z = z_ref[rows, :].astype(F32)
            u = (y_ref[rows, :].astype(F32) * (z * jax.nn.sigmoid(z))).astype(BF16)
            acc = acc + jnp.dot(u, w_ref[gi * GROUP_WIDTH:(gi + 1) * GROUP_WIDTH, :],
                                preferred_element_type=F32)
        xn = x_ref[rows, :] + g_ref[...] * acc
        if final:
            o_ref[rows, :] = (xn * lax.rsqrt(jnp.mean(xn * xn, axis=-1, keepdims=True) + EPS)
                              * fw_ref[...])
        else:
            o_ref[rows, :] = xn
            h_ref[rows, :] = _modulated_norm(xn, nw_ref[...], sc_ref[...], sh_ref[...])


def _outproj(x, ys, proj, w_bf16, layer, gate, vecs):
    L, d = x.shape
    tm = min(512, L)
    final = len(vecs) == 1
    row = lambda width: pl.BlockSpec((tm, width), lambda i: (i, 0))
    vec = pl.BlockSpec((1, d), lambda i: (0, 0))
    in_specs = ([row(d)] + [row(GROUP_WIDTH)] * 4 + [_piece(z, tm) for z in ("z_a", "z_b", "z_c", "z_d")]
                + [pl.BlockSpec((None,) + w_bf16.shape[1:], lambda i: (layer, 0, 0)), vec]
                + [vec] * len(vecs))
    x_out = jax.ShapeDtypeStruct((L, d), F32)
    return pl.pallas_call(
        functools.partial(_outproj_kernel, final=final),
        grid=(L // tm,),
        in_specs=in_specs,
        out_specs=row(d) if final else [row(d), row(d)],
        out_shape=x_out if final else [x_out, jax.ShapeDtypeStruct((L, d), BF16)],
        compiler_params=pltpu.CompilerParams(
            dimension_semantics=("arbitrary",), vmem_limit_bytes=VMEM_LIMIT),
        name="outproj",
    )(x, *ys, proj, proj, proj, proj, w_bf16, gate, *vecs)


def _rope_tables(L):
    t = np.arange(L)
    axis_dim = HEAD_DIM // 2
    inv = jnp.asarray(ROPE_THETA, F32) ** (-jnp.arange(0, axis_dim, 2, dtype=F32) / axis_dim)
    row = jnp.asarray(t // GRID_W, F32)
    col = jnp.asarray(t % GRID_W, F32)
    ang = jnp.concatenate([row[:, None] * inv[None], col[:, None] * inv[None]], axis=-1)
    cos = jnp.repeat(jnp.cos(ang), 2, axis=-1)
    sin = jnp.repeat(jnp.sin(ang), 2, axis=-1) * jnp.asarray(np.tile([-1.0, 1.0], HEAD_DIM // 2), F32)
    return jnp.tile(cos, (1, LANES // HEAD_DIM)), jnp.tile(sin, (1, LANES // HEAD_DIM))


A_TQ = 256
A_CHUNK = 256
NA_CHUNK = 256
D_CHUNK = 768
BAND_LANES = 2
NA_LANES = 4
D_TQ = 256
D_REACH = D_CONFIGS[-1][0] // 2
C_TQ = 512
C_TK = 256
C_UNROLL = 16
C_AHEAD = 2
C_SPLIT = 256
OUT_SPLIT = 2


def _table_a(t5_table):
    return _t5_table(t5_table, 0, A_TQ, A_RADIUS, A_TQ + 2 * A_RADIUS, _window_mult)


def _table_d(t5_table):
    return _t5_table(t5_table, GROUP_HEADS, D_TQ, D_REACH, D_TQ + 2 * D_REACH, _dilated_mult)


PREP_T = 1024
assert PREP_T >= D_REACH


def _seq_window(L, tq, r_lo):
    return lambda n: (n * tq + PREP_T - r_lo, (n * tq - r_lo, 0, L))


def _mixer_a(proj, qt, tab_a, sink):
    L = proj.shape[0]
    k_pad, vt_pad = _kv_prep(proj, "k_a", "v_a", LANES, PREP_T, PREP_T)
    return _band_attention(qt, k_pad, vt_pad, tab_a, chunk=A_CHUNK, lanes=BAND_LANES,
                           group=GROUP_HEADS // A_KV_HEADS, window=_seq_window(L, A_TQ, A_RADIUS),
                           sink=sink.astype(F32))


def _mixer_b(proj, qt, tabs_b, layer):
    L = proj.shape[0]
    rows = L // GRID_W
    n_tiles = rows // NA_TILE_ROWS
    k, vt = _kv_prep(proj, "k_b", "v_b", GROUP_WIDTH, 0, PREP_T)
    window = lambda n: (jnp.clip(n * NA_TILE_ROWS - NA_ROWS // 2, 0, rows - NA_WIN_ROWS) * GRID_W, None)
    variant = lambda n: jnp.where(n == 0, 0, jnp.where(n == n_tiles - 1, 2, 1))
    return _band_attention(qt, k, vt, tabs_b, chunk=NA_CHUNK, lanes=NA_LANES, group=1,
                           window=window, variant=variant, layer=layer)


def _mixer_c(proj, q_norm_w, k_norm_w):
    L = proj.shape[0]
    cos2, sin2 = _rope_tables(L)
    ones_bd = jnp.asarray(np.kron(np.eye(GROUP_HEADS), np.ones((HEAD_DIM, HEAD_DIM))), BF16)
    qt, k2, vt = _cprep(proj, cos2, sin2, jnp.tile(q_norm_w.astype(F32), GROUP_HEADS)[None],
                        jnp.tile(k_norm_w.astype(F32), LANES // HEAD_DIM)[None], ones_bd, C_TK)
    return _flash(qt, k2, vt, C_TQ)


def _mixer_d(proj, qt, tab_d):
    L = proj.shape[0]
    k_pad, vt_pad = _kv_prep(proj, "k_d", "v_d", GROUP_WIDTH, PREP_T, PREP_T)
    return _band_attention(qt, k_pad, vt_pad, tab_d, chunk=D_CHUNK, lanes=BAND_LANES, group=1,
                           window=_seq_window(L, D_TQ, D_REACH))


def kernel(x, c, w_ada, b_ada, norm_w, w_in, w_out, attn_sink, na_rpb, q_norm_w, k_norm_w,
           t5_table, final_norm_w):
    B, L, D = x.shape
    assert B == 1 and L % 1024 == 0 and L // GRID_W >= NA_WIN_ROWS
    depth = w_ada.shape[0]
    x = x[0]

    mod = _ada_mod(jnp.broadcast_to(c, (8, D)), w_ada, b_ada)[:, 0:1, :]
    tab_a = _table_a(t5_table)
    tab_d = _table_d(t5_table)
    tabs_b = _na_tables(na_rpb)
    w_out_b = w_out.astype(BF16)
    shift, scale, gate = jnp.split(mod, 3, axis=-1)
    h = _norm(x, norm_w[0][None], scale[0], shift[0])
    for i in range(depth):
        proj = _inproj(h, w_in, i)
        qt_a, qt_b, qt_d = _qt_prep(proj, ("q_a", "q_b", "q_d"))
        ys = (_mixer_a(proj, qt_a, tab_a, attn_sink[i]),
              _mixer_b(proj, qt_b, tabs_b, i),
              _mixer_c(proj, q_norm_w[i], k_norm_w[i]),
              _mixer_d(proj, qt_d, tab_d))
        if i + 1 < depth:
            x, h = _outproj(x, ys, proj, w_out_b, i, gate[i],
                            (norm_w[i + 1][None], scale[i + 1], shift[i + 1]))
        else:
            x = _outproj(x, ys, proj, w_out_b, i, gate[i], (final_norm_w[None],))
    return x[None]
```

```python
import functools
import math

import numpy as np
import jax
import jax.numpy as jnp
from jax import lax
from jax.experimental import pallas as pl
from jax.experimental.pallas import tpu as pltpu

HEAD_DIM = 64
GROUP_WIDTH = 512
GROUP_HEADS = 8
A_KV_HEADS = 2
A_RADIUS = 128
C_KV_HEADS = 2
ROPE_THETA = 10000.0
NA_ROWS = 8
NA_COLS = 16
D_CONFIGS = ((128, 1), (512, 4), (2048, 16))
GRID_W = 64
T5_BUCKETS = 32
T5_MAX_DIST = 1024
EPS = 1e-6
NEG = -1e30

LANES = 128
BF16_SUBLANES = 16
V_ROWS = HEAD_DIM + BF16_SUBLANES
VMEM_LIMIT = 56 * 1024 * 1024

F32 = jnp.float32
BF16 = jnp.bfloat16

_SRC = {}
_off = 0
for _name, _w in (("q_a", 512), ("k_a", 128), ("v_a", 128), ("z_a", 512),
                  ("q_b", 512), ("k_b", 512), ("v_b", 512), ("z_b", 512),
                  ("q_c", 512), ("k_c", 128), ("v_c", 128), ("z_c", 512),
                  ("q_d", 512), ("k_d", 512), ("v_d", 512), ("z_d", 512)):
    _SRC[_name] = (_off, _w)
    _off += _w
IN_WIDTH = _off


def _piece(name, t, row_block=lambda n: n):
    off, width = _SRC[name]
    return pl.BlockSpec((pl.Element(t), pl.Element(width)),
                        lambda n: (pl.multiple_of(row_block(n) * t, t), off))


def _ada_kernel(c_ref, w_ref, b_ref, o_ref):
    c = c_ref[...]
    cond = c * jax.nn.sigmoid(c)
    o_ref[0] = jnp.dot(cond, w_ref[0], preferred_element_type=F32,
                       precision=lax.Precision.HIGHEST) + b_ref[0]


def _ada_mod(c8, w_ada, b_ada):
    depth, d, n3 = w_ada.shape
    tn = 1024
    return pl.pallas_call(
        _ada_kernel,
        grid=(depth, n3 // tn),
        in_specs=[pl.BlockSpec((8, d), lambda i, j: (0, 0)),
                  pl.BlockSpec((1, d, tn), lambda i, j: (i, 0, j)),
                  pl.BlockSpec((1, 1, tn), lambda i, j: (i, 0, j))],
        out_specs=pl.BlockSpec((1, 8, tn), lambda i, j: (i, 0, j)),
        out_shape=jax.ShapeDtypeStruct((depth, 8, n3), F32),
        compiler_params=pltpu.CompilerParams(
            dimension_semantics=("arbitrary", "arbitrary"), vmem_limit_bytes=VMEM_LIMIT),
        name="ada_mod",
    )(c8, w_ada, b_ada.reshape(depth, 1, n3))


def _modulated_norm(x, nw, scale, shift):
    y = x * lax.rsqrt(jnp.mean(x * x, axis=-1, keepdims=True) + EPS)
    return ((y * nw) * (1.0 + scale) + shift).astype(BF16)


def _norm_kernel(x_ref, nw_ref, sc_ref, sh_ref, h_ref):
    h_ref[...] = _modulated_norm(x_ref[...], nw_ref[...], sc_ref[...], sh_ref[...])


def _norm(x, nw, scale, shift):
    L, d = x.shape
    tm = min(512, L)
    vec = pl.BlockSpec((1, d), lambda i: (0, 0))
    return pl.pallas_call(
        _norm_kernel,
        grid=(L // tm,),
        in_specs=[pl.BlockSpec((tm, d), lambda i: (i, 0)), vec, vec, vec],
        out_specs=pl.BlockSpec((tm, d), lambda i: (i, 0)),
        out_shape=jax.ShapeDtypeStruct((L, d), BF16),
        compiler_params=pltpu.CompilerParams(
            dimension_semantics=("arbitrary",), vmem_limit_bytes=VMEM_LIMIT),
        name="norm",
    )(x, nw, scale, shift)


def _inproj_kernel(h_ref, w_ref, o_ref):
    o_ref[...] = jnp.dot(h_ref[...], w_ref[...].astype(BF16),
                         preferred_element_type=F32).astype(o_ref.dtype)


def _inproj(h, w, layer):
    L, d = h.shape
    n = w.shape[2]
    tm = min(2048, L)
    tn = 512
    return pl.pallas_call(
        _inproj_kernel,
        grid=(L // tm, n // tn),
        in_specs=[pl.BlockSpec((tm, d), lambda i, j: (i, 0)),
                  pl.BlockSpec((None, d, tn), lambda i, j: (layer, 0, j))],
        out_specs=pl.BlockSpec((tm, tn), lambda i, j: (i, j)),
        out_shape=jax.ShapeDtypeStruct((L, n), BF16),
        compiler_params=pltpu.CompilerParams(
            dimension_semantics=("arbitrary", "arbitrary"), vmem_limit_bytes=VMEM_LIMIT),
        name="inproj",
    )(h, w)


LOG2E = math.log2(math.e)
Q_SCALE = HEAD_DIM ** -0.5 * LOG2E


def _qt_prep_kernel(*refs):
    n = len(refs) // 2
    for q_ref, qt_ref in zip(refs[:n], refs[n:]):
        qt_ref[...] = (q_ref[...].astype(F32) * Q_SCALE).T.astype(qt_ref.dtype)


def _qt_prep(proj, names, t=1024):
    L = proj.shape[0]
    return pl.pallas_call(
        _qt_prep_kernel,
        grid=(L // t,),
        in_specs=[_piece(name, t) for name in names],
        out_specs=[pl.BlockSpec((GROUP_WIDTH, t), lambda n: (0, n)) for _ in names],
        out_shape=[jax.ShapeDtypeStruct((GROUP_WIDTH, L), BF16) for _ in names],
        compiler_params=pltpu.CompilerParams(
            dimension_semantics=("arbitrary",), vmem_limit_bytes=VMEM_LIMIT),
        name="qt_prep",
    )(*([proj] * len(names)))


def _kv_prep_kernel(k_ref, v_ref, kp_ref, vt_ref, *, pad_blocks, tok_blocks):
    b = pl.program_id(0)
    is_token = (b >= pad_blocks) & (b < pad_blocks + tok_blocks)

    @pl.when(is_token)
    def _():
        kp_ref[...] = k_ref[...]
        vt = v_ref[...].astype(F32).T.astype(vt_ref.dtype)
        ones = jnp.ones((V_ROWS - HEAD_DIM, vt.shape[1]), vt_ref.dtype)
        for g in range(vt.shape[0] // HEAD_DIM):
            vt_ref[g * V_ROWS:(g + 1) * V_ROWS, :] = jnp.concatenate(
                [vt[g * HEAD_DIM:(g + 1) * HEAD_DIM], ones], axis=0)

    @pl.when(jnp.logical_not(is_token))
    def _():
        kp_ref[...] = jnp.zeros(kp_ref.shape, kp_ref.dtype)
        vt_ref[...] = jnp.zeros(vt_ref.shape, vt_ref.dtype)


def _kv_prep(proj, k_name, v_name, width, pad, t):
    L = proj.shape[0]
    assert pad % t == 0 and L % t == 0 and width % LANES == 0
    pad_blocks, tok_blocks = pad // t, L // t
    lp = L + 2 * pad
    assert _SRC[k_name][1] == width and _SRC[v_name][1] == width
    tok = lambda b: jnp.clip(b - pad_blocks, 0, tok_blocks - 1)
    return pl.pallas_call(
        functools.partial(_kv_prep_kernel, pad_blocks=pad_blocks, tok_blocks=tok_blocks),
        grid=(lp // t,),
        in_specs=[_piece(k_name, t, tok), _piece(v_name, t, tok)],
        out_specs=[pl.BlockSpec((t, width), lambda b: (b, 0)),
                   pl.BlockSpec((width // HEAD_DIM * V_ROWS, t), lambda b: (0, b))],
        out_shape=[jax.ShapeDtypeStruct((lp, width), BF16),
                   jax.ShapeDtypeStruct((width // HEAD_DIM * V_ROWS, lp), BF16)],
        compiler_params=pltpu.CompilerParams(
            dimension_semantics=("arbitrary",), vmem_limit_bytes=VMEM_LIMIT),
        name="kv_prep",
    )(proj, proj)


def _band_kernel(*refs, w_keys, chunk, lanes, heads, group, use_sink, window, variant):
    if use_sink:
        sink_ref, qt_ref, k_ref, vt_ref, tab_ref, o_ref = refs
    else:
        qt_ref, k_ref, vt_ref, tab_ref, o_ref = refs
    n = pl.program_id(0)
    _, pos = window(n)
    kwin = k_ref[...]
    vwin = vt_ref[...]
    var = variant(n)

    n_chunks = w_keys // chunk

    def attend(kbias):
        def scores(h, c):
            kv = h // group
            rows = slice(c * chunk, (c + 1) * chunk)
            st = jnp.dot(kwin[rows, kv * HEAD_DIM:(kv + 1) * HEAD_DIM],
                         qt_ref[h * HEAD_DIM:(h + 1) * HEAD_DIM, :],
                         preferred_element_type=F32) + tab_ref[var, h, rows, :]
            return st if kbias is None else st + kbias[rows]

        def col_max(m, st):
            cm = jnp.max(st, axis=0, keepdims=True)
            return cm if m is None else jnp.maximum(m, cm)

        def finish(h, acc, m):
            o, l = acc[:HEAD_DIM], acc[HEAD_DIM:HEAD_DIM + 1]
            if use_sink:
                sk = sink_ref[h] * LOG2E
                m2 = jnp.maximum(m, sk)
                a = jnp.exp2(m - m2)
                return o * (a / (l * a + jnp.exp2(sk - m2)))
            return o / l

        outs = []
        groups = [range(g, g + lanes) for g in range(0, heads, lanes)]
        cur = {h: [] for h in groups[0]}
        m = {h: None for h in groups[0]}
        for c in range(n_chunks):
            for h in groups[0]:
                cur[h].append(scores(h, c))
                m[h] = col_max(m[h], cur[h][-1])
        for gi, grp in enumerate(groups):
            following = groups[gi + 1] if gi + 1 < len(groups) else ()
            nxt = {h: [] for h in following}
            m_next = {h: None for h in following}
            acc = {h: None for h in grp}
            for c in range(n_chunks):
                for h in following:
                    nxt[h].append(scores(h, c))
                    m_next[h] = col_max(m_next[h], nxt[h][-1])
                for h in grp:
                    kv = h // group
                    pt = jnp.exp2(cur[h][c] - m[h]).astype(BF16)
                    pv = jnp.dot(vwin[kv * V_ROWS:(kv + 1) * V_ROWS, c * chunk:(c + 1) * chunk], pt,
                                 preferred_element_type=F32)
                    acc[h] = pv if acc[h] is None else acc[h] + pv
            outs.extend(finish(h, acc[h], m[h]) for h in grp)
            cur, m = nxt, m_next
        o_ref[...] = jnp.concatenate(outs, axis=0).T.astype(o_ref.dtype)

    if pos is None:
        attend(None)
    else:
        first, lo, hi = pos
        inside = (first >= lo) & (first + w_keys <= hi)

        @pl.when(inside)
        def _():
            attend(None)

        @pl.when(jnp.logical_not(inside))
        def _():
            kpos = first + lax.broadcasted_iota(jnp.int32, (w_keys, 1), 0)
            attend(jnp.where((kpos >= lo) & (kpos < hi), 0.0, NEG).astype(F32))


def _band_attention(qt, k_pad, vt_pad, table, *, chunk, lanes, group, window, variant=lambda n: 0,
                    sink=None, layer=None):
    qw, L = qt.shape
    heads = qw // HEAD_DIM
    w_keys, tq = table.shape[-2:]
    assert w_keys % chunk == 0
    kern = functools.partial(_band_kernel, w_keys=w_keys, chunk=chunk, lanes=lanes, heads=heads, group=group,
                             use_sink=sink is not None, window=window, variant=variant)
    start = lambda n: pl.multiple_of(window(n)[0], LANES)
    if layer is None:
        table_spec = pl.BlockSpec(table.shape, lambda n: (0, 0, 0, 0), pipeline_mode=pl.Buffered(1))
    else:
        table_spec = pl.BlockSpec((None,) + table.shape[1:], lambda n: (layer, 0, 0, 0, 0),
                                  pipeline_mode=pl.Buffered(1))
    in_specs = [pl.BlockSpec((qw, tq), lambda n: (0, n)),
                pl.BlockSpec((pl.Element(w_keys), pl.Element(k_pad.shape[1])), lambda n: (start(n), 0)),
                pl.BlockSpec((pl.Element(vt_pad.shape[0]), pl.Element(w_keys)), lambda n: (0, start(n))),
                table_spec]
    args = [qt, k_pad, vt_pad, table]
    if sink is not None:
        in_specs = [pl.BlockSpec(memory_space=pltpu.SMEM)] + in_specs
        args = [sink] + args
    return pl.pallas_call(
        kern,
        grid=(L // tq,),
        in_specs=in_specs,
        out_specs=pl.BlockSpec((tq, qw), lambda n: (n, 0)),
        out_shape=jax.ShapeDtypeStruct((L, qw), BF16),
        compiler_params=pltpu.CompilerParams(
            dimension_semantics=("arbitrary",), vmem_limit_bytes=VMEM_LIMIT),
        name="band_attention",
    )(*args)


def _t5_bucket(rel):
    half = T5_BUCKETS // 2
    exact = half // 2
    n = jnp.abs(rel)
    big = exact + (jnp.log(jnp.maximum(n, exact).astype(F32) / exact)
                   / math.log(T5_MAX_DIST / exact) * (half - exact)).astype(jnp.int32)
    big = jnp.minimum(big, half - 1)
    return jnp.where(rel > 0, half, 0) + jnp.where(n < exact, n, big)


def _toeplitz_kernel(rv_ref, o_ref, *, n_diag):
    w, tq = o_ref.shape[1:]
    blocks = []
    for d in range(n_diag):
        s = LANES * (n_diag - 1 - d)
        x = jnp.broadcast_to(rv_ref[0, :, s:s + 2 * LANES], (LANES, 2 * LANES))
        blocks.append(pltpu.roll(x, LANES, 1, stride=1, stride_axis=0)[:, :LANES])
    for cb in range(w // LANES):
        for ib in range(tq // LANES):
            o_ref[0, cb * LANES:(cb + 1) * LANES, ib * LANES:(ib + 1) * LANES] = (
                blocks[cb - ib + tq // LANES - 1])


def _toeplitz(v, tq, w):
    heads, n_rel = v.shape
    n_diag = w // LANES + tq // LANES - 1
    assert w % LANES == 0 and tq % LANES == 0 and n_rel == LANES * n_diag + LANES - 1
    rv = jnp.pad(v, ((0, 0), (0, 1)))[:, None, ::-1]
    return pl.pallas_call(
        functools.partial(_toeplitz_kernel, n_diag=n_diag),
        grid=(heads,),
        in_specs=[pl.BlockSpec((1, 1, rv.shape[-1]), lambda h: (h, 0, 0))],
        out_specs=pl.BlockSpec((1, w, tq), lambda h: (h, 0, 0)),
        out_shape=jax.ShapeDtypeStruct((heads, w, tq), F32),
        compiler_params=pltpu.CompilerParams(
            dimension_semantics=("arbitrary",), vmem_limit_bytes=VMEM_LIMIT),
        name="toeplitz",
    )(rv)


def _t5_table(t5_table, head_lo, tq, r_lo, w_keys, mult):
    rel = np.arange(-r_lo - (tq - 1), w_keys - r_lo)
    m = mult(rel)
    b = t5_table.astype(F32)[:, head_lo:head_lo + GROUP_HEADS][_t5_bucket(jnp.asarray(rel, jnp.int32))]
    logm = np.log(np.maximum(m, 1)).astype(np.float32)
    vec = jnp.where(jnp.asarray(m > 0)[:, None], (b + logm[:, None]) * LOG2E, NEG).T
    return _toeplitz(vec, tq, w_keys)[None]


def _window_mult(rel):
    return (np.abs(rel) <= A_RADIUS).astype(np.int32)


def _dilated_mult(rel):
    m = np.zeros(rel.shape, np.int32)
    for window, dil in D_CONFIGS:
        m += ((rel % dil == 0) & (np.abs(rel) <= window // 2)).astype(np.int32)
    return m


NA_TILE_ROWS = 4
NA_WIN_ROWS = NA_TILE_ROWS + NA_ROWS
NA_TQ = NA_TILE_ROWS * GRID_W
NA_W = NA_WIN_ROWS * GRID_W


def _na_row_index(variant, j, i):
    dr, ok = ((j - i, j < NA_ROWS),
              (j - NA_ROWS // 2 - i, i <= j < i + NA_ROWS),
              (j - NA_ROWS - i, j >= NA_TILE_ROWS))[variant]
    return dr + NA_ROWS - 1 if ok else None


def _na_table_kernel(rp_ref, o_ref):
    kc = lax.broadcasted_iota(jnp.int32, (GRID_W, LANES), 0)
    lane = lax.broadcasted_iota(jnp.int32, (GRID_W, LANES), 1)
    col_start = jnp.clip(lane % GRID_W - NA_COLS // 2, 0, GRID_W - NA_COLS)
    col_ok = (kc >= col_start) & (kc < col_start + NA_COLS)
    neg = jnp.full((GRID_W, LANES), NEG, F32)
    cache = {}

    def half_block(d, side):
        if d is None:
            return neg
        if (d, side) not in cache:
            x = jnp.broadcast_to(rp_ref[0, 0, d:d + 1, :], (GRID_W, LANES))
            y = pltpu.roll(x, GRID_W * (1 - side), 1, stride=1, stride_axis=0)
            cache[d, side] = jnp.where(col_ok, y, NEG)
        return cache[d, side]

    for variant in range(3):
        for j in range(NA_WIN_ROWS):
            for ip in range(NA_TILE_ROWS // 2):
                left = half_block(_na_row_index(variant, j, 2 * ip), 0)
                right = half_block(_na_row_index(variant, j, 2 * ip + 1), 1)
                o_ref[0, variant, 0, j * GRID_W:(j + 1) * GRID_W, ip * LANES:(ip + 1) * LANES] = (
                    jnp.where(lane < GRID_W, left, right))


def _na_tables(rpb):
    depth, heads, n_dr, n_dc = rpb.shape
    assert n_dr == NA_WIN_ROWS + NA_TILE_ROWS - 1 and 2 * GRID_W == LANES
    front = GRID_W - NA_COLS
    rp = jnp.pad(rpb.astype(F32) * LOG2E, ((0, 0), (0, 0), (0, 1), (front, LANES - n_dc - front)))
    rp = rp[..., ::-1]
    return pl.pallas_call(
        _na_table_kernel,
        grid=(depth, heads),
        in_specs=[pl.BlockSpec((1, 1, n_dr + 1, LANES), lambda l, h: (l, h, 0, 0))],
        out_specs=pl.BlockSpec((1, 3, 1, NA_W, NA_TQ), lambda l, h: (l, 0, h, 0, 0)),
        out_shape=jax.ShapeDtypeStruct((depth, 3, heads, NA_W, NA_TQ), F32),
        compiler_params=pltpu.CompilerParams(
            dimension_semantics=("arbitrary", "arbitrary"), vmem_limit_bytes=VMEM_LIMIT),
        name="na_tables",
    )(rp)


def _swap_pairs(x):
    n = x.shape[-1]
    lane = lax.broadcasted_iota(jnp.int32, x.shape, x.ndim - 1)
    return jnp.where(lane % 2 == 0, pltpu.roll(x, n - 1, x.ndim - 1), pltpu.roll(x, 1, x.ndim - 1))


def _head_rms(x, ones_bd, w):
    sq = x * x
    hi = sq.astype(BF16)
    lo = (sq - hi.astype(F32)).astype(BF16)
    ms = (jnp.dot(hi, ones_bd, preferred_element_type=F32)
          + jnp.dot(lo, ones_bd, preferred_element_type=F32)) * (1.0 / HEAD_DIM)
    return x * lax.rsqrt(ms + EPS) * w


def _cprep_kernel(q_ref, k_ref, v_ref, cos_ref, sin_ref, qw_ref, kw_ref, bd_ref,
                  qt_ref, k2_ref, vt_ref):
    cos = cos_ref[...]
    sin = sin_ref[...]
    q = _head_rms(q_ref[...].astype(F32), bd_ref[...], qw_ref[...])
    reps = q.shape[1] // LANES
    q = q * jnp.concatenate([cos] * reps, axis=1) + _swap_pairs(q) * jnp.concatenate([sin] * reps, axis=1)
    qt_ref[...] = (q * Q_SCALE).T.astype(qt_ref.dtype)
    k = _head_rms(k_ref[...].astype(F32), bd_ref[:LANES, :LANES], kw_ref[...])
    k = (k * cos + _swap_pairs(k) * sin).astype(k2_ref.dtype)
    vt = v_ref[...].astype(F32).T.astype(vt_ref.dtype)
    ones = jnp.ones((V_ROWS - HEAD_DIM, vt.shape[1]), vt_ref.dtype)
    tk = vt_ref.shape[-1]
    for g in range(C_KV_HEADS):
        k2_ref[g] = k[:, g * HEAD_DIM:(g + 1) * HEAD_DIM]
        vg = jnp.concatenate([vt[g * HEAD_DIM:(g + 1) * HEAD_DIM], ones], axis=0)
        for u in range(vt_ref.shape[1]):
            vt_ref[g, u] = vg[:, u * tk:(u + 1) * tk]


def _cprep(proj, cos2, sin2, qw, kw, ones_bd, tk, t=1024):
    L = proj.shape[0]
    vec = lambda width: pl.BlockSpec((1, width), lambda n: (0, 0))
    return pl.pallas_call(
        _cprep_kernel,
        grid=(L // t,),
        in_specs=[_piece("q_c", t), _piece("k_c", t), _piece("v_c", t),
                  pl.BlockSpec((t, LANES), lambda n: (n, 0)),
                  pl.BlockSpec((t, LANES), lambda n: (n, 0)),
                  vec(GROUP_WIDTH), vec(LANES),
                  pl.BlockSpec((GROUP_WIDTH, GROUP_WIDTH), lambda n: (0, 0))],
        out_specs=[pl.BlockSpec((GROUP_WIDTH, t), lambda n: (0, n)),
                   pl.BlockSpec((C_KV_HEADS, t, HEAD_DIM), lambda n: (0, n, 0)),
                   pl.BlockSpec((C_KV_HEADS, t // tk, V_ROWS, tk), lambda n: (0, n, 0, 0))],
        out_shape=[jax.ShapeDtypeStruct((GROUP_WIDTH, L), BF16),
                   jax.ShapeDtypeStruct((C_KV_HEADS, L, HEAD_DIM), BF16),
                   jax.ShapeDtypeStruct((C_KV_HEADS, L // tk, V_ROWS, tk), BF16)],
        compiler_params=pltpu.CompilerParams(
            dimension_semantics=("arbitrary",), vmem_limit_bytes=VMEM_LIMIT),
        name="dense_prep",
    )(proj, proj, proj, cos2, sin2, qw, kw, ones_bd)


def _flash_kernel(qt_ref, k_ref, vt_ref, o_ref, m_scr, acc_scr, st_scr, *, tk, heads, unroll, ahead):
    n_kv = k_ref.shape[1] // tk
    m_scr[...] = jnp.full(m_scr.shape, NEG, F32)
    acc_scr[...] = jnp.zeros(acc_scr.shape, F32)

    def scores(j, h):
        k = k_ref[0, pl.ds(pl.multiple_of(j * tk, tk), tk), :]
        qt = qt_ref[h * HEAD_DIM:(h + 1) * HEAD_DIM, :]
        return jnp.dot(k, qt, preferred_element_type=F32)

    n_items = unroll * heads

    def item_scores(t, idx):
        j = t * unroll + idx // heads
        return scores(jnp.minimum(j, n_kv - 1), idx % heads)

    for a in range(ahead):
        st_scr[a] = item_scores(0, a)

    def body(t, carry):
        pending = [st_scr[a] for a in range(ahead)]
        for idx in range(n_items):
            pending.append(item_scores(t, idx + ahead))
            st = pending.pop(0)
            h = idx % heads
            vt = vt_ref[0, t * unroll + idx // heads]
            for q0 in range(0, st.shape[1], C_SPLIT):
                cols = slice(q0, q0 + C_SPLIT)
                m = m_scr[h, :, cols]
                m_new = jnp.maximum(m, jnp.max(st[:, cols], axis=0, keepdims=True))
                pt = jnp.exp2(st[:, cols] - m_new).astype(BF16)
                acc_scr[h, :, cols] = (jnp.exp2(m - m_new) * acc_scr[h, :, cols]
                                       + jnp.dot(vt, pt, preferred_element_type=F32))
                m_scr[h, :, cols] = m_new
        for a in range(ahead):
            st_scr[a] = pending[a]
        return carry

    lax.fori_loop(0, n_kv // unroll, body, 0)
    outs = [acc_scr[h, :HEAD_DIM] / acc_scr[h, HEAD_DIM:HEAD_DIM + 1] for h in range(heads)]
    o_ref[...] = jnp.concatenate(outs, axis=0).T.astype(o_ref.dtype)


def _flash(qt, k2, vt, tq):
    L = qt.shape[1]
    _, n_kv, v_rows, tk = vt.shape
    heads = GROUP_HEADS // C_KV_HEADS
    qrows = heads * HEAD_DIM
    return pl.pallas_call(
        functools.partial(_flash_kernel, tk=tk, heads=heads, unroll=math.gcd(C_UNROLL, n_kv),
                          ahead=C_AHEAD),
        grid=(C_KV_HEADS, L // tq),
        in_specs=[pl.BlockSpec((qrows, tq), lambda g, n: (g, n)),
                  pl.BlockSpec((1, L, HEAD_DIM), lambda g, n: (g, 0, 0)),
                  pl.BlockSpec((1, n_kv, v_rows, tk), lambda g, n: (g, 0, 0, 0))],
        out_specs=pl.BlockSpec((tq, qrows), lambda g, n: (n, g)),
        out_shape=jax.ShapeDtypeStruct((L, GROUP_WIDTH), BF16),
        scratch_shapes=[pltpu.VMEM((heads, 1, tq), F32), pltpu.VMEM((heads, v_rows, tq), F32),
                        pltpu.VMEM((C_AHEAD, tk, tq), F32)],
        compiler_params=pltpu.CompilerParams(
            dimension_semantics=("arbitrary", "arbitrary"), vmem_limit_bytes=VMEM_LIMIT),
        name="dense_flash",
    )(qt, k2, vt)


def _outproj_kernel(*refs, final):
    if final:
        x_ref, ya, yb, yc, yd, za, zb, zc, zd, w_ref, g_ref, fw_ref, o_ref = refs
    else:
        x_ref, ya, yb, yc, yd, za, zb, zc, zd, w_ref, g_ref, nw_ref, sc_ref, sh_ref, o_ref, h_ref = refs
    sub = x_ref.shape[0] // OUT_SPLIT
    for r0 in range(0, x_ref.shape[0], sub):
        rows = slice(r0, r0 + sub)
        acc = jnp.zeros((sub, x_ref.shape[1]), F32)
        for gi, (y_ref, z_ref) in enumerate(zip((ya, yb, yc, yd), (za, zb, zc, zd))):
            z = z_ref[rows, :].astype(F32)
            u = (y_ref[rows, :].astype(F32) * (z * jax.nn.sigmoid(z))).astype(BF16)
            acc = acc + jnp.dot(u, w_ref[gi * GROUP_WIDTH:(gi + 1) * GROUP_WIDTH, :],
                                preferred_element_type=F32)
        xn = x_ref[rows, :] + g_ref[...] * acc
        if final:
            o_ref[rows, :] = (xn * lax.rsqrt(jnp.mean(xn * xn, axis=-1, keepdims=True) + EPS)
                              * fw_ref[...])
        else:
            o_ref[rows, :] = xn
            h_ref[rows, :] = _modulated_norm(xn, nw_ref[...], sc_ref[...], sh_ref[...])


def _outproj(x, ys, proj, w_bf16, layer, gate, vecs):
    L, d = x.shape
    tm = min(512, L)
    final = len(vecs) == 1
    row = lambda width: pl.BlockSpec((tm, width), lambda i: (i, 0))
    vec = pl.BlockSpec((1, d), lambda i: (0, 0))
    in_specs = ([row(d)] + [row(GROUP_WIDTH)] * 4 + [_piece(z, tm) for z in ("z_a", "z_b", "z_c", "z_d")]
                + [pl.BlockSpec((None,) + w_bf16.shape[1:], lambda i: (layer, 0, 0)), vec]
                + [vec] * len(vecs))
    x_out = jax.ShapeDtypeStruct((L, d), F32)
    return pl.pallas_call(
        functools.partial(_outproj_kernel, final=final),
        grid=(L // tm,),
        in_specs=in_specs,
        out_specs=row(d) if final else [row(d), row(d)],
        out_shape=x_out if final else [x_out, jax.ShapeDtypeStruct((L, d), BF16)],
        compiler_params=pltpu.CompilerParams(
            dimension_semantics=("arbitrary",), vmem_limit_bytes=VMEM_LIMIT),
        name="outproj",
    )(x, *ys, proj, proj, proj, proj, w_bf16, gate, *vecs)


def _rope_tables(L):
    t = np.arange(L)
    axis_dim = HEAD_DIM // 2
    inv = jnp.asarray(ROPE_THETA, F32) ** (-jnp.arange(0, axis_dim, 2, dtype=F32) / axis_dim)
    row = jnp.asarray(t // GRID_W, F32)
    col = jnp.asarray(t % GRID_W, F32)
    ang = jnp.concatenate([row[:, None] * inv[None], col[:, None] * inv[None]], axis=-1)
    cos = jnp.repeat(jnp.cos(ang), 2, axis=-1)
    sin = jnp.repeat(jnp.sin(ang), 2, axis=-1) * jnp.asarray(np.tile([-1.0, 1.0], HEAD_DIM // 2), F32)
    return jnp.tile(cos, (1, LANES // HEAD_DIM)), jnp.tile(sin, (1, LANES // HEAD_DIM))


A_TQ = 256
A_CHUNK = 256
NA_CHUNK = 256
D_CHUNK = 768
BAND_LANES = 2
NA_LANES = 4
D_TQ = 256
D_REACH = D_CONFIGS[-1][0] // 2
C_TQ = 512
C_TK = 256
C_UNROLL = 16
C_AHEAD = 2
C_SPLIT = 256
OUT_SPLIT = 2


def _table_a(t5_table):
    return _t5_table(t5_table, 0, A_TQ, A_RADIUS, A_TQ + 2 * A_RADIUS, _window_mult)


def _table_d(t5_table):
    return _t5_table(t5_table, GROUP_HEADS, D_TQ, D_REACH, D_TQ + 2 * D_REACH, _dilated_mult)


D_DIL = D_CONFIGS[-1][1]
D_NEAR = D_CONFIGS[-2][0] // 2
D_FAR_ROWS = (D_REACH - D_NEAR) // D_DIL


def _dil_kernel(qt_ref, kn_ref, k16_ref, vt_ref, tabn_ref, tabf_ref, o_ref, *, tq, L, lanes):
    n = pl.program_id(0)
    t0 = n * tq
    heads = qt_ref.shape[0] // HEAD_DIM
    w_near = kn_ref.shape[0]
    far = D_FAR_ROWS
    n_chunks = w_near // tq
    w_far_keys = far * D_DIL
    lane = lax.broadcasted_iota(jnp.int32, (1, tq), 1)
    qmask = (lax.broadcasted_iota(jnp.int32, (D_DIL * HEAD_DIM, tq), 0) // HEAD_DIM) == (lane % D_DIL)
    pmask = (lax.broadcasted_iota(jnp.int32, (w_far_keys, tq), 0) % D_DIL) == (lane % D_DIL)

    def attend(kb_near, kb_far):
        def scores(h):
            cols = slice(h * HEAD_DIM, (h + 1) * HEAD_DIM)
            q = qt_ref[cols, :]
            near = []
            for c in range(n_chunks):
                rows = slice(c * tq, (c + 1) * tq)
                st = jnp.dot(kn_ref[rows, cols], q, preferred_element_type=F32) + tabn_ref[h, rows, :]
                near.append(st if kb_near is None else st + kb_near[rows])
            qbd = jnp.where(qmask, jnp.concatenate([q] * D_DIL, axis=0), jnp.zeros((), BF16))
            sides = []
            for r0 in (0, k16_ref.shape[0] - far):
                kcat = jnp.concatenate(
                    [k16_ref[r0:r0 + far, r * GROUP_WIDTH + h * HEAD_DIM:r * GROUP_WIDTH + (h + 1) * HEAD_DIM]
                     for r in range(D_DIL)], axis=1)
                sides.append(jnp.dot(kcat, qbd, preferred_element_type=F32))
            sf = jnp.concatenate(sides, axis=0) + tabf_ref[h]
            if kb_far is not None:
                sf = sf + kb_far
            m = jnp.max(sf, axis=0, keepdims=True)
            for st in near:
                m = jnp.maximum(m, jnp.max(st, axis=0, keepdims=True))
            return near, sf, m

        def values(h, near, sf, m):
            v = vt_ref[h * V_ROWS:(h + 1) * V_ROWS, :]
            acc = None
            for c, st in enumerate(near):
                pt = jnp.exp2(st - m).astype(BF16)
                lo = w_far_keys + c * tq
                pv = jnp.dot(v[:, lo:lo + tq], pt, preferred_element_type=F32)
                acc = pv if acc is None else acc + pv
            pf = jnp.exp2(sf - m)
            for side, lo in ((0, 0), (1, w_far_keys + w_near)):
                rep = jnp.concatenate(
                    [jnp.broadcast_to(pf[side * far + a:side * far + a + 1, :], (D_DIL, tq))
                     for a in range(far)], axis=0)
                pbd = jnp.where(pmask, rep, 0.0).astype(BF16)
                acc = acc + jnp.dot(v[:, lo:lo + w_far_keys], pbd, preferred_element_type=F32)
            return acc[:HEAD_DIM] / acc[HEAD_DIM:HEAD_DIM + 1]

        outs = []
        groups = [range(g, g + lanes) for g in range(0, heads, lanes)]
        cur = {h: scores(h) for h in groups[0]}
        for gi, grp in enumerate(groups):
            nxt = {h: scores(h) for h in groups[gi + 1]} if gi + 1 < len(groups) else {}
            outs.extend(values(h, *cur[h]) for h in grp)
            cur = nxt
        o_ref[...] = jnp.concatenate(outs, axis=0).T.astype(o_ref.dtype)

    inside = (t0 - D_REACH >= 0) & (t0 + tq + D_REACH <= L)

    @pl.when(inside)
    def _():
        attend(None, None)

    @pl.when(jnp.logical_not(inside))
    def _():
        kpos = t0 - D_NEAR + lax.broadcasted_iota(jnp.int32, (w_near, 1), 0)
        kb_near = jnp.where((kpos >= 0) & (kpos < L), 0.0, NEG).astype(F32)
        a = lax.broadcasted_iota(jnp.int32, (2 * far, 1), 0)
        base = jnp.where(a < far, t0 - D_REACH + D_DIL * a, t0 + tq + D_NEAR + D_DIL * (a - far))
        kb_far = jnp.where((base >= 0) & (base < L), 0.0, NEG).astype(F32)
        attend(kb_near, kb_far)


def _far_table(t5_table, tq):
    a = np.arange(D_FAR_ROWS)[:, None]
    b = np.arange(tq // D_DIL)[None, :]
    rel = np.concatenate([-D_REACH + D_DIL * (a - b), tq + D_NEAR + D_DIL * (a - b)], axis=0)
    valid = np.abs(rel) <= D_REACH
    bias = t5_table.astype(F32)[:, GROUP_HEADS:2 * GROUP_HEADS][_t5_bucket(jnp.asarray(rel, jnp.int32))]
    tab = jnp.where(jnp.asarray(valid)[..., None], bias * LOG2E, NEG)
    return jnp.repeat(jnp.transpose(tab, (2, 0, 1)), D_DIL, axis=-1)


def _near_table_d(t5_table, tq):
    return _t5_table(t5_table, GROUP_HEADS, tq, D_NEAR, tq + 2 * D_NEAR, _dilated_mult)[0]


def _dilated_attention(qt, k_pad, vt_pad, tab_near, tab_far, *, tq, pad, lanes):
    qw, L = qt.shape
    lp = k_pad.shape[0]
    w_near = tq + 2 * D_NEAR
    w_all = tq + 2 * D_REACH
    k16 = k_pad.reshape(lp // D_DIL, D_DIL * k_pad.shape[1])
    return pl.pallas_call(
        functools.partial(_dil_kernel, tq=tq, L=L, lanes=lanes),
        grid=(L // tq,),
        in_specs=[pl.BlockSpec((qw, tq), lambda n: (0, n)),
                  pl.BlockSpec((pl.Element(w_near), pl.Element(k_pad.shape[1])),
                               lambda n: (pl.multiple_of(n * tq + pad - D_NEAR, LANES), 0)),
                  pl.BlockSpec((pl.Element(w_all // D_DIL), pl.Element(k16.shape[1])),
                               lambda n: (pl.multiple_of((n * tq + pad - D_REACH) // D_DIL, BF16_SUBLANES), 0)),
                  pl.BlockSpec((pl.Element(vt_pad.shape[0]), pl.Element(w_all)),
                               lambda n: (0, pl.multiple_of(n * tq + pad - D_REACH, LANES))),
                  pl.BlockSpec(tab_near.shape, lambda n: (0, 0, 0), pipeline_mode=pl.Buffered(1)),
                  pl.BlockSpec(tab_far.shape, lambda n: (0, 0, 0), pipeline_mode=pl.Buffered(1))],
        out_specs=pl.BlockSpec((tq, qw), lambda n: (n, 0)),
        out_shape=jax.ShapeDtypeStruct((L, qw), BF16),
        compiler_params=pltpu.CompilerParams(
            dimension_semantics=("arbitrary",), vmem_limit_bytes=VMEM_LIMIT),
        name="dilated_attention",
    )(qt, k_pad, k16, vt_pad, tab_near, tab_far)


PREP_T = 1024
assert PREP_T >= D_REACH


def _seq_window(L, tq, r_lo):
    return lambda n: (n * tq + PREP_T - r_lo, (n * tq - r_lo, 0, L))


def _mixer_a(proj, qt, tab_a, sink):
    L = proj.shape[0]
    k_pad, vt_pad = _kv_prep(proj, "k_a", "v_a", LANES, PREP_T, PREP_T)
    return _band_attention(qt, k_pad, vt_pad, tab_a, chunk=A_CHUNK, lanes=BAND_LANES,
                           group=GROUP_HEADS // A_KV_HEADS, window=_seq_window(L, A_TQ, A_RADIUS),
                           sink=sink.astype(F32))


def _mixer_b(proj, qt, tabs_b, layer):
    L = proj.shape[0]
    rows = L // GRID_W
    n_tiles = rows // NA_TILE_ROWS
    k, vt = _kv_prep(proj, "k_b", "v_b", GROUP_WIDTH, 0, PREP_T)
    window = lambda n: (jnp.clip(n * NA_TILE_ROWS - NA_ROWS // 2, 0, rows - NA_WIN_ROWS) * GRID_W, None)
    variant = lambda n: jnp.where(n == 0, 0, jnp.where(n == n_tiles - 1, 2, 1))
    return _band_attention(qt, k, vt, tabs_b, chunk=NA_CHUNK, lanes=NA_LANES, group=1,
                           window=window, variant=variant, layer=layer)


def _mixer_c(proj, q_norm_w, k_norm_w):
    L = proj.shape[0]
    cos2, sin2 = _rope_tables(L)
    ones_bd = jnp.asarray(np.kron(np.eye(GROUP_HEADS), np.ones((HEAD_DIM, HEAD_DIM))), BF16)
    qt, k2, vt = _cprep(proj, cos2, sin2, jnp.tile(q_norm_w.astype(F32), GROUP_HEADS)[None],
                        jnp.tile(k_norm_w.astype(F32), LANES // HEAD_DIM)[None], ones_bd, C_TK)
    return _flash(qt, k2, vt, C_TQ)


def _mixer_d(proj, qt, tabs_d):
    k_pad, vt_pad = _kv_prep(proj, "k_d", "v_d", GROUP_WIDTH, PREP_T, PREP_T)
    return _dilated_attention(qt, k_pad, vt_pad, *tabs_d, tq=D_TQ, pad=PREP_T, lanes=BAND_LANES)


def kernel(x, c, w_ada, b_ada, norm_w, w_in, w_out, attn_sink, na_rpb, q_norm_w, k_norm_w,
           t5_table, final_norm_w):
    B, L, D = x.shape
    assert B == 1 and L % 1024 == 0 and L // GRID_W >= NA_WIN_ROWS
    depth = w_ada.shape[0]
    x = x[0]

    mod = _ada_mod(jnp.broadcast_to(c, (8, D)), w_ada, b_ada)[:, 0:1, :]
    tab_a = _table_a(t5_table)
    tabs_d = (_near_table_d(t5_table, D_TQ), _far_table(t5_table, D_TQ))
    tabs_b = _na_tables(na_rpb)
    w_out_b = w_out.astype(BF16)
    shift, scale, gate = jnp.split(mod, 3, axis=-1)
    h = _norm(x, norm_w[0][None], scale[0], shift[0])
    for i in range(depth):
        proj = _inproj(h, w_in, i)
        qt_a, qt_b, qt_d = _qt_prep(proj, ("q_a", "q_b", "q_d"))
        ys = (_mixer_a(proj, qt_a, tab_a, attn_sink[i]),
              _mixer_b(proj, qt_b, tabs_b, i),
              _mixer_c(proj, q_norm_w[i], k_norm_w[i]),
              _mixer_d(proj, qt_d, tabs_d))
        if i + 1 < depth:
            x, h = _outproj(x, ys, proj, w_out_b, i, gate[i],
                            (norm_w[i + 1][None], scale[i + 1], shift[i + 1]))
        else:
            x = _outproj(x, ys, proj, w_out_b, i, gate[i], (final_norm_w[None],))
    return x[None]
```

```python
import functools
import math

import numpy as np
import jax
import jax.numpy as jnp
from jax import lax
from jax.experimental import pallas as pl
from jax.experimental.pallas import tpu as pltpu

HEAD_DIM = 64
GROUP_WIDTH = 512
GROUP_HEADS = 8
A_KV_HEADS = 2
A_RADIUS = 128
C_KV_HEADS = 2
ROPE_THETA = 10000.0
NA_ROWS = 8
NA_COLS = 16
D_CONFIGS = ((128, 1), (512, 4), (2048, 16))
GRID_W = 64
T5_BUCKETS = 32
T5_MAX_DIST = 1024
EPS = 1e-6
NEG = -1e30

LANES = 128
BF16_SUBLANES = 16
V_ROWS = HEAD_DIM + BF16_SUBLANES
VMEM_LIMIT = 56 * 1024 * 1024

F32 = jnp.float32
BF16 = jnp.bfloat16

_SRC = {}
_off = 0
for _name, _w in (("q_a", 512), ("k_a", 128), ("v_a", 128), ("z_a", 512),
                  ("q_b", 512), ("k_b", 512), ("v_b", 512), ("z_b", 512),
                  ("q_c", 512), ("k_c", 128), ("v_c", 128), ("z_c", 512),
                  ("q_d", 512), ("k_d", 512), ("v_d", 512), ("z_d", 512)):
    _SRC[_name] = (_off, _w)
    _off += _w
IN_WIDTH = _off


def _piece(name, t, row_block=lambda n: n):
    off, width = _SRC[name]
    return pl.BlockSpec((pl.Element(t), pl.Element(width)),
                        lambda n: (pl.multiple_of(row_block(n) * t, t), off))


def _ada_kernel(c_ref, w_ref, b_ref, o_ref):
    c = c_ref[...]
    cond = c * jax.nn.sigmoid(c)
    o_ref[0] = jnp.dot(cond, w_ref[0], preferred_element_type=F32,
                       precision=lax.Precision.HIGHEST) + b_ref[0]


def _ada_mod(c8, w_ada, b_ada):
    depth, d, n3 = w_ada.shape
    tn = 1024
    return pl.pallas_call(
        _ada_kernel,
        grid=(depth, n3 // tn),
        in_specs=[pl.BlockSpec((8, d), lambda i, j: (0, 0)),
                  pl.BlockSpec((1, d, tn), lambda i, j: (i, 0, j)),
                  pl.BlockSpec((1, 1, tn), lambda i, j: (i, 0, j))],
        out_specs=pl.BlockSpec((1, 8, tn), lambda i, j: (i, 0, j)),
        out_shape=jax.ShapeDtypeStruct((depth, 8, n3), F32),
        compiler_params=pltpu.CompilerParams(
            dimension_semantics=("arbitrary", "arbitrary"), vmem_limit_bytes=VMEM_LIMIT),
        name="ada_mod",
    )(c8, w_ada, b_ada.reshape(depth, 1, n3))


def _modulated_norm(x, nw, scale, shift):
    y = x * lax.rsqrt(jnp.mean(x * x, axis=-1, keepdims=True) + EPS)
    return ((y * nw) * (1.0 + scale) + shift).astype(BF16)


def _norm_kernel(x_ref, nw_ref, sc_ref, sh_ref, h_ref):
    h_ref[...] = _modulated_norm(x_ref[...], nw_ref[...], sc_ref[...], sh_ref[...])


def _norm(x, nw, scale, shift):
    L, d = x.shape
    tm = min(512, L)
    vec = pl.BlockSpec((1, d), lambda i: (0, 0))
    return pl.pallas_call(
        _norm_kernel,
        grid=(L // tm,),
        in_specs=[pl.BlockSpec((tm, d), lambda i: (i, 0)), vec, vec, vec],
        out_specs=pl.BlockSpec((tm, d), lambda i: (i, 0)),
        out_shape=jax.ShapeDtypeStruct((L, d), BF16),
        compiler_params=pltpu.CompilerParams(
            dimension_semantics=("arbitrary",), vmem_limit_bytes=VMEM_LIMIT),
        name="norm",
    )(x, nw, scale, shift)


def _inproj_kernel(h_ref, w_ref, o_ref):
    o_ref[...] = jnp.dot(h_ref[...], w_ref[...].astype(BF16),
                         preferred_element_type=F32).astype(o_ref.dtype)


def _inproj(h, w, layer):
    L, d = h.shape
    n = w.shape[2]
    tm = min(2048, L)
    tn = 512
    return pl.pallas_call(
        _inproj_kernel,
        grid=(L // tm, n // tn),
        in_specs=[pl.BlockSpec((tm, d), lambda i, j: (i, 0)),
                  pl.BlockSpec((None, d, tn), lambda i, j: (layer, 0, j))],
        out_specs=pl.BlockSpec((tm, tn), lambda i, j: (i, j)),
        out_shape=jax.ShapeDtypeStruct((L, n), BF16),
        compiler_params=pltpu.CompilerParams(
            dimension_semantics=("arbitrary", "arbitrary"), vmem_limit_bytes=VMEM_LIMIT),
        name="inproj",
    )(h, w)


LOG2E = math.log2(math.e)
Q_SCALE = HEAD_DIM ** -0.5 * LOG2E


def _qt_prep_kernel(*refs):
    n = len(refs) // 2
    for q_ref, qt_ref in zip(refs[:n], refs[n:]):
        qt_ref[...] = (q_ref[...].astype(F32) * Q_SCALE).T.astype(qt_ref.dtype)


def _qt_prep(proj, names, t=1024):
    L = proj.shape[0]
    return pl.pallas_call(
        _qt_prep_kernel,
        grid=(L // t,),
        in_specs=[_piece(name, t) for name in names],
        out_specs=[pl.BlockSpec((GROUP_WIDTH, t), lambda n: (0, n)) for _ in names],
        out_shape=[jax.ShapeDtypeStruct((GROUP_WIDTH, L), BF16) for _ in names],
        compiler_params=pltpu.CompilerParams(
            dimension_semantics=("arbitrary",), vmem_limit_bytes=VMEM_LIMIT),
        name="qt_prep",
    )(*([proj] * len(names)))


def _kv_prep_kernel(k_ref, v_ref, kp_ref, vt_ref, *rest, pad_blocks, tok_blocks, group):
    b = pl.program_id(0)
    is_token = (b >= pad_blocks) & (b < pad_blocks + tok_blocks)

    @pl.when(is_token)
    def _():
        kp_ref[...] = k_ref[...]
        if group:
            kg_ref, k32_scr = rest
            width = k_ref.shape[1]
            for c in range(width // LANES):
                k32_scr[c] = k_ref[:, c * LANES:(c + 1) * LANES].astype(F32)
            for r in range(group):
                for c in range(width // LANES):
                    kg_ref[:, r * width + c * LANES:r * width + (c + 1) * LANES] = (
                        k32_scr[c, pl.ds(r, kg_ref.shape[0], stride=group), :].astype(kg_ref.dtype))
        vt = v_ref[...].astype(F32).T.astype(vt_ref.dtype)
        ones = jnp.ones((V_ROWS - HEAD_DIM, vt.shape[1]), vt_ref.dtype)
        for g in range(vt.shape[0] // HEAD_DIM):
            vt_ref[g * V_ROWS:(g + 1) * V_ROWS, :] = jnp.concatenate(
                [vt[g * HEAD_DIM:(g + 1) * HEAD_DIM], ones], axis=0)

    @pl.when(jnp.logical_not(is_token))
    def _():
        kp_ref[...] = jnp.zeros(kp_ref.shape, kp_ref.dtype)
        vt_ref[...] = jnp.zeros(vt_ref.shape, vt_ref.dtype)
        if group:
            rest[0][...] = jnp.zeros(rest[0].shape, rest[0].dtype)


def _kv_prep(proj, k_name, v_name, width, pad, t, group=0):
    L = proj.shape[0]
    assert pad % t == 0 and L % t == 0 and width % LANES == 0
    pad_blocks, tok_blocks = pad // t, L // t
    lp = L + 2 * pad
    assert _SRC[k_name][1] == width and _SRC[v_name][1] == width
    tok = lambda b: jnp.clip(b - pad_blocks, 0, tok_blocks - 1)
    return pl.pallas_call(
        functools.partial(_kv_prep_kernel, pad_blocks=pad_blocks, tok_blocks=tok_blocks, group=group),
        grid=(lp // t,),
        in_specs=[_piece(k_name, t, tok), _piece(v_name, t, tok)],
        out_specs=[pl.BlockSpec((t, width), lambda b: (b, 0)),
                   pl.BlockSpec((width // HEAD_DIM * V_ROWS, t), lambda b: (0, b))]
        + ([pl.BlockSpec((t // group, group * width), lambda b: (b, 0))] if group else []),
        out_shape=[jax.ShapeDtypeStruct((lp, width), BF16),
                   jax.ShapeDtypeStruct((width // HEAD_DIM * V_ROWS, lp), BF16)]
        + ([jax.ShapeDtypeStruct((lp // group, group * width), BF16)] if group else []),
        scratch_shapes=[pltpu.VMEM((width // LANES, t, LANES), F32)] if group else [],
        compiler_params=pltpu.CompilerParams(
            dimension_semantics=("arbitrary",), vmem_limit_bytes=VMEM_LIMIT),
        name="kv_prep",
    )(proj, proj)


def _band_kernel(*refs, w_keys, chunk, lanes, heads, group, use_sink, window, variant):
    if use_sink:
        sink_ref, qt_ref, k_ref, vt_ref, tab_ref, o_ref = refs
    else:
        qt_ref, k_ref, vt_ref, tab_ref, o_ref = refs
    n = pl.program_id(0)
    _, pos = window(n)
    kwin = k_ref[...]
    vwin = vt_ref[...]
    var = variant(n)

    n_chunks = w_keys // chunk

    def attend(kbias):
        def scores(h, c):
            kv = h // group
            rows = slice(c * chunk, (c + 1) * chunk)
            st = jnp.dot(kwin[rows, kv * HEAD_DIM:(kv + 1) * HEAD_DIM],
                         qt_ref[h * HEAD_DIM:(h + 1) * HEAD_DIM, :],
                         preferred_element_type=F32) + tab_ref[var, h, rows, :]
            return st if kbias is None else st + kbias[rows]

        def col_max(m, st):
            cm = jnp.max(st, axis=0, keepdims=True)
            return cm if m is None else jnp.maximum(m, cm)

        def finish(h, acc, m):
            o, l = acc[:HEAD_DIM], acc[HEAD_DIM:HEAD_DIM + 1]
            if use_sink:
                sk = sink_ref[h] * LOG2E
                m2 = jnp.maximum(m, sk)
                a = jnp.exp2(m - m2)
                return o * (a / (l * a + jnp.exp2(sk - m2)))
            return o / l

        outs = []
        groups = [range(g, g + lanes) for g in range(0, heads, lanes)]
        cur = {h: [] for h in groups[0]}
        m = {h: None for h in groups[0]}
        for c in range(n_chunks):
            for h in groups[0]:
                cur[h].append(scores(h, c))
                m[h] = col_max(m[h], cur[h][-1])
        for gi, grp in enumerate(groups):
            following = groups[gi + 1] if gi + 1 < len(groups) else ()
            nxt = {h: [] for h in following}
            m_next = {h: None for h in following}
            acc = {h: None for h in grp}
            for c in range(n_chunks):
                for h in following:
                    nxt[h].append(scores(h, c))
                    m_next[h] = col_max(m_next[h], nxt[h][-1])
                for h in grp:
                    kv = h // group
                    pt = jnp.exp2(cur[h][c] - m[h]).astype(BF16)
                    pv = jnp.dot(vwin[kv * V_ROWS:(kv + 1) * V_ROWS, c * chunk:(c + 1) * chunk], pt,
                                 preferred_element_type=F32)
                    acc[h] = pv if acc[h] is None else acc[h] + pv
            outs.extend(finish(h, acc[h], m[h]) for h in grp)
            cur, m = nxt, m_next
        o_ref[...] = jnp.concatenate(outs, axis=0).T.astype(o_ref.dtype)

    if pos is None:
        attend(None)
    else:
        first, lo, hi = pos
        inside = (first >= lo) & (first + w_keys <= hi)

        @pl.when(inside)
        def _():
            attend(None)

        @pl.when(jnp.logical_not(inside))
        def _():
            kpos = first + lax.broadcasted_iota(jnp.int32, (w_keys, 1), 0)
            attend(jnp.where((kpos >= lo) & (kpos < hi), 0.0, NEG).astype(F32))


def _band_attention(qt, k_pad, vt_pad, table, *, chunk, lanes, group, window, variant=lambda n: 0,
                    sink=None, layer=None):
    qw, L = qt.shape
    heads = qw // HEAD_DIM
    w_keys, tq = table.shape[-2:]
    assert w_keys % chunk == 0
    kern = functools.partial(_band_kernel, w_keys=w_keys, chunk=chunk, lanes=lanes, heads=heads, group=group,
                             use_sink=sink is not None, window=window, variant=variant)
    start = lambda n: pl.multiple_of(window(n)[0], LANES)
    if layer is None:
        table_spec = pl.BlockSpec(table.shape, lambda n: (0, 0, 0, 0), pipeline_mode=pl.Buffered(1))
    else:
        table_spec = pl.BlockSpec((None,) + table.shape[1:], lambda n: (layer, 0, 0, 0, 0),
                                  pipeline_mode=pl.Buffered(1))
    in_specs = [pl.BlockSpec((qw, tq), lambda n: (0, n)),
                pl.BlockSpec((pl.Element(w_keys), pl.Element(k_pad.shape[1])), lambda n: (start(n), 0)),
                pl.BlockSpec((pl.Element(vt_pad.shape[0]), pl.Element(w_keys)), lambda n: (0, start(n))),
                table_spec]
    args = [qt, k_pad, vt_pad, table]
    if sink is not None:
        in_specs = [pl.BlockSpec(memory_space=pltpu.SMEM)] + in_specs
        args = [sink] + args
    return pl.pallas_call(
        kern,
        grid=(L // tq,),
        in_specs=in_specs,
        out_specs=pl.BlockSpec((tq, qw), lambda n: (n, 0)),
        out_shape=jax.ShapeDtypeStruct((L, qw), BF16),
        compiler_params=pltpu.CompilerParams(
            dimension_semantics=("arbitrary",), vmem_limit_bytes=VMEM_LIMIT),
        name="band_attention",
    )(*args)


def _t5_bucket(rel):
    half = T5_BUCKETS // 2
    exact = half // 2
    n = jnp.abs(rel)
    big = exact + (jnp.log(jnp.maximum(n, exact).astype(F32) / exact)
                   / math.log(T5_MAX_DIST / exact) * (half - exact)).astype(jnp.int32)
    big = jnp.minimum(big, half - 1)
    return jnp.where(rel > 0, half, 0) + jnp.where(n < exact, n, big)


def _toeplitz_kernel(rv_ref, o_ref, *, n_diag):
    w, tq = o_ref.shape[1:]
    blocks = []
    for d in range(n_diag):
        s = LANES * (n_diag - 1 - d)
        x = jnp.broadcast_to(rv_ref[0, :, s:s + 2 * LANES], (LANES, 2 * LANES))
        blocks.append(pltpu.roll(x, LANES, 1, stride=1, stride_axis=0)[:, :LANES])
    for cb in range(w // LANES):
        for ib in range(tq // LANES):
            o_ref[0, cb * LANES:(cb + 1) * LANES, ib * LANES:(ib + 1) * LANES] = (
                blocks[cb - ib + tq // LANES - 1])


def _toeplitz(v, tq, w):
    heads, n_rel = v.shape
    n_diag = w // LANES + tq // LANES - 1
    assert w % LANES == 0 and tq % LANES == 0 and n_rel == LANES * n_diag + LANES - 1
    rv = jnp.pad(v, ((0, 0), (0, 1)))[:, None, ::-1]
    return pl.pallas_call(
        functools.partial(_toeplitz_kernel, n_diag=n_diag),
        grid=(heads,),
        in_specs=[pl.BlockSpec((1, 1, rv.shape[-1]), lambda h: (h, 0, 0))],
        out_specs=pl.BlockSpec((1, w, tq), lambda h: (h, 0, 0)),
        out_shape=jax.ShapeDtypeStruct((heads, w, tq), F32),
        compiler_params=pltpu.CompilerParams(
            dimension_semantics=("arbitrary",), vmem_limit_bytes=VMEM_LIMIT),
        name="toeplitz",
    )(rv)


def _t5_table(t5_table, head_lo, tq, r_lo, w_keys, mult):
    rel = np.arange(-r_lo - (tq - 1), w_keys - r_lo)
    m = mult(rel)
    b = t5_table.astype(F32)[:, head_lo:head_lo + GROUP_HEADS][_t5_bucket(jnp.asarray(rel, jnp.int32))]
    logm = np.log(np.maximum(m, 1)).astype(np.float32)
    vec = jnp.where(jnp.asarray(m > 0)[:, None], (b + logm[:, None]) * LOG2E, NEG).T
    return _toeplitz(vec, tq, w_keys)[None]


def _window_mult(rel):
    return (np.abs(rel) <= A_RADIUS).astype(np.int32)


def _dilated_mult(rel):
    m = np.zeros(rel.shape, np.int32)
    for window, dil in D_CONFIGS:
        m += ((rel % dil == 0) & (np.abs(rel) <= window // 2)).astype(np.int32)
    return m


NA_TILE_ROWS = 4
NA_WIN_ROWS = NA_TILE_ROWS + NA_ROWS
NA_TQ = NA_TILE_ROWS * GRID_W
NA_W = NA_WIN_ROWS * GRID_W


def _na_row_index(variant, j, i):
    dr, ok = ((j - i, j < NA_ROWS),
              (j - NA_ROWS // 2 - i, i <= j < i + NA_ROWS),
              (j - NA_ROWS - i, j >= NA_TILE_ROWS))[variant]
    return dr + NA_ROWS - 1 if ok else None


def _na_table_kernel(rp_ref, o_ref):
    kc = lax.broadcasted_iota(jnp.int32, (GRID_W, LANES), 0)
    lane = lax.broadcasted_iota(jnp.int32, (GRID_W, LANES), 1)
    col_start = jnp.clip(lane % GRID_W - NA_COLS // 2, 0, GRID_W - NA_COLS)
    col_ok = (kc >= col_start) & (kc < col_start + NA_COLS)
    neg = jnp.full((GRID_W, LANES), NEG, F32)
    cache = {}

    def half_block(d, side):
        if d is None:
            return neg
        if (d, side) not in cache:
            x = jnp.broadcast_to(rp_ref[0, 0, d:d + 1, :], (GRID_W, LANES))
            y = pltpu.roll(x, GRID_W * (1 - side), 1, stride=1, stride_axis=0)
            cache[d, side] = jnp.where(col_ok, y, NEG)
        return cache[d, side]

    for variant in range(3):
        for j in range(NA_WIN_ROWS):
            for ip in range(NA_TILE_ROWS // 2):
                left = half_block(_na_row_index(variant, j, 2 * ip), 0)
                right = half_block(_na_row_index(variant, j, 2 * ip + 1), 1)
                o_ref[0, variant, 0, j * GRID_W:(j + 1) * GRID_W, ip * LANES:(ip + 1) * LANES] = (
                    jnp.where(lane < GRID_W, left, right))


def _na_tables(rpb):
    depth, heads, n_dr, n_dc = rpb.shape
    assert n_dr == NA_WIN_ROWS + NA_TILE_ROWS - 1 and 2 * GRID_W == LANES
    front = GRID_W - NA_COLS
    rp = jnp.pad(rpb.astype(F32) * LOG2E, ((0, 0), (0, 0), (0, 1), (front, LANES - n_dc - front)))
    rp = rp[..., ::-1]
    return pl.pallas_call(
        _na_table_kernel,
        grid=(depth, heads),
        in_specs=[pl.BlockSpec((1, 1, n_dr + 1, LANES), lambda l, h: (l, h, 0, 0))],
        out_specs=pl.BlockSpec((1, 3, 1, NA_W, NA_TQ), lambda l, h: (l, 0, h, 0, 0)),
        out_shape=jax.ShapeDtypeStruct((depth, 3, heads, NA_W, NA_TQ), F32),
        compiler_params=pltpu.CompilerParams(
            dimension_semantics=("arbitrary", "arbitrary"), vmem_limit_bytes=VMEM_LIMIT),
        name="na_tables",
    )(rp)


def _swap_pairs(x):
    n = x.shape[-1]
    lane = lax.broadcasted_iota(jnp.int32, x.shape, x.ndim - 1)
    return jnp.where(lane % 2 == 0, pltpu.roll(x, n - 1, x.ndim - 1), pltpu.roll(x, 1, x.ndim - 1))


def _head_rms(x, ones_bd, w):
    sq = x * x
    hi = sq.astype(BF16)
    lo = (sq - hi.astype(F32)).astype(BF16)
    ms = (jnp.dot(hi, ones_bd, preferred_element_type=F32)
          + jnp.dot(lo, ones_bd, preferred_element_type=F32)) * (1.0 / HEAD_DIM)
    return x * lax.rsqrt(ms + EPS) * w


def _cprep_kernel(q_ref, k_ref, v_ref, cos_ref, sin_ref, qw_ref, kw_ref, bd_ref,
                  qt_ref, k2_ref, vt_ref):
    cos = cos_ref[...]
    sin = sin_ref[...]
    q = _head_rms(q_ref[...].astype(F32), bd_ref[...], qw_ref[...])
    reps = q.shape[1] // LANES
    q = q * jnp.concatenate([cos] * reps, axis=1) + _swap_pairs(q) * jnp.concatenate([sin] * reps, axis=1)
    qt_ref[...] = (q * Q_SCALE).T.astype(qt_ref.dtype)
    k = _head_rms(k_ref[...].astype(F32), bd_ref[:LANES, :LANES], kw_ref[...])
    k = (k * cos + _swap_pairs(k) * sin).astype(k2_ref.dtype)
    vt = v_ref[...].astype(F32).T.astype(vt_ref.dtype)
    ones = jnp.ones((V_ROWS - HEAD_DIM, vt.shape[1]), vt_ref.dtype)
    tk = vt_ref.shape[-1]
    for g in range(C_KV_HEADS):
        k2_ref[g] = k[:, g * HEAD_DIM:(g + 1) * HEAD_DIM]
        vg = jnp.concatenate([vt[g * HEAD_DIM:(g + 1) * HEAD_DIM], ones], axis=0)
        for u in range(vt_ref.shape[1]):
            vt_ref[g, u] = vg[:, u * tk:(u + 1) * tk]


def _cprep(proj, cos2, sin2, qw, kw, ones_bd, tk, t=1024):
    L = proj.shape[0]
    vec = lambda width: pl.BlockSpec((1, width), lambda n: (0, 0))
    return pl.pallas_call(
        _cprep_kernel,
        grid=(L // t,),
        in_specs=[_piece("q_c", t), _piece("k_c", t), _piece("v_c", t),
                  pl.BlockSpec((t, LANES), lambda n: (n, 0)),
                  pl.BlockSpec((t, LANES), lambda n: (n, 0)),
                  vec(GROUP_WIDTH), vec(LANES),
                  pl.BlockSpec((GROUP_WIDTH, GROUP_WIDTH), lambda n: (0, 0))],
        out_specs=[pl.BlockSpec((GROUP_WIDTH, t), lambda n: (0, n)),
                   pl.BlockSpec((C_KV_HEADS, t, HEAD_DIM), lambda n: (0, n, 0)),
                   pl.BlockSpec((C_KV_HEADS, t // tk, V_ROWS, tk), lambda n: (0, n, 0, 0))],
        out_shape=[jax.ShapeDtypeStruct((GROUP_WIDTH, L), BF16),
                   jax.ShapeDtypeStruct((C_KV_HEADS, L, HEAD_DIM), BF16),
                   jax.ShapeDtypeStruct((C_KV_HEADS, L // tk, V_ROWS, tk), BF16)],
        compiler_params=pltpu.CompilerParams(
            dimension_semantics=("arbitrary",), vmem_limit_bytes=VMEM_LIMIT),
        name="dense_prep",
    )(proj, proj, proj, cos2, sin2, qw, kw, ones_bd)


def _flash_kernel(qt_ref, k_ref, vt_ref, o_ref, m_scr, acc_scr, st_scr, *, tk, heads, unroll, ahead):
    n_kv = k_ref.shape[1] // tk
    m_scr[...] = jnp.full(m_scr.shape, NEG, F32)
    acc_scr[...] = jnp.zeros(acc_scr.shape, F32)

    def scores(j, h):
        k = k_ref[0, pl.ds(pl.multiple_of(j * tk, tk), tk), :]
        qt = qt_ref[h * HEAD_DIM:(h + 1) * HEAD_DIM, :]
        return jnp.dot(k, qt, preferred_element_type=F32)

    n_items = unroll * heads

    def item_scores(t, idx):
        j = t * unroll + idx // heads
        return scores(jnp.minimum(j, n_kv - 1), idx % heads)

    for a in range(ahead):
        st_scr[a] = item_scores(0, a)

    def body(t, carry):
        pending = [st_scr[a] for a in range(ahead)]
        for idx in range(n_items):
            pending.append(item_scores(t, idx + ahead))
            st = pending.pop(0)
            h = idx % heads
            vt = vt_ref[0, t * unroll + idx // heads]
            for q0 in range(0, st.shape[1], C_SPLIT):
                cols = slice(q0, q0 + C_SPLIT)
                m = m_scr[h, :, cols]
                m_new = jnp.maximum(m, jnp.max(st[:, cols], axis=0, keepdims=True))
                pt = jnp.exp2(st[:, cols] - m_new).astype(BF16)
                acc_scr[h, :, cols] = (jnp.exp2(m - m_new) * acc_scr[h, :, cols]
                                       + jnp.dot(vt, pt, preferred_element_type=F32))
                m_scr[h, :, cols] = m_new
        for a in range(ahead):
            st_scr[a] = pending[a]
        return carry

    lax.fori_loop(0, n_kv // unroll, body, 0)
    outs = [acc_scr[h, :HEAD_DIM] / acc_scr[h, HEAD_DIM:HEAD_DIM + 1] for h in range(heads)]
    o_ref[...] = jnp.concatenate(outs, axis=0).T.astype(o_ref.dtype)


def _flash(qt, k2, vt, tq):
    L = qt.shape[1]
    _, n_kv, v_rows, tk = vt.shape
    heads = GROUP_HEADS // C_KV_HEADS
    qrows = heads * HEAD_DIM
    return pl.pallas_call(
        functools.partial(_flash_kernel, tk=tk, heads=heads, unroll=math.gcd(C_UNROLL, n_kv),
                          ahead=C_AHEAD),
        grid=(C_KV_HEADS, L // tq),
        in_specs=[pl.BlockSpec((qrows, tq), lambda g, n: (g, n)),
                  pl.BlockSpec((1, L, HEAD_DIM), lambda g, n: (g, 0, 0)),
                  pl.BlockSpec((1, n_kv, v_rows, tk), lambda g, n: (g, 0, 0, 0))],
        out_specs=pl.BlockSpec((tq, qrows), lambda g, n: (n, g)),
        out_shape=jax.ShapeDtypeStruct((L, GROUP_WIDTH), BF16),
        scratch_shapes=[pltpu.VMEM((heads, 1, tq), F32), pltpu.VMEM((heads, v_rows, tq), F32),
                        pltpu.VMEM((C_AHEAD, tk, tq), F32)],
        compiler_params=pltpu.CompilerParams(
            dimension_semantics=("arbitrary", "arbitrary"), vmem_limit_bytes=VMEM_LIMIT),
        name="dense_flash",
    )(qt, k2, vt)


def _outproj_kernel(*refs, final):
    if final:
        x_ref, ya, yb, yc, yd, za, zb, zc, zd, w_ref, g_ref, fw_ref, o_ref = refs
    else:
        x_ref, ya, yb, yc, yd, za, zb, zc, zd, w_ref, g_ref, nw_ref, sc_ref, sh_ref, o_ref, h_ref = refs
    sub = x_ref.shape[0] // OUT_SPLIT
    for r0 in range(0, x_ref.shape[0], sub):
        rows = slice(r0, r0 + sub)
        acc = jnp.zeros((sub, x_ref.shape[1]), F32)
        for gi, (y_ref, z_ref) in enumerate(zip((ya, yb, yc, yd), (za, zb, zc, zd))):
            z = z_ref[rows, :].astype(F32)
            u = (y_ref[rows, :].astype(F32) * (z * jax.nn.sigmoid(z))).astype(BF16)
            acc = acc + jnp.dot(u, w_ref[gi * GROUP_WIDTH:(gi + 1) * GROUP_WIDTH, :],
                                preferred_element_type=F32)
        xn = x_ref[rows, :] + g_ref[...] * acc
        if final:
            o_ref[rows, :] = (xn * lax.rsqrt(jnp.mean(xn * xn, axis=-1, keepdims=True) + EPS)
                              * fw_ref[...])
        else:
            o_ref[rows, :] = xn
            h_ref[rows, :] = _modulated_norm(xn, nw_ref[...], sc_ref[...], sh_ref[...])


def _outproj(x, ys, proj, w_bf16, layer, gate, vecs):
    L, d = x.shape
    tm = min(512, L)
    final = len(vecs) == 1
    row = lambda width: pl.BlockSpec((tm, width), lambda i: (i, 0))
    vec = pl.BlockSpec((1, d), lambda i: (0, 0))
    in_specs = ([row(d)] + [row(GROUP_WIDTH)] * 4 + [_piece(z, tm) for z in ("z_a", "z_b", "z_c", "z_d")]
                + [pl.BlockSpec((None,) + w_bf16.shape[1:], lambda i: (layer, 0, 0)), vec]
                + [vec] * len(vecs))
    x_out = jax.ShapeDtypeStruct((L, d), F32)
    return pl.pallas_call(
        functools.partial(_outproj_kernel, final=final),
        grid=(L // tm,),
        in_specs=in_specs,
        out_specs=row(d) if final else [row(d), row(d)],
        out_shape=x_out if final else [x_out, jax.ShapeDtypeStruct((L, d), BF16)],
        compiler_params=pltpu.CompilerParams(
            dimension_semantics=("arbitrary",), vmem_limit_bytes=VMEM_LIMIT),
        name="outproj",
    )(x, *ys, proj, proj, proj, proj, w_bf16, gate, *vecs)


def _rope_tables(L):
    t = np.arange(L)
    axis_dim = HEAD_DIM // 2
    inv = jnp.asarray(ROPE_THETA, F32) ** (-jnp.arange(0, axis_dim, 2, dtype=F32) / axis_dim)
    row = jnp.asarray(t // GRID_W, F32)
    col = jnp.asarray(t % GRID_W, F32)
    ang = jnp.concatenate([row[:, None] * inv[None], col[:, None] * inv[None]], axis=-1)
    cos = jnp.repeat(jnp.cos(ang), 2, axis=-1)
    sin = jnp.repeat(jnp.sin(ang), 2, axis=-1) * jnp.asarray(np.tile([-1.0, 1.0], HEAD_DIM // 2), F32)
    return jnp.tile(cos, (1, LANES // HEAD_DIM)), jnp.tile(sin, (1, LANES // HEAD_DIM))


A_TQ = 256
A_CHUNK = 256
NA_CHUNK = 256
D_CHUNK = 768
BAND_LANES = 2
NA_LANES = 4
D_TQ = 256
D_REACH = D_CONFIGS[-1][0] // 2
C_TQ = 512
C_TK = 256
C_UNROLL = 16
C_AHEAD = 2
C_SPLIT = 256
OUT_SPLIT = 2


def _table_a(t5_table):
    return _t5_table(t5_table, 0, A_TQ, A_RADIUS, A_TQ + 2 * A_RADIUS, _window_mult)


def _table_d(t5_table):
    return _t5_table(t5_table, GROUP_HEADS, D_TQ, D_REACH, D_TQ + 2 * D_REACH, _dilated_mult)


D_DIL = D_CONFIGS[-1][1]
D_NEAR = D_CONFIGS[-2][0] // 2
D_FAR_ROWS = (D_REACH - D_NEAR) // D_DIL


def _dil_kernel(qt_ref, kn_ref, k16_ref, vt_ref, tabn_ref, tabf_ref, o_ref, *, tq, L, lanes):
    n = pl.program_id(0)
    t0 = n * tq
    heads = qt_ref.shape[0] // HEAD_DIM
    w_near = kn_ref.shape[0]
    far = D_FAR_ROWS
    n_chunks = w_near // tq
    w_far_keys = far * D_DIL
    lane = lax.broadcasted_iota(jnp.int32, (1, tq), 1)
    qmask = (lax.broadcasted_iota(jnp.int32, (D_DIL * HEAD_DIM, tq), 0) // HEAD_DIM) == (lane % D_DIL)
    pmask = (lax.broadcasted_iota(jnp.int32, (w_far_keys, tq), 0) % D_DIL) == (lane % D_DIL)

    def attend(kb_near, kb_far):
        def scores(h):
            cols = slice(h * HEAD_DIM, (h + 1) * HEAD_DIM)
            q = qt_ref[cols, :]
            near = []
            for c in range(n_chunks):
                rows = slice(c * tq, (c + 1) * tq)
                st = jnp.dot(kn_ref[rows, cols], q, preferred_element_type=F32) + tabn_ref[h, rows, :]
                near.append(st if kb_near is None else st + kb_near[rows])
            qbd = jnp.where(qmask, jnp.concatenate([q] * D_DIL, axis=0), jnp.zeros((), BF16))
            sides = []
            for r0 in (0, k16_ref.shape[0] - far):
                kcat = jnp.concatenate(
                    [k16_ref[r0:r0 + far, r * GROUP_WIDTH + h * HEAD_DIM:r * GROUP_WIDTH + (h + 1) * HEAD_DIM]
                     for r in range(D_DIL)], axis=1)
                sides.append(jnp.dot(kcat, qbd, preferred_element_type=F32))
            sf = jnp.concatenate(sides, axis=0) + tabf_ref[h]
            if kb_far is not None:
                sf = sf + kb_far
            m = jnp.max(sf, axis=0, keepdims=True)
            for st in near:
                m = jnp.maximum(m, jnp.max(st, axis=0, keepdims=True))
            return near, sf, m

        def values(h, near, sf, m):
            v = vt_ref[h * V_ROWS:(h + 1) * V_ROWS, :]
            acc = None
            for c, st in enumerate(near):
                pt = jnp.exp2(st - m).astype(BF16)
                lo = w_far_keys + c * tq
                pv = jnp.dot(v[:, lo:lo + tq], pt, preferred_element_type=F32)
                acc = pv if acc is None else acc + pv
            pf = jnp.exp2(sf - m)
            for side, lo in ((0, 0), (1, w_far_keys + w_near)):
                rep = jnp.concatenate(
                    [jnp.broadcast_to(pf[side * far + a:side * far + a + 1, :], (D_DIL, tq))
                     for a in range(far)], axis=0)
                pbd = jnp.where(pmask, rep, 0.0).astype(BF16)
                acc = acc + jnp.dot(v[:, lo:lo + w_far_keys], pbd, preferred_element_type=F32)
            return acc[:HEAD_DIM] / acc[HEAD_DIM:HEAD_DIM + 1]

        outs = []
        groups = [range(g, g + lanes) for g in range(0, heads, lanes)]
        cur = {h: scores(h) for h in groups[0]}
        for gi, grp in enumerate(groups):
            nxt = {h: scores(h) for h in groups[gi + 1]} if gi + 1 < len(groups) else {}
            outs.extend(values(h, *cur[h]) for h in grp)
            cur = nxt
        o_ref[...] = jnp.concatenate(outs, axis=0).T.astype(o_ref.dtype)

    inside = (t0 - D_REACH >= 0) & (t0 + tq + D_REACH <= L)

    @pl.when(inside)
    def _():
        attend(None, None)

    @pl.when(jnp.logical_not(inside))
    def _():
        kpos = t0 - D_NEAR + lax.broadcasted_iota(jnp.int32, (w_near, 1), 0)
        kb_near = jnp.where((kpos >= 0) & (kpos < L), 0.0, NEG).astype(F32)
        a = lax.broadcasted_iota(jnp.int32, (2 * far, 1), 0)
        base = jnp.where(a < far, t0 - D_REACH + D_DIL * a, t0 + tq + D_NEAR + D_DIL * (a - far))
        kb_far = jnp.where((base >= 0) & (base < L), 0.0, NEG).astype(F32)
        attend(kb_near, kb_far)


def _far_table(t5_table, tq):
    a = np.arange(D_FAR_ROWS)[:, None]
    b = np.arange(tq // D_DIL)[None, :]
    rel = np.concatenate([-D_REACH + D_DIL * (a - b), tq + D_NEAR + D_DIL * (a - b)], axis=0)
    valid = np.abs(rel) <= D_REACH
    bias = t5_table.astype(F32)[:, GROUP_HEADS:2 * GROUP_HEADS][_t5_bucket(jnp.asarray(rel, jnp.int32))]
    tab = jnp.where(jnp.asarray(valid)[..., None], bias * LOG2E, NEG)
    return jnp.repeat(jnp.transpose(tab, (2, 0, 1)), D_DIL, axis=-1)


def _near_table_d(t5_table, tq):
    return _t5_table(t5_table, GROUP_HEADS, tq, D_NEAR, tq + 2 * D_NEAR, _dilated_mult)[0]


def _dilated_attention(qt, k_pad, k16, vt_pad, tab_near, tab_far, *, tq, pad, lanes):
    qw, L = qt.shape
    w_near = tq + 2 * D_NEAR
    w_all = tq + 2 * D_REACH
    return pl.pallas_call(
        functools.partial(_dil_kernel, tq=tq, L=L, lanes=lanes),
        grid=(L // tq,),
        in_specs=[pl.BlockSpec((qw, tq), lambda n: (0, n)),
                  pl.BlockSpec((pl.Element(w_near), pl.Element(k_pad.shape[1])),
                               lambda n: (pl.multiple_of(n * tq + pad - D_NEAR, LANES), 0)),
                  pl.BlockSpec((pl.Element(w_all // D_DIL), pl.Element(k16.shape[1])),
                               lambda n: (pl.multiple_of((n * tq + pad - D_REACH) // D_DIL, BF16_SUBLANES), 0)),
                  pl.BlockSpec((pl.Element(vt_pad.shape[0]), pl.Element(w_all)),
                               lambda n: (0, pl.multiple_of(n * tq + pad - D_REACH, LANES))),
                  pl.BlockSpec(tab_near.shape, lambda n: (0, 0, 0), pipeline_mode=pl.Buffered(1)),
                  pl.BlockSpec(tab_far.shape, lambda n: (0, 0, 0), pipeline_mode=pl.Buffered(1))],
        out_specs=pl.BlockSpec((tq, qw), lambda n: (n, 0)),
        out_shape=jax.ShapeDtypeStruct((L, qw), BF16),
        compiler_params=pltpu.CompilerParams(
            dimension_semantics=("arbitrary",), vmem_limit_bytes=VMEM_LIMIT),
        name="dilated_attention",
    )(qt, k_pad, k16, vt_pad, tab_near, tab_far)


PREP_T = 1024
assert PREP_T >= D_REACH


def _seq_window(L, tq, r_lo):
    return lambda n: (n * tq + PREP_T - r_lo, (n * tq - r_lo, 0, L))


def _mixer_a(proj, qt, tab_a, sink):
    L = proj.shape[0]
    k_pad, vt_pad = _kv_prep(proj, "k_a", "v_a", LANES, PREP_T, PREP_T)
    return _band_attention(qt, k_pad, vt_pad, tab_a, chunk=A_CHUNK, lanes=BAND_LANES,
                           group=GROUP_HEADS // A_KV_HEADS, window=_seq_window(L, A_TQ, A_RADIUS),
                           sink=sink.astype(F32))


def _mixer_b(proj, qt, tabs_b, layer):
    L = proj.shape[0]
    rows = L // GRID_W
    n_tiles = rows // NA_TILE_ROWS
    k, vt = _kv_prep(proj, "k_b", "v_b", GROUP_WIDTH, 0, PREP_T)
    window = lambda n: (jnp.clip(n * NA_TILE_ROWS - NA_ROWS // 2, 0, rows - NA_WIN_ROWS) * GRID_W, None)
    variant = lambda n: jnp.where(n == 0, 0, jnp.where(n == n_tiles - 1, 2, 1))
    return _band_attention(qt, k, vt, tabs_b, chunk=NA_CHUNK, lanes=NA_LANES, group=1,
                           window=window, variant=variant, layer=layer)


def _mixer_c(proj, q_norm_w, k_norm_w):
    L = proj.shape[0]
    cos2, sin2 = _rope_tables(L)
    ones_bd = jnp.asarray(np.kron(np.eye(GROUP_HEADS), np.ones((HEAD_DIM, HEAD_DIM))), BF16)
    qt, k2, vt = _cprep(proj, cos2, sin2, jnp.tile(q_norm_w.astype(F32), GROUP_HEADS)[None],
                        jnp.tile(k_norm_w.astype(F32), LANES // HEAD_DIM)[None], ones_bd, C_TK)
    return _flash(qt, k2, vt, C_TQ)


def _mixer_d(proj, qt, tabs_d):
    k_pad, vt_pad, k16 = _kv_prep(proj, "k_d", "v_d", GROUP_WIDTH, PREP_T, PREP_T, group=D_DIL)
    return _dilated_attention(qt, k_pad, k16, vt_pad, *tabs_d, tq=D_TQ, pad=PREP_T, lanes=BAND_LANES)


def kernel(x, c, w_ada, b_ada, norm_w, w_in, w_out, attn_sink, na_rpb, q_norm_w, k_norm_w,
           t5_table, final_norm_w):
    B, L, D = x.shape
    assert B == 1 and L % 1024 == 0 and L // GRID_W >= NA_WIN_ROWS
    depth = w_ada.shape[0]
    x = x[0]

    mod = _ada_mod(jnp.broadcast_to(c, (8, D)), w_ada, b_ada)[:, 0:1, :]
    tab_a = _table_a(t5_table)
    tabs_d = (_near_table_d(t5_table, D_TQ), _far_table(t5_table, D_TQ))
    tabs_b = _na_tables(na_rpb)
    w_out_b = w_out.astype(BF16)
    shift, scale, gate = jnp.split(mod, 3, axis=-1)
    h = _norm(x, norm_w[0][None], scale[0], shift[0])
    for i in range(depth):
        proj = _inproj(h, w_in, i)
        qt_a, qt_b, qt_d = _qt_prep(proj, ("q_a", "q_b", "q_d"))
        ys = (_mixer_a(proj, qt_a, tab_a, attn_sink[i]),
              _mixer_b(proj, qt_b, tabs_b, i),
              _mixer_c(proj, q_norm_w[i], k_norm_w[i]),
              _mixer_d(proj, qt_d, tabs_d))
        if i + 1 < depth:
            x, h = _outproj(x, ys, proj, w_out_b, i, gate[i],
                            (norm_w[i + 1][None], scale[i + 1], shift[i + 1]))
        else:
            x = _outproj(x, ys, proj, w_out_b, i, gate[i], (final_norm_w[None],))
    return x[None]
```

```python
import functools
import math

import numpy as np
import jax
import jax.numpy as jnp
from jax import lax
from jax.experimental import pallas as pl
from jax.experimental.pallas import tpu as pltpu

HEAD_DIM = 64
GROUP_WIDTH = 512
GROUP_HEADS = 8
A_KV_HEADS = 2
A_RADIUS = 128
C_KV_HEADS = 2
ROPE_THETA = 10000.0
NA_ROWS = 8
NA_COLS = 16
D_CONFIGS = ((128, 1), (512, 4), (2048, 16))
GRID_W = 64
T5_BUCKETS = 32
T5_MAX_DIST = 1024
EPS = 1e-6
NEG = -1e30

LANES = 128
BF16_SUBLANES = 16
V_ROWS = HEAD_DIM + BF16_SUBLANES
VMEM_LIMIT = 56 * 1024 * 1024

F32 = jnp.float32
BF16 = jnp.bfloat16

_SRC = {}
_off = 0
for _name, _w in (("q_a", 512), ("k_a", 128), ("v_a", 128), ("z_a", 512),
                  ("q_b", 512), ("k_b", 512), ("v_b", 512), ("z_b", 512),
                  ("q_c", 512), ("k_c", 128), ("v_c", 128), ("z_c", 512),
                  ("q_d", 512), ("k_d", 512), ("v_d", 512), ("z_d", 512)):
    _SRC[_name] = (_off, _w)
    _off += _w
IN_WIDTH = _off


def _piece(name, t, row_block=lambda n: n):
    off, width = _SRC[name]
    return pl.BlockSpec((pl.Element(t), pl.Element(width)),
                        lambda n: (pl.multiple_of(row_block(n) * t, t), off))


def _ada_kernel(c_ref, w_ref, b_ref, o_ref):
    c = c_ref[...]
    cond = c * jax.nn.sigmoid(c)
    o_ref[0] = jnp.dot(cond, w_ref[0], preferred_element_type=F32,
                       precision=lax.Precision.HIGHEST) + b_ref[0]


def _ada_mod(c8, w_ada, b_ada):
    depth, d, n3 = w_ada.shape
    tn = 1024
    return pl.pallas_call(
        _ada_kernel,
        grid=(depth, n3 // tn),
        in_specs=[pl.BlockSpec((8, d), lambda i, j: (0, 0)),
                  pl.BlockSpec((1, d, tn), lambda i, j: (i, 0, j)),
                  pl.BlockSpec((1, 1, tn), lambda i, j: (i, 0, j))],
        out_specs=pl.BlockSpec((1, 8, tn), lambda i, j: (i, 0, j)),
        out_shape=jax.ShapeDtypeStruct((depth, 8, n3), F32),
        compiler_params=pltpu.CompilerParams(
            dimension_semantics=("arbitrary", "arbitrary"), vmem_limit_bytes=VMEM_LIMIT),
        name="ada_mod",
    )(c8, w_ada, b_ada.reshape(depth, 1, n3))


def _modulated_norm(x, nw, scale, shift):
    y = x * lax.rsqrt(jnp.mean(x * x, axis=-1, keepdims=True) + EPS)
    return ((y * nw) * (1.0 + scale) + shift).astype(BF16)


def _norm_kernel(x_ref, nw_ref, sc_ref, sh_ref, h_ref):
    h_ref[...] = _modulated_norm(x_ref[...], nw_ref[...], sc_ref[...], sh_ref[...])


def _norm(x, nw, scale, shift):
    L, d = x.shape
    tm = min(512, L)
    vec = pl.BlockSpec((1, d), lambda i: (0, 0))
    return pl.pallas_call(
        _norm_kernel,
        grid=(L // tm,),
        in_specs=[pl.BlockSpec((tm, d), lambda i: (i, 0)), vec, vec, vec],
        out_specs=pl.BlockSpec((tm, d), lambda i: (i, 0)),
        out_shape=jax.ShapeDtypeStruct((L, d), BF16),
        compiler_params=pltpu.CompilerParams(
            dimension_semantics=("arbitrary",), vmem_limit_bytes=VMEM_LIMIT),
        name="norm",
    )(x, nw, scale, shift)


def _inproj_kernel(h_ref, w_ref, o_ref):
    o_ref[...] = jnp.dot(h_ref[...], w_ref[...].astype(BF16),
                         preferred_element_type=F32).astype(o_ref.dtype)


def _inproj(h, w, layer):
    L, d = h.shape
    n = w.shape[2]
    tm = min(2048, L)
    tn = 512
    return pl.pallas_call(
        _inproj_kernel,
        grid=(L // tm, n // tn),
        in_specs=[pl.BlockSpec((tm, d), lambda i, j: (i, 0)),
                  pl.BlockSpec((None, d, tn), lambda i, j: (layer, 0, j))],
        out_specs=pl.BlockSpec((tm, tn), lambda i, j: (i, j)),
        out_shape=jax.ShapeDtypeStruct((L, n), BF16),
        compiler_params=pltpu.CompilerParams(
            dimension_semantics=("arbitrary", "arbitrary"), vmem_limit_bytes=VMEM_LIMIT),
        name="inproj",
    )(h, w)


LOG2E = math.log2(math.e)
Q_SCALE = HEAD_DIM ** -0.5 * LOG2E


def _qt_prep_kernel(*refs):
    n = len(refs) // 2
    for q_ref, qt_ref in zip(refs[:n], refs[n:]):
        qt_ref[...] = (q_ref[...].astype(F32) * Q_SCALE).T.astype(qt_ref.dtype)


def _qt_prep(proj, names, t=1024):
    L = proj.shape[0]
    return pl.pallas_call(
        _qt_prep_kernel,
        grid=(L // t,),
        in_specs=[_piece(name, t) for name in names],
        out_specs=[pl.BlockSpec((GROUP_WIDTH, t), lambda n: (0, n)) for _ in names],
        out_shape=[jax.ShapeDtypeStruct((GROUP_WIDTH, L), BF16) for _ in names],
        compiler_params=pltpu.CompilerParams(
            dimension_semantics=("arbitrary",), vmem_limit_bytes=VMEM_LIMIT),
        name="qt_prep",
    )(*([proj] * len(names)))


def _kv_prep_kernel(k_ref, v_ref, kp_ref, vt_ref, *rest, pad_blocks, tok_blocks, group):
    b = pl.program_id(0)
    is_token = (b >= pad_blocks) & (b < pad_blocks + tok_blocks)

    @pl.when(is_token)
    def _():
        kp_ref[...] = k_ref[...]
        if group:
            kg_ref, k32_scr = rest
            width = k_ref.shape[1]
            for c in range(width // LANES):
                k32_scr[c] = k_ref[:, c * LANES:(c + 1) * LANES].astype(F32)
            for r in range(group):
                for c in range(width // LANES):
                    kg_ref[:, r * width + c * LANES:r * width + (c + 1) * LANES] = (
                        k32_scr[c, pl.ds(r, kg_ref.shape[0], stride=group), :].astype(kg_ref.dtype))
        vt = v_ref[...].astype(F32).T.astype(vt_ref.dtype)
        ones = jnp.ones((V_ROWS - HEAD_DIM, vt.shape[1]), vt_ref.dtype)
        for g in range(vt.shape[0] // HEAD_DIM):
            vt_ref[g * V_ROWS:(g + 1) * V_ROWS, :] = jnp.concatenate(
                [vt[g * HEAD_DIM:(g + 1) * HEAD_DIM], ones], axis=0)

    @pl.when(jnp.logical_not(is_token))
    def _():
        kp_ref[...] = jnp.zeros(kp_ref.shape, kp_ref.dtype)
        vt_ref[...] = jnp.zeros(vt_ref.shape, vt_ref.dtype)
        if group:
            rest[0][...] = jnp.zeros(rest[0].shape, rest[0].dtype)


def _kv_prep(proj, k_name, v_name, width, pad, t, group=0):
    L = proj.shape[0]
    assert pad % t == 0 and L % t == 0 and width % LANES == 0
    pad_blocks, tok_blocks = pad // t, L // t
    lp = L + 2 * pad
    assert _SRC[k_name][1] == width and _SRC[v_name][1] == width
    tok = lambda b: jnp.clip(b - pad_blocks, 0, tok_blocks - 1)
    return pl.pallas_call(
        functools.partial(_kv_prep_kernel, pad_blocks=pad_blocks, tok_blocks=tok_blocks, group=group),
        grid=(lp // t,),
        in_specs=[_piece(k_name, t, tok), _piece(v_name, t, tok)],
        out_specs=[pl.BlockSpec((t, width), lambda b: (b, 0)),
                   pl.BlockSpec((width // HEAD_DIM * V_ROWS, t), lambda b: (0, b))]
        + ([pl.BlockSpec((t // group, group * width), lambda b: (b, 0))] if group else []),
        out_shape=[jax.ShapeDtypeStruct((lp, width), BF16),
                   jax.ShapeDtypeStruct((width // HEAD_DIM * V_ROWS, lp), BF16)]
        + ([jax.ShapeDtypeStruct((lp // group, group * width), BF16)] if group else []),
        scratch_shapes=[pltpu.VMEM((width // LANES, t, LANES), F32)] if group else [],
        compiler_params=pltpu.CompilerParams(
            dimension_semantics=("arbitrary",), vmem_limit_bytes=VMEM_LIMIT),
        name="kv_prep",
    )(proj, proj)


def _band_kernel(*refs, w_keys, chunk, lanes, heads, group, use_sink, window, variant):
    if use_sink:
        sink_ref, qt_ref, k_ref, vt_ref, tab_ref, o_ref = refs
    else:
        qt_ref, k_ref, vt_ref, tab_ref, o_ref = refs
    n = pl.program_id(0)
    _, pos = window(n)
    kwin = k_ref[...]
    vwin = vt_ref[...]
    var = variant(n)

    n_chunks = w_keys // chunk

    def attend(kbias):
        def scores(h, c):
            kv = h // group
            rows = slice(c * chunk, (c + 1) * chunk)
            st = jnp.dot(kwin[rows, kv * HEAD_DIM:(kv + 1) * HEAD_DIM],
                         qt_ref[h * HEAD_DIM:(h + 1) * HEAD_DIM, :],
                         preferred_element_type=F32) + tab_ref[var, h, rows, :]
            return st if kbias is None else st + kbias[rows]

        def col_max(m, st):
            cm = jnp.max(st, axis=0, keepdims=True)
            return cm if m is None else jnp.maximum(m, cm)

        def finish(h, acc, m):
            o, l = acc[:HEAD_DIM], acc[HEAD_DIM:HEAD_DIM + 1]
            if use_sink:
                sk = sink_ref[h] * LOG2E
                m2 = jnp.maximum(m, sk)
                a = jnp.exp2(m - m2)
                return o * (a / (l * a + jnp.exp2(sk - m2)))
            return o / l

        outs = []
        groups = [range(g, g + lanes) for g in range(0, heads, lanes)]
        cur = {h: [] for h in groups[0]}
        m = {h: None for h in groups[0]}
        for c in range(n_chunks):
            for h in groups[0]:
                cur[h].append(scores(h, c))
                m[h] = col_max(m[h], cur[h][-1])
        for gi, grp in enumerate(groups):
            following = groups[gi + 1] if gi + 1 < len(groups) else ()
            nxt = {h: [] for h in following}
            m_next = {h: None for h in following}
            acc = {h: None for h in grp}
            for c in range(n_chunks):
                for h in following:
                    nxt[h].append(scores(h, c))
                    m_next[h] = col_max(m_next[h], nxt[h][-1])
                for h in grp:
                    kv = h // group
                    pt = jnp.exp2(cur[h][c] - m[h]).astype(BF16)
                    pv = jnp.dot(vwin[kv * V_ROWS:(kv + 1) * V_ROWS, c * chunk:(c + 1) * chunk], pt,
                                 preferred_element_type=F32)
                    acc[h] = pv if acc[h] is None else acc[h] + pv
            outs.extend(finish(h, acc[h], m[h]) for h in grp)
            cur, m = nxt, m_next
        o_ref[...] = jnp.concatenate(outs, axis=0).T.astype(o_ref.dtype)

    if pos is None:
        attend(None)
    else:
        first, lo, hi = pos
        inside = (first >= lo) & (first + w_keys <= hi)

        @pl.when(inside)
        def _():
            attend(None)

        @pl.when(jnp.logical_not(inside))
        def _():
            kpos = first + lax.broadcasted_iota(jnp.int32, (w_keys, 1), 0)
            attend(jnp.where((kpos >= lo) & (kpos < hi), 0.0, NEG).astype(F32))


def _band_attention(qt, k_pad, vt_pad, table, *, chunk, lanes, group, window, variant=lambda n: 0,
                    sink=None, layer=None):
    qw, L = qt.shape
    heads = qw // HEAD_DIM
    w_keys, tq = table.shape[-2:]
    assert w_keys % chunk == 0
    kern = functools.partial(_band_kernel, w_keys=w_keys, chunk=chunk, lanes=lanes, heads=heads, group=group,
                             use_sink=sink is not None, window=window, variant=variant)
    start = lambda n: pl.multiple_of(window(n)[0], LANES)
    if layer is None:
        table_spec = pl.BlockSpec(table.shape, lambda n: (0, 0, 0, 0), pipeline_mode=pl.Buffered(1))
    else:
        table_spec = pl.BlockSpec((None,) + table.shape[1:], lambda n: (layer, 0, 0, 0, 0),
                                  pipeline_mode=pl.Buffered(1))
    in_specs = [pl.BlockSpec((qw, tq), lambda n: (0, n)),
                pl.BlockSpec((pl.Element(w_keys), pl.Element(k_pad.shape[1])), lambda n: (start(n), 0)),
                pl.BlockSpec((pl.Element(vt_pad.shape[0]), pl.Element(w_keys)), lambda n: (0, start(n))),
                table_spec]
    args = [qt, k_pad, vt_pad, table]
    if sink is not None:
        in_specs = [pl.BlockSpec(memory_space=pltpu.SMEM)] + in_specs
        args = [sink] + args
    return pl.pallas_call(
        kern,
        grid=(L // tq,),
        in_specs=in_specs,
        out_specs=pl.BlockSpec((tq, qw), lambda n: (n, 0)),
        out_shape=jax.ShapeDtypeStruct((L, qw), BF16),
        compiler_params=pltpu.CompilerParams(
            dimension_semantics=("arbitrary",), vmem_limit_bytes=VMEM_LIMIT),
        name="band_attention",
    )(*args)


def _t5_bucket(rel):
    half = T5_BUCKETS // 2
    exact = half // 2
    n = jnp.abs(rel)
    big = exact + (jnp.log(jnp.maximum(n, exact).astype(F32) / exact)
                   / math.log(T5_MAX_DIST / exact) * (half - exact)).astype(jnp.int32)
    big = jnp.minimum(big, half - 1)
    return jnp.where(rel > 0, half, 0) + jnp.where(n < exact, n, big)


def _toeplitz_kernel(rv_ref, o_ref, *, n_diag):
    w, tq = o_ref.shape[1:]
    blocks = []
    for d in range(n_diag):
        s = LANES * (n_diag - 1 - d)
        x = jnp.broadcast_to(rv_ref[0, :, s:s + 2 * LANES], (LANES, 2 * LANES))
        blocks.append(pltpu.roll(x, LANES, 1, stride=1, stride_axis=0)[:, :LANES])
    for cb in range(w // LANES):
        for ib in range(tq // LANES):
            o_ref[0, cb * LANES:(cb + 1) * LANES, ib * LANES:(ib + 1) * LANES] = (
                blocks[cb - ib + tq // LANES - 1])


def _toeplitz(v, tq, w):
    heads, n_rel = v.shape
    n_diag = w // LANES + tq // LANES - 1
    assert w % LANES == 0 and tq % LANES == 0 and n_rel == LANES * n_diag + LANES - 1
    rv = jnp.pad(v, ((0, 0), (0, 1)))[:, None, ::-1]
    return pl.pallas_call(
        functools.partial(_toeplitz_kernel, n_diag=n_diag),
        grid=(heads,),
        in_specs=[pl.BlockSpec((1, 1, rv.shape[-1]), lambda h: (h, 0, 0))],
        out_specs=pl.BlockSpec((1, w, tq), lambda h: (h, 0, 0)),
        out_shape=jax.ShapeDtypeStruct((heads, w, tq), F32),
        compiler_params=pltpu.CompilerParams(
            dimension_semantics=("arbitrary",), vmem_limit_bytes=VMEM_LIMIT),
        name="toeplitz",
    )(rv)


def _t5_table(t5_table, head_lo, tq, r_lo, w_keys, mult):
    rel = np.arange(-r_lo - (tq - 1), w_keys - r_lo)
    m = mult(rel)
    b = t5_table.astype(F32)[:, head_lo:head_lo + GROUP_HEADS][_t5_bucket(jnp.asarray(rel, jnp.int32))]
    logm = np.log(np.maximum(m, 1)).astype(np.float32)
    vec = jnp.where(jnp.asarray(m > 0)[:, None], (b + logm[:, None]) * LOG2E, NEG).T
    return _toeplitz(vec, tq, w_keys)[None]


def _window_mult(rel):
    return (np.abs(rel) <= A_RADIUS).astype(np.int32)


def _dilated_mult(rel):
    m = np.zeros(rel.shape, np.int32)
    for window, dil in D_CONFIGS:
        m += ((rel % dil == 0) & (np.abs(rel) <= window // 2)).astype(np.int32)
    return m


NA_TILE_ROWS = 4
NA_WIN_ROWS = NA_TILE_ROWS + NA_ROWS
NA_TQ = NA_TILE_ROWS * GRID_W
NA_W = NA_WIN_ROWS * GRID_W


def _na_row_index(variant, j, i):
    dr, ok = ((j - i, j < NA_ROWS),
              (j - NA_ROWS // 2 - i, i <= j < i + NA_ROWS),
              (j - NA_ROWS - i, j >= NA_TILE_ROWS))[variant]
    return dr + NA_ROWS - 1 if ok else None


def _na_table_kernel(rp_ref, o_ref):
    kc = lax.broadcasted_iota(jnp.int32, (GRID_W, LANES), 0)
    lane = lax.broadcasted_iota(jnp.int32, (GRID_W, LANES), 1)
    col_start = jnp.clip(lane % GRID_W - NA_COLS // 2, 0, GRID_W - NA_COLS)
    col_ok = (kc >= col_start) & (kc < col_start + NA_COLS)
    neg = jnp.full((GRID_W, LANES), NEG, F32)
    cache = {}

    def half_block(d, side):
        if d is None:
            return neg
        if (d, side) not in cache:
            x = jnp.broadcast_to(rp_ref[0, 0, d:d + 1, :], (GRID_W, LANES))
            y = pltpu.roll(x, GRID_W * (1 - side), 1, stride=1, stride_axis=0)
            cache[d, side] = jnp.where(col_ok, y, NEG)
        return cache[d, side]

    for variant in range(3):
        for j in range(NA_WIN_ROWS):
            for ip in range(NA_TILE_ROWS // 2):
                left = half_block(_na_row_index(variant, j, 2 * ip), 0)
                right = half_block(_na_row_index(variant, j, 2 * ip + 1), 1)
                o_ref[0, variant, 0, j * GRID_W:(j + 1) * GRID_W, ip * LANES:(ip + 1) * LANES] = (
                    jnp.where(lane < GRID_W, left, right))


def _na_tables(rpb):
    depth, heads, n_dr, n_dc = rpb.shape
    assert n_dr == NA_WIN_ROWS + NA_TILE_ROWS - 1 and 2 * GRID_W == LANES
    front = GRID_W - NA_COLS
    rp = jnp.pad(rpb.astype(F32) * LOG2E, ((0, 0), (0, 0), (0, 1), (front, LANES - n_dc - front)))
    rp = rp[..., ::-1]
    return pl.pallas_call(
        _na_table_kernel,
        grid=(depth, heads),
        in_specs=[pl.BlockSpec((1, 1, n_dr + 1, LANES), lambda l, h: (l, h, 0, 0))],
        out_specs=pl.BlockSpec((1, 3, 1, NA_W, NA_TQ), lambda l, h: (l, 0, h, 0, 0)),
        out_shape=jax.ShapeDtypeStruct((depth, 3, heads, NA_W, NA_TQ), F32),
        compiler_params=pltpu.CompilerParams(
            dimension_semantics=("arbitrary", "arbitrary"), vmem_limit_bytes=VMEM_LIMIT),
        name="na_tables",
    )(rp)


def _swap_pairs(x):
    n = x.shape[-1]
    lane = lax.broadcasted_iota(jnp.int32, x.shape, x.ndim - 1)
    return jnp.where(lane % 2 == 0, pltpu.roll(x, n - 1, x.ndim - 1), pltpu.roll(x, 1, x.ndim - 1))


def _head_rms(x, ones_bd, w):
    sq = x * x
    hi = sq.astype(BF16)
    lo = (sq - hi.astype(F32)).astype(BF16)
    ms = (jnp.dot(hi, ones_bd, preferred_element_type=F32)
          + jnp.dot(lo, ones_bd, preferred_element_type=F32)) * (1.0 / HEAD_DIM)
    return x * lax.rsqrt(ms + EPS) * w


def _cprep_kernel(q_ref, k_ref, v_ref, cos_ref, sin_ref, qw_ref, kw_ref, bd_ref,
                  qt_ref, k2_ref, vt_ref):
    cos = cos_ref[...]
    sin = sin_ref[...]
    q = _head_rms(q_ref[...].astype(F32), bd_ref[...], qw_ref[...])
    reps = q.shape[1] // LANES
    q = q * jnp.concatenate([cos] * reps, axis=1) + _swap_pairs(q) * jnp.concatenate([sin] * reps, axis=1)
    qt_ref[...] = (q * Q_SCALE).T.astype(qt_ref.dtype)
    k = _head_rms(k_ref[...].astype(F32), bd_ref[:LANES, :LANES], kw_ref[...])
    k = (k * cos + _swap_pairs(k) * sin).astype(k2_ref.dtype)
    vt = v_ref[...].astype(F32).T.astype(vt_ref.dtype)
    ones = jnp.ones((V_ROWS - HEAD_DIM, vt.shape[1]), vt_ref.dtype)
    tk = vt_ref.shape[-1]
    for g in range(C_KV_HEADS):
        k2_ref[g] = k[:, g * HEAD_DIM:(g + 1) * HEAD_DIM]
        vg = jnp.concatenate([vt[g * HEAD_DIM:(g + 1) * HEAD_DIM], ones], axis=0)
        for u in range(vt_ref.shape[1]):
            vt_ref[g, u] = vg[:, u * tk:(u + 1) * tk]


def _cprep(proj, cos2, sin2, qw, kw, ones_bd, tk, t=1024):
    L = proj.shape[0]
    vec = lambda width: pl.BlockSpec((1, width), lambda n: (0, 0))
    return pl.pallas_call(
        _cprep_kernel,
        grid=(L // t,),
        in_specs=[_piece("q_c", t), _piece("k_c", t), _piece("v_c", t),
                  pl.BlockSpec((t, LANES), lambda n: (n, 0)),
                  pl.BlockSpec((t, LANES), lambda n: (n, 0)),
                  vec(GROUP_WIDTH), vec(LANES),
                  pl.BlockSpec((GROUP_WIDTH, GROUP_WIDTH), lambda n: (0, 0))],
        out_specs=[pl.BlockSpec((GROUP_WIDTH, t), lambda n: (0, n)),
                   pl.BlockSpec((C_KV_HEADS, t, HEAD_DIM), lambda n: (0, n, 0)),
                   pl.BlockSpec((C_KV_HEADS, t // tk, V_ROWS, tk), lambda n: (0, n, 0, 0))],
        out_shape=[jax.ShapeDtypeStruct((GROUP_WIDTH, L), BF16),
                   jax.ShapeDtypeStruct((C_KV_HEADS, L, HEAD_DIM), BF16),
                   jax.ShapeDtypeStruct((C_KV_HEADS, L // tk, V_ROWS, tk), BF16)],
        compiler_params=pltpu.CompilerParams(
            dimension_semantics=("arbitrary",), vmem_limit_bytes=VMEM_LIMIT),
        name="dense_prep",
    )(proj, proj, proj, cos2, sin2, qw, kw, ones_bd)


def _flash_kernel(qt_ref, k_ref, vt_ref, o_ref, m_scr, acc_scr, st_scr, *, tk, heads, unroll, ahead):
    n_kv = k_ref.shape[1] // tk
    m_scr[...] = jnp.full(m_scr.shape, NEG, F32)
    acc_scr[...] = jnp.zeros(acc_scr.shape, F32)

    def scores(j, h):
        k = k_ref[0, pl.ds(pl.multiple_of(j * tk, tk), tk), :]
        qt = qt_ref[h * HEAD_DIM:(h + 1) * HEAD_DIM, :]
        return jnp.dot(k, qt, preferred_element_type=F32)

    n_items = unroll * heads

    def item_scores(t, idx):
        j = t * unroll + idx // heads
        return scores(jnp.minimum(j, n_kv - 1), idx % heads)

    for a in range(ahead):
        st_scr[a] = item_scores(0, a)

    def body(t, carry):
        pending = [st_scr[a] for a in range(ahead)]
        for idx in range(n_items):
            pending.append(item_scores(t, idx + ahead))
            st = pending.pop(0)
            h = idx % heads
            vt = vt_ref[0, t * unroll + idx // heads]
            for q0 in range(0, st.shape[1], C_SPLIT):
                cols = slice(q0, q0 + C_SPLIT)
                m = m_scr[h, :, cols]
                m_new = jnp.maximum(m, jnp.max(st[:, cols], axis=0, keepdims=True))
                pt = jnp.exp2(st[:, cols] - m_new).astype(BF16)
                acc_scr[h, :, cols] = (jnp.exp2(m - m_new) * acc_scr[h, :, cols]
                                       + jnp.dot(vt, pt, preferred_element_type=F32))
                m_scr[h, :, cols] = m_new
        for a in range(ahead):
            st_scr[a] = pending[a]
        return carry

    lax.fori_loop(0, n_kv // unroll, body, 0)
    outs = [acc_scr[h, :HEAD_DIM] / acc_scr[h, HEAD_DIM:HEAD_DIM + 1] for h in range(heads)]
    o_ref[...] = jnp.concatenate(outs, axis=0).T.astype(o_ref.dtype)


def _flash(qt, k2, vt, tq):
    L = qt.shape[1]
    _, n_kv, v_rows, tk = vt.shape
    heads = GROUP_HEADS // C_KV_HEADS
    qrows = heads * HEAD_DIM
    return pl.pallas_call(
        functools.partial(_flash_kernel, tk=tk, heads=heads, unroll=math.gcd(C_UNROLL, n_kv),
                          ahead=C_AHEAD),
        grid=(C_KV_HEADS, L // tq),
        in_specs=[pl.BlockSpec((qrows, tq), lambda g, n: (g, n)),
                  pl.BlockSpec((1, L, HEAD_DIM), lambda g, n: (g, 0, 0)),
                  pl.BlockSpec((1, n_kv, v_rows, tk), lambda g, n: (g, 0, 0, 0))],
        out_specs=pl.BlockSpec((tq, qrows), lambda g, n: (n, g)),
        out_shape=jax.ShapeDtypeStruct((L, GROUP_WIDTH), BF16),
        scratch_shapes=[pltpu.VMEM((heads, 1, tq), F32), pltpu.VMEM((heads, v_rows, tq), F32),
                        pltpu.VMEM((C_AHEAD, tk, tq), F32)],
        compiler_params=pltpu.CompilerParams(
            dimension_semantics=("arbitrary", "arbitrary"), vmem_limit_bytes=VMEM_LIMIT),
        name="dense_flash",
    )(qt, k2, vt)


def _outproj_kernel(*refs, final):
    if final:
        x_ref, ya, yb, yc, yd, za, zb, zc, zd, w_ref, g_ref, fw_ref, o_ref = refs
    else:
        x_ref, ya, yb, yc, yd, za, zb, zc, zd, w_ref, g_ref, nw_ref, sc_ref, sh_ref, o_ref, h_ref = refs
    sub = x_ref.shape[0] // OUT_SPLIT
    for r0 in range(0, x_ref.shape[0], sub):
        rows = slice(r0, r0 + sub)
        acc = jnp.zeros((sub, x_ref.shape[1]), F32)
        for gi, (y_ref, z_ref) in enumerate(zip((ya, yb, yc, yd), (za, zb, zc, zd))):
            z = z_ref[rows, :].astype(F32)
            u = (y_ref[rows, :].astype(F32) * (z * jax.nn.sigmoid(z))).astype(BF16)
            acc = acc + jnp.dot(u, w_ref[gi * GROUP_WIDTH:(gi + 1) * GROUP_WIDTH, :],
                                preferred_element_type=F32)
        xn = x_ref[rows, :] + g_ref[...] * acc
        if final:
            o_ref[rows, :] = (xn * lax.rsqrt(jnp.mean(xn * xn, axis=-1, keepdims=True) + EPS)
                              * fw_ref[...])
        else:
            o_ref[rows, :] = xn
            h_ref[rows, :] = _modulated_norm(xn, nw_ref[...], sc_ref[...], sh_ref[...])


def _outproj(x, ys, proj, w_bf16, layer, gate, vecs):
    L, d = x.shape
    tm = min(512, L)
    final = len(vecs) == 1
    row = lambda width: pl.BlockSpec((tm, width), lambda i: (i, 0))
    vec = pl.BlockSpec((1, d), lambda i: (0, 0))
    in_specs = ([row(d)] + [row(GROUP_WIDTH)] * 4 + [_piece(z, tm) for z in ("z_a", "z_b", "z_c", "z_d")]
                + [pl.BlockSpec((None,) + w_bf16.shape[1:], lambda i: (layer, 0, 0)), vec]
                + [vec] * len(vecs))
    x_out = jax.ShapeDtypeStruct((L, d), F32)
    return pl.pallas_call(
        functools.partial(_outproj_kernel, final=final),
        grid=(L // tm,),
        in_specs=in_specs,
        out_specs=row(d) if final else [row(d), row(d)],
        out_shape=x_out if final else [x_out, jax.ShapeDtypeStruct((L, d), BF16)],
        compiler_params=pltpu.CompilerParams(
            dimension_semantics=("arbitrary",), vmem_limit_bytes=VMEM_LIMIT),
        name="outproj",
    )(x, *ys, proj, proj, proj, proj, w_bf16, gate, *vecs)


def _rope_tables(L):
    t = np.arange(L)
    axis_dim = HEAD_DIM // 2
    inv = jnp.asarray(ROPE_THETA, F32) ** (-jnp.arange(0, axis_dim, 2, dtype=F32) / axis_dim)
    row = jnp.asarray(t // GRID_W, F32)
    col = jnp.asarray(t % GRID_W, F32)
    ang = jnp.concatenate([row[:, None] * inv[None], col[:, None] * inv[None]], axis=-1)
    cos = jnp.repeat(jnp.cos(ang), 2, axis=-1)
    sin = jnp.repeat(jnp.sin(ang), 2, axis=-1) * jnp.asarray(np.tile([-1.0, 1.0], HEAD_DIM // 2), F32)
    return jnp.tile(cos, (1, LANES // HEAD_DIM)), jnp.tile(sin, (1, LANES // HEAD_DIM))


A_TQ = 256
A_CHUNK = 256
NA_CHUNK = 256
BAND_LANES = 2
NA_LANES = 4
D_LANES = 4
D_TQ = 256
D_REACH = D_CONFIGS[-1][0] // 2
C_TQ = 512
C_TK = 256
C_UNROLL = 16
C_AHEAD = 2
C_SPLIT = 256
OUT_SPLIT = 2


def _table_a(t5_table):
    return _t5_table(t5_table, 0, A_TQ, A_RADIUS, A_TQ + 2 * A_RADIUS, _window_mult)


D_DIL = D_CONFIGS[-1][1]
D_NEAR = D_CONFIGS[-2][0] // 2
D_FAR_ROWS = (D_REACH - D_NEAR) // D_DIL


def _dil_kernel(qt_ref, kn_ref, k16_ref, vt_ref, tabn_ref, tabf_ref, o_ref, *, tq, L, lanes):
    n = pl.program_id(0)
    t0 = n * tq
    heads = qt_ref.shape[0] // HEAD_DIM
    w_near = kn_ref.shape[0]
    far = D_FAR_ROWS
    n_chunks = w_near // tq
    w_far_keys = far * D_DIL
    lane = lax.broadcasted_iota(jnp.int32, (1, tq), 1)
    qmask = (lax.broadcasted_iota(jnp.int32, (D_DIL * HEAD_DIM, tq), 0) // HEAD_DIM) == (lane % D_DIL)
    pmask = (lax.broadcasted_iota(jnp.int32, (w_far_keys, tq), 0) % D_DIL) == (lane % D_DIL)

    def attend(kb_near, kb_far):
        def scores(h):
            cols = slice(h * HEAD_DIM, (h + 1) * HEAD_DIM)
            q = qt_ref[cols, :]
            near = []
            for c in range(n_chunks):
                rows = slice(c * tq, (c + 1) * tq)
                st = jnp.dot(kn_ref[rows, cols], q, preferred_element_type=F32) + tabn_ref[h, rows, :]
                near.append(st if kb_near is None else st + kb_near[rows])
            qbd = jnp.where(qmask, jnp.concatenate([q] * D_DIL, axis=0), jnp.zeros((), BF16))
            sides = []
            for r0 in (0, k16_ref.shape[0] - far):
                kcat = jnp.concatenate(
                    [k16_ref[r0:r0 + far, r * GROUP_WIDTH + h * HEAD_DIM:r * GROUP_WIDTH + (h + 1) * HEAD_DIM]
                     for r in range(D_DIL)], axis=1)
                sides.append(jnp.dot(kcat, qbd, preferred_element_type=F32))
            sf = jnp.concatenate(sides, axis=0) + tabf_ref[h]
            if kb_far is not None:
                sf = sf + kb_far
            m = jnp.max(sf, axis=0, keepdims=True)
            for st in near:
                m = jnp.maximum(m, jnp.max(st, axis=0, keepdims=True))
            return near, sf, m

        def values(h, near, sf, m):
            v = vt_ref[h * V_ROWS:(h + 1) * V_ROWS, :]
            acc = None
            for c, st in enumerate(near):
                pt = jnp.exp2(st - m).astype(BF16)
                lo = w_far_keys + c * tq
                pv = jnp.dot(v[:, lo:lo + tq], pt, preferred_element_type=F32)
                acc = pv if acc is None else acc + pv
            pf = jnp.exp2(sf - m)
            for side, lo in ((0, 0), (1, w_far_keys + w_near)):
                rep = jnp.concatenate(
                    [jnp.broadcast_to(pf[side * far + a:side * far + a + 1, :], (D_DIL, tq))
                     for a in range(far)], axis=0)
                pbd = jnp.where(pmask, rep, 0.0).astype(BF16)
                acc = acc + jnp.dot(v[:, lo:lo + w_far_keys], pbd, preferred_element_type=F32)
            return acc[:HEAD_DIM] / acc[HEAD_DIM:HEAD_DIM + 1]

        outs = []
        groups = [range(g, g + lanes) for g in range(0, heads, lanes)]
        cur = {h: scores(h) for h in groups[0]}
        for gi, grp in enumerate(groups):
            nxt = {h: scores(h) for h in groups[gi + 1]} if gi + 1 < len(groups) else {}
            outs.extend(values(h, *cur[h]) for h in grp)
            cur = nxt
        o_ref[...] = jnp.concatenate(outs, axis=0).T.astype(o_ref.dtype)

    inside = (t0 - D_REACH >= 0) & (t0 + tq + D_REACH <= L)

    @pl.when(inside)
    def _():
        attend(None, None)

    @pl.when(jnp.logical_not(inside))
    def _():
        kpos = t0 - D_NEAR + lax.broadcasted_iota(jnp.int32, (w_near, 1), 0)
        kb_near = jnp.where((kpos >= 0) & (kpos < L), 0.0, NEG).astype(F32)
        a = lax.broadcasted_iota(jnp.int32, (2 * far, 1), 0)
        base = jnp.where(a < far, t0 - D_REACH + D_DIL * a, t0 + tq + D_NEAR + D_DIL * (a - far))
        kb_far = jnp.where((base >= 0) & (base < L), 0.0, NEG).astype(F32)
        attend(kb_near, kb_far)


def _far_table(t5_table, tq):
    a = np.arange(D_FAR_ROWS)[:, None]
    b = np.arange(tq // D_DIL)[None, :]
    rel = np.concatenate([-D_REACH + D_DIL * (a - b), tq + D_NEAR + D_DIL * (a - b)], axis=0)
    valid = np.abs(rel) <= D_REACH
    bias = t5_table.astype(F32)[:, GROUP_HEADS:2 * GROUP_HEADS][_t5_bucket(jnp.asarray(rel, jnp.int32))]
    tab = jnp.where(jnp.asarray(valid)[..., None], bias * LOG2E, NEG)
    return jnp.repeat(jnp.transpose(tab, (2, 0, 1)), D_DIL, axis=-1)


def _near_table_d(t5_table, tq):
    return _t5_table(t5_table, GROUP_HEADS, tq, D_NEAR, tq + 2 * D_NEAR, _dilated_mult)[0]


def _dilated_attention(qt, k_pad, k16, vt_pad, tab_near, tab_far, *, tq, pad, lanes):
    qw, L = qt.shape
    w_near = tq + 2 * D_NEAR
    w_all = tq + 2 * D_REACH
    return pl.pallas_call(
        functools.partial(_dil_kernel, tq=tq, L=L, lanes=lanes),
        grid=(L // tq,),
        in_specs=[pl.BlockSpec((qw, tq), lambda n: (0, n)),
                  pl.BlockSpec((pl.Element(w_near), pl.Element(k_pad.shape[1])),
                               lambda n: (pl.multiple_of(n * tq + pad - D_NEAR, LANES), 0)),
                  pl.BlockSpec((pl.Element(w_all // D_DIL), pl.Element(k16.shape[1])),
                               lambda n: (pl.multiple_of((n * tq + pad - D_REACH) // D_DIL,
                                                         BF16_SUBLANES), 0)),
                  pl.BlockSpec((pl.Element(vt_pad.shape[0]), pl.Element(w_all)),
                               lambda n: (0, pl.multiple_of(n * tq + pad - D_REACH, LANES))),
                  pl.BlockSpec(tab_near.shape, lambda n: (0, 0, 0), pipeline_mode=pl.Buffered(1)),
                  pl.BlockSpec(tab_far.shape, lambda n: (0, 0, 0), pipeline_mode=pl.Buffered(1))],
        out_specs=pl.BlockSpec((tq, qw), lambda n: (n, 0)),
        out_shape=jax.ShapeDtypeStruct((L, qw), BF16),
        compiler_params=pltpu.CompilerParams(
            dimension_semantics=("arbitrary",), vmem_limit_bytes=VMEM_LIMIT),
        name="dilated_attention",
    )(qt, k_pad, k16, vt_pad, tab_near, tab_far)


PREP_T = 1024
assert PREP_T >= D_REACH


def _seq_window(L, tq, r_lo):
    return lambda n: (n * tq + PREP_T - r_lo, (n * tq - r_lo, 0, L))


def _mixer_a(proj, qt, tab_a, sink):
    L = proj.shape[0]
    k_pad, vt_pad = _kv_prep(proj, "k_a", "v_a", LANES, PREP_T, PREP_T)
    return _band_attention(qt, k_pad, vt_pad, tab_a, chunk=A_CHUNK, lanes=BAND_LANES,
                           group=GROUP_HEADS // A_KV_HEADS, window=_seq_window(L, A_TQ, A_RADIUS),
                           sink=sink.astype(F32))


def _mixer_b(proj, qt, tabs_b, layer):
    L = proj.shape[0]
    rows = L // GRID_W
    n_tiles = rows // NA_TILE_ROWS
    k, vt = _kv_prep(proj, "k_b", "v_b", GROUP_WIDTH, 0, PREP_T)
    window = lambda n: (jnp.clip(n * NA_TILE_ROWS - NA_ROWS // 2, 0, rows - NA_WIN_ROWS) * GRID_W, None)
    variant = lambda n: jnp.where(n == 0, 0, jnp.where(n == n_tiles - 1, 2, 1))
    return _band_attention(qt, k, vt, tabs_b, chunk=NA_CHUNK, lanes=NA_LANES, group=1,
                           window=window, variant=variant, layer=layer)


def _mixer_c(proj, q_norm_w, k_norm_w):
    L = proj.shape[0]
    cos2, sin2 = _rope_tables(L)
    ones_bd = jnp.asarray(np.kron(np.eye(GROUP_HEADS), np.ones((HEAD_DIM, HEAD_DIM))), BF16)
    qt, k2, vt = _cprep(proj, cos2, sin2, jnp.tile(q_norm_w.astype(F32), GROUP_HEADS)[None],
                        jnp.tile(k_norm_w.astype(F32), LANES // HEAD_DIM)[None], ones_bd, C_TK)
    return _flash(qt, k2, vt, C_TQ)


def _mixer_d(proj, qt, tabs_d):
    k_pad, vt_pad, k16 = _kv_prep(proj, "k_d", "v_d", GROUP_WIDTH, PREP_T, PREP_T, group=D_DIL)
    return _dilated_attention(qt, k_pad, k16, vt_pad, *tabs_d, tq=D_TQ, pad=PREP_T, lanes=D_LANES)


def kernel(x, c, w_ada, b_ada, norm_w, w_in, w_out, attn_sink, na_rpb, q_norm_w, k_norm_w,
           t5_table, final_norm_w):
    B, L, D = x.shape
    assert B == 1 and L % 1024 == 0 and L // GRID_W >= NA_WIN_ROWS
    depth = w_ada.shape[0]
    x = x[0]

    mod = _ada_mod(jnp.broadcast_to(c, (8, D)), w_ada, b_ada)[:, 0:1, :]
    tab_a = _table_a(t5_table)
    tabs_d = (_near_table_d(t5_table, D_TQ), _far_table(t5_table, D_TQ))
    tabs_b = _na_tables(na_rpb)
    w_out_b = w_out.astype(BF16)
    shift, scale, gate = jnp.split(mod, 3, axis=-1)
    h = _norm(x, norm_w[0][None], scale[0], shift[0])
    for i in range(depth):
        proj = _inproj(h, w_in, i)
        qt_a, qt_b, qt_d = _qt_prep(proj, ("q_a", "q_b", "q_d"))
        ys = (_mixer_a(proj, qt_a, tab_a, attn_sink[i]),
              _mixer_b(proj, qt_b, tabs_b, i),
              _mixer_c(proj, q_norm_w[i], k_norm_w[i]),
              _mixer_d(proj, qt_d, tabs_d))
        if i + 1 < depth:
            x, h = _outproj(x, ys, proj, w_out_b, i, gate[i],
                            (norm_w[i + 1][None], scale[i + 1], shift[i + 1]))
        else:
            x = _outproj(x, ys, proj, w_out_b, i, gate[i], (final_norm_w[None],))
    return x[None]
```

```python
import functools
import math

import numpy as np
import jax
import jax.numpy as jnp
from jax import lax
from jax.experimental import pallas as pl
from jax.experimental.pallas import tpu as pltpu

HEAD_DIM = 64
GROUP_WIDTH = 512
GROUP_HEADS = 8
A_KV_HEADS = 2
A_RADIUS = 128
C_KV_HEADS = 2
ROPE_THETA = 10000.0
NA_ROWS = 8
NA_COLS = 16
D_CONFIGS = ((128, 1), (512, 4), (2048, 16))
GRID_W = 64
T5_BUCKETS = 32
T5_MAX_DIST = 1024
EPS = 1e-6
NEG = -1e30

LANES = 128
BF16_SUBLANES = 16
V_ROWS = HEAD_DIM + BF16_SUBLANES
VMEM_LIMIT = 56 * 1024 * 1024

F32 = jnp.float32
BF16 = jnp.bfloat16

_SRC = {}
_off = 0
for _name, _w in (("q_a", 512), ("k_a", 128), ("v_a", 128), ("z_a", 512),
                  ("q_b", 512), ("k_b", 512), ("v_b", 512), ("z_b", 512),
                  ("q_c", 512), ("k_c", 128), ("v_c", 128), ("z_c", 512),
                  ("q_d", 512), ("k_d", 512), ("v_d", 512), ("z_d", 512)):
    _SRC[_name] = (_off, _w)
    _off += _w
IN_WIDTH = _off


def _piece(name, t, row_block=lambda n: n):
    off, width = _SRC[name]
    return pl.BlockSpec((pl.Element(t), pl.Element(width)),
                        lambda n: (pl.multiple_of(row_block(n) * t, t), off))


def _ada_kernel(c_ref, w_ref, b_ref, o_ref):
    c = c_ref[...]
    cond = c * jax.nn.sigmoid(c)
    o_ref[0] = jnp.dot(cond, w_ref[0], preferred_element_type=F32,
                       precision=lax.Precision.HIGHEST) + b_ref[0]


def _ada_mod(c8, w_ada, b_ada):
    depth, d, n3 = w_ada.shape
    tn = 1024
    return pl.pallas_call(
        _ada_kernel,
        grid=(depth, n3 // tn),
        in_specs=[pl.BlockSpec((8, d), lambda i, j: (0, 0)),
                  pl.BlockSpec((1, d, tn), lambda i, j: (i, 0, j)),
                  pl.BlockSpec((1, 1, tn), lambda i, j: (i, 0, j))],
        out_specs=pl.BlockSpec((1, 8, tn), lambda i, j: (i, 0, j)),
        out_shape=jax.ShapeDtypeStruct((depth, 8, n3), F32),
        compiler_params=pltpu.CompilerParams(
            dimension_semantics=("arbitrary", "arbitrary"), vmem_limit_bytes=VMEM_LIMIT),
        name="ada_mod",
    )(c8, w_ada, b_ada.reshape(depth, 1, n3))


def _modulated_norm(x, nw, scale, shift):
    y = x * lax.rsqrt(jnp.mean(x * x, axis=-1, keepdims=True) + EPS)
    return ((y * nw) * (1.0 + scale) + shift).astype(BF16)


def _norm_kernel(x_ref, nw_ref, sc_ref, sh_ref, h_ref):
    h_ref[...] = _modulated_norm(x_ref[...], nw_ref[...], sc_ref[...], sh_ref[...])


def _norm(x, nw, scale, shift):
    L, d = x.shape
    tm = min(512, L)
    vec = pl.BlockSpec((1, d), lambda i: (0, 0))
    return pl.pallas_call(
        _norm_kernel,
        grid=(L // tm,),
        in_specs=[pl.BlockSpec((tm, d), lambda i: (i, 0)), vec, vec, vec],
        out_specs=pl.BlockSpec((tm, d), lambda i: (i, 0)),
        out_shape=jax.ShapeDtypeStruct((L, d), BF16),
        compiler_params=pltpu.CompilerParams(
            dimension_semantics=("arbitrary",), vmem_limit_bytes=VMEM_LIMIT),
        name="norm",
    )(x, nw, scale, shift)


def _inproj_kernel(h_ref, w_ref, o_ref):
    o_ref[...] = jnp.dot(h_ref[...], w_ref[...].astype(BF16),
                         preferred_element_type=F32).astype(o_ref.dtype)


def _inproj(h, w, layer):
    L, d = h.shape
    n = w.shape[2]
    tm = min(2048, L)
    tn = 512
    return pl.pallas_call(
        _inproj_kernel,
        grid=(L // tm, n // tn),
        in_specs=[pl.BlockSpec((tm, d), lambda i, j: (i, 0)),
                  pl.BlockSpec((None, d, tn), lambda i, j: (layer, 0, j))],
        out_specs=pl.BlockSpec((tm, tn), lambda i, j: (i, j)),
        out_shape=jax.ShapeDtypeStruct((L, n), BF16),
        compiler_params=pltpu.CompilerParams(
            dimension_semantics=("arbitrary", "arbitrary"), vmem_limit_bytes=VMEM_LIMIT),
        name="inproj",
    )(h, w)


LOG2E = math.log2(math.e)
Q_SCALE = HEAD_DIM ** -0.5 * LOG2E


def _qt_prep_kernel(*refs):
    n = len(refs) // 2
    for q_ref, qt_ref in zip(refs[:n], refs[n:]):
        qt_ref[...] = (q_ref[...].astype(F32) * Q_SCALE).T.astype(qt_ref.dtype)


def _qt_prep(proj, names, t=1024):
    L = proj.shape[0]
    return pl.pallas_call(
        _qt_prep_kernel,
        grid=(L // t,),
        in_specs=[_piece(name, t) for name in names],
        out_specs=[pl.BlockSpec((GROUP_WIDTH, t), lambda n: (0, n)) for _ in names],
        out_shape=[jax.ShapeDtypeStruct((GROUP_WIDTH, L), BF16) for _ in names],
        compiler_params=pltpu.CompilerParams(
            dimension_semantics=("arbitrary",), vmem_limit_bytes=VMEM_LIMIT),
        name="qt_prep",
    )(*([proj] * len(names)))


def _kv_prep_kernel(k_ref, v_ref, kp_ref, vt_ref, *rest, pad_blocks, tok_blocks, group):
    b = pl.program_id(0)
    is_token = (b >= pad_blocks) & (b < pad_blocks + tok_blocks)

    @pl.when(is_token)
    def _():
        kp_ref[...] = k_ref[...]
        if group:
            kg_ref, k32_scr = rest
            width = k_ref.shape[1]
            for c in range(width // LANES):
                k32_scr[c] = k_ref[:, c * LANES:(c + 1) * LANES].astype(F32)
            for r in range(group):
                for c in range(width // LANES):
                    kg_ref[:, r * width + c * LANES:r * width + (c + 1) * LANES] = (
                        k32_scr[c, pl.ds(r, kg_ref.shape[0], stride=group), :].astype(kg_ref.dtype))
        vt = v_ref[...].astype(F32).T.astype(vt_ref.dtype)
        ones = jnp.ones((V_ROWS - HEAD_DIM, vt.shape[1]), vt_ref.dtype)
        for g in range(vt.shape[0] // HEAD_DIM):
            vt_ref[g * V_ROWS:(g + 1) * V_ROWS, :] = jnp.concatenate(
                [vt[g * HEAD_DIM:(g + 1) * HEAD_DIM], ones], axis=0)

    @pl.when(jnp.logical_not(is_token))
    def _():
        kp_ref[...] = jnp.zeros(kp_ref.shape, kp_ref.dtype)
        vt_ref[...] = jnp.zeros(vt_ref.shape, vt_ref.dtype)
        if group:
            rest[0][...] = jnp.zeros(rest[0].shape, rest[0].dtype)


def _kv_prep(proj, k_name, v_name, width, pad, t, group=0):
    L = proj.shape[0]
    assert pad % t == 0 and L % t == 0 and width % LANES == 0
    pad_blocks, tok_blocks = pad // t, L // t
    lp = L + 2 * pad
    assert _SRC[k_name][1] == width and _SRC[v_name][1] == width
    tok = lambda b: jnp.clip(b - pad_blocks, 0, tok_blocks - 1)
    return pl.pallas_call(
        functools.partial(_kv_prep_kernel, pad_blocks=pad_blocks, tok_blocks=tok_blocks, group=group),
        grid=(lp // t,),
        in_specs=[_piece(k_name, t, tok), _piece(v_name, t, tok)],
        out_specs=[pl.BlockSpec((t, width), lambda b: (b, 0)),
                   pl.BlockSpec((width // HEAD_DIM * V_ROWS, t), lambda b: (0, b))]
        + ([pl.BlockSpec((t // group, group * width), lambda b: (b, 0))] if group else []),
        out_shape=[jax.ShapeDtypeStruct((lp, width), BF16),
                   jax.ShapeDtypeStruct((width // HEAD_DIM * V_ROWS, lp), BF16)]
        + ([jax.ShapeDtypeStruct((lp // group, group * width), BF16)] if group else []),
        scratch_shapes=[pltpu.VMEM((width // LANES, t, LANES), F32)] if group else [],
        compiler_params=pltpu.CompilerParams(
            dimension_semantics=("arbitrary",), vmem_limit_bytes=VMEM_LIMIT),
        name="kv_prep",
    )(proj, proj)


def _band_kernel(*refs, w_keys, chunk, lanes, heads, group, use_sink, window, variant):
    if use_sink:
        sink_ref, qt_ref, k_ref, vt_ref, tab_ref, o_ref = refs
    else:
        qt_ref, k_ref, vt_ref, tab_ref, o_ref = refs
    n = pl.program_id(0)
    _, pos = window(n)
    kwin = k_ref[...]
    vwin = vt_ref[...]
    var = variant(n)

    n_chunks = w_keys // chunk

    def attend(kbias):
        def scores(h, c):
            kv = h // group
            rows = slice(c * chunk, (c + 1) * chunk)
            st = jnp.dot(kwin[rows, kv * HEAD_DIM:(kv + 1) * HEAD_DIM],
                         qt_ref[h * HEAD_DIM:(h + 1) * HEAD_DIM, :],
                         preferred_element_type=F32) + tab_ref[var, h, rows, :]
            return st if kbias is None else st + kbias[rows]

        def col_max(m, st):
            cm = jnp.max(st, axis=0, keepdims=True)
            return cm if m is None else jnp.maximum(m, cm)

        def finish(h, acc, m):
            o, l = acc[:HEAD_DIM], acc[HEAD_DIM:HEAD_DIM + 1]
            if use_sink:
                sk = sink_ref[h] * LOG2E
                m2 = jnp.maximum(m, sk)
                a = jnp.exp2(m - m2)
                return o * (a / (l * a + jnp.exp2(sk - m2)))
            return o / l

        outs = []
        groups = [range(g, g + lanes) for g in range(0, heads, lanes)]
        cur = {h: [] for h in groups[0]}
        m = {h: None for h in groups[0]}
        for c in range(n_chunks):
            for h in groups[0]:
                cur[h].append(scores(h, c))
                m[h] = col_max(m[h], cur[h][-1])
        for gi, grp in enumerate(groups):
            following = groups[gi + 1] if gi + 1 < len(groups) else ()
            nxt = {h: [] for h in following}
            m_next = {h: None for h in following}
            acc = {h: None for h in grp}
            for c in range(n_chunks):
                for h in following:
                    nxt[h].append(scores(h, c))
                    m_next[h] = col_max(m_next[h], nxt[h][-1])
                for h in grp:
                    kv = h // group
                    pt = jnp.exp2(cur[h][c] - m[h]).astype(BF16)
                    pv = jnp.dot(vwin[kv * V_ROWS:(kv + 1) * V_ROWS, c * chunk:(c + 1) * chunk], pt,
                                 preferred_element_type=F32)
                    acc[h] = pv if acc[h] is None else acc[h] + pv
            outs.extend(finish(h, acc[h], m[h]) for h in grp)
            cur, m = nxt, m_next
        o_ref[...] = jnp.concatenate(outs, axis=0).T.astype(o_ref.dtype)

    if pos is None:
        attend(None)
    else:
        first, lo, hi = pos
        inside = (first >= lo) & (first + w_keys <= hi)

        @pl.when(inside)
        def _():
            attend(None)

        @pl.when(jnp.logical_not(inside))
        def _():
            kpos = first + lax.broadcasted_iota(jnp.int32, (w_keys, 1), 0)
            attend(jnp.where((kpos >= lo) & (kpos < hi), 0.0, NEG).astype(F32))


def _band_attention(qt, k_pad, vt_pad, table, *, chunk, lanes, group, window, variant=lambda n: 0,
                    sink=None, layer=None):
    qw, L = qt.shape
    heads = qw // HEAD_DIM
    w_keys, tq = table.shape[-2:]
    assert w_keys % chunk == 0
    kern = functools.partial(_band_kernel, w_keys=w_keys, chunk=chunk, lanes=lanes, heads=heads, group=group,
                             use_sink=sink is not None, window=window, variant=variant)
    start = lambda n: pl.multiple_of(window(n)[0], LANES)
    if layer is None:
        table_spec = pl.BlockSpec(table.shape, lambda n: (0, 0, 0, 0), pipeline_mode=pl.Buffered(1))
    else:
        table_spec = pl.BlockSpec((None,) + table.shape[1:], lambda n: (layer, 0, 0, 0, 0),
                                  pipeline_mode=pl.Buffered(1))
    in_specs = [pl.BlockSpec((qw, tq), lambda n: (0, n)),
                pl.BlockSpec((pl.Element(w_keys), pl.Element(k_pad.shape[1])), lambda n: (start(n), 0)),
                pl.BlockSpec((pl.Element(vt_pad.shape[0]), pl.Element(w_keys)), lambda n: (0, start(n))),
                table_spec]
    args = [qt, k_pad, vt_pad, table]
    if sink is not None:
        in_specs = [pl.BlockSpec(memory_space=pltpu.SMEM)] + in_specs
        args = [sink] + args
    return pl.pallas_call(
        kern,
        grid=(L // tq,),
        in_specs=in_specs,
        out_specs=pl.BlockSpec((tq, qw), lambda n: (n, 0)),
        out_shape=jax.ShapeDtypeStruct((L, qw), BF16),
        compiler_params=pltpu.CompilerParams(
            dimension_semantics=("arbitrary",), vmem_limit_bytes=VMEM_LIMIT),
        name="band_attention",
    )(*args)


def _t5_bucket(rel):
    half = T5_BUCKETS // 2
    exact = half // 2
    n = jnp.abs(rel)
    big = exact + (jnp.log(jnp.maximum(n, exact).astype(F32) / exact)
                   / math.log(T5_MAX_DIST / exact) * (half - exact)).astype(jnp.int32)
    big = jnp.minimum(big, half - 1)
    return jnp.where(rel > 0, half, 0) + jnp.where(n < exact, n, big)


def _toeplitz_kernel(rv_ref, o_ref, *, n_diag):
    w, tq = o_ref.shape[1:]
    blocks = []
    for d in range(n_diag):
        s = LANES * (n_diag - 1 - d)
        x = jnp.broadcast_to(rv_ref[0, :, s:s + 2 * LANES], (LANES, 2 * LANES))
        blocks.append(pltpu.roll(x, LANES, 1, stride=1, stride_axis=0)[:, :LANES])
    for cb in range(w // LANES):
        for ib in range(tq // LANES):
            o_ref[0, cb * LANES:(cb + 1) * LANES, ib * LANES:(ib + 1) * LANES] = (
                blocks[cb - ib + tq // LANES - 1])


def _toeplitz(v, tq, w):
    heads, n_rel = v.shape
    n_diag = w // LANES + tq // LANES - 1
    assert w % LANES == 0 and tq % LANES == 0 and n_rel == LANES * n_diag + LANES - 1
    rv = jnp.pad(v, ((0, 0), (0, 1)))[:, None, ::-1]
    return pl.pallas_call(
        functools.partial(_toeplitz_kernel, n_diag=n_diag),
        grid=(heads,),
        in_specs=[pl.BlockSpec((1, 1, rv.shape[-1]), lambda h: (h, 0, 0))],
        out_specs=pl.BlockSpec((1, w, tq), lambda h: (h, 0, 0)),
        out_shape=jax.ShapeDtypeStruct((heads, w, tq), F32),
        compiler_params=pltpu.CompilerParams(
            dimension_semantics=("arbitrary",), vmem_limit_bytes=VMEM_LIMIT),
        name="toeplitz",
    )(rv)


def _t5_table(t5_table, head_lo, tq, r_lo, w_keys, mult):
    rel = np.arange(-r_lo - (tq - 1), w_keys - r_lo)
    m = mult(rel)
    b = t5_table.astype(F32)[:, head_lo:head_lo + GROUP_HEADS][_t5_bucket(jnp.asarray(rel, jnp.int32))]
    logm = np.log(np.maximum(m, 1)).astype(np.float32)
    vec = jnp.where(jnp.asarray(m > 0)[:, None], (b + logm[:, None]) * LOG2E, NEG).T
    return _toeplitz(vec, tq, w_keys)[None]


def _window_mult(rel):
    return (np.abs(rel) <= A_RADIUS).astype(np.int32)


def _dilated_mult(rel):
    m = np.zeros(rel.shape, np.int32)
    for window, dil in D_CONFIGS:
        m += ((rel % dil == 0) & (np.abs(rel) <= window // 2)).astype(np.int32)
    return m


NA_TILE_ROWS = 4
NA_WIN_ROWS = NA_TILE_ROWS + NA_ROWS
NA_TQ = NA_TILE_ROWS * GRID_W
NA_W = NA_WIN_ROWS * GRID_W


def _na_row_index(variant, j, i):
    dr, ok = ((j - i, j < NA_ROWS),
              (j - NA_ROWS // 2 - i, i <= j < i + NA_ROWS),
              (j - NA_ROWS - i, j >= NA_TILE_ROWS))[variant]
    return dr + NA_ROWS - 1 if ok else None


def _na_table_kernel(rp_ref, o_ref):
    kc = lax.broadcasted_iota(jnp.int32, (GRID_W, LANES), 0)
    lane = lax.broadcasted_iota(jnp.int32, (GRID_W, LANES), 1)
    col_start = jnp.clip(lane % GRID_W - NA_COLS // 2, 0, GRID_W - NA_COLS)
    col_ok = (kc >= col_start) & (kc < col_start + NA_COLS)
    neg = jnp.full((GRID_W, LANES), NEG, F32)
    cache = {}

    def half_block(d, side):
        if d is None:
            return neg
        if (d, side) not in cache:
            x = jnp.broadcast_to(rp_ref[0, 0, d:d + 1, :], (GRID_W, LANES))
            y = pltpu.roll(x, GRID_W * (1 - side), 1, stride=1, stride_axis=0)
            cache[d, side] = jnp.where(col_ok, y, NEG)
        return cache[d, side]

    for variant in range(3):
        for j in range(NA_WIN_ROWS):
            for ip in range(NA_TILE_ROWS // 2):
                left = half_block(_na_row_index(variant, j, 2 * ip), 0)
                right = half_block(_na_row_index(variant, j, 2 * ip + 1), 1)
                o_ref[0, variant, 0, j * GRID_W:(j + 1) * GRID_W, ip * LANES:(ip + 1) * LANES] = (
                    jnp.where(lane < GRID_W, left, right))


def _na_tables(rpb):
    depth, heads, n_dr, n_dc = rpb.shape
    assert n_dr == NA_WIN_ROWS + NA_TILE_ROWS - 1 and 2 * GRID_W == LANES
    front = GRID_W - NA_COLS
    rp = jnp.pad(rpb.astype(F32) * LOG2E, ((0, 0), (0, 0), (0, 1), (front, LANES - n_dc - front)))
    rp = rp[..., ::-1]
    return pl.pallas_call(
        _na_table_kernel,
        grid=(depth, heads),
        in_specs=[pl.BlockSpec((1, 1, n_dr + 1, LANES), lambda l, h: (l, h, 0, 0))],
        out_specs=pl.BlockSpec((1, 3, 1, NA_W, NA_TQ), lambda l, h: (l, 0, h, 0, 0)),
        out_shape=jax.ShapeDtypeStruct((depth, 3, heads, NA_W, NA_TQ), F32),
        compiler_params=pltpu.CompilerParams(
            dimension_semantics=("arbitrary", "arbitrary"), vmem_limit_bytes=VMEM_LIMIT),
        name="na_tables",
    )(rp)


def _swap_pairs(x):
    n = x.shape[-1]
    lane = lax.broadcasted_iota(jnp.int32, x.shape, x.ndim - 1)
    return jnp.where(lane % 2 == 0, pltpu.roll(x, n - 1, x.ndim - 1), pltpu.roll(x, 1, x.ndim - 1))


def _head_rms(x, ones_bd, w):
    sq = x * x
    hi = sq.astype(BF16)
    lo = (sq - hi.astype(F32)).astype(BF16)
    ms = (jnp.dot(hi, ones_bd, preferred_element_type=F32)
          + jnp.dot(lo, ones_bd, preferred_element_type=F32)) * (1.0 / HEAD_DIM)
    return x * lax.rsqrt(ms + EPS) * w


def _cprep_kernel(q_ref, k_ref, v_ref, cos_ref, sin_ref, qw_ref, kw_ref, bd_ref,
                  qt_ref, k2_ref, vt_ref):
    cos = cos_ref[...]
    sin = sin_ref[...]
    q = _head_rms(q_ref[...].astype(F32), bd_ref[...], qw_ref[...])
    reps = q.shape[1] // LANES
    q = q * jnp.concatenate([cos] * reps, axis=1) + _swap_pairs(q) * jnp.concatenate([sin] * reps, axis=1)
    qt_ref[...] = (q * Q_SCALE).T.astype(qt_ref.dtype)
    k = _head_rms(k_ref[...].astype(F32), bd_ref[:LANES, :LANES], kw_ref[...])
    k = (k * cos + _swap_pairs(k) * sin).astype(k2_ref.dtype)
    vt = v_ref[...].astype(F32).T.astype(vt_ref.dtype)
    ones = jnp.ones((V_ROWS - HEAD_DIM, vt.shape[1]), vt_ref.dtype)
    tk = vt_ref.shape[-1]
    for g in range(C_KV_HEADS):
        k2_ref[g] = k[:, g * HEAD_DIM:(g + 1) * HEAD_DIM]
        vg = jnp.concatenate([vt[g * HEAD_DIM:(g + 1) * HEAD_DIM], ones], axis=0)
        for u in range(vt_ref.shape[1]):
            vt_ref[g, u] = vg[:, u * tk:(u + 1) * tk]


def _cprep(proj, cos2, sin2, qw, kw, ones_bd, tk, t=1024):
    L = proj.shape[0]
    vec = lambda width: pl.BlockSpec((1, width), lambda n: (0, 0))
    return pl.pallas_call(
        _cprep_kernel,
        grid=(L // t,),
        in_specs=[_piece("q_c", t), _piece("k_c", t), _piece("v_c", t),
                  pl.BlockSpec((t, LANES), lambda n: (n, 0)),
                  pl.BlockSpec((t, LANES), lambda n: (n, 0)),
                  vec(GROUP_WIDTH), vec(LANES),
                  pl.BlockSpec((GROUP_WIDTH, GROUP_WIDTH), lambda n: (0, 0))],
        out_specs=[pl.BlockSpec((GROUP_WIDTH, t), lambda n: (0, n)),
                   pl.BlockSpec((C_KV_HEADS, t, HEAD_DIM), lambda n: (0, n, 0)),
                   pl.BlockSpec((C_KV_HEADS, t // tk, V_ROWS, tk), lambda n: (0, n, 0, 0))],
        out_shape=[jax.ShapeDtypeStruct((GROUP_WIDTH, L), BF16),
                   jax.ShapeDtypeStruct((C_KV_HEADS, L, HEAD_DIM), BF16),
                   jax.ShapeDtypeStruct((C_KV_HEADS, L // tk, V_ROWS, tk), BF16)],
        compiler_params=pltpu.CompilerParams(
            dimension_semantics=("arbitrary",), vmem_limit_bytes=VMEM_LIMIT),
        name="dense_prep",
    )(proj, proj, proj, cos2, sin2, qw, kw, ones_bd)


def _flash_kernel(qt_ref, k_ref, vt_ref, o_ref, m_scr, acc_scr, st_scr, *, tk, heads, unroll, ahead):
    n_kv = k_ref.shape[1] // tk
    m_scr[...] = jnp.full(m_scr.shape, NEG, F32)
    acc_scr[...] = jnp.zeros(acc_scr.shape, F32)

    def scores(j, h):
        k = k_ref[0, pl.ds(pl.multiple_of(j * tk, tk), tk), :]
        qt = qt_ref[h * HEAD_DIM:(h + 1) * HEAD_DIM, :]
        return jnp.dot(k, qt, preferred_element_type=F32)

    n_items = unroll * heads

    def item_scores(t, idx):
        j = t * unroll + idx // heads
        return scores(jnp.minimum(j, n_kv - 1), idx % heads)

    for a in range(ahead):
        st_scr[a] = item_scores(0, a)

    def body(t, carry):
        pending = [st_scr[a] for a in range(ahead)]
        for idx in range(n_items):
            pending.append(item_scores(t, idx + ahead))
            st = pending.pop(0)
            h = idx % heads
            vt = vt_ref[0, t * unroll + idx // heads]
            for q0 in range(0, st.shape[1], C_SPLIT):
                cols = slice(q0, q0 + C_SPLIT)
                m = m_scr[h, :, cols]
                m_new = jnp.maximum(m, jnp.max(st[:, cols], axis=0, keepdims=True))
                pt = jnp.exp2(st[:, cols] - m_new).astype(BF16)
                acc_scr[h, :, cols] = (jnp.exp2(m - m_new) * acc_scr[h, :, cols]
                                       + jnp.dot(vt, pt, preferred_element_type=F32))
                m_scr[h, :, cols] = m_new
        for a in range(ahead):
            st_scr[a] = pending[a]
        return carry

    lax.fori_loop(0, n_kv // unroll, body, 0)
    outs = [acc_scr[h, :HEAD_DIM] / acc_scr[h, HEAD_DIM:HEAD_DIM + 1] for h in range(heads)]
    o_ref[...] = jnp.concatenate(outs, axis=0).T.astype(o_ref.dtype)


def _flash(qt, k2, vt, tq):
    L = qt.shape[1]
    _, n_kv, v_rows, tk = vt.shape
    heads = GROUP_HEADS // C_KV_HEADS
    qrows = heads * HEAD_DIM
    return pl.pallas_call(
        functools.partial(_flash_kernel, tk=tk, heads=heads, unroll=math.gcd(C_UNROLL, n_kv),
                          ahead=C_AHEAD),
        grid=(C_KV_HEADS, L // tq),
        in_specs=[pl.BlockSpec((qrows, tq), lambda g, n: (g, n)),
                  pl.BlockSpec((1, L, HEAD_DIM), lambda g, n: (g, 0, 0)),
                  pl.BlockSpec((1, n_kv, v_rows, tk), lambda g, n: (g, 0, 0, 0))],
        out_specs=pl.BlockSpec((tq, qrows), lambda g, n: (n, g)),
        out_shape=jax.ShapeDtypeStruct((L, GROUP_WIDTH), BF16),
        scratch_shapes=[pltpu.VMEM((heads, 1, tq), F32), pltpu.VMEM((heads, v_rows, tq), F32),
                        pltpu.VMEM((C_AHEAD, tk, tq), F32)],
        compiler_params=pltpu.CompilerParams(
            dimension_semantics=("arbitrary", "arbitrary"), vmem_limit_bytes=VMEM_LIMIT),
        name="dense_flash",
    )(qt, k2, vt)


def _outproj_kernel(*refs, final):
    if final:
        x_ref, ya, yb, yc, yd, za, zb, zc, zd, w_ref, g_ref, fw_ref, o_ref = refs
    else:
        x_ref, ya, yb, yc, yd, za, zb, zc, zd, w_ref, g_ref, nw_ref, sc_ref, sh_ref, o_ref, h_ref = refs
    sub = x_ref.shape[0] // OUT_SPLIT
    for r0 in range(0, x_ref.shape[0], sub):
        rows = slice(r0, r0 + sub)
        acc = jnp.zeros((sub, x_ref.shape[1]), F32)
        for gi, (y_ref, z_ref) in enumerate(zip((ya, yb, yc, yd), (za, zb, zc, zd))):
            z = z_ref[rows, :].astype(F32)
            u = (y_ref[rows, :].astype(F32) * (z * jax.nn.sigmoid(z))).astype(BF16)
            acc = acc + jnp.dot(u, w_ref[gi * GROUP_WIDTH:(gi + 1) * GROUP_WIDTH, :],
                                preferred_element_type=F32)
        xn = x_ref[rows, :] + g_ref[...] * acc
        if final:
            o_ref[rows, :] = (xn * lax.rsqrt(jnp.mean(xn * xn, axis=-1, keepdims=True) + EPS)
                              * fw_ref[...])
        else:
            o_ref[rows, :] = xn
            h_ref[rows, :] = _modulated_norm(xn, nw_ref[...], sc_ref[...], sh_ref[...])


def _outproj(x, ys, proj, w_bf16, layer, gate, vecs):
    L, d = x.shape
    tm = min(512, L)
    final = len(vecs) == 1
    row = lambda width: pl.BlockSpec((tm, width), lambda i: (i, 0))
    vec = pl.BlockSpec((1, d), lambda i: (0, 0))
    in_specs = ([row(d)] + [row(GROUP_WIDTH)] * 4 + [_piece(z, tm) for z in ("z_a", "z_b", "z_c", "z_d")]
                + [pl.BlockSpec((None,) + w_bf16.shape[1:], lambda i: (layer, 0, 0)), vec]
                + [vec] * len(vecs))
    x_out = jax.ShapeDtypeStruct((L, d), F32)
    return pl.pallas_call(
        functools.partial(_outproj_kernel, final=final),
        grid=(L // tm,),
        in_specs=in_specs,
        out_specs=row(d) if final else [row(d), row(d)],
        out_shape=x_out if final else [x_out, jax.ShapeDtypeStruct((L, d), BF16)],
        compiler_params=pltpu.CompilerParams(
            dimension_semantics=("arbitrary",), vmem_limit_bytes=VMEM_LIMIT),
        name="outproj",
    )(x, *ys, proj, proj, proj, proj, w_bf16, gate, *vecs)


def _rope_tables(L):
    t = np.arange(L)
    axis_dim = HEAD_DIM // 2
    inv = jnp.asarray(ROPE_THETA, F32) ** (-jnp.arange(0, axis_dim, 2, dtype=F32) / axis_dim)
    row = jnp.asarray(t // GRID_W, F32)
    col = jnp.asarray(t % GRID_W, F32)
    ang = jnp.concatenate([row[:, None] * inv[None], col[:, None] * inv[None]], axis=-1)
    cos = jnp.repeat(jnp.cos(ang), 2, axis=-1)
    sin = jnp.repeat(jnp.sin(ang), 2, axis=-1) * jnp.asarray(np.tile([-1.0, 1.0], HEAD_DIM // 2), F32)
    return jnp.tile(cos, (1, LANES // HEAD_DIM)), jnp.tile(sin, (1, LANES // HEAD_DIM))


A_TQ = 256
A_CHUNK = 256
NA_CHUNK = 256
BAND_LANES = 2
NA_LANES = 4
D_LANES = 4
D_TQ = 256
D_REACH = D_CONFIGS[-1][0] // 2
C_TQ = 512
C_TK = 256
C_UNROLL = 32
C_AHEAD = 2
C_SPLIT = 256
OUT_SPLIT = 2


def _table_a(t5_table):
    return _t5_table(t5_table, 0, A_TQ, A_RADIUS, A_TQ + 2 * A_RADIUS, _window_mult)


D_DIL = D_CONFIGS[-1][1]
D_NEAR = D_CONFIGS[-2][0] // 2
D_FAR_ROWS = (D_REACH - D_NEAR) // D_DIL


def _dil_kernel(qt_ref, kn_ref, k16_ref, vt_ref, tabn_ref, tabf_ref, o_ref, *, tq, L, lanes):
    n = pl.program_id(0)
    t0 = n * tq
    heads = qt_ref.shape[0] // HEAD_DIM
    w_near = kn_ref.shape[0]
    far = D_FAR_ROWS
    n_chunks = w_near // tq
    w_far_keys = far * D_DIL
    lane = lax.broadcasted_iota(jnp.int32, (1, tq), 1)
    qmask = (lax.broadcasted_iota(jnp.int32, (D_DIL * HEAD_DIM, tq), 0) // HEAD_DIM) == (lane % D_DIL)
    pmask = (lax.broadcasted_iota(jnp.int32, (w_far_keys, tq), 0) % D_DIL) == (lane % D_DIL)

    def attend(kb_near, kb_far):
        def scores(h):
            cols = slice(h * HEAD_DIM, (h + 1) * HEAD_DIM)
            q = qt_ref[cols, :]
            near = []
            for c in range(n_chunks):
                rows = slice(c * tq, (c + 1) * tq)
                st = jnp.dot(kn_ref[rows, cols], q, preferred_element_type=F32) + tabn_ref[h, rows, :]
                near.append(st if kb_near is None else st + kb_near[rows])
            qbd = jnp.where(qmask, jnp.concatenate([q] * D_DIL, axis=0), jnp.zeros((), BF16))
            sides = []
            for r0 in (0, k16_ref.shape[0] - far):
                kcat = jnp.concatenate(
                    [k16_ref[r0:r0 + far, r * GROUP_WIDTH + h * HEAD_DIM:r * GROUP_WIDTH + (h + 1) * HEAD_DIM]
                     for r in range(D_DIL)], axis=1)
                sides.append(jnp.dot(kcat, qbd, preferred_element_type=F32))
            sf = jnp.concatenate(sides, axis=0) + tabf_ref[h]
            if kb_far is not None:
                sf = sf + kb_far
            m = jnp.max(sf, axis=0, keepdims=True)
            for st in near:
                m = jnp.maximum(m, jnp.max(st, axis=0, keepdims=True))
            return near, sf, m

        def values(h, near, sf, m):
            v = vt_ref[h * V_ROWS:(h + 1) * V_ROWS, :]
            acc = None
            for c, st in enumerate(near):
                pt = jnp.exp2(st - m).astype(BF16)
                lo = w_far_keys + c * tq
                pv = jnp.dot(v[:, lo:lo + tq], pt, preferred_element_type=F32)
                acc = pv if acc is None else acc + pv
            pf = jnp.exp2(sf - m)
            for side, lo in ((0, 0), (1, w_far_keys + w_near)):
                rep = jnp.concatenate(
                    [jnp.broadcast_to(pf[side * far + a:side * far + a + 1, :], (D_DIL, tq))
                     for a in range(far)], axis=0)
                pbd = jnp.where(pmask, rep, 0.0).astype(BF16)
                acc = acc + jnp.dot(v[:, lo:lo + w_far_keys], pbd, preferred_element_type=F32)
            return acc[:HEAD_DIM] / acc[HEAD_DIM:HEAD_DIM + 1]

        outs = []
        groups = [range(g, g + lanes) for g in range(0, heads, lanes)]
        cur = {h: scores(h) for h in groups[0]}
        for gi, grp in enumerate(groups):
            nxt = {h: scores(h) for h in groups[gi + 1]} if gi + 1 < len(groups) else {}
            outs.extend(values(h, *cur[h]) for h in grp)
            cur = nxt
        o_ref[...] = jnp.concatenate(outs, axis=0).T.astype(o_ref.dtype)

    inside = (t0 - D_REACH >= 0) & (t0 + tq + D_REACH <= L)

    @pl.when(inside)
    def _():
        attend(None, None)

    @pl.when(jnp.logical_not(inside))
    def _():
        kpos = t0 - D_NEAR + lax.broadcasted_iota(jnp.int32, (w_near, 1), 0)
        kb_near = jnp.where((kpos >= 0) & (kpos < L), 0.0, NEG).astype(F32)
        a = lax.broadcasted_iota(jnp.int32, (2 * far, 1), 0)
        base = jnp.where(a < far, t0 - D_REACH + D_DIL * a, t0 + tq + D_NEAR + D_DIL * (a - far))
        kb_far = jnp.where((base >= 0) & (base < L), 0.0, NEG).astype(F32)
        attend(kb_near, kb_far)


def _far_table(t5_table, tq):
    a = np.arange(D_FAR_ROWS)[:, None]
    b = np.arange(tq // D_DIL)[None, :]
    rel = np.concatenate([-D_REACH + D_DIL * (a - b), tq + D_NEAR + D_DIL * (a - b)], axis=0)
    valid = np.abs(rel) <= D_REACH
    bias = t5_table.astype(F32)[:, GROUP_HEADS:2 * GROUP_HEADS][_t5_bucket(jnp.asarray(rel, jnp.int32))]
    tab = jnp.where(jnp.asarray(valid)[..., None], bias * LOG2E, NEG)
    return jnp.repeat(jnp.transpose(tab, (2, 0, 1)), D_DIL, axis=-1)


def _near_table_d(t5_table, tq):
    return _t5_table(t5_table, GROUP_HEADS, tq, D_NEAR, tq + 2 * D_NEAR, _dilated_mult)[0]


def _dilated_attention(qt, k_pad, k16, vt_pad, tab_near, tab_far, *, tq, pad, lanes):
    qw, L = qt.shape
    w_near = tq + 2 * D_NEAR
    w_all = tq + 2 * D_REACH
    return pl.pallas_call(
        functools.partial(_dil_kernel, tq=tq, L=L, lanes=lanes),
        grid=(L // tq,),
        in_specs=[pl.BlockSpec((qw, tq), lambda n: (0, n)),
                  pl.BlockSpec((pl.Element(w_near), pl.Element(k_pad.shape[1])),
                               lambda n: (pl.multiple_of(n * tq + pad - D_NEAR, LANES), 0)),
                  pl.BlockSpec((pl.Element(w_all // D_DIL), pl.Element(k16.shape[1])),
                               lambda n: (pl.multiple_of((n * tq + pad - D_REACH) // D_DIL,
                                                         BF16_SUBLANES), 0)),
                  pl.BlockSpec((pl.Element(vt_pad.shape[0]), pl.Element(w_all)),
                               lambda n: (0, pl.multiple_of(n * tq + pad - D_REACH, LANES))),
                  pl.BlockSpec(tab_near.shape, lambda n: (0, 0, 0), pipeline_mode=pl.Buffered(1)),
                  pl.BlockSpec(tab_far.shape, lambda n: (0, 0, 0), pipeline_mode=pl.Buffered(1))],
        out_specs=pl.BlockSpec((tq, qw), lambda n: (n, 0)),
        out_shape=jax.ShapeDtypeStruct((L, qw), BF16),
        compiler_params=pltpu.CompilerParams(
            dimension_semantics=("arbitrary",), vmem_limit_bytes=VMEM_LIMIT),
        name="dilated_attention",
    )(qt, k_pad, k16, vt_pad, tab_near, tab_far)


PREP_T = 1024
assert PREP_T >= D_REACH


def _seq_window(L, tq, r_lo):
    return lambda n: (n * tq + PREP_T - r_lo, (n * tq - r_lo, 0, L))


def _mixer_a(proj, qt, tab_a, sink):
    L = proj.shape[0]
    k_pad, vt_pad = _kv_prep(proj, "k_a", "v_a", LANES, PREP_T, PREP_T)
    return _band_attention(qt, k_pad, vt_pad, tab_a, chunk=A_CHUNK, lanes=BAND_LANES,
                           group=GROUP_HEADS // A_KV_HEADS, window=_seq_window(L, A_TQ, A_RADIUS),
                           sink=sink.astype(F32))


def _mixer_b(proj, qt, tabs_b, layer):
    L = proj.shape[0]
    rows = L // GRID_W
    n_tiles = rows // NA_TILE_ROWS
    k, vt = _kv_prep(proj, "k_b", "v_b", GROUP_WIDTH, 0, PREP_T)
    window = lambda n: (jnp.clip(n * NA_TILE_ROWS - NA_ROWS // 2, 0, rows - NA_WIN_ROWS) * GRID_W, None)
    variant = lambda n: jnp.where(n == 0, 0, jnp.where(n == n_tiles - 1, 2, 1))
    return _band_attention(qt, k, vt, tabs_b, chunk=NA_CHUNK, lanes=NA_LANES, group=1,
                           window=window, variant=variant, layer=layer)


def _mixer_c(proj, q_norm_w, k_norm_w):
    L = proj.shape[0]
    cos2, sin2 = _rope_tables(L)
    ones_bd = jnp.asarray(np.kron(np.eye(GROUP_HEADS), np.ones((HEAD_DIM, HEAD_DIM))), BF16)
    qt, k2, vt = _cprep(proj, cos2, sin2, jnp.tile(q_norm_w.astype(F32), GROUP_HEADS)[None],
                        jnp.tile(k_norm_w.astype(F32), LANES // HEAD_DIM)[None], ones_bd, C_TK)
    return _flash(qt, k2, vt, C_TQ)


def _mixer_d(proj, qt, tabs_d):
    k_pad, vt_pad, k16 = _kv_prep(proj, "k_d", "v_d", GROUP_WIDTH, PREP_T, PREP_T, group=D_DIL)
    return _dilated_attention(qt, k_pad, k16, vt_pad, *tabs_d, tq=D_TQ, pad=PREP_T, lanes=D_LANES)


def kernel(x, c, w_ada, b_ada, norm_w, w_in, w_out, attn_sink, na_rpb, q_norm_w, k_norm_w,
           t5_table, final_norm_w):
    B, L, D = x.shape
    assert B == 1 and L % 1024 == 0 and L // GRID_W >= NA_WIN_ROWS
    depth = w_ada.shape[0]
    x = x[0]

    mod = _ada_mod(jnp.broadcast_to(c, (8, D)), w_ada, b_ada)[:, 0:1, :]
    tab_a = _table_a(t5_table)
    tabs_d = (_near_table_d(t5_table, D_TQ), _far_table(t5_table, D_TQ))
    tabs_b = _na_tables(na_rpb)
    w_out_b = w_out.astype(BF16)
    shift, scale, gate = jnp.split(mod, 3, axis=-1)
    h = _norm(x, norm_w[0][None], scale[0], shift[0])
    for i in range(depth):
        proj = _inproj(h, w_in, i)
        qt_a, qt_b, qt_d = _qt_prep(proj, ("q_a", "q_b", "q_d"))
        ys = (_mixer_a(proj, qt_a, tab_a, attn_sink[i]),
              _mixer_b(proj, qt_b, tabs_b, i),
              _mixer_c(proj, q_norm_w[i], k_norm_w[i]),
              _mixer_d(proj, qt_d, tabs_d))
        if i + 1 < depth:
            x, h = _outproj(x, ys, proj, w_out_b, i, gate[i],
                            (norm_w[i + 1][None], scale[i + 1], shift[i + 1]))
        else:
            x = _outproj(x, ys, proj, w_out_b, i, gate[i], (final_norm_w[None],))
    return x[None]
```

```python
import functools
import math

import numpy as np
import jax
import jax.numpy as jnp
from jax import lax
from jax.experimental import pallas as pl
from jax.experimental.pallas import tpu as pltpu

HEAD_DIM = 64
GROUP_WIDTH = 512
GROUP_HEADS = 8
A_KV_HEADS = 2
A_RADIUS = 128
C_KV_HEADS = 2
ROPE_THETA = 10000.0
NA_ROWS = 8
NA_COLS = 16
D_CONFIGS = ((128, 1), (512, 4), (2048, 16))
GRID_W = 64
T5_BUCKETS = 32
T5_MAX_DIST = 1024
EPS = 1e-6
NEG = -1e30

LANES = 128
BF16_SUBLANES = 16
V_ROWS = HEAD_DIM + BF16_SUBLANES
VMEM_LIMIT = 56 * 1024 * 1024

F32 = jnp.float32
BF16 = jnp.bfloat16

_SRC = {}
_off = 0
for _name, _w in (("q_a", 512), ("k_a", 128), ("v_a", 128), ("z_a", 512),
                  ("q_b", 512), ("k_b", 512), ("v_b", 512), ("z_b", 512),
                  ("q_c", 512), ("k_c", 128), ("v_c", 128), ("z_c", 512),
                  ("q_d", 512), ("k_d", 512), ("v_d", 512), ("z_d", 512)):
    _SRC[_name] = (_off, _w)
    _off += _w
IN_WIDTH = _off


def _piece(name, t, row_block=lambda n: n):
    off, width = _SRC[name]
    return pl.BlockSpec((pl.Element(t), pl.Element(width)),
                        lambda n: (pl.multiple_of(row_block(n) * t, t), off))


def _ada_kernel(c_ref, w_ref, b_ref, o_ref):
    c = c_ref[...]
    cond = c * jax.nn.sigmoid(c)
    o_ref[0] = jnp.dot(cond, w_ref[0], preferred_element_type=F32,
                       precision=lax.Precision.HIGHEST) + b_ref[0]


def _ada_mod(c8, w_ada, b_ada):
    depth, d, n3 = w_ada.shape
    tn = 1024
    return pl.pallas_call(
        _ada_kernel,
        grid=(depth, n3 // tn),
        in_specs=[pl.BlockSpec((8, d), lambda i, j: (0, 0)),
                  pl.BlockSpec((1, d, tn), lambda i, j: (i, 0, j)),
                  pl.BlockSpec((1, 1, tn), lambda i, j: (i, 0, j))],
        out_specs=pl.BlockSpec((1, 8, tn), lambda i, j: (i, 0, j)),
        out_shape=jax.ShapeDtypeStruct((depth, 8, n3), F32),
        compiler_params=pltpu.CompilerParams(
            dimension_semantics=("arbitrary", "arbitrary"), vmem_limit_bytes=VMEM_LIMIT),
        name="ada_mod",
    )(c8, w_ada, b_ada.reshape(depth, 1, n3))


def _modulated_norm(x, nw, scale, shift):
    y = x * lax.rsqrt(jnp.mean(x * x, axis=-1, keepdims=True) + EPS)
    return ((y * nw) * (1.0 + scale) + shift).astype(BF16)


def _norm_kernel(x_ref, nw_ref, sc_ref, sh_ref, h_ref):
    h_ref[...] = _modulated_norm(x_ref[...], nw_ref[...], sc_ref[...], sh_ref[...])


def _norm(x, nw, scale, shift):
    L, d = x.shape
    tm = min(512, L)
    vec = pl.BlockSpec((1, d), lambda i: (0, 0))
    return pl.pallas_call(
        _norm_kernel,
        grid=(L // tm,),
        in_specs=[pl.BlockSpec((tm, d), lambda i: (i, 0)), vec, vec, vec],
        out_specs=pl.BlockSpec((tm, d), lambda i: (i, 0)),
        out_shape=jax.ShapeDtypeStruct((L, d), BF16),
        compiler_params=pltpu.CompilerParams(
            dimension_semantics=("arbitrary",), vmem_limit_bytes=VMEM_LIMIT),
        name="norm",
    )(x, nw, scale, shift)


def _inproj_kernel(h_ref, w_ref, o_ref):
    o_ref[...] = jnp.dot(h_ref[...], w_ref[...].astype(BF16),
                         preferred_element_type=F32).astype(o_ref.dtype)


def _inproj(h, w, layer):
    L, d = h.shape
    n = w.shape[2]
    tm = min(2048, L)
    tn = 512
    return pl.pallas_call(
        _inproj_kernel,
        grid=(L // tm, n // tn),
        in_specs=[pl.BlockSpec((tm, d), lambda i, j: (i, 0)),
                  pl.BlockSpec((None, d, tn), lambda i, j: (layer, 0, j))],
        out_specs=pl.BlockSpec((tm, tn), lambda i, j: (i, j)),
        out_shape=jax.ShapeDtypeStruct((L, n), BF16),
        compiler_params=pltpu.CompilerParams(
            dimension_semantics=("arbitrary", "arbitrary"), vmem_limit_bytes=VMEM_LIMIT),
        name="inproj",
    )(h, w)


LOG2E = math.log2(math.e)
Q_SCALE = HEAD_DIM ** -0.5 * LOG2E


def _scaled_qt(q_ref):
    return (q_ref[...].astype(F32) * Q_SCALE).T.astype(BF16)


def _kv_prep_kernel(k_ref, v_ref, kp_ref, vt_ref, *rest, pad_blocks, tok_blocks, group):
    b = pl.program_id(0)
    is_token = (b >= pad_blocks) & (b < pad_blocks + tok_blocks)

    @pl.when(is_token)
    def _():
        kp_ref[...] = k_ref[...]
        if group:
            kg_ref, k32_scr = rest
            width = k_ref.shape[1]
            for c in range(width // LANES):
                k32_scr[c] = k_ref[:, c * LANES:(c + 1) * LANES].astype(F32)
            for r in range(group):
                for c in range(width // LANES):
                    kg_ref[:, r * width + c * LANES:r * width + (c + 1) * LANES] = (
                        k32_scr[c, pl.ds(r, kg_ref.shape[0], stride=group), :].astype(kg_ref.dtype))
        vt = v_ref[...].astype(F32).T.astype(vt_ref.dtype)
        ones = jnp.ones((V_ROWS - HEAD_DIM, vt.shape[1]), vt_ref.dtype)
        for g in range(vt.shape[0] // HEAD_DIM):
            vt_ref[g * V_ROWS:(g + 1) * V_ROWS, :] = jnp.concatenate(
                [vt[g * HEAD_DIM:(g + 1) * HEAD_DIM], ones], axis=0)

    @pl.when(jnp.logical_not(is_token))
    def _():
        kp_ref[...] = jnp.zeros(kp_ref.shape, kp_ref.dtype)
        vt_ref[...] = jnp.zeros(vt_ref.shape, vt_ref.dtype)
        if group:
            rest[0][...] = jnp.zeros(rest[0].shape, rest[0].dtype)


def _kv_prep(proj, k_name, v_name, width, pad, t, group=0):
    L = proj.shape[0]
    assert pad % t == 0 and L % t == 0 and width % LANES == 0
    pad_blocks, tok_blocks = pad // t, L // t
    lp = L + 2 * pad
    assert _SRC[k_name][1] == width and _SRC[v_name][1] == width
    tok = lambda b: jnp.clip(b - pad_blocks, 0, tok_blocks - 1)
    return pl.pallas_call(
        functools.partial(_kv_prep_kernel, pad_blocks=pad_blocks, tok_blocks=tok_blocks, group=group),
        grid=(lp // t,),
        in_specs=[_piece(k_name, t, tok), _piece(v_name, t, tok)],
        out_specs=[pl.BlockSpec((t, width), lambda b: (b, 0)),
                   pl.BlockSpec((width // HEAD_DIM * V_ROWS, t), lambda b: (0, b))]
        + ([pl.BlockSpec((t // group, group * width), lambda b: (b, 0))] if group else []),
        out_shape=[jax.ShapeDtypeStruct((lp, width), BF16),
                   jax.ShapeDtypeStruct((width // HEAD_DIM * V_ROWS, lp), BF16)]
        + ([jax.ShapeDtypeStruct((lp // group, group * width), BF16)] if group else []),
        scratch_shapes=[pltpu.VMEM((width // LANES, t, LANES), F32)] if group else [],
        compiler_params=pltpu.CompilerParams(
            dimension_semantics=("arbitrary",), vmem_limit_bytes=VMEM_LIMIT),
        name="kv_prep",
    )(proj, proj)


def _band_kernel(*refs, w_keys, chunk, lanes, heads, group, use_sink, window, variant):
    if use_sink:
        sink_ref, q_ref, k_ref, vt_ref, tab_ref, o_ref = refs
    else:
        q_ref, k_ref, vt_ref, tab_ref, o_ref = refs
    qt_all = _scaled_qt(q_ref)
    n = pl.program_id(0)
    _, pos = window(n)
    kwin = k_ref[...]
    vwin = vt_ref[...]
    var = variant(n)

    n_chunks = w_keys // chunk

    def attend(kbias):
        def scores(h, c):
            kv = h // group
            rows = slice(c * chunk, (c + 1) * chunk)
            st = jnp.dot(kwin[rows, kv * HEAD_DIM:(kv + 1) * HEAD_DIM],
                         qt_all[h * HEAD_DIM:(h + 1) * HEAD_DIM, :],
                         preferred_element_type=F32) + tab_ref[var, h, rows, :]
            return st if kbias is None else st + kbias[rows]

        def col_max(m, st):
            cm = jnp.max(st, axis=0, keepdims=True)
            return cm if m is None else jnp.maximum(m, cm)

        def finish(h, acc, m):
            o, l = acc[:HEAD_DIM], acc[HEAD_DIM:HEAD_DIM + 1]
            if use_sink:
                sk = sink_ref[h] * LOG2E
                m2 = jnp.maximum(m, sk)
                a = jnp.exp2(m - m2)
                return o * (a / (l * a + jnp.exp2(sk - m2)))
            return o / l

        outs = []
        groups = [range(g, g + lanes) for g in range(0, heads, lanes)]
        cur = {h: [] for h in groups[0]}
        m = {h: None for h in groups[0]}
        for c in range(n_chunks):
            for h in groups[0]:
                cur[h].append(scores(h, c))
                m[h] = col_max(m[h], cur[h][-1])
        for gi, grp in enumerate(groups):
            following = groups[gi + 1] if gi + 1 < len(groups) else ()
            nxt = {h: [] for h in following}
            m_next = {h: None for h in following}
            acc = {h: None for h in grp}
            for c in range(n_chunks):
                for h in following:
                    nxt[h].append(scores(h, c))
                    m_next[h] = col_max(m_next[h], nxt[h][-1])
                for h in grp:
                    kv = h // group
                    pt = jnp.exp2(cur[h][c] - m[h]).astype(BF16)
                    pv = jnp.dot(vwin[kv * V_ROWS:(kv + 1) * V_ROWS, c * chunk:(c + 1) * chunk], pt,
                                 preferred_element_type=F32)
                    acc[h] = pv if acc[h] is None else acc[h] + pv
            outs.extend(finish(h, acc[h], m[h]) for h in grp)
            cur, m = nxt, m_next
        o_ref[...] = jnp.concatenate(outs, axis=0).T.astype(o_ref.dtype)

    if pos is None:
        attend(None)
    else:
        first, lo, hi = pos
        inside = (first >= lo) & (first + w_keys <= hi)

        @pl.when(inside)
        def _():
            attend(None)

        @pl.when(jnp.logical_not(inside))
        def _():
            kpos = first + lax.broadcasted_iota(jnp.int32, (w_keys, 1), 0)
            attend(jnp.where((kpos >= lo) & (kpos < hi), 0.0, NEG).astype(F32))


def _band_attention(proj, q_name, k_pad, vt_pad, table, *, chunk, lanes, group, window,
                    variant=lambda n: 0, sink=None, layer=None):
    L, qw = proj.shape[0], _SRC[q_name][1]
    heads = qw // HEAD_DIM
    w_keys, tq = table.shape[-2:]
    assert w_keys % chunk == 0
    kern = functools.partial(_band_kernel, w_keys=w_keys, chunk=chunk, lanes=lanes, heads=heads, group=group,
                             use_sink=sink is not None, window=window, variant=variant)
    start = lambda n: pl.multiple_of(window(n)[0], LANES)
    if layer is None:
        table_spec = pl.BlockSpec(table.shape, lambda n: (0, 0, 0, 0), pipeline_mode=pl.Buffered(1))
    else:
        table_spec = pl.BlockSpec((None,) + table.shape[1:], lambda n: (layer, 0, 0, 0, 0),
                                  pipeline_mode=pl.Buffered(1))
    in_specs = [_piece(q_name, tq),
                pl.BlockSpec((pl.Element(w_keys), pl.Element(k_pad.shape[1])), lambda n: (start(n), 0)),
                pl.BlockSpec((pl.Element(vt_pad.shape[0]), pl.Element(w_keys)), lambda n: (0, start(n))),
                table_spec]
    args = [proj, k_pad, vt_pad, table]
    if sink is not None:
        in_specs = [pl.BlockSpec(memory_space=pltpu.SMEM)] + in_specs
        args = [sink] + args
    return pl.pallas_call(
        kern,
        grid=(L // tq,),
        in_specs=in_specs,
        out_specs=pl.BlockSpec((tq, qw), lambda n: (n, 0)),
        out_shape=jax.ShapeDtypeStruct((L, qw), BF16),
        compiler_params=pltpu.CompilerParams(
            dimension_semantics=("arbitrary",), vmem_limit_bytes=VMEM_LIMIT),
        name="band_attention",
    )(*args)


def _t5_bucket(rel):
    half = T5_BUCKETS // 2
    exact = half // 2
    n = jnp.abs(rel)
    big = exact + (jnp.log(jnp.maximum(n, exact).astype(F32) / exact)
                   / math.log(T5_MAX_DIST / exact) * (half - exact)).astype(jnp.int32)
    big = jnp.minimum(big, half - 1)
    return jnp.where(rel > 0, half, 0) + jnp.where(n < exact, n, big)


def _toeplitz_kernel(rv_ref, o_ref, *, n_diag):
    w, tq = o_ref.shape[1:]
    blocks = []
    for d in range(n_diag):
        s = LANES * (n_diag - 1 - d)
        x = jnp.broadcast_to(rv_ref[0, :, s:s + 2 * LANES], (LANES, 2 * LANES))
        blocks.append(pltpu.roll(x, LANES, 1, stride=1, stride_axis=0)[:, :LANES])
    for cb in range(w // LANES):
        for ib in range(tq // LANES):
            o_ref[0, cb * LANES:(cb + 1) * LANES, ib * LANES:(ib + 1) * LANES] = (
                blocks[cb - ib + tq // LANES - 1])


def _toeplitz(v, tq, w):
    heads, n_rel = v.shape
    n_diag = w // LANES + tq // LANES - 1
    assert w % LANES == 0 and tq % LANES == 0 and n_rel == LANES * n_diag + LANES - 1
    rv = jnp.pad(v, ((0, 0), (0, 1)))[:, None, ::-1]
    return pl.pallas_call(
        functools.partial(_toeplitz_kernel, n_diag=n_diag),
        grid=(heads,),
        in_specs=[pl.BlockSpec((1, 1, rv.shape[-1]), lambda h: (h, 0, 0))],
        out_specs=pl.BlockSpec((1, w, tq), lambda h: (h, 0, 0)),
        out_shape=jax.ShapeDtypeStruct((heads, w, tq), F32),
        compiler_params=pltpu.CompilerParams(
            dimension_semantics=("arbitrary",), vmem_limit_bytes=VMEM_LIMIT),
        name="toeplitz",
    )(rv)


def _t5_table(t5_table, head_lo, tq, r_lo, w_keys, mult):
    rel = np.arange(-r_lo - (tq - 1), w_keys - r_lo)
    m = mult(rel)
    b = t5_table.astype(F32)[:, head_lo:head_lo + GROUP_HEADS][_t5_bucket(jnp.asarray(rel, jnp.int32))]
    logm = np.log(np.maximum(m, 1)).astype(np.float32)
    vec = jnp.where(jnp.asarray(m > 0)[:, None], (b + logm[:, None]) * LOG2E, NEG).T
    return _toeplitz(vec, tq, w_keys)[None]


def _window_mult(rel):
    return (np.abs(rel) <= A_RADIUS).astype(np.int32)


def _dilated_mult(rel):
    m = np.zeros(rel.shape, np.int32)
    for window, dil in D_CONFIGS:
        m += ((rel % dil == 0) & (np.abs(rel) <= window // 2)).astype(np.int32)
    return m


NA_TILE_ROWS = 4
NA_WIN_ROWS = NA_TILE_ROWS + NA_ROWS
NA_TQ = NA_TILE_ROWS * GRID_W
NA_W = NA_WIN_ROWS * GRID_W


def _na_row_index(variant, j, i):
    dr, ok = ((j - i, j < NA_ROWS),
              (j - NA_ROWS // 2 - i, i <= j < i + NA_ROWS),
              (j - NA_ROWS - i, j >= NA_TILE_ROWS))[variant]
    return dr + NA_ROWS - 1 if ok else None


def _na_table_kernel(rp_ref, o_ref):
    kc = lax.broadcasted_iota(jnp.int32, (GRID_W, LANES), 0)
    lane = lax.broadcasted_iota(jnp.int32, (GRID_W, LANES), 1)
    col_start = jnp.clip(lane % GRID_W - NA_COLS // 2, 0, GRID_W - NA_COLS)
    col_ok = (kc >= col_start) & (kc < col_start + NA_COLS)
    neg = jnp.full((GRID_W, LANES), NEG, F32)
    cache = {}

    def half_block(d, side):
        if d is None:
            return neg
        if (d, side) not in cache:
            x = jnp.broadcast_to(rp_ref[0, 0, d:d + 1, :], (GRID_W, LANES))
            y = pltpu.roll(x, GRID_W * (1 - side), 1, stride=1, stride_axis=0)
            cache[d, side] = jnp.where(col_ok, y, NEG)
        return cache[d, side]

    for variant in range(3):
        for j in range(NA_WIN_ROWS):
            for ip in range(NA_TILE_ROWS // 2):
                left = half_block(_na_row_index(variant, j, 2 * ip), 0)
                right = half_block(_na_row_index(variant, j, 2 * ip + 1), 1)
                o_ref[0, variant, 0, j * GRID_W:(j + 1) * GRID_W, ip * LANES:(ip + 1) * LANES] = (
                    jnp.where(lane < GRID_W, left, right))


def _na_tables(rpb):
    depth, heads, n_dr, n_dc = rpb.shape
    assert n_dr == NA_WIN_ROWS + NA_TILE_ROWS - 1 and 2 * GRID_W == LANES
    front = GRID_W - NA_COLS
    rp = jnp.pad(rpb.astype(F32) * LOG2E, ((0, 0), (0, 0), (0, 1), (front, LANES - n_dc - front)))
    rp = rp[..., ::-1]
    return pl.pallas_call(
        _na_table_kernel,
        grid=(depth, heads),
        in_specs=[pl.BlockSpec((1, 1, n_dr + 1, LANES), lambda l, h: (l, h, 0, 0))],
        out_specs=pl.BlockSpec((1, 3, 1, NA_W, NA_TQ), lambda l, h: (l, 0, h, 0, 0)),
        out_shape=jax.ShapeDtypeStruct((depth, 3, heads, NA_W, NA_TQ), F32),
        compiler_params=pltpu.CompilerParams(
            dimension_semantics=("arbitrary", "arbitrary"), vmem_limit_bytes=VMEM_LIMIT),
        name="na_tables",
    )(rp)


def _swap_pairs(x):
    n = x.shape[-1]
    lane = lax.broadcasted_iota(jnp.int32, x.shape, x.ndim - 1)
    return jnp.where(lane % 2 == 0, pltpu.roll(x, n - 1, x.ndim - 1), pltpu.roll(x, 1, x.ndim - 1))


def _head_rms(x, ones_bd, w):
    sq = x * x
    hi = sq.astype(BF16)
    lo = (sq - hi.astype(F32)).astype(BF16)
    ms = (jnp.dot(hi, ones_bd, preferred_element_type=F32)
          + jnp.dot(lo, ones_bd, preferred_element_type=F32)) * (1.0 / HEAD_DIM)
    return x * lax.rsqrt(ms + EPS) * w


def _cprep_kernel(q_ref, k_ref, v_ref, cos_ref, sin_ref, qw_ref, kw_ref, bd_ref,
                  qt_ref, k2_ref, vt_ref):
    cos = cos_ref[...]
    sin = sin_ref[...]
    q = _head_rms(q_ref[...].astype(F32), bd_ref[...], qw_ref[...])
    reps = q.shape[1] // LANES
    q = q * jnp.concatenate([cos] * reps, axis=1) + _swap_pairs(q) * jnp.concatenate([sin] * reps, axis=1)
    qt_ref[...] = (q * Q_SCALE).T.astype(qt_ref.dtype)
    k = _head_rms(k_ref[...].astype(F32), bd_ref[:LANES, :LANES], kw_ref[...])
    k = (k * cos + _swap_pairs(k) * sin).astype(k2_ref.dtype)
    vt = v_ref[...].astype(F32).T.astype(vt_ref.dtype)
    ones = jnp.ones((V_ROWS - HEAD_DIM, vt.shape[1]), vt_ref.dtype)
    tk = vt_ref.shape[-1]
    for g in range(C_KV_HEADS):
        k2_ref[g] = k[:, g * HEAD_DIM:(g + 1) * HEAD_DIM]
        vg = jnp.concatenate([vt[g * HEAD_DIM:(g + 1) * HEAD_DIM], ones], axis=0)
        for u in range(vt_ref.shape[1]):
            vt_ref[g, u] = vg[:, u * tk:(u + 1) * tk]


def _cprep(proj, cos2, sin2, qw, kw, ones_bd, tk, t=1024):
    L = proj.shape[0]
    vec = lambda width: pl.BlockSpec((1, width), lambda n: (0, 0))
    return pl.pallas_call(
        _cprep_kernel,
        grid=(L // t,),
        in_specs=[_piece("q_c", t), _piece("k_c", t), _piece("v_c", t),
                  pl.BlockSpec((t, LANES), lambda n: (n, 0)),
                  pl.BlockSpec((t, LANES), lambda n: (n, 0)),
                  vec(GROUP_WIDTH), vec(LANES),
                  pl.BlockSpec((GROUP_WIDTH, GROUP_WIDTH), lambda n: (0, 0))],
        out_specs=[pl.BlockSpec((GROUP_WIDTH, t), lambda n: (0, n)),
                   pl.BlockSpec((C_KV_HEADS, t, HEAD_DIM), lambda n: (0, n, 0)),
                   pl.BlockSpec((C_KV_HEADS, t // tk, V_ROWS, tk), lambda n: (0, n, 0, 0))],
        out_shape=[jax.ShapeDtypeStruct((GROUP_WIDTH, L), BF16),
                   jax.ShapeDtypeStruct((C_KV_HEADS, L, HEAD_DIM), BF16),
                   jax.ShapeDtypeStruct((C_KV_HEADS, L // tk, V_ROWS, tk), BF16)],
        compiler_params=pltpu.CompilerParams(
            dimension_semantics=("arbitrary",), vmem_limit_bytes=VMEM_LIMIT),
        name="dense_prep",
    )(proj, proj, proj, cos2, sin2, qw, kw, ones_bd)


def _flash_kernel(qt_ref, k_ref, vt_ref, o_ref, m_scr, acc_scr, st_scr, *, tk, heads, unroll, ahead):
    n_kv = k_ref.shape[1] // tk
    m_scr[...] = jnp.full(m_scr.shape, NEG, F32)
    acc_scr[...] = jnp.zeros(acc_scr.shape, F32)

    def scores(j, h):
        k = k_ref[0, pl.ds(pl.multiple_of(j * tk, tk), tk), :]
        qt = qt_ref[h * HEAD_DIM:(h + 1) * HEAD_DIM, :]
        return jnp.dot(k, qt, preferred_element_type=F32)

    n_items = unroll * heads

    def item_scores(t, idx):
        j = t * unroll + idx // heads
        return scores(jnp.minimum(j, n_kv - 1), idx % heads)

    for a in range(ahead):
        st_scr[a] = item_scores(0, a)

    def body(t, carry):
        pending = [st_scr[a] for a in range(ahead)]
        for idx in range(n_items):
            pending.append(item_scores(t, idx + ahead))
            st = pending.pop(0)
            h = idx % heads
            vt = vt_ref[0, t * unroll + idx // heads]
            for q0 in range(0, st.shape[1], C_SPLIT):
                cols = slice(q0, q0 + C_SPLIT)
                m = m_scr[h, :, cols]
                m_new = jnp.maximum(m, jnp.max(st[:, cols], axis=0, keepdims=True))
                pt = jnp.exp2(st[:, cols] - m_new).astype(BF16)
                acc_scr[h, :, cols] = (jnp.exp2(m - m_new) * acc_scr[h, :, cols]
                                       + jnp.dot(vt, pt, preferred_element_type=F32))
                m_scr[h, :, cols] = m_new
        for a in range(ahead):
            st_scr[a] = pending[a]
        return carry

    lax.fori_loop(0, n_kv // unroll, body, 0)
    outs = [acc_scr[h, :HEAD_DIM] / acc_scr[h, HEAD_DIM:HEAD_DIM + 1] for h in range(heads)]
    o_ref[...] = jnp.concatenate(outs, axis=0).T.astype(o_ref.dtype)


def _flash(qt, k2, vt, tq):
    L = qt.shape[1]
    _, n_kv, v_rows, tk = vt.shape
    heads = GROUP_HEADS // C_KV_HEADS
    qrows = heads * HEAD_DIM
    return pl.pallas_call(
        functools.partial(_flash_kernel, tk=tk, heads=heads, unroll=math.gcd(C_UNROLL, n_kv),
                          ahead=C_AHEAD),
        grid=(C_KV_HEADS, L // tq),
        in_specs=[pl.BlockSpec((qrows, tq), lambda g, n: (g, n)),
                  pl.BlockSpec((1, L, HEAD_DIM), lambda g, n: (g, 0, 0)),
                  pl.BlockSpec((1, n_kv, v_rows, tk), lambda g, n: (g, 0, 0, 0))],
        out_specs=pl.BlockSpec((tq, qrows), lambda g, n: (n, g)),
        out_shape=jax.ShapeDtypeStruct((L, GROUP_WIDTH), BF16),
        scratch_shapes=[pltpu.VMEM((heads, 1, tq), F32), pltpu.VMEM((heads, v_rows, tq), F32),
                        pltpu.VMEM((C_AHEAD, tk, tq), F32)],
        compiler_params=pltpu.CompilerParams(
            dimension_semantics=("arbitrary", "arbitrary"), vmem_limit_bytes=VMEM_LIMIT),
        name="dense_flash",
    )(qt, k2, vt)


def _outproj_kernel(*refs, final):
    if final:
        x_ref, ya, yb, yc, yd, za, zb, zc, zd, w_ref, g_ref, fw_ref, o_ref = refs
    else:
        x_ref, ya, yb, yc, yd, za, zb, zc, zd, w_ref, g_ref, nw_ref, sc_ref, sh_ref, o_ref, h_ref = refs
    sub = x_ref.shape[0] // OUT_SPLIT
    for r0 in range(0, x_ref.shape[0], sub):
        rows = slice(r0, r0 + sub)
        acc = jnp.zeros((sub, x_ref.shape[1]), F32)
        for gi, (y_ref, z_ref) in enumerate(zip((ya, yb, yc, yd), (za, zb, zc, zd))):
            z = z_ref[rows, :].astype(F32)
            u = (y_ref[rows, :].astype(F32) * (z * jax.nn.sigmoid(z))).astype(BF16)
            acc = acc + jnp.dot(u, w_ref[gi * GROUP_WIDTH:(gi + 1) * GROUP_WIDTH, :],
                                preferred_element_type=F32)
        xn = x_ref[rows, :] + g_ref[...] * acc
        if final:
            o_ref[rows, :] = (xn * lax.rsqrt(jnp.mean(xn * xn, axis=-1, keepdims=True) + EPS)
                              * fw_ref[...])
        else:
            o_ref[rows, :] = xn
            h_ref[rows, :] = _modulated_norm(xn, nw_ref[...], sc_ref[...], sh_ref[...])


def _outproj(x, ys, proj, w_bf16, layer, gate, vecs):
    L, d = x.shape
    tm = min(512, L)
    final = len(vecs) == 1
    row = lambda width: pl.BlockSpec((tm, width), lambda i: (i, 0))
    vec = pl.BlockSpec((1, d), lambda i: (0, 0))
    in_specs = ([row(d)] + [row(GROUP_WIDTH)] * 4 + [_piece(z, tm) for z in ("z_a", "z_b", "z_c", "z_d")]
                + [pl.BlockSpec((None,) + w_bf16.shape[1:], lambda i: (layer, 0, 0)), vec]
                + [vec] * len(vecs))
    x_out = jax.ShapeDtypeStruct((L, d), F32)
    return pl.pallas_call(
        functools.partial(_outproj_kernel, final=final),
        grid=(L // tm,),
        in_specs=in_specs,
        out_specs=row(d) if final else [row(d), row(d)],
        out_shape=x_out if final else [x_out, jax.ShapeDtypeStruct((L, d), BF16)],
        compiler_params=pltpu.CompilerParams(
            dimension_semantics=("arbitrary",), vmem_limit_bytes=VMEM_LIMIT),
        name="outproj",
    )(x, *ys, proj, proj, proj, proj, w_bf16, gate, *vecs)


def _rope_tables(L):
    t = np.arange(L)
    axis_dim = HEAD_DIM // 2
    inv = jnp.asarray(ROPE_THETA, F32) ** (-jnp.arange(0, axis_dim, 2, dtype=F32) / axis_dim)
    row = jnp.asarray(t // GRID_W, F32)
    col = jnp.asarray(t % GRID_W, F32)
    ang = jnp.concatenate([row[:, None] * inv[None], col[:, None] * inv[None]], axis=-1)
    cos = jnp.repeat(jnp.cos(ang), 2, axis=-1)
    sin = jnp.repeat(jnp.sin(ang), 2, axis=-1) * jnp.asarray(np.tile([-1.0, 1.0], HEAD_DIM // 2), F32)
    return jnp.tile(cos, (1, LANES // HEAD_DIM)), jnp.tile(sin, (1, LANES // HEAD_DIM))


A_TQ = 256
A_CHUNK = 256
NA_CHUNK = 256
BAND_LANES = 2
NA_LANES = 4
D_LANES = 4
D_TQ = 256
D_REACH = D_CONFIGS[-1][0] // 2
C_TQ = 512
C_TK = 256
C_UNROLL = 32
C_AHEAD = 2
C_SPLIT = 256
OUT_SPLIT = 2


def _table_a(t5_table):
    return _t5_table(t5_table, 0, A_TQ, A_RADIUS, A_TQ + 2 * A_RADIUS, _window_mult)


D_DIL = D_CONFIGS[-1][1]
D_NEAR = D_CONFIGS[-2][0] // 2
D_FAR_ROWS = (D_REACH - D_NEAR) // D_DIL


def _dil_kernel(q_ref, kn_ref, k16_ref, vt_ref, tabn_ref, tabf_ref, o_ref, *, tq, L, lanes):
    n = pl.program_id(0)
    t0 = n * tq
    qt_all = _scaled_qt(q_ref)
    heads = qt_all.shape[0] // HEAD_DIM
    w_near = kn_ref.shape[0]
    far = D_FAR_ROWS
    n_chunks = w_near // tq
    w_far_keys = far * D_DIL
    lane = lax.broadcasted_iota(jnp.int32, (1, tq), 1)
    qmask = (lax.broadcasted_iota(jnp.int32, (D_DIL * HEAD_DIM, tq), 0) // HEAD_DIM) == (lane % D_DIL)
    pmask = (lax.broadcasted_iota(jnp.int32, (w_far_keys, tq), 0) % D_DIL) == (lane % D_DIL)

    def attend(kb_near, kb_far):
        def scores(h):
            cols = slice(h * HEAD_DIM, (h + 1) * HEAD_DIM)
            q = qt_all[cols, :]
            near = []
            for c in range(n_chunks):
                rows = slice(c * tq, (c + 1) * tq)
                st = jnp.dot(kn_ref[rows, cols], q, preferred_element_type=F32) + tabn_ref[h, rows, :]
                near.append(st if kb_near is None else st + kb_near[rows])
            qbd = jnp.where(qmask, jnp.concatenate([q] * D_DIL, axis=0), jnp.zeros((), BF16))
            sides = []
            for r0 in (0, k16_ref.shape[0] - far):
                kcat = jnp.concatenate(
                    [k16_ref[r0:r0 + far, r * GROUP_WIDTH + h * HEAD_DIM:r * GROUP_WIDTH + (h + 1) * HEAD_DIM]
                     for r in range(D_DIL)], axis=1)
                sides.append(jnp.dot(kcat, qbd, preferred_element_type=F32))
            sf = jnp.concatenate(sides, axis=0) + tabf_ref[h]
            if kb_far is not None:
                sf = sf + kb_far
            m = jnp.max(sf, axis=0, keepdims=True)
            for st in near:
                m = jnp.maximum(m, jnp.max(st, axis=0, keepdims=True))
            return near, sf, m

        def values(h, near, sf, m):
            v = vt_ref[h * V_ROWS:(h + 1) * V_ROWS, :]
            acc = None
            for c, st in enumerate(near):
                pt = jnp.exp2(st - m).astype(BF16)
                lo = w_far_keys + c * tq
                pv = jnp.dot(v[:, lo:lo + tq], pt, preferred_element_type=F32)
                acc = pv if acc is None else acc + pv
            pf = jnp.exp2(sf - m)
            for side, lo in ((0, 0), (1, w_far_keys + w_near)):
                rep = jnp.concatenate(
                    [jnp.broadcast_to(pf[side * far + a:side * far + a + 1, :], (D_DIL, tq))
                     for a in range(far)], axis=0)
                pbd = jnp.where(pmask, rep, 0.0).astype(BF16)
                acc = acc + jnp.dot(v[:, lo:lo + w_far_keys], pbd, preferred_element_type=F32)
            return acc[:HEAD_DIM] / acc[HEAD_DIM:HEAD_DIM + 1]

        outs = []
        groups = [range(g, g + lanes) for g in range(0, heads, lanes)]
        cur = {h: scores(h) for h in groups[0]}
        for gi, grp in enumerate(groups):
            nxt = {h: scores(h) for h in groups[gi + 1]} if gi + 1 < len(groups) else {}
            outs.extend(values(h, *cur[h]) for h in grp)
            cur = nxt
        o_ref[...] = jnp.concatenate(outs, axis=0).T.astype(o_ref.dtype)

    inside = (t0 - D_REACH >= 0) & (t0 + tq + D_REACH <= L)

    @pl.when(inside)
    def _():
        attend(None, None)

    @pl.when(jnp.logical_not(inside))
    def _():
        kpos = t0 - D_NEAR + lax.broadcasted_iota(jnp.int32, (w_near, 1), 0)
        kb_near = jnp.where((kpos >= 0) & (kpos < L), 0.0, NEG).astype(F32)
        a = lax.broadcasted_iota(jnp.int32, (2 * far, 1), 0)
        base = jnp.where(a < far, t0 - D_REACH + D_DIL * a, t0 + tq + D_NEAR + D_DIL * (a - far))
        kb_far = jnp.where((base >= 0) & (base < L), 0.0, NEG).astype(F32)
        attend(kb_near, kb_far)


def _far_table(t5_table, tq):
    a = np.arange(D_FAR_ROWS)[:, None]
    b = np.arange(tq // D_DIL)[None, :]
    rel = np.concatenate([-D_REACH + D_DIL * (a - b), tq + D_NEAR + D_DIL * (a - b)], axis=0)
    valid = np.abs(rel) <= D_REACH
    bias = t5_table.astype(F32)[:, GROUP_HEADS:2 * GROUP_HEADS][_t5_bucket(jnp.asarray(rel, jnp.int32))]
    tab = jnp.where(jnp.asarray(valid)[..., None], bias * LOG2E, NEG)
    return jnp.repeat(jnp.transpose(tab, (2, 0, 1)), D_DIL, axis=-1)


def _near_table_d(t5_table, tq):
    return _t5_table(t5_table, GROUP_HEADS, tq, D_NEAR, tq + 2 * D_NEAR, _dilated_mult)[0]


def _dilated_attention(proj, q_name, k_pad, k16, vt_pad, tab_near, tab_far, *, tq, pad, lanes):
    L, qw = proj.shape[0], _SRC[q_name][1]
    w_near = tq + 2 * D_NEAR
    w_all = tq + 2 * D_REACH
    return pl.pallas_call(
        functools.partial(_dil_kernel, tq=tq, L=L, lanes=lanes),
        grid=(L // tq,),
        in_specs=[_piece(q_name, tq),
                  pl.BlockSpec((pl.Element(w_near), pl.Element(k_pad.shape[1])),
                               lambda n: (pl.multiple_of(n * tq + pad - D_NEAR, LANES), 0)),
                  pl.BlockSpec((pl.Element(w_all // D_DIL), pl.Element(k16.shape[1])),
                               lambda n: (pl.multiple_of((n * tq + pad - D_REACH) // D_DIL,
                                                         BF16_SUBLANES), 0)),
                  pl.BlockSpec((pl.Element(vt_pad.shape[0]), pl.Element(w_all)),
                               lambda n: (0, pl.multiple_of(n * tq + pad - D_REACH, LANES))),
                  pl.BlockSpec(tab_near.shape, lambda n: (0, 0, 0), pipeline_mode=pl.Buffered(1)),
                  pl.BlockSpec(tab_far.shape, lambda n: (0, 0, 0), pipeline_mode=pl.Buffered(1))],
        out_specs=pl.BlockSpec((tq, qw), lambda n: (n, 0)),
        out_shape=jax.ShapeDtypeStruct((L, qw), BF16),
        compiler_params=pltpu.CompilerParams(
            dimension_semantics=("arbitrary",), vmem_limit_bytes=VMEM_LIMIT),
        name="dilated_attention",
    )(proj, k_pad, k16, vt_pad, tab_near, tab_far)


PREP_T = 1024
assert PREP_T >= D_REACH


def _seq_window(L, tq, r_lo):
    return lambda n: (n * tq + PREP_T - r_lo, (n * tq - r_lo, 0, L))


def _mixer_a(proj, tab_a, sink):
    L = proj.shape[0]
    k_pad, vt_pad = _kv_prep(proj, "k_a", "v_a", LANES, PREP_T, PREP_T)
    return _band_attention(proj, "q_a", k_pad, vt_pad, tab_a, chunk=A_CHUNK, lanes=BAND_LANES,
                           group=GROUP_HEADS // A_KV_HEADS, window=_seq_window(L, A_TQ, A_RADIUS),
                           sink=sink.astype(F32))


def _mixer_b(proj, tabs_b, layer):
    L = proj.shape[0]
    rows = L // GRID_W
    n_tiles = rows // NA_TILE_ROWS
    k, vt = _kv_prep(proj, "k_b", "v_b", GROUP_WIDTH, 0, PREP_T)
    window = lambda n: (jnp.clip(n * NA_TILE_ROWS - NA_ROWS // 2, 0, rows - NA_WIN_ROWS) * GRID_W, None)
    variant = lambda n: jnp.where(n == 0, 0, jnp.where(n == n_tiles - 1, 2, 1))
    return _band_attention(proj, "q_b", k, vt, tabs_b, chunk=NA_CHUNK, lanes=NA_LANES, group=1,
                           window=window, variant=variant, layer=layer)


def _mixer_c(proj, q_norm_w, k_norm_w):
    L = proj.shape[0]
    cos2, sin2 = _rope_tables(L)
    ones_bd = jnp.asarray(np.kron(np.eye(GROUP_HEADS), np.ones((HEAD_DIM, HEAD_DIM))), BF16)
    qt, k2, vt = _cprep(proj, cos2, sin2, jnp.tile(q_norm_w.astype(F32), GROUP_HEADS)[None],
                        jnp.tile(k_norm_w.astype(F32), LANES // HEAD_DIM)[None], ones_bd, C_TK)
    return _flash(qt, k2, vt, C_TQ)


def _mixer_d(proj, tabs_d):
    k_pad, vt_pad, k16 = _kv_prep(proj, "k_d", "v_d", GROUP_WIDTH, PREP_T, PREP_T, group=D_DIL)
    return _dilated_attention(proj, "q_d", k_pad, k16, vt_pad, *tabs_d, tq=D_TQ, pad=PREP_T, lanes=D_LANES)


def kernel(x, c, w_ada, b_ada, norm_w, w_in, w_out, attn_sink, na_rpb, q_norm_w, k_norm_w,
           t5_table, final_norm_w):
    B, L, D = x.shape
    assert B == 1 and L % 1024 == 0 and L // GRID_W >= NA_WIN_ROWS
    depth = w_ada.shape[0]
    x = x[0]

    mod = _ada_mod(jnp.broadcast_to(c, (8, D)), w_ada, b_ada)[:, 0:1, :]
    tab_a = _table_a(t5_table)
    tabs_d = (_near_table_d(t5_table, D_TQ), _far_table(t5_table, D_TQ))
    tabs_b = _na_tables(na_rpb)
    w_out_b = w_out.astype(BF16)
    shift, scale, gate = jnp.split(mod, 3, axis=-1)
    h = _norm(x, norm_w[0][None], scale[0], shift[0])
    for i in range(depth):
        proj = _inproj(h, w_in, i)
        ys = (_mixer_a(proj, tab_a, attn_sink[i]),
              _mixer_b(proj, tabs_b, i),
              _mixer_c(proj, q_norm_w[i], k_norm_w[i]),
              _mixer_d(proj, tabs_d))
        if i + 1 < depth:
            x, h = _outproj(x, ys, proj, w_out_b, i, gate[i],
                            (norm_w[i + 1][None], scale[i + 1], shift[i + 1]))
        else:
            x = _outproj(x, ys, proj, w_out_b, i, gate[i], (final_norm_w[None],))
    return x[None]
```
